```python
import math
import jax, jax.numpy as jnp
from jax import lax
import numpy as np

D_MODEL = 1024
BATCH = 2
SEQ = 8192
DEPTH = 1

PLE_DIM = 256
ROPE_THETA = 10000.0
EPS = 1e-6
BLOCK = 128
NEG_INF = -1e30

SWA_HEAD_DIM = 64
SWA_HEADS = D_MODEL // SWA_HEAD_DIM
SWA_KV_HEADS = SWA_HEADS // 8
SWA_WINDOW = 128

MLA_NOPE_DIM = 128
MLA_ROPE_DIM = 64
MLA_V_DIM = 128
MLA_HEADS = D_MODEL // MLA_V_DIM
MLA_Q_RANK = 256
MLA_KV_RANK = 128

N_GROUPS = 4
EXPERTS_PER_GROUP = 4
N_EXPERTS = N_GROUPS * EXPERTS_PER_GROUP
EXPERT_TOP_K = 2
D_EXPERT = 256

IN_SIZES = (SWA_HEADS * SWA_HEAD_DIM, SWA_KV_HEADS * SWA_HEAD_DIM, SWA_KV_HEADS * SWA_HEAD_DIM,
            MLA_Q_RANK, MLA_KV_RANK, MLA_ROPE_DIM, D_MODEL, D_MODEL)
D_IN = sum(IN_SIZES)

kernel_name = "hybrid_swa_mla_hier_moe_block"


def rmsnorm(x, g):
    xf = x.astype(jnp.float32)
    r = lax.rsqrt(jnp.mean(xf * xf, axis=-1, keepdims=True) + EPS)
    return (xf * r * g.astype(jnp.float32)).astype(x.dtype)


def rope_tables(seq, dim, dtype):
    pos = jnp.arange(seq, dtype=jnp.float32)
    inv = ROPE_THETA ** (-jnp.arange(0, dim, 2, dtype=jnp.float32) / dim)
    ang = pos[:, None] * inv[None, :]
    return jnp.cos(ang).astype(dtype), jnp.sin(ang).astype(dtype)


def apply_rope(x, cos, sin):
    x1, x2 = jnp.split(x, 2, axis=-1)
    c, s = cos[:, None, :], sin[:, None, :]
    return jnp.concatenate([x1 * c - x2 * s, x2 * c + x1 * s], axis=-1)


def split_points():
    points, acc = [], 0
    for size in IN_SIZES[:-1]:
        acc += size
        points.append(acc)
    return points


def sliding_window_attention(q, k, v, sinks):
    B, S, HQ, d = q.shape
    HKV = k.shape[2]
    G = HQ // HKV
    nb = S // BLOCK
    qb = q.reshape(B, nb, BLOCK, HKV, G, d)
    kb = k.reshape(B, nb, BLOCK, HKV, d)
    vb = v.reshape(B, nb, BLOCK, HKV, d)
    pad = ((0, 0), (1, 0), (0, 0), (0, 0), (0, 0))
    kk = jnp.concatenate([jnp.pad(kb[:, :-1], pad), kb], axis=2)
    vv = jnp.concatenate([jnp.pad(vb[:, :-1], pad), vb], axis=2)
    s = jnp.einsum('bnqhgd,bnkhd->bnhgqk', qb, kk,
                   preferred_element_type=jnp.float32) * (1.0 / math.sqrt(d))
    qi = jnp.arange(BLOCK)[:, None]
    kj = jnp.arange(2 * BLOCK)[None, :]
    dist = BLOCK + qi - kj
    key_pos = jnp.arange(nb)[:, None, None] * BLOCK - BLOCK + kj[None]
    allowed = (dist >= 0)[None] & (dist < SWA_WINDOW)[None] & (key_pos >= 0)
    s = jnp.where(allowed[None, :, None, None], s, NEG_INF)
    sink = sinks.astype(jnp.float32).reshape(HKV, G)[None, None, :, :, None, None]
    m = jnp.maximum(jnp.max(s, axis=-1, keepdims=True), sink)
    e = jnp.exp(s - m)
    pr = e / (jnp.sum(e, axis=-1, keepdims=True) + jnp.exp(sink - m))
    o = jnp.einsum('bnhgqk,bnkhd->bnqhgd', pr.astype(v.dtype), vv)
    return o.reshape(B, S, HQ * d)


def latent_attention(q_nope, q_rope, k_nope, k_rope, v):
    B, S, H, _ = q_nope.shape
    nb = S // BLOCK
    qn = q_nope.reshape(B, nb, BLOCK, H, MLA_NOPE_DIM).transpose(1, 0, 2, 3, 4)
    qr = q_rope.reshape(B, nb, BLOCK, H, MLA_ROPE_DIM).transpose(1, 0, 2, 3, 4)
    starts = jnp.arange(nb, dtype=jnp.int32) * BLOCK
    kpos = jnp.arange(S, dtype=jnp.int32)
    scale = 1.0 / math.sqrt(MLA_NOPE_DIM + MLA_ROPE_DIM)

    def one_block(args):
        qn_b, qr_b, start = args
        s = (jnp.einsum('bqhd,bkhd->bhqk', qn_b, k_nope, preferred_element_type=jnp.float32)
             + jnp.einsum('bqhr,bkr->bhqk', qr_b, k_rope, preferred_element_type=jnp.float32)) * scale
        qpos = start + jnp.arange(BLOCK, dtype=jnp.int32)
        s = jnp.where(kpos[None, :] <= qpos[:, None], s, NEG_INF)
        pr = jax.nn.softmax(s, axis=-1)
        return jnp.einsum('bhqk,bkhd->bqhd', pr.astype(v.dtype), v)

    o = lax.map(one_block, (qn, qr, starts))
    return o.transpose(1, 0, 2, 3, 4).reshape(B, S, H * MLA_V_DIM)


def hybrid_mixer(xn, w_in, swa_sinks, mla_g_q, mla_w_uq, mla_g_kv, mla_w_ukv, w_out,
                 cos_a, sin_a, cos_b, sin_b):
    B, S, _ = xn.shape
    proj = xn @ w_in
    q_a, k_a, v_a, c_q, c_kv, k_rope, gate_a, gate_b = jnp.split(proj, split_points(), axis=-1)
    q_a = apply_rope(q_a.reshape(B, S, SWA_HEADS, SWA_HEAD_DIM), cos_a, sin_a)
    k_a = apply_rope(k_a.reshape(B, S, SWA_KV_HEADS, SWA_HEAD_DIM), cos_a, sin_a)
    v_a = v_a.reshape(B, S, SWA_KV_HEADS, SWA_HEAD_DIM)
    o_a = sliding_window_attention(q_a, k_a, v_a, swa_sinks)
    q = (rmsnorm(c_q, mla_g_q) @ mla_w_uq).reshape(B, S, MLA_HEADS, MLA_NOPE_DIM + MLA_ROPE_DIM)
    q_nope, q_rope = jnp.split(q, [MLA_NOPE_DIM], axis=-1)
    q_rope = apply_rope(q_rope, cos_b, sin_b)
    kv = (rmsnorm(c_kv, mla_g_kv) @ mla_w_ukv).reshape(B, S, MLA_HEADS, MLA_NOPE_DIM + MLA_V_DIM)
    k_nope, v_b = jnp.split(kv, [MLA_NOPE_DIM], axis=-1)
    k_rope = apply_rope(k_rope[:, :, None, :], cos_b, sin_b)[:, :, 0, :]
    o_b = latent_attention(q_nope, q_rope, k_nope, k_rope, v_b)
    merged = jax.nn.sigmoid(gate_a) * o_a + jax.nn.sigmoid(gate_b) * o_b
    return merged @ w_out


def hierarchical_moe(xn, w_router_group, b_router_group, w_router_expert, b_router_expert,
                     w_expert_in, w_expert_out):
    B, S, D = xn.shape
    T = B * S
    xt = xn.reshape(T, D)
    g_prob = jax.nn.softmax((xt @ w_router_group + b_router_group).astype(jnp.float32), axis=-1)
    gidx = jnp.argmax(g_prob, axis=-1)
    g_w = jnp.take_along_axis(g_prob, gidx[:, None], axis=-1)
    e_logits = (xt @ w_router_expert + b_router_expert).astype(jnp.float32)
    e_logits = e_logits.reshape(T, N_GROUPS, EXPERTS_PER_GROUP)
    e_sel = jnp.take_along_axis(e_logits, gidx[:, None, None], axis=1)[:, 0]
    e_prob = jax.nn.softmax(e_sel, axis=-1)
    top_v, top_i = lax.top_k(e_prob, EXPERT_TOP_K)
    top_v = top_v / jnp.sum(top_v, axis=-1, keepdims=True)
    w_group = jnp.sum(jax.nn.one_hot(top_i, EXPERTS_PER_GROUP, dtype=jnp.float32) * top_v[..., None], axis=1)
    combine = (jax.nn.one_hot(gidx, N_GROUPS, dtype=jnp.float32)[:, :, None]
               * w_group[:, None, :] * g_w[:, :, None]).reshape(T, N_EXPERTS)
    hid = jnp.einsum('td,edf->tef', xt, w_expert_in)
    gate, up = jnp.split(hid, 2, axis=-1)
    act = jax.nn.silu(gate) * up * combine[..., None].astype(xn.dtype)
    y = jnp.einsum('tef,efd->td', act, w_expert_out)
    return y.reshape(B, S, D)


def setup_inputs(seed: int = 0) -> dict:
    key = jax.random.key(seed)
    ks = jax.random.split(key, 24)

    def nrm(k, shape, scale):
        return jax.random.normal(k, shape, jnp.float32) * scale

    def gain(k, shape):
        return 1.0 + 0.05 * jax.random.normal(k, shape, jnp.float32)

    L = DEPTH
    return {
        "x": nrm(ks[0], (BATCH, SEQ, D_MODEL), 1.0),
        "p": nrm(ks[1], (DEPTH, BATCH, SEQ, PLE_DIM), 1.0),
        "g_mix": gain(ks[2], (L, D_MODEL)),
        "w_in": nrm(ks[3], (L, D_MODEL, D_IN), D_MODEL ** -0.5),
        "swa_sinks": nrm(ks[4], (L, SWA_HEADS), 0.5),
        "mla_g_q": gain(ks[5], (L, MLA_Q_RANK)),
        "mla_w_uq": nrm(ks[6], (L, MLA_Q_RANK, MLA_HEADS * (MLA_NOPE_DIM + MLA_ROPE_DIM)), MLA_Q_RANK ** -0.5),
        "mla_g_kv": gain(ks[7], (L, MLA_KV_RANK)),
        "mla_w_ukv": nrm(ks[8], (L, MLA_KV_RANK, MLA_HEADS * (MLA_NOPE_DIM + MLA_V_DIM)), MLA_KV_RANK ** -0.5),
        "w_out": nrm(ks[9], (L, D_MODEL, D_MODEL), D_MODEL ** -0.5),
        "g_ffn": gain(ks[10], (L, D_MODEL)),
        "w_router_group": nrm(ks[11], (L, D_MODEL, N_GROUPS), D_MODEL ** -0.5),
        "b_router_group": nrm(ks[12], (L, N_GROUPS), 0.01),
        "w_router_expert": nrm(ks[13], (L, D_MODEL, N_EXPERTS), D_MODEL ** -0.5),
        "b_router_expert": nrm(ks[14], (L, N_EXPERTS), 0.01),
        "w_expert_in": nrm(ks[15], (L, N_EXPERTS, D_MODEL, 2 * D_EXPERT), D_MODEL ** -0.5),
        "w_expert_out": nrm(ks[16], (L, N_EXPERTS, D_EXPERT, D_MODEL), D_EXPERT ** -0.5),
        "g_ple": gain(ks[17], (L, D_MODEL)),
        "w_ple_gate": nrm(ks[18], (L, D_MODEL, D_MODEL), D_MODEL ** -0.5),
        "w_ple_proj": nrm(ks[19], (L, PLE_DIM, D_MODEL), PLE_DIM ** -0.5),
        "g_final": gain(ks[20], (D_MODEL,)),
    }


def reference(x, p, g_mix, w_in, swa_sinks, mla_g_q, mla_w_uq, mla_g_kv, mla_w_ukv, w_out,
              g_ffn, w_router_group, b_router_group, w_router_expert, b_router_expert,
              w_expert_in, w_expert_out, g_ple, w_ple_gate, w_ple_proj, g_final):
    S = x.shape[1]
    cos_a, sin_a = rope_tables(S, SWA_HEAD_DIM, x.dtype)
    cos_b, sin_b = rope_tables(S, MLA_ROPE_DIM, x.dtype)
    h = x
    for i in range(DEPTH):
        h = h + hybrid_mixer(rmsnorm(h, g_mix[i]), w_in[i], swa_sinks[i], mla_g_q[i], mla_w_uq[i],
                             mla_g_kv[i], mla_w_ukv[i], w_out[i], cos_a, sin_a, cos_b, sin_b)
        h = h + hierarchical_moe(rmsnorm(h, g_ffn[i]), w_router_group[i], b_router_group[i],
                                 w_router_expert[i], b_router_expert[i], w_expert_in[i], w_expert_out[i])
        ple_gate = jax.nn.sigmoid(rmsnorm(h, g_ple[i]) @ w_ple_gate[i])
        h = h + ple_gate * (p[i] @ w_ple_proj[i])
    return rmsnorm(h, g_final)
```

```python
import functools
import math

import jax
import jax.numpy as jnp
from jax import lax
from jax.experimental import pallas as pl
from jax.experimental.pallas import tpu as pltpu

D_MODEL = 1024
PLE_DIM = 256
ROPE_THETA = 10000.0
EPS = 1e-6
NEG_INF = -1e30

SWA_HEAD_DIM = 64
SWA_HEADS = D_MODEL // SWA_HEAD_DIM
SWA_KV_HEADS = SWA_HEADS // 8
SWA_GROUP = SWA_HEADS // SWA_KV_HEADS
SWA_WINDOW = 128

MLA_NOPE_DIM = 128
MLA_ROPE_DIM = 64
MLA_V_DIM = 128
MLA_HEADS = D_MODEL // MLA_V_DIM
MLA_Q_RANK = 256
MLA_KV_RANK = 128
MLA_QK_DIM = MLA_NOPE_DIM + MLA_ROPE_DIM

N_GROUPS = 4
EXPERTS_PER_GROUP = 4
N_EXPERTS = N_GROUPS * EXPERTS_PER_GROUP
D_EXPERT = 256

LANES = 128
VMEM_LIMIT = 56 * 1024 * 1024

_C_QA = 0
_C_KA = _C_QA + SWA_HEADS * SWA_HEAD_DIM
_C_VA = _C_KA + SWA_KV_HEADS * SWA_HEAD_DIM
_C_CQ = _C_VA + SWA_KV_HEADS * SWA_HEAD_DIM
_C_CKV = _C_CQ + MLA_Q_RANK
_C_GA = _C_CKV + MLA_KV_RANK
_C_GB = _C_GA + D_MODEL
_C_KR = _C_GB + D_MODEL
_C_END = _C_KR + LANES

BF16 = jnp.bfloat16
F32 = jnp.float32


def _dot(a, b):
    return jnp.dot(a, b, preferred_element_type=F32)


def _dot_nt(a, b):
    return lax.dot_general(a, b, (((1,), (1,)), ((), ())), preferred_element_type=F32)


def _rms(x, g):
    r = lax.rsqrt(jnp.mean(x * x, axis=-1, keepdims=True) + EPS)
    return x * r * g


def _sigmoid(x):
    return 1.0 / (1.0 + jnp.exp(-x))


def _rope_lanes(x, cos, sin_signed):
    lane = lax.broadcasted_iota(jnp.int32, x.shape, 1)
    upper = (lane & (SWA_HEAD_DIM // 2)) != 0
    partner = jnp.where(upper, pltpu.roll(x, SWA_HEAD_DIM // 2, 1),
                        pltpu.roll(x, LANES - SWA_HEAD_DIM // 2, 1))
    return x * cos + partner * sin_signed


def _inproj_kernel(x_ref, g_ref, w_ref, cos_ref, sin_ref, gq_ref, wuq_ref, gkv_ref, wukv_ref,
                   qa_ref, ka_ref, va_ref, ga_ref, gb_ref, qm_ref, km_ref, vm_ref):
    xn = _rms(x_ref[...], g_ref[...]).astype(BF16)
    cos = cos_ref[...]
    sin = sin_ref[...]

    qa_scale = 1.0 / math.sqrt(SWA_HEAD_DIM)
    for j in range(SWA_HEADS // 2):
        q = _dot(xn, w_ref[:, _C_QA + j * LANES:_C_QA + (j + 1) * LANES])
        q = (_rope_lanes(q, cos, sin) * qa_scale).astype(BF16)
        qa_ref[0, 2 * j] = q[:, :SWA_HEAD_DIM]
        qa_ref[0, 2 * j + 1] = q[:, SWA_HEAD_DIM:]
    k = _rope_lanes(_dot(xn, w_ref[:, _C_KA:_C_VA]), cos, sin).astype(BF16)
    ka_ref[0, 0] = k[:, :SWA_HEAD_DIM]
    ka_ref[0, 1] = k[:, SWA_HEAD_DIM:]
    v = _dot(xn, w_ref[:, _C_VA:_C_CQ]).astype(BF16)
    va_ref[0, 0] = v[:, :SWA_HEAD_DIM]
    va_ref[0, 1] = v[:, SWA_HEAD_DIM:]

    ga_ref[...] = _sigmoid(_dot(xn, w_ref[:, _C_GA:_C_GB])).astype(BF16)
    gb_ref[...] = _sigmoid(_dot(xn, w_ref[:, _C_GB:_C_KR])).astype(BF16)

    cq = _rms(_dot(xn, w_ref[:, _C_CQ:_C_CKV]), gq_ref[...]).astype(BF16)
    ckv = _rms(_dot(xn, w_ref[:, _C_CKV:_C_GA]), gkv_ref[...]).astype(BF16)
    kr = _rope_lanes(_dot(xn, w_ref[:, _C_KR:_C_END]), cos, sin)[:, :MLA_ROPE_DIM].astype(BF16)
    qm_scale = 1.0 / math.sqrt(MLA_QK_DIM)
    n_nope = MLA_HEADS * MLA_NOPE_DIM
    for h in range(MLA_HEADS):
        qn = _dot(cq, wuq_ref[:, h * MLA_NOPE_DIM:(h + 1) * MLA_NOPE_DIM]) * qm_scale
        qm_ref[0, h, :, :MLA_NOPE_DIM] = qn.astype(BF16)
        km_ref[0, h, :, :MLA_NOPE_DIM] = _dot(
            ckv, wukv_ref[:, h * MLA_NOPE_DIM:(h + 1) * MLA_NOPE_DIM]).astype(BF16)
        km_ref[0, h, :, MLA_NOPE_DIM:] = kr
        vm_ref[0, h] = _dot(
            ckv, wukv_ref[:, n_nope + h * MLA_V_DIM:n_nope + (h + 1) * MLA_V_DIM]).astype(BF16)
    for j in range(MLA_HEADS // 2):
        qr = _dot(cq, wuq_ref[:, n_nope + j * LANES:n_nope + (j + 1) * LANES])
        qr = (_rope_lanes(qr, cos, sin) * qm_scale).astype(BF16)
        qm_ref[0, 2 * j, :, MLA_NOPE_DIM:] = qr[:, :MLA_ROPE_DIM]
        qm_ref[0, 2 * j + 1, :, MLA_NOPE_DIM:] = qr[:, MLA_ROPE_DIM:]


def _inproj(x2, g_mix, w_cat, cos_t, sin_t, g_q, w_uq, g_kv, w_ukv, batch, seq, tm):
    tokens = batch * seq
    nt = seq // tm
    const = lambda t: (0, 0)
    tok = lambda t: (t, 0)
    head = lambda t: (t // nt, 0, t % nt, 0)
    out_shape = (
        jax.ShapeDtypeStruct((batch, SWA_HEADS, seq, SWA_HEAD_DIM), BF16),
        jax.ShapeDtypeStruct((batch, SWA_KV_HEADS, seq, SWA_HEAD_DIM), BF16),
        jax.ShapeDtypeStruct((batch, SWA_KV_HEADS, seq, SWA_HEAD_DIM), BF16),
        jax.ShapeDtypeStruct((tokens, D_MODEL), BF16),
        jax.ShapeDtypeStruct((tokens, D_MODEL), BF16),
        jax.ShapeDtypeStruct((batch, MLA_HEADS, seq, MLA_QK_DIM), BF16),
        jax.ShapeDtypeStruct((batch, MLA_HEADS, seq, MLA_QK_DIM), BF16),
        jax.ShapeDtypeStruct((batch, MLA_HEADS, seq, MLA_V_DIM), BF16),
    )
    return pl.pallas_call(
        _inproj_kernel,
        grid=(tokens // tm,),
        in_specs=[
            pl.BlockSpec((tm, D_MODEL), tok),
            pl.BlockSpec((1, D_MODEL), const),
            pl.BlockSpec((D_MODEL, _C_END), const, pipeline_mode=pl.Buffered(1)),
            pl.BlockSpec((tm, LANES), lambda t: (t % nt, 0)),
            pl.BlockSpec((tm, LANES), lambda t: (t % nt, 0)),
            pl.BlockSpec((1, MLA_Q_RANK), const),
            pl.BlockSpec(w_uq.shape, const, pipeline_mode=pl.Buffered(1)),
            pl.BlockSpec((1, MLA_KV_RANK), const),
            pl.BlockSpec(w_ukv.shape, const, pipeline_mode=pl.Buffered(1)),
        ],
        out_specs=(
            pl.BlockSpec((1, SWA_HEADS, tm, SWA_HEAD_DIM), head),
            pl.BlockSpec((1, SWA_KV_HEADS, tm, SWA_HEAD_DIM), head),
            pl.BlockSpec((1, SWA_KV_HEADS, tm, SWA_HEAD_DIM), head),
            pl.BlockSpec((tm, D_MODEL), tok),
            pl.BlockSpec((tm, D_MODEL), tok),
            pl.BlockSpec((1, MLA_HEADS, tm, MLA_QK_DIM), head),
            pl.BlockSpec((1, MLA_HEADS, tm, MLA_QK_DIM), head),
            pl.BlockSpec((1, MLA_HEADS, tm, MLA_V_DIM), head),
        ),
        out_shape=out_shape,
        compiler_params=pltpu.CompilerParams(
            dimension_semantics=("arbitrary",), vmem_limit_bytes=VMEM_LIMIT),
        name="inproj",
    )(x2, g_mix, w_cat, cos_t, sin_t, g_q, w_uq, g_kv, w_ukv)


def _swa_kernel(q_ref, k_ref, kp_ref, v_ref, vp_ref, sink_ref, o_ref, *, n_sub):
    first_tile = pl.program_id(1) == 0
    blk = SWA_WINDOW
    qi = lax.broadcasted_iota(jnp.int32, (blk, 2 * blk), 0)
    kj = lax.broadcasted_iota(jnp.int32, (blk, 2 * blk), 1)
    dist = blk + qi - kj
    band = (dist >= 0) & (dist < SWA_WINDOW)
    rows = SWA_GROUP * blk
    for g in range(SWA_KV_HEADS):
        sink = sink_ref[g * rows:(g + 1) * rows, :]
        for n in range(n_sub):
            q = q_ref[0, g * SWA_GROUP:(g + 1) * SWA_GROUP, n * blk:(n + 1) * blk, :]
            q = q.reshape(rows, SWA_HEAD_DIM)
            if n == 0:
                k_prev, v_prev = kp_ref[0, g], vp_ref[0, g]
            else:
                k_prev = k_ref[0, g, (n - 1) * blk:n * blk, :]
                v_prev = v_ref[0, g, (n - 1) * blk:n * blk, :]
            kk = jnp.concatenate([k_prev, k_ref[0, g, n * blk:(n + 1) * blk, :]], axis=0)
            vv = jnp.concatenate([v_prev, v_ref[0, g, n * blk:(n + 1) * blk, :]], axis=0)
            s = _dot_nt(q, kk)
            allowed = band
            if n == 0:
                allowed = band & ((kj >= blk) | jnp.logical_not(first_tile))
            s = jnp.where(allowed[None], s.reshape(SWA_GROUP, blk, 2 * blk), NEG_INF)
            s = s.reshape(rows, 2 * blk)
            m = jnp.maximum(jnp.max(s, axis=-1, keepdims=True), sink)
            e = jnp.exp(s - m)
            denom = jnp.sum(e, axis=-1, keepdims=True) + jnp.exp(sink - m)
            o = _dot(e.astype(BF16), vv) / denom
            for hh in range(SWA_GROUP):
                h = g * SWA_GROUP + hh
                o_ref[0, n * blk:(n + 1) * blk, h * SWA_HEAD_DIM:(h + 1) * SWA_HEAD_DIM] = (
                    o[hh * blk:(hh + 1) * blk, :].astype(BF16))


def _swa(q_a, k_a, v_a, sink_rows, batch, seq, tq):
    n_sub = tq // SWA_WINDOW
    cur = lambda b, i: (b, 0, i, 0)
    prev = lambda b, i: (b, 0, jnp.maximum(i * n_sub - 1, 0), 0)
    return pl.pallas_call(
        functools.partial(_swa_kernel, n_sub=n_sub),
        grid=(batch, seq // tq),
        in_specs=[
            pl.BlockSpec((1, SWA_HEADS, tq, SWA_HEAD_DIM), cur),
            pl.BlockSpec((1, SWA_KV_HEADS, tq, SWA_HEAD_DIM), cur),
            pl.BlockSpec((1, SWA_KV_HEADS, SWA_WINDOW, SWA_HEAD_DIM), prev),
            pl.BlockSpec((1, SWA_KV_HEADS, tq, SWA_HEAD_DIM), cur),
            pl.BlockSpec((1, SWA_KV_HEADS, SWA_WINDOW, SWA_HEAD_DIM), prev),
            pl.BlockSpec(sink_rows.shape, lambda b, i: (0, 0)),
        ],
        out_specs=pl.BlockSpec((1, tq, D_MODEL), lambda b, i: (b, i, 0)),
        out_shape=jax.ShapeDtypeStruct((batch, seq, D_MODEL), BF16),
        compiler_params=pltpu.CompilerParams(
            dimension_semantics=("arbitrary", "arbitrary"), vmem_limit_bytes=VMEM_LIMIT),
        name="swa",
    )(q_a, k_a, k_a, v_a, v_a, sink_rows)


def _mla_kernel(q_ref, k_ref, v_ref, o_ref, m_scr, l_scr, acc_scr, *, tq, tk):
    i = pl.program_id(2)
    q = q_ref[0, 0]
    m_scr[...] = jnp.full(m_scr.shape, NEG_INF, F32)
    l_scr[...] = jnp.zeros(l_scr.shape, F32)
    acc_scr[...] = jnp.zeros(acc_scr.shape, F32)

    def step(start, masked):
        k = k_ref[0, 0, pl.ds(start, tk), :]
        v = v_ref[0, 0, pl.ds(start, tk), :]
        s = _dot_nt(q, k)
        if masked:
            qpos = lax.broadcasted_iota(jnp.int32, (tq, tk), 0)
            kpos = lax.broadcasted_iota(jnp.int32, (tq, tk), 1)
            s = jnp.where(kpos <= qpos, s, NEG_INF)
        m_old = m_scr[...]
        m_new = jnp.maximum(m_old, jnp.max(s, axis=-1, keepdims=True))
        alpha = jnp.exp(m_old - m_new)
        p = jnp.exp(s - m_new)
        l_scr[...] = alpha * l_scr[...] + jnp.sum(p, axis=-1, keepdims=True)
        acc_scr[...] = alpha * acc_scr[...] + _dot(p.astype(BF16), v)
        m_scr[...] = m_new

    def body(j, carry):
        step(pl.multiple_of(j * tk, tk), masked=False)
        return carry

    lax.fori_loop(0, i * (tq // tk), body, 0)
    step(pl.multiple_of(i * tq, tq), masked=True)
    o_ref[0] = (acc_scr[...] / l_scr[...]).astype(o_ref.dtype)


def _mla(q_m, k_m, v_m, batch, seq, tq):
    tk = tq
    return pl.pallas_call(
        functools.partial(_mla_kernel, tq=tq, tk=tk),
        grid=(batch, MLA_HEADS, seq // tq),
        in_specs=[
            pl.BlockSpec((1, 1, tq, MLA_QK_DIM), lambda b, h, i: (b, h, i, 0)),
            pl.BlockSpec((1, 1, seq, MLA_QK_DIM), lambda b, h, i: (b, h, 0, 0)),
            pl.BlockSpec((1, 1, seq, MLA_V_DIM), lambda b, h, i: (b, h, 0, 0)),
        ],
        out_specs=pl.BlockSpec((1, tq, MLA_V_DIM), lambda b, h, i: (b, i, h)),
        out_shape=jax.ShapeDtypeStruct((batch, seq, D_MODEL), BF16),
        scratch_shapes=[
            pltpu.VMEM((tq, 1), F32),
            pltpu.VMEM((tq, 1), F32),
            pltpu.VMEM((tq, MLA_V_DIM), F32),
        ],
        compiler_params=pltpu.CompilerParams(
            dimension_semantics=("arbitrary", "arbitrary", "arbitrary"),
            vmem_limit_bytes=VMEM_LIMIT),
        name="mla",
    )(q_m, k_m, v_m)


def _outproj_kernel(x_ref, oa_ref, ob_ref, ga_ref, gb_ref, wo_ref, gf_ref, wr_ref, br_ref,
                    h_ref, xn_ref, comb_ref):
    merged = (ga_ref[...].astype(F32) * oa_ref[...].astype(F32)
              + gb_ref[...].astype(F32) * ob_ref[...].astype(F32)).astype(BF16)
    h = x_ref[...] + _dot(merged, wo_ref[...])
    h_ref[...] = h
    xn = _rms(h, gf_ref[...])
    xn_ref[...] = xn.astype(BF16)

    x_hi = xn.astype(BF16)
    x_lo = (xn - x_hi.astype(F32)).astype(BF16)
    lg = (_dot(x_hi, wr_ref[0]) + _dot(x_lo, wr_ref[0]) + _dot(x_hi, wr_ref[1])) + br_ref[...]

    lane = lax.broadcasted_iota(jnp.int32, lg.shape, 1)
    is_group = (lane >= N_EXPERTS) & (lane < N_EXPERTS + N_GROUPS)
    gl = jnp.where(is_group, lg, NEG_INF)
    g_max = jnp.max(gl, axis=-1, keepdims=True)
    g_sum = jnp.sum(jnp.where(is_group, jnp.exp(gl - g_max), 0.0), axis=-1, keepdims=True)
    g_w = 1.0 / g_sum
    big = jnp.int32(LANES)
    g_lane = jnp.min(jnp.where(is_group & (gl == g_max), lane, big), axis=-1, keepdims=True)
    gidx = g_lane - N_EXPERTS

    sel = (lane >= gidx * EXPERTS_PER_GROUP) & (lane < (gidx + 1) * EXPERTS_PER_GROUP)
    el = jnp.where(sel, lg, NEG_INF)
    e_max = jnp.max(el, axis=-1, keepdims=True)
    e_exp = jnp.where(sel, jnp.exp(el - e_max), 0.0)
    e_prob = e_exp / jnp.sum(e_exp, axis=-1, keepdims=True)
    p1 = jnp.max(e_prob, axis=-1, keepdims=True)
    i1 = jnp.min(jnp.where(sel & (e_prob == p1), lane, big), axis=-1, keepdims=True)
    rest = sel & (lane != i1)
    p2 = jnp.max(jnp.where(rest, e_prob, -1.0), axis=-1, keepdims=True)
    i2 = jnp.min(jnp.where(rest & (e_prob == p2), lane, big), axis=-1, keepdims=True)
    top_sum = p1 + p2
    comb = jnp.where(lane == i1, p1 / top_sum, jnp.where(lane == i2, p2 / top_sum, 0.0))
    comb_ref[...] = comb * g_w


def _outproj(x2, o_a, o_b, g_a, g_b, w_o, g_ffn, w_r, b_r, tm):
    tokens = x2.shape[0]
    tok = lambda t: (t, 0)
    const = lambda t: (0, 0)
    return pl.pallas_call(
        _outproj_kernel,
        grid=(tokens // tm,),
        in_specs=[
            pl.BlockSpec((tm, D_MODEL), tok),
            pl.BlockSpec((tm, D_MODEL), tok),
            pl.BlockSpec((tm, D_MODEL), tok),
            pl.BlockSpec((tm, D_MODEL), tok),
            pl.BlockSpec((tm, D_MODEL), tok),
            pl.BlockSpec((D_MODEL, D_MODEL), const),
            pl.BlockSpec((1, D_MODEL), const),
            pl.BlockSpec((2, D_MODEL, LANES), lambda t: (0, 0, 0)),
            pl.BlockSpec((1, LANES), const),
        ],
        out_specs=(
            pl.BlockSpec((tm, D_MODEL), tok),
            pl.BlockSpec((tm, D_MODEL), tok),
            pl.BlockSpec((tm, LANES), tok),
        ),
        out_shape=(
            jax.ShapeDtypeStruct((tokens, D_MODEL), F32),
            jax.ShapeDtypeStruct((tokens, D_MODEL), BF16),
            jax.ShapeDtypeStruct((tokens, LANES), F32),
        ),
        compiler_params=pltpu.CompilerParams(
            dimension_semantics=("arbitrary",), vmem_limit_bytes=VMEM_LIMIT),
        name="outproj",
    )(x2, o_a, o_b, g_a, g_b, w_o, g_ffn, w_r, b_r)


def _moe_kernel(h_ref, xn_ref, comb_ref, wi_ref, wo_ref, o_ref):
    e = pl.program_id(1)

    @pl.when(e == 0)
    def _():
        o_ref[...] = h_ref[...]

    comb = comb_ref[...]
    lane = lax.broadcasted_iota(jnp.int32, comb.shape, 1)
    c = jnp.sum(jnp.where(lane == e, comb, 0.0), axis=-1, keepdims=True)
    hid = _dot(xn_ref[...], wi_ref[0])
    gate = hid[:, :D_EXPERT]
    up = hid[:, D_EXPERT:]
    act = (gate * _sigmoid(gate)) * up * c
    o_ref[...] += _dot(act.astype(BF16), wo_ref[0])


def _moe(h1, xn, comb, w_ei, w_eo, tm):
    tokens = h1.shape[0]
    tok = lambda t, e: (t, 0)
    return pl.pallas_call(
        _moe_kernel,
        grid=(tokens // tm, N_EXPERTS),
        in_specs=[
            pl.BlockSpec((tm, D_MODEL), tok),
            pl.BlockSpec((tm, D_MODEL), tok),
            pl.BlockSpec((tm, LANES), tok),
            pl.BlockSpec((1, D_MODEL, 2 * D_EXPERT), lambda t, e: (e, 0, 0)),
            pl.BlockSpec((1, D_EXPERT, D_MODEL), lambda t, e: (e, 0, 0)),
        ],
        out_specs=pl.BlockSpec((tm, D_MODEL), tok),
        out_shape=jax.ShapeDtypeStruct((tokens, D_MODEL), F32),
        compiler_params=pltpu.CompilerParams(
            dimension_semantics=("arbitrary", "arbitrary"), vmem_limit_bytes=VMEM_LIMIT),
        name="moe",
    )(h1, xn, comb, w_ei, w_eo)


def _ple_kernel(h_ref, p_ref, gp_ref, wg_ref, wp_ref, gfin_ref, o_ref):
    h = h_ref[...]
    gate = _sigmoid(_dot(_rms(h, gp_ref[...]).astype(BF16), wg_ref[...]))
    h = h + gate * _dot(p_ref[...].astype(BF16), wp_ref[...])
    o_ref[...] = _rms(h, gfin_ref[...])


def _ple(h2, p2, g_ple, w_pg, w_pp, g_final, tm):
    tokens = h2.shape[0]
    tok = lambda t: (t, 0)
    const = lambda t: (0, 0)
    return pl.pallas_call(
        _ple_kernel,
        grid=(tokens // tm,),
        in_specs=[
            pl.BlockSpec((tm, D_MODEL), tok),
            pl.BlockSpec((tm, PLE_DIM), tok),
            pl.BlockSpec((1, D_MODEL), const),
            pl.BlockSpec((D_MODEL, D_MODEL), const),
            pl.BlockSpec((PLE_DIM, D_MODEL), const),
            pl.BlockSpec((1, D_MODEL), const),
        ],
        out_specs=pl.BlockSpec((tm, D_MODEL), tok),
        out_shape=jax.ShapeDtypeStruct((tokens, D_MODEL), F32),
        compiler_params=pltpu.CompilerParams(
            dimension_semantics=("arbitrary",), vmem_limit_bytes=VMEM_LIMIT),
        name="ple",
    )(h2, p2, g_ple, w_pg, w_pp, g_final)


def _rope_tables(seq):
    half = SWA_HEAD_DIM // 2
    pos = jnp.arange(seq, dtype=F32)
    inv = ROPE_THETA ** (-jnp.arange(0, SWA_HEAD_DIM, 2, dtype=F32) / SWA_HEAD_DIM)
    ang = pos[:, None] * inv[None, :]
    cos, sin = jnp.cos(ang), jnp.sin(ang)
    reps = LANES // half
    cos_t = jnp.tile(cos, (1, reps))
    sin_t = jnp.tile(jnp.concatenate([-sin, sin], axis=1), (1, reps // 2))
    return cos_t, sin_t


def _layer(h2d, p2d, g_mix, w_in, sinks, g_q, w_uq, g_kv, w_ukv, w_out, g_ffn, w_rg, b_rg,
           w_re, b_re, w_ei, w_eo, g_ple, w_pg, w_pp, g_final, cos_t, sin_t, batch, seq):
    assert SWA_HEAD_DIM == MLA_ROPE_DIM, "one rotary table serves both mixers"
    sizes = (SWA_HEADS * SWA_HEAD_DIM, SWA_KV_HEADS * SWA_HEAD_DIM, SWA_KV_HEADS * SWA_HEAD_DIM,
             MLA_Q_RANK, MLA_KV_RANK, MLA_ROPE_DIM, D_MODEL, D_MODEL)
    offs = [0]
    for s in sizes:
        offs.append(offs[-1] + s)
    seg = [w_in[:, offs[i]:offs[i + 1]] for i in range(len(sizes))]
    w_cat = jnp.concatenate(
        seg[:5] + [seg[6], seg[7], seg[5], jnp.zeros((D_MODEL, LANES - MLA_ROPE_DIM), w_in.dtype)],
        axis=1).astype(BF16)
    uq = w_uq.reshape(MLA_Q_RANK, MLA_HEADS, MLA_QK_DIM)
    w_uq_c = jnp.concatenate(
        [uq[:, :, :MLA_NOPE_DIM].reshape(MLA_Q_RANK, -1),
         uq[:, :, MLA_NOPE_DIM:].reshape(MLA_Q_RANK, -1)], axis=1).astype(BF16)
    ukv = w_ukv.reshape(MLA_KV_RANK, MLA_HEADS, MLA_NOPE_DIM + MLA_V_DIM)
    w_ukv_c = jnp.concatenate(
        [ukv[:, :, :MLA_NOPE_DIM].reshape(MLA_KV_RANK, -1),
         ukv[:, :, MLA_NOPE_DIM:].reshape(MLA_KV_RANK, -1)], axis=1).astype(BF16)

    q_a, k_a, v_a, g_a, g_b, q_m, k_m, v_m = _inproj(
        h2d, g_mix[None], w_cat, cos_t, sin_t, g_q[None], w_uq_c, g_kv[None], w_ukv_c,
        batch, seq, tm=256)

    sink_rows = jnp.repeat(sinks.astype(F32), SWA_WINDOW)[:, None]
    o_a = _swa(q_a, k_a, v_a, sink_rows, batch, seq, tq=512)
    o_b = _mla(q_m, k_m, v_m, batch, seq, tq=512)

    w_r = jnp.concatenate(
        [w_re, w_rg, jnp.zeros((D_MODEL, LANES - N_EXPERTS - N_GROUPS), F32)], axis=1)
    w_r_hi = w_r.astype(BF16)
    w_r_lo = (w_r - w_r_hi.astype(F32)).astype(BF16)
    w_r2 = jnp.stack([w_r_hi, w_r_lo])
    b_r = jnp.concatenate(
        [b_re, b_rg, jnp.zeros((LANES - N_EXPERTS - N_GROUPS,), F32)])[None]
    tokens = batch * seq
    h1, xn2, comb = _outproj(
        h2d, o_a.reshape(tokens, D_MODEL), o_b.reshape(tokens, D_MODEL), g_a, g_b,
        w_out.astype(BF16), g_ffn[None], w_r2, b_r, tm=512)

    h2 = _moe(h1, xn2, comb, w_ei.astype(BF16), w_eo.astype(BF16), tm=1024)
    return _ple(h2, p2d, g_ple[None], w_pg.astype(BF16), w_pp.astype(BF16), g_final[None], tm=512)


def kernel(x, p, g_mix, w_in, swa_sinks, mla_g_q, mla_w_uq, mla_g_kv, mla_w_ukv, w_out, g_ffn,
           w_router_group, b_router_group, w_router_expert, b_router_expert, w_expert_in,
           w_expert_out, g_ple, w_ple_gate, w_ple_proj, g_final):
    batch, seq, d = x.shape
    depth = p.shape[0]
    assert d == D_MODEL and depth == 1, "final RMSNorm is fused into the single layer"
    cos_t, sin_t = _rope_tables(seq)
    out = _layer(
        x.reshape(batch * seq, d), p[0].reshape(batch * seq, PLE_DIM), g_mix[0], w_in[0],
        swa_sinks[0], mla_g_q[0], mla_w_uq[0], mla_g_kv[0], mla_w_ukv[0], w_out[0], g_ffn[0],
        w_router_group[0], b_router_group[0], w_router_expert[0], b_router_expert[0],
        w_expert_in[0], w_expert_out[0], g_ple[0], w_ple_gate[0], w_ple_proj[0], g_final,
        cos_t, sin_t, batch, seq)
    return out.reshape(batch, seq, d)
```

```python
import functools
import math

import jax
import jax.numpy as jnp
from jax import lax
from jax.experimental import pallas as pl
from jax.experimental.pallas import tpu as pltpu

D_MODEL = 1024
PLE_DIM = 256
ROPE_THETA = 10000.0
EPS = 1e-6
NEG_INF = -1e30

SWA_HEAD_DIM = 64
SWA_HEADS = D_MODEL // SWA_HEAD_DIM
SWA_KV_HEADS = SWA_HEADS // 8
SWA_GROUP = SWA_HEADS // SWA_KV_HEADS
SWA_WINDOW = 128

MLA_NOPE_DIM = 128
MLA_ROPE_DIM = 64
MLA_V_DIM = 128
MLA_HEADS = D_MODEL // MLA_V_DIM
MLA_Q_RANK = 256
MLA_KV_RANK = 128
MLA_QK_DIM = MLA_NOPE_DIM + MLA_ROPE_DIM

N_GROUPS = 4
EXPERTS_PER_GROUP = 4
N_EXPERTS = N_GROUPS * EXPERTS_PER_GROUP
D_EXPERT = 256

LANES = 128
LOG2_E = 1.4426950408889634
MLA_KV_CHUNK = 256
VMEM_LIMIT = 56 * 1024 * 1024

_C_QA = 0
_C_KA = _C_QA + SWA_HEADS * SWA_HEAD_DIM
_C_VA = _C_KA + SWA_KV_HEADS * SWA_HEAD_DIM
_C_CQ = _C_VA + SWA_KV_HEADS * SWA_HEAD_DIM
_C_CKV = _C_CQ + MLA_Q_RANK
_C_GA = _C_CKV + MLA_KV_RANK
_C_GB = _C_GA + D_MODEL
_C_KR = _C_GB + D_MODEL
_C_END = _C_KR + LANES

BF16 = jnp.bfloat16
F32 = jnp.float32


def _dot(a, b):
    return jnp.dot(a, b, preferred_element_type=F32)


def _dot_nt(a, b):
    return lax.dot_general(a, b, (((1,), (1,)), ((), ())), preferred_element_type=F32)


def _rms(x, g):
    r = lax.rsqrt(jnp.mean(x * x, axis=-1, keepdims=True) + EPS)
    return x * r * g


def _sigmoid(x):
    return 1.0 / (1.0 + jnp.exp(-x))


def _rope_lanes(x, cos, sin_signed):
    lane = lax.broadcasted_iota(jnp.int32, x.shape, 1)
    upper = (lane & (SWA_HEAD_DIM // 2)) != 0
    partner = jnp.where(upper, pltpu.roll(x, SWA_HEAD_DIM // 2, 1),
                        pltpu.roll(x, LANES - SWA_HEAD_DIM // 2, 1))
    return x * cos + partner * sin_signed


def _inproj_kernel(x_ref, g_ref, w_ref, cos_ref, sin_ref, gq_ref, wuq_ref, gkv_ref, wuk_ref,
                   wuvt_ref, qa_ref, ka_ref, va_ref, ga_ref, gb_ref, qm_ref, km_ref, vt_ref):
    xn = _rms(x_ref[...], g_ref[...]).astype(BF16)
    cos = cos_ref[...]
    sin = sin_ref[...]

    qa_scale = 1.0 / math.sqrt(SWA_HEAD_DIM)
    for j in range(SWA_HEADS // 2):
        q = _dot(xn, w_ref[:, _C_QA + j * LANES:_C_QA + (j + 1) * LANES])
        q = (_rope_lanes(q, cos, sin) * qa_scale).astype(BF16)
        qa_ref[0, 2 * j] = q[:, :SWA_HEAD_DIM]
        qa_ref[0, 2 * j + 1] = q[:, SWA_HEAD_DIM:]
    k = _rope_lanes(_dot(xn, w_ref[:, _C_KA:_C_VA]), cos, sin).astype(BF16)
    ka_ref[0, 0] = k[:, :SWA_HEAD_DIM]
    ka_ref[0, 1] = k[:, SWA_HEAD_DIM:]
    v = _dot(xn, w_ref[:, _C_VA:_C_CQ]).astype(BF16)
    va_ref[0, 0] = v[:, :SWA_HEAD_DIM]
    va_ref[0, 1] = v[:, SWA_HEAD_DIM:]

    ga_ref[...] = _sigmoid(_dot(xn, w_ref[:, _C_GA:_C_GB])).astype(BF16)
    gb_ref[...] = _sigmoid(_dot(xn, w_ref[:, _C_GB:_C_KR])).astype(BF16)

    cq = _rms(_dot(xn, w_ref[:, _C_CQ:_C_CKV]), gq_ref[...]).astype(BF16)
    ckv = _rms(_dot(xn, w_ref[:, _C_CKV:_C_GA]), gkv_ref[...]).astype(BF16)
    kr = _rope_lanes(_dot(xn, w_ref[:, _C_KR:_C_END]), cos, sin)[:, :MLA_ROPE_DIM].astype(BF16)
    qm_scale = LOG2_E / math.sqrt(MLA_QK_DIM)
    n_nope = MLA_HEADS * MLA_NOPE_DIM
    for h in range(MLA_HEADS):
        qn = _dot(cq, wuq_ref[:, h * MLA_NOPE_DIM:(h + 1) * MLA_NOPE_DIM]) * qm_scale
        qm_ref[0, h, :, :MLA_NOPE_DIM] = qn.astype(BF16)
        km_ref[0, h, :, :MLA_NOPE_DIM] = _dot(
            ckv, wuk_ref[:, h * MLA_NOPE_DIM:(h + 1) * MLA_NOPE_DIM]).astype(BF16)
        km_ref[0, h, :, MLA_NOPE_DIM:] = kr
        vt_ref[0, h, 0] = _dot_nt(wuvt_ref[h], ckv).astype(BF16)
    for j in range(MLA_HEADS // 2):
        qr = _dot(cq, wuq_ref[:, n_nope + j * LANES:n_nope + (j + 1) * LANES])
        qr = (_rope_lanes(qr, cos, sin) * qm_scale).astype(BF16)
        qm_ref[0, 2 * j, :, MLA_NOPE_DIM:] = qr[:, :MLA_ROPE_DIM]
        qm_ref[0, 2 * j + 1, :, MLA_NOPE_DIM:] = qr[:, MLA_ROPE_DIM:]


def _inproj(x2, g_mix, w_cat, cos_t, sin_t, g_q, w_uq, g_kv, w_uk, w_uvt, batch, seq, tm):
    tokens = batch * seq
    nt = seq // tm
    const = lambda t: (0, 0)
    tok = lambda t: (t, 0)
    head = lambda t: (t // nt, 0, t % nt, 0)
    out_shape = (
        jax.ShapeDtypeStruct((batch, SWA_HEADS, seq, SWA_HEAD_DIM), BF16),
        jax.ShapeDtypeStruct((batch, SWA_KV_HEADS, seq, SWA_HEAD_DIM), BF16),
        jax.ShapeDtypeStruct((batch, SWA_KV_HEADS, seq, SWA_HEAD_DIM), BF16),
        jax.ShapeDtypeStruct((tokens, D_MODEL), BF16),
        jax.ShapeDtypeStruct((tokens, D_MODEL), BF16),
        jax.ShapeDtypeStruct((batch, MLA_HEADS, seq, MLA_QK_DIM), BF16),
        jax.ShapeDtypeStruct((batch, MLA_HEADS, seq, MLA_QK_DIM), BF16),
        jax.ShapeDtypeStruct((batch, MLA_HEADS, nt, MLA_V_DIM, tm), BF16),
    )
    return pl.pallas_call(
        _inproj_kernel,
        grid=(tokens // tm,),
        in_specs=[
            pl.BlockSpec((tm, D_MODEL), tok),
            pl.BlockSpec((1, D_MODEL), const),
            pl.BlockSpec((D_MODEL, _C_END), const, pipeline_mode=pl.Buffered(1)),
            pl.BlockSpec((tm, LANES), lambda t: (t % nt, 0)),
            pl.BlockSpec((tm, LANES), lambda t: (t % nt, 0)),
            pl.BlockSpec((1, MLA_Q_RANK), const),
            pl.BlockSpec(w_uq.shape, const, pipeline_mode=pl.Buffered(1)),
            pl.BlockSpec((1, MLA_KV_RANK), const),
            pl.BlockSpec(w_uk.shape, const, pipeline_mode=pl.Buffered(1)),
            pl.BlockSpec(w_uvt.shape, lambda t: (0, 0, 0), pipeline_mode=pl.Buffered(1)),
        ],
        out_specs=(
            pl.BlockSpec((1, SWA_HEADS, tm, SWA_HEAD_DIM), head),
            pl.BlockSpec((1, SWA_KV_HEADS, tm, SWA_HEAD_DIM), head),
            pl.BlockSpec((1, SWA_KV_HEADS, tm, SWA_HEAD_DIM), head),
            pl.BlockSpec((tm, D_MODEL), tok),
            pl.BlockSpec((tm, D_MODEL), tok),
            pl.BlockSpec((1, MLA_HEADS, tm, MLA_QK_DIM), head),
            pl.BlockSpec((1, MLA_HEADS, tm, MLA_QK_DIM), head),
            pl.BlockSpec((1, MLA_HEADS, 1, MLA_V_DIM, tm), lambda t: (t // nt, 0, t % nt, 0, 0)),
        ),
        out_shape=out_shape,
        compiler_params=pltpu.CompilerParams(
            dimension_semantics=("arbitrary",), vmem_limit_bytes=VMEM_LIMIT),
        name="inproj",
    )(x2, g_mix, w_cat, cos_t, sin_t, g_q, w_uq, g_kv, w_uk, w_uvt)


def _swa_kernel(q_ref, k_ref, kp_ref, v_ref, vp_ref, sink_ref, o_ref, *, n_sub):
    first_tile = pl.program_id(1) == 0
    blk = SWA_WINDOW
    qi = lax.broadcasted_iota(jnp.int32, (blk, 2 * blk), 0)
    kj = lax.broadcasted_iota(jnp.int32, (blk, 2 * blk), 1)
    dist = blk + qi - kj
    band = (dist >= 0) & (dist < SWA_WINDOW)
    rows = SWA_GROUP * blk
    for g in range(SWA_KV_HEADS):
        sink = sink_ref[g * rows:(g + 1) * rows, :]
        for n in range(n_sub):
            q = q_ref[0, g * SWA_GROUP:(g + 1) * SWA_GROUP, n * blk:(n + 1) * blk, :]
            q = q.reshape(rows, SWA_HEAD_DIM)
            if n == 0:
                k_prev, v_prev = kp_ref[0, g], vp_ref[0, g]
            else:
                k_prev = k_ref[0, g, (n - 1) * blk:n * blk, :]
                v_prev = v_ref[0, g, (n - 1) * blk:n * blk, :]
            kk = jnp.concatenate([k_prev, k_ref[0, g, n * blk:(n + 1) * blk, :]], axis=0)
            vv = jnp.concatenate([v_prev, v_ref[0, g, n * blk:(n + 1) * blk, :]], axis=0)
            s = _dot_nt(q, kk)
            allowed = band
            if n == 0:
                allowed = band & ((kj >= blk) | jnp.logical_not(first_tile))
            s = jnp.where(allowed[None], s.reshape(SWA_GROUP, blk, 2 * blk), NEG_INF)
            s = s.reshape(rows, 2 * blk)
            m = jnp.maximum(jnp.max(s, axis=-1, keepdims=True), sink)
            e = jnp.exp(s - m)
            denom = jnp.sum(e, axis=-1, keepdims=True) + jnp.exp(sink - m)
            o = _dot(e.astype(BF16), vv) / denom
            for hh in range(SWA_GROUP):
                h = g * SWA_GROUP + hh
                o_ref[0, n * blk:(n + 1) * blk, h * SWA_HEAD_DIM:(h + 1) * SWA_HEAD_DIM] = (
                    o[hh * blk:(hh + 1) * blk, :].astype(BF16))


def _swa(q_a, k_a, v_a, sink_rows, batch, seq, tq):
    n_sub = tq // SWA_WINDOW
    cur = lambda b, i: (b, 0, i, 0)
    prev = lambda b, i: (b, 0, jnp.maximum(i * n_sub - 1, 0), 0)
    return pl.pallas_call(
        functools.partial(_swa_kernel, n_sub=n_sub),
        grid=(batch, seq // tq),
        in_specs=[
            pl.BlockSpec((1, SWA_HEADS, tq, SWA_HEAD_DIM), cur),
            pl.BlockSpec((1, SWA_KV_HEADS, tq, SWA_HEAD_DIM), cur),
            pl.BlockSpec((1, SWA_KV_HEADS, SWA_WINDOW, SWA_HEAD_DIM), prev),
            pl.BlockSpec((1, SWA_KV_HEADS, tq, SWA_HEAD_DIM), cur),
            pl.BlockSpec((1, SWA_KV_HEADS, SWA_WINDOW, SWA_HEAD_DIM), prev),
            pl.BlockSpec(sink_rows.shape, lambda b, i: (0, 0)),
        ],
        out_specs=pl.BlockSpec((1, tq, D_MODEL), lambda b, i: (b, i, 0)),
        out_shape=jax.ShapeDtypeStruct((batch, seq, D_MODEL), BF16),
        compiler_params=pltpu.CompilerParams(
            dimension_semantics=("arbitrary", "arbitrary"), vmem_limit_bytes=VMEM_LIMIT),
        name="swa",
    )(q_a, k_a, k_a, v_a, v_a, sink_rows)


def _mla_kernel(q_ref, k_ref, vt_ref, o_ref, m_scr, l_scr, acc_scr, s_scr, *, tq, tk):
    i = pl.program_id(2)
    chunks = tk // MLA_KV_CHUNK
    m_scr[...] = jnp.full(m_scr.shape, NEG_INF, F32)
    l_scr[...] = jnp.zeros(l_scr.shape, F32)
    acc_scr[...] = jnp.zeros(acc_scr.shape, F32)
    q = q_ref[0, 0]

    def scores(j, buf):
        k = k_ref[0, 0, pl.ds(pl.multiple_of(j * tk, tk), tk), :]
        s_scr[buf] = _dot_nt(k, q)

    def softmax_pv(j, buf, masked):
        s = s_scr[buf]
        if masked:
            kpos = lax.broadcasted_iota(jnp.int32, (tk, tq), 0)
            qpos = lax.broadcasted_iota(jnp.int32, (tk, tq), 1)
            s = jnp.where(kpos <= qpos, s, NEG_INF)
        m_old = m_scr[...]
        m_new = jnp.maximum(m_old, jnp.max(s, axis=0, keepdims=True))
        alpha = jnp.exp2(m_old - m_new)
        p = jnp.exp2(s - m_new)
        l_scr[...] = alpha * l_scr[...] + jnp.sum(p, axis=0, keepdims=True)
        pb = p.astype(BF16)
        pv = _dot(vt_ref[0, 0, j * chunks], pb[:MLA_KV_CHUNK])
        for c in range(1, chunks):
            pv += _dot(vt_ref[0, 0, j * chunks + c],
                       pb[c * MLA_KV_CHUNK:(c + 1) * MLA_KV_CHUNK])
        acc_scr[...] = alpha * acc_scr[...] + pv
        m_scr[...] = m_new

    scores(0, 0)

    def body(jj, carry):
        t = 2 * jj
        scores(t + 1, 1)
        softmax_pv(t, 0, False)
        scores(t + 2, 0)
        softmax_pv(t + 1, 1, False)
        return carry

    lax.fori_loop(0, i // 2, body, 0)

    @pl.when(i % 2 == 1)
    def _():
        scores(i, 1)
        softmax_pv(i - 1, 0, False)
        softmax_pv(i, 1, True)

    @pl.when(i % 2 == 0)
    def _():
        softmax_pv(i, 0, True)

    o = acc_scr[...] / l_scr[...]
    o_ref[0] = o.T.astype(o_ref.dtype)


def _mla(q_m, k_m, vt_m, batch, seq, tq):
    tk = tq
    return pl.pallas_call(
        functools.partial(_mla_kernel, tq=tq, tk=tk),
        grid=(batch, MLA_HEADS, seq // tq),
        in_specs=[
            pl.BlockSpec((1, 1, tq, MLA_QK_DIM), lambda b, h, i: (b, h, i, 0)),
            pl.BlockSpec((1, 1, seq, MLA_QK_DIM), lambda b, h, i: (b, h, 0, 0)),
            pl.BlockSpec((1, 1, seq // MLA_KV_CHUNK, MLA_V_DIM, MLA_KV_CHUNK),
                         lambda b, h, i: (b, h, 0, 0, 0)),
        ],
        out_specs=pl.BlockSpec((1, tq, MLA_V_DIM), lambda b, h, i: (b, i, h)),
        out_shape=jax.ShapeDtypeStruct((batch, seq, D_MODEL), BF16),
        scratch_shapes=[
            pltpu.VMEM((1, tq), F32),
            pltpu.VMEM((1, tq), F32),
            pltpu.VMEM((MLA_V_DIM, tq), F32),
            pltpu.VMEM((2, tk, tq), F32),
        ],
        compiler_params=pltpu.CompilerParams(
            dimension_semantics=("arbitrary", "arbitrary", "arbitrary"),
            vmem_limit_bytes=VMEM_LIMIT),
        name="mla",
    )(q_m, k_m, vt_m)


def _outproj_kernel(x_ref, oa_ref, ob_ref, ga_ref, gb_ref, wo_ref, gf_ref, wr_ref, br_ref,
                    h_ref, xn_ref, comb_ref):
    merged = (ga_ref[...].astype(F32) * oa_ref[...].astype(F32)
              + gb_ref[...].astype(F32) * ob_ref[...].astype(F32)).astype(BF16)
    h = x_ref[...] + _dot(merged, wo_ref[...])
    h_ref[...] = h
    xn = _rms(h, gf_ref[...])
    xn_ref[...] = xn.astype(BF16)

    x_hi = xn.astype(BF16)
    x_lo = (xn - x_hi.astype(F32)).astype(BF16)
    lg = (_dot(x_hi, wr_ref[0]) + _dot(x_lo, wr_ref[0]) + _dot(x_hi, wr_ref[1])) + br_ref[...]

    lane = lax.broadcasted_iota(jnp.int32, lg.shape, 1)
    is_group = (lane >= N_EXPERTS) & (lane < N_EXPERTS + N_GROUPS)
    gl = jnp.where(is_group, lg, NEG_INF)
    g_max = jnp.max(gl, axis=-1, keepdims=True)
    g_sum = jnp.sum(jnp.where(is_group, jnp.exp(gl - g_max), 0.0), axis=-1, keepdims=True)
    g_w = 1.0 / g_sum
    big = jnp.int32(LANES)
    g_lane = jnp.min(jnp.where(is_group & (gl == g_max), lane, big), axis=-1, keepdims=True)
    gidx = g_lane - N_EXPERTS

    sel = (lane >= gidx * EXPERTS_PER_GROUP) & (lane < (gidx + 1) * EXPERTS_PER_GROUP)
    el = jnp.where(sel, lg, NEG_INF)
    e_max = jnp.max(el, axis=-1, keepdims=True)
    e_exp = jnp.where(sel, jnp.exp(el - e_max), 0.0)
    e_prob = e_exp / jnp.sum(e_exp, axis=-1, keepdims=True)
    p1 = jnp.max(e_prob, axis=-1, keepdims=True)
    i1 = jnp.min(jnp.where(sel & (e_prob == p1), lane, big), axis=-1, keepdims=True)
    rest = sel & (lane != i1)
    p2 = jnp.max(jnp.where(rest, e_prob, -1.0), axis=-1, keepdims=True)
    i2 = jnp.min(jnp.where(rest & (e_prob == p2), lane, big), axis=-1, keepdims=True)
    top_sum = p1 + p2
    comb = jnp.where(lane == i1, p1 / top_sum, jnp.where(lane == i2, p2 / top_sum, 0.0))
    comb_ref[...] = comb * g_w


def _outproj(x2, o_a, o_b, g_a, g_b, w_o, g_ffn, w_r, b_r, tm):
    tokens = x2.shape[0]
    tok = lambda t: (t, 0)
    const = lambda t: (0, 0)
    return pl.pallas_call(
        _outproj_kernel,
        grid=(tokens // tm,),
        in_specs=[
            pl.BlockSpec((tm, D_MODEL), tok),
            pl.BlockSpec((tm, D_MODEL), tok),
            pl.BlockSpec((tm, D_MODEL), tok),
            pl.BlockSpec((tm, D_MODEL), tok),
            pl.BlockSpec((tm, D_MODEL), tok),
            pl.BlockSpec((D_MODEL, D_MODEL), const),
            pl.BlockSpec((1, D_MODEL), const),
            pl.BlockSpec((2, D_MODEL, LANES), lambda t: (0, 0, 0)),
            pl.BlockSpec((1, LANES), const),
        ],
        out_specs=(
            pl.BlockSpec((tm, D_MODEL), tok),
            pl.BlockSpec((tm, D_MODEL), tok),
            pl.BlockSpec((tm, LANES), tok),
        ),
        out_shape=(
            jax.ShapeDtypeStruct((tokens, D_MODEL), F32),
            jax.ShapeDtypeStruct((tokens, D_MODEL), BF16),
            jax.ShapeDtypeStruct((tokens, LANES), F32),
        ),
        compiler_params=pltpu.CompilerParams(
            dimension_semantics=("arbitrary",), vmem_limit_bytes=VMEM_LIMIT),
        name="outproj",
    )(x2, o_a, o_b, g_a, g_b, w_o, g_ffn, w_r, b_r)


def _moe_kernel(h_ref, xn_ref, comb_ref, wi_ref, wo_ref, o_ref):
    e = pl.program_id(1)

    @pl.when(e == 0)
    def _():
        o_ref[...] = h_ref[...]

    comb = comb_ref[...]
    lane = lax.broadcasted_iota(jnp.int32, comb.shape, 1)
    c = jnp.sum(jnp.where(lane == e, comb, 0.0), axis=-1, keepdims=True)
    hid = _dot(xn_ref[...], wi_ref[0])
    gate = hid[:, :D_EXPERT]
    up = hid[:, D_EXPERT:]
    act = (gate * _sigmoid(gate)) * up * c
    o_ref[...] += _dot(act.astype(BF16), wo_ref[0])


def _moe(h1, xn, comb, w_ei, w_eo, tm):
    tokens = h1.shape[0]
    tok = lambda t, e: (t, 0)
    return pl.pallas_call(
        _moe_kernel,
        grid=(tokens // tm, N_EXPERTS),
        in_specs=[
            pl.BlockSpec((tm, D_MODEL), tok),
            pl.BlockSpec((tm, D_MODEL), tok),
            pl.BlockSpec((tm, LANES), tok),
            pl.BlockSpec((1, D_MODEL, 2 * D_EXPERT), lambda t, e: (e, 0, 0)),
            pl.BlockSpec((1, D_EXPERT, D_MODEL), lambda t, e: (e, 0, 0)),
        ],
        out_specs=pl.BlockSpec((tm, D_MODEL), tok),
        out_shape=jax.ShapeDtypeStruct((tokens, D_MODEL), F32),
        compiler_params=pltpu.CompilerParams(
            dimension_semantics=("arbitrary", "arbitrary"), vmem_limit_bytes=VMEM_LIMIT),
        name="moe",
    )(h1, xn, comb, w_ei, w_eo)


def _ple_kernel(h_ref, p_ref, gp_ref, wg_ref, wp_ref, gfin_ref, o_ref):
    h = h_ref[...]
    gate = _sigmoid(_dot(_rms(h, gp_ref[...]).astype(BF16), wg_ref[...]))
    h = h + gate * _dot(p_ref[...].astype(BF16), wp_ref[...])
    o_ref[...] = _rms(h, gfin_ref[...])


def _ple(h2, p2, g_ple, w_pg, w_pp, g_final, tm):
    tokens = h2.shape[0]
    tok = lambda t: (t, 0)
    const = lambda t: (0, 0)
    return pl.pallas_call(
        _ple_kernel,
        grid=(tokens // tm,),
        in_specs=[
            pl.BlockSpec((tm, D_MODEL), tok),
            pl.BlockSpec((tm, PLE_DIM), tok),
            pl.BlockSpec((1, D_MODEL), const),
            pl.BlockSpec((D_MODEL, D_MODEL), const),
            pl.BlockSpec((PLE_DIM, D_MODEL), const),
            pl.BlockSpec((1, D_MODEL), const),
        ],
        out_specs=pl.BlockSpec((tm, D_MODEL), tok),
        out_shape=jax.ShapeDtypeStruct((tokens, D_MODEL), F32),
        compiler_params=pltpu.CompilerParams(
            dimension_semantics=("arbitrary",), vmem_limit_bytes=VMEM_LIMIT),
        name="ple",
    )(h2, p2, g_ple, w_pg, w_pp, g_final)


def _rope_tables(seq):
    half = SWA_HEAD_DIM // 2
    pos = jnp.arange(seq, dtype=F32)
    inv = ROPE_THETA ** (-jnp.arange(0, SWA_HEAD_DIM, 2, dtype=F32) / SWA_HEAD_DIM)
    ang = pos[:, None] * inv[None, :]
    cos, sin = jnp.cos(ang), jnp.sin(ang)
    reps = LANES // half
    cos_t = jnp.tile(cos, (1, reps))
    sin_t = jnp.tile(jnp.concatenate([-sin, sin], axis=1), (1, reps // 2))
    return cos_t, sin_t


def _layer(h2d, p2d, g_mix, w_in, sinks, g_q, w_uq, g_kv, w_ukv, w_out, g_ffn, w_rg, b_rg,
           w_re, b_re, w_ei, w_eo, g_ple, w_pg, w_pp, g_final, cos_t, sin_t, batch, seq):
    assert SWA_HEAD_DIM == MLA_ROPE_DIM, "one rotary table serves both mixers"
    sizes = (SWA_HEADS * SWA_HEAD_DIM, SWA_KV_HEADS * SWA_HEAD_DIM, SWA_KV_HEADS * SWA_HEAD_DIM,
             MLA_Q_RANK, MLA_KV_RANK, MLA_ROPE_DIM, D_MODEL, D_MODEL)
    offs = [0]
    for s in sizes:
        offs.append(offs[-1] + s)
    seg = [w_in[:, offs[i]:offs[i + 1]] for i in range(len(sizes))]
    w_cat = jnp.concatenate(
        seg[:5] + [seg[6], seg[7], seg[5], jnp.zeros((D_MODEL, LANES - MLA_ROPE_DIM), w_in.dtype)],
        axis=1).astype(BF16)
    uq = w_uq.reshape(MLA_Q_RANK, MLA_HEADS, MLA_QK_DIM)
    w_uq_c = jnp.concatenate(
        [uq[:, :, :MLA_NOPE_DIM].reshape(MLA_Q_RANK, -1),
         uq[:, :, MLA_NOPE_DIM:].reshape(MLA_Q_RANK, -1)], axis=1).astype(BF16)
    ukv = w_ukv.reshape(MLA_KV_RANK, MLA_HEADS, MLA_NOPE_DIM + MLA_V_DIM)
    w_uk_c = ukv[:, :, :MLA_NOPE_DIM].reshape(MLA_KV_RANK, -1).astype(BF16)
    w_uvt = jnp.transpose(ukv[:, :, MLA_NOPE_DIM:], (1, 2, 0)).astype(BF16)

    q_a, k_a, v_a, g_a, g_b, q_m, k_m, v_m = _inproj(
        h2d, g_mix[None], w_cat, cos_t, sin_t, g_q[None], w_uq_c, g_kv[None], w_uk_c, w_uvt,
        batch, seq, tm=MLA_KV_CHUNK)

    sink_rows = jnp.repeat(sinks.astype(F32), SWA_WINDOW)[:, None]
    o_a = _swa(q_a, k_a, v_a, sink_rows, batch, seq, tq=512)
    o_b = _mla(q_m, k_m, v_m, batch, seq, tq=512)

    w_r = jnp.concatenate(
        [w_re, w_rg, jnp.zeros((D_MODEL, LANES - N_EXPERTS - N_GROUPS), F32)], axis=1)
    w_r_hi = w_r.astype(BF16)
    w_r_lo = (w_r - w_r_hi.astype(F32)).astype(BF16)
    w_r2 = jnp.stack([w_r_hi, w_r_lo])
    b_r = jnp.concatenate(
        [b_re, b_rg, jnp.zeros((LANES - N_EXPERTS - N_GROUPS,), F32)])[None]
    tokens = batch * seq
    h1, xn2, comb = _outproj(
        h2d, o_a.reshape(tokens, D_MODEL), o_b.reshape(tokens, D_MODEL), g_a, g_b,
        w_out.astype(BF16), g_ffn[None], w_r2, b_r, tm=512)

    h2 = _moe(h1, xn2, comb, w_ei.astype(BF16), w_eo.astype(BF16), tm=1024)
    return _ple(h2, p2d, g_ple[None], w_pg.astype(BF16), w_pp.astype(BF16), g_final[None], tm=512)


def kernel(x, p, g_mix, w_in, swa_sinks, mla_g_q, mla_w_uq, mla_g_kv, mla_w_ukv, w_out, g_ffn,
           w_router_group, b_router_group, w_router_expert, b_router_expert, w_expert_in,
           w_expert_out, g_ple, w_ple_gate, w_ple_proj, g_final):
    batch, seq, d = x.shape
    depth = p.shape[0]
    assert d == D_MODEL and depth == 1, "final RMSNorm is fused into the single layer"
    cos_t, sin_t = _rope_tables(seq)
    out = _layer(
        x.reshape(batch * seq, d), p[0].reshape(batch * seq, PLE_DIM), g_mix[0], w_in[0],
        swa_sinks[0], mla_g_q[0], mla_w_uq[0], mla_g_kv[0], mla_w_ukv[0], w_out[0], g_ffn[0],
        w_router_group[0], b_router_group[0], w_router_expert[0], b_router_expert[0],
        w_expert_in[0], w_expert_out[0], g_ple[0], w_ple_gate[0], w_ple_proj[0], g_final,
        cos_t, sin_t, batch, seq)
    return out.reshape(batch, seq, d)
```

```python
import functools
import math

import jax
import jax.numpy as jnp
from jax import lax
from jax.experimental import pallas as pl
from jax.experimental.pallas import tpu as pltpu

D_MODEL = 1024
PLE_DIM = 256
ROPE_THETA = 10000.0
EPS = 1e-6
NEG_INF = -1e30

SWA_HEAD_DIM = 64
SWA_HEADS = D_MODEL // SWA_HEAD_DIM
SWA_KV_HEADS = SWA_HEADS // 8
SWA_GROUP = SWA_HEADS // SWA_KV_HEADS
SWA_WINDOW = 128

MLA_NOPE_DIM = 128
MLA_ROPE_DIM = 64
MLA_V_DIM = 128
MLA_HEADS = D_MODEL // MLA_V_DIM
MLA_Q_RANK = 256
MLA_KV_RANK = 128
MLA_QK_DIM = MLA_NOPE_DIM + MLA_ROPE_DIM

N_GROUPS = 4
EXPERTS_PER_GROUP = 4
N_EXPERTS = N_GROUPS * EXPERTS_PER_GROUP
D_EXPERT = 256

LANES = 128
LOG2_E = 1.4426950408889634
MLA_KV_CHUNK = 256
VMEM_LIMIT = 56 * 1024 * 1024

_C_QA = 0
_C_KA = _C_QA + SWA_HEADS * SWA_HEAD_DIM
_C_VA = _C_KA + SWA_KV_HEADS * SWA_HEAD_DIM
_C_CQ = _C_VA + SWA_KV_HEADS * SWA_HEAD_DIM
_C_CKV = _C_CQ + MLA_Q_RANK
_C_GA = _C_CKV + MLA_KV_RANK
_C_GB = _C_GA + D_MODEL
_C_KR = _C_GB + D_MODEL
_C_END = _C_KR + LANES

BF16 = jnp.bfloat16
F32 = jnp.float32


def _dot(a, b):
    return jnp.dot(a, b, preferred_element_type=F32)


def _dot_nt(a, b):
    return lax.dot_general(a, b, (((1,), (1,)), ((), ())), preferred_element_type=F32)


def _rms(x, g):
    r = lax.rsqrt(jnp.mean(x * x, axis=-1, keepdims=True) + EPS)
    return x * r * g


def _sigmoid(x):
    return 1.0 / (1.0 + jnp.exp(-x))


def _rope_lanes(x, cos, sin_signed):
    lane = lax.broadcasted_iota(jnp.int32, x.shape, 1)
    upper = (lane & (SWA_HEAD_DIM // 2)) != 0
    partner = jnp.where(upper, pltpu.roll(x, SWA_HEAD_DIM // 2, 1),
                        pltpu.roll(x, LANES - SWA_HEAD_DIM // 2, 1))
    return x * cos + partner * sin_signed


def _inproj_kernel(x_ref, g_ref, w_ref, cos_ref, sin_ref, gq_ref, wuq_ref, gkv_ref, wuk_ref,
                   wuvt_ref, qa_ref, ka_ref, va_ref, ga_ref, gb_ref, qm_ref, km_ref, vt_ref):
    xn = _rms(x_ref[...], g_ref[...]).astype(BF16)
    cos = cos_ref[...]
    sin = sin_ref[...]

    qa_scale = 1.0 / math.sqrt(SWA_HEAD_DIM)
    for j in range(SWA_HEADS // 2):
        q = _dot(xn, w_ref[:, _C_QA + j * LANES:_C_QA + (j + 1) * LANES])
        q = (_rope_lanes(q, cos, sin) * qa_scale).astype(BF16)
        qa_ref[0, 2 * j] = q[:, :SWA_HEAD_DIM]
        qa_ref[0, 2 * j + 1] = q[:, SWA_HEAD_DIM:]
    k = _rope_lanes(_dot(xn, w_ref[:, _C_KA:_C_VA]), cos, sin).astype(BF16)
    ka_ref[0, 0] = k[:, :SWA_HEAD_DIM]
    ka_ref[0, 1] = k[:, SWA_HEAD_DIM:]
    v = _dot(xn, w_ref[:, _C_VA:_C_CQ]).astype(BF16)
    va_ref[0, 0] = v[:, :SWA_HEAD_DIM]
    va_ref[0, 1] = v[:, SWA_HEAD_DIM:]

    ga_ref[...] = _sigmoid(_dot(xn, w_ref[:, _C_GA:_C_GB])).astype(BF16)
    gb_ref[...] = _sigmoid(_dot(xn, w_ref[:, _C_GB:_C_KR])).astype(BF16)

    cq = _rms(_dot(xn, w_ref[:, _C_CQ:_C_CKV]), gq_ref[...]).astype(BF16)
    ckv = _rms(_dot(xn, w_ref[:, _C_CKV:_C_GA]), gkv_ref[...]).astype(BF16)
    kr = _rope_lanes(_dot(xn, w_ref[:, _C_KR:_C_END]), cos, sin)[:, :MLA_ROPE_DIM].astype(BF16)
    qm_scale = LOG2_E / math.sqrt(MLA_QK_DIM)
    n_nope = MLA_HEADS * MLA_NOPE_DIM
    for h in range(MLA_HEADS):
        qn = _dot(cq, wuq_ref[:, h * MLA_NOPE_DIM:(h + 1) * MLA_NOPE_DIM]) * qm_scale
        qm_ref[0, h, :, :MLA_NOPE_DIM] = qn.astype(BF16)
        km_ref[0, h, :, :MLA_NOPE_DIM] = _dot(
            ckv, wuk_ref[:, h * MLA_NOPE_DIM:(h + 1) * MLA_NOPE_DIM]).astype(BF16)
        km_ref[0, h, :, MLA_NOPE_DIM:] = kr
        vt_ref[0, h, 0] = _dot_nt(wuvt_ref[h], ckv).astype(BF16)
    for j in range(MLA_HEADS // 2):
        qr = _dot(cq, wuq_ref[:, n_nope + j * LANES:n_nope + (j + 1) * LANES])
        qr = (_rope_lanes(qr, cos, sin) * qm_scale).astype(BF16)
        qm_ref[0, 2 * j, :, MLA_NOPE_DIM:] = qr[:, :MLA_ROPE_DIM]
        qm_ref[0, 2 * j + 1, :, MLA_NOPE_DIM:] = qr[:, MLA_ROPE_DIM:]


def _inproj(x2, g_mix, w_cat, cos_t, sin_t, g_q, w_uq, g_kv, w_uk, w_uvt, batch, seq, tm):
    tokens = batch * seq
    nt = seq // tm
    const = lambda t: (0, 0)
    tok = lambda t: (t, 0)
    head = lambda t: (t // nt, 0, t % nt, 0)
    out_shape = (
        jax.ShapeDtypeStruct((batch, SWA_HEADS, seq, SWA_HEAD_DIM), BF16),
        jax.ShapeDtypeStruct((batch, SWA_KV_HEADS, seq, SWA_HEAD_DIM), BF16),
        jax.ShapeDtypeStruct((batch, SWA_KV_HEADS, seq, SWA_HEAD_DIM), BF16),
        jax.ShapeDtypeStruct((tokens, D_MODEL), BF16),
        jax.ShapeDtypeStruct((tokens, D_MODEL), BF16),
        jax.ShapeDtypeStruct((batch, MLA_HEADS, seq, MLA_QK_DIM), BF16),
        jax.ShapeDtypeStruct((batch, MLA_HEADS, seq, MLA_QK_DIM), BF16),
        jax.ShapeDtypeStruct((batch, MLA_HEADS, nt, MLA_V_DIM, tm), BF16),
    )
    return pl.pallas_call(
        _inproj_kernel,
        grid=(tokens // tm,),
        in_specs=[
            pl.BlockSpec((tm, D_MODEL), tok),
            pl.BlockSpec((1, D_MODEL), const),
            pl.BlockSpec((D_MODEL, _C_END), const, pipeline_mode=pl.Buffered(1)),
            pl.BlockSpec((tm, LANES), lambda t: (t % nt, 0)),
            pl.BlockSpec((tm, LANES), lambda t: (t % nt, 0)),
            pl.BlockSpec((1, MLA_Q_RANK), const),
            pl.BlockSpec(w_uq.shape, const, pipeline_mode=pl.Buffered(1)),
            pl.BlockSpec((1, MLA_KV_RANK), const),
            pl.BlockSpec(w_uk.shape, const, pipeline_mode=pl.Buffered(1)),
            pl.BlockSpec(w_uvt.shape, lambda t: (0, 0, 0), pipeline_mode=pl.Buffered(1)),
        ],
        out_specs=(
            pl.BlockSpec((1, SWA_HEADS, tm, SWA_HEAD_DIM), head),
            pl.BlockSpec((1, SWA_KV_HEADS, tm, SWA_HEAD_DIM), head),
            pl.BlockSpec((1, SWA_KV_HEADS, tm, SWA_HEAD_DIM), head),
            pl.BlockSpec((tm, D_MODEL), tok),
            pl.BlockSpec((tm, D_MODEL), tok),
            pl.BlockSpec((1, MLA_HEADS, tm, MLA_QK_DIM), head),
            pl.BlockSpec((1, MLA_HEADS, tm, MLA_QK_DIM), head),
            pl.BlockSpec((1, MLA_HEADS, 1, MLA_V_DIM, tm), lambda t: (t // nt, 0, t % nt, 0, 0)),
        ),
        out_shape=out_shape,
        compiler_params=pltpu.CompilerParams(
            dimension_semantics=("arbitrary",), vmem_limit_bytes=VMEM_LIMIT),
        name="inproj",
    )(x2, g_mix, w_cat, cos_t, sin_t, g_q, w_uq, g_kv, w_uk, w_uvt)


def _swa_kernel(q_ref, k_ref, kp_ref, v_ref, vp_ref, sink_ref, o_ref, *, n_sub):
    first_tile = pl.program_id(1) == 0
    blk = SWA_WINDOW
    qi = lax.broadcasted_iota(jnp.int32, (blk, 2 * blk), 0)
    kj = lax.broadcasted_iota(jnp.int32, (blk, 2 * blk), 1)
    dist = blk + qi - kj
    band = (dist >= 0) & (dist < SWA_WINDOW)
    rows = SWA_GROUP * blk
    for g in range(SWA_KV_HEADS):
        sink = sink_ref[g * rows:(g + 1) * rows, :]
        for n in range(n_sub):
            q = q_ref[0, g * SWA_GROUP:(g + 1) * SWA_GROUP, n * blk:(n + 1) * blk, :]
            q = q.reshape(rows, SWA_HEAD_DIM)
            if n == 0:
                k_prev, v_prev = kp_ref[0, g], vp_ref[0, g]
            else:
                k_prev = k_ref[0, g, (n - 1) * blk:n * blk, :]
                v_prev = v_ref[0, g, (n - 1) * blk:n * blk, :]
            kk = jnp.concatenate([k_prev, k_ref[0, g, n * blk:(n + 1) * blk, :]], axis=0)
            vv = jnp.concatenate([v_prev, v_ref[0, g, n * blk:(n + 1) * blk, :]], axis=0)
            s = _dot_nt(q, kk)
            allowed = band
            if n == 0:
                allowed = band & ((kj >= blk) | jnp.logical_not(first_tile))
            s = jnp.where(allowed[None], s.reshape(SWA_GROUP, blk, 2 * blk), NEG_INF)
            s = s.reshape(rows, 2 * blk)
            m = jnp.maximum(jnp.max(s, axis=-1, keepdims=True), sink)
            e = jnp.exp(s - m)
            denom = jnp.sum(e, axis=-1, keepdims=True) + jnp.exp(sink - m)
            o = _dot(e.astype(BF16), vv) / denom
            for hh in range(SWA_GROUP):
                h = g * SWA_GROUP + hh
                o_ref[0, n * blk:(n + 1) * blk, h * SWA_HEAD_DIM:(h + 1) * SWA_HEAD_DIM] = (
                    o[hh * blk:(hh + 1) * blk, :].astype(BF16))


def _swa(q_a, k_a, v_a, sink_rows, batch, seq, tq):
    n_sub = tq // SWA_WINDOW
    cur = lambda b, i: (b, 0, i, 0)
    prev = lambda b, i: (b, 0, jnp.maximum(i * n_sub - 1, 0), 0)
    return pl.pallas_call(
        functools.partial(_swa_kernel, n_sub=n_sub),
        grid=(batch, seq // tq),
        in_specs=[
            pl.BlockSpec((1, SWA_HEADS, tq, SWA_HEAD_DIM), cur),
            pl.BlockSpec((1, SWA_KV_HEADS, tq, SWA_HEAD_DIM), cur),
            pl.BlockSpec((1, SWA_KV_HEADS, SWA_WINDOW, SWA_HEAD_DIM), prev),
            pl.BlockSpec((1, SWA_KV_HEADS, tq, SWA_HEAD_DIM), cur),
            pl.BlockSpec((1, SWA_KV_HEADS, SWA_WINDOW, SWA_HEAD_DIM), prev),
            pl.BlockSpec(sink_rows.shape, lambda b, i: (0, 0)),
        ],
        out_specs=pl.BlockSpec((1, tq, D_MODEL), lambda b, i: (b, i, 0)),
        out_shape=jax.ShapeDtypeStruct((batch, seq, D_MODEL), BF16),
        compiler_params=pltpu.CompilerParams(
            dimension_semantics=("arbitrary", "arbitrary"), vmem_limit_bytes=VMEM_LIMIT),
        name="swa",
    )(q_a, k_a, k_a, v_a, v_a, sink_rows)


def _mla_kernel(q_ref, k_ref, vt_ref, o_ref, m_scr, l_scr, acc_scr, s_scr, *, tq, tk):
    assert tq == 2 * tk
    chunks = tk // MLA_KV_CHUNK
    n_q = q_ref.shape[2] // tq

    def q_block(i, carry):
        m_scr[...] = jnp.full(m_scr.shape, NEG_INF, F32)
        l_scr[...] = jnp.zeros(l_scr.shape, F32)
        acc_scr[...] = jnp.zeros(acc_scr.shape, F32)

        def scores(j, buf, q0=0):
            k = k_ref[0, 0, pl.ds(pl.multiple_of(j * tk, tk), tk), :]
            q = q_ref[0, 0, pl.ds(pl.multiple_of(i * tq + q0, tk), tq - q0), :]
            s_scr[buf, :, q0:] = _dot_nt(k, q)

        def softmax_pv(j, buf, q0=0, key_offset=None):
            s = s_scr[buf, :, q0:]
            if key_offset is not None:
                kpos = lax.broadcasted_iota(jnp.int32, s.shape, 0) + key_offset
                qpos = lax.broadcasted_iota(jnp.int32, s.shape, 1) + q0
                s = jnp.where(kpos <= qpos, s, NEG_INF)
            m_old = m_scr[:, q0:]
            m_new = jnp.maximum(m_old, jnp.max(s, axis=0, keepdims=True))
            alpha = jnp.exp2(m_old - m_new)
            p = jnp.exp2(s - m_new)
            l_scr[:, q0:] = alpha * l_scr[:, q0:] + jnp.sum(p, axis=0, keepdims=True)
            pb = p.astype(BF16)
            pv = _dot(vt_ref[0, 0, j * chunks], pb[:MLA_KV_CHUNK])
            for c in range(1, chunks):
                pv += _dot(vt_ref[0, 0, j * chunks + c],
                           pb[c * MLA_KV_CHUNK:(c + 1) * MLA_KV_CHUNK])
            acc_scr[:, q0:] = alpha * acc_scr[:, q0:] + pv
            m_scr[:, q0:] = m_new

        scores(0, 0)

        def pair(jj, c):
            t = 2 * jj
            scores(t + 1, 1)
            softmax_pv(t, 0)
            scores(t + 2, 0)
            softmax_pv(t + 1, 1)
            return c

        lax.fori_loop(0, i, pair, 0)
        scores(2 * i + 1, 1, q0=tk)
        softmax_pv(2 * i, 0, key_offset=0)
        softmax_pv(2 * i + 1, 1, q0=tk, key_offset=tk)

        o = acc_scr[...] / l_scr[...]
        o_ref[0, pl.ds(pl.multiple_of(i * tq, tq), tq), :] = o.T.astype(o_ref.dtype)
        return carry

    lax.fori_loop(0, n_q, q_block, 0)


def _mla(q_m, k_m, vt_m, batch, seq, tq, tk):
    return pl.pallas_call(
        functools.partial(_mla_kernel, tq=tq, tk=tk),
        grid=(batch, MLA_HEADS),
        in_specs=[
            pl.BlockSpec((1, 1, seq, MLA_QK_DIM), lambda b, h: (b, h, 0, 0)),
            pl.BlockSpec((1, 1, seq, MLA_QK_DIM), lambda b, h: (b, h, 0, 0)),
            pl.BlockSpec((1, 1, seq // MLA_KV_CHUNK, MLA_V_DIM, MLA_KV_CHUNK),
                         lambda b, h: (b, h, 0, 0, 0)),
        ],
        out_specs=pl.BlockSpec((1, seq, MLA_V_DIM), lambda b, h: (b, 0, h)),
        out_shape=jax.ShapeDtypeStruct((batch, seq, D_MODEL), BF16),
        scratch_shapes=[
            pltpu.VMEM((1, tq), F32),
            pltpu.VMEM((1, tq), F32),
            pltpu.VMEM((MLA_V_DIM, tq), F32),
            pltpu.VMEM((2, tk, tq), F32),
        ],
        compiler_params=pltpu.CompilerParams(
            dimension_semantics=("arbitrary", "arbitrary"), vmem_limit_bytes=VMEM_LIMIT),
        name="mla",
    )(q_m, k_m, vt_m)


def _outproj_kernel(x_ref, oa_ref, ob_ref, ga_ref, gb_ref, wo_ref, gf_ref, wr_ref, br_ref,
                    h_ref, xn_ref, comb_ref):
    merged = (ga_ref[...].astype(F32) * oa_ref[...].astype(F32)
              + gb_ref[...].astype(F32) * ob_ref[...].astype(F32)).astype(BF16)
    h = x_ref[...] + _dot(merged, wo_ref[...])
    h_ref[...] = h
    xn = _rms(h, gf_ref[...])
    xn_ref[...] = xn.astype(BF16)

    x_hi = xn.astype(BF16)
    x_lo = (xn - x_hi.astype(F32)).astype(BF16)
    lg = (_dot(x_hi, wr_ref[0]) + _dot(x_lo, wr_ref[0]) + _dot(x_hi, wr_ref[1])) + br_ref[...]

    lane = lax.broadcasted_iota(jnp.int32, lg.shape, 1)
    is_group = (lane >= N_EXPERTS) & (lane < N_EXPERTS + N_GROUPS)
    gl = jnp.where(is_group, lg, NEG_INF)
    g_max = jnp.max(gl, axis=-1, keepdims=True)
    g_sum = jnp.sum(jnp.where(is_group, jnp.exp(gl - g_max), 0.0), axis=-1, keepdims=True)
    g_w = 1.0 / g_sum
    big = jnp.int32(LANES)
    g_lane = jnp.min(jnp.where(is_group & (gl == g_max), lane, big), axis=-1, keepdims=True)
    gidx = g_lane - N_EXPERTS

    sel = (lane >= gidx * EXPERTS_PER_GROUP) & (lane < (gidx + 1) * EXPERTS_PER_GROUP)
    el = jnp.where(sel, lg, NEG_INF)
    e_max = jnp.max(el, axis=-1, keepdims=True)
    e_exp = jnp.where(sel, jnp.exp(el - e_max), 0.0)
    e_prob = e_exp / jnp.sum(e_exp, axis=-1, keepdims=True)
    p1 = jnp.max(e_prob, axis=-1, keepdims=True)
    i1 = jnp.min(jnp.where(sel & (e_prob == p1), lane, big), axis=-1, keepdims=True)
    rest = sel & (lane != i1)
    p2 = jnp.max(jnp.where(rest, e_prob, -1.0), axis=-1, keepdims=True)
    i2 = jnp.min(jnp.where(rest & (e_prob == p2), lane, big), axis=-1, keepdims=True)
    top_sum = p1 + p2
    comb = jnp.where(lane == i1, p1 / top_sum, jnp.where(lane == i2, p2 / top_sum, 0.0))
    comb_ref[...] = comb * g_w


def _outproj(x2, o_a, o_b, g_a, g_b, w_o, g_ffn, w_r, b_r, tm):
    tokens = x2.shape[0]
    tok = lambda t: (t, 0)
    const = lambda t: (0, 0)
    return pl.pallas_call(
        _outproj_kernel,
        grid=(tokens // tm,),
        in_specs=[
            pl.BlockSpec((tm, D_MODEL), tok),
            pl.BlockSpec((tm, D_MODEL), tok),
            pl.BlockSpec((tm, D_MODEL), tok),
            pl.BlockSpec((tm, D_MODEL), tok),
            pl.BlockSpec((tm, D_MODEL), tok),
            pl.BlockSpec((D_MODEL, D_MODEL), const),
            pl.BlockSpec((1, D_MODEL), const),
            pl.BlockSpec((2, D_MODEL, LANES), lambda t: (0, 0, 0)),
            pl.BlockSpec((1, LANES), const),
        ],
        out_specs=(
            pl.BlockSpec((tm, D_MODEL), tok),
            pl.BlockSpec((tm, D_MODEL), tok),
            pl.BlockSpec((tm, LANES), tok),
        ),
        out_shape=(
            jax.ShapeDtypeStruct((tokens, D_MODEL), F32),
            jax.ShapeDtypeStruct((tokens, D_MODEL), BF16),
            jax.ShapeDtypeStruct((tokens, LANES), F32),
        ),
        compiler_params=pltpu.CompilerParams(
            dimension_semantics=("arbitrary",), vmem_limit_bytes=VMEM_LIMIT),
        name="outproj",
    )(x2, o_a, o_b, g_a, g_b, w_o, g_ffn, w_r, b_r)


def _moe_kernel(h_ref, xn_ref, comb_ref, wi_ref, wo_ref, o_ref):
    e = pl.program_id(1)

    @pl.when(e == 0)
    def _():
        o_ref[...] = h_ref[...]

    comb = comb_ref[...]
    lane = lax.broadcasted_iota(jnp.int32, comb.shape, 1)
    c = jnp.sum(jnp.where(lane == e, comb, 0.0), axis=-1, keepdims=True)
    hid = _dot(xn_ref[...], wi_ref[0])
    gate = hid[:, :D_EXPERT]
    up = hid[:, D_EXPERT:]
    act = (gate * _sigmoid(gate)) * up * c
    o_ref[...] += _dot(act.astype(BF16), wo_ref[0])


def _moe(h1, xn, comb, w_ei, w_eo, tm):
    tokens = h1.shape[0]
    tok = lambda t, e: (t, 0)
    return pl.pallas_call(
        _moe_kernel,
        grid=(tokens // tm, N_EXPERTS),
        in_specs=[
            pl.BlockSpec((tm, D_MODEL), tok),
            pl.BlockSpec((tm, D_MODEL), tok),
            pl.BlockSpec((tm, LANES), tok),
            pl.BlockSpec((1, D_MODEL, 2 * D_EXPERT), lambda t, e: (e, 0, 0)),
            pl.BlockSpec((1, D_EXPERT, D_MODEL), lambda t, e: (e, 0, 0)),
        ],
        out_specs=pl.BlockSpec((tm, D_MODEL), tok),
        out_shape=jax.ShapeDtypeStruct((tokens, D_MODEL), F32),
        compiler_params=pltpu.CompilerParams(
            dimension_semantics=("arbitrary", "arbitrary"), vmem_limit_bytes=VMEM_LIMIT),
        name="moe",
    )(h1, xn, comb, w_ei, w_eo)


def _ple_kernel(h_ref, p_ref, gp_ref, wg_ref, wp_ref, gfin_ref, o_ref):
    h = h_ref[...]
    gate = _sigmoid(_dot(_rms(h, gp_ref[...]).astype(BF16), wg_ref[...]))
    h = h + gate * _dot(p_ref[...].astype(BF16), wp_ref[...])
    o_ref[...] = _rms(h, gfin_ref[...])


def _ple(h2, p2, g_ple, w_pg, w_pp, g_final, tm):
    tokens = h2.shape[0]
    tok = lambda t: (t, 0)
    const = lambda t: (0, 0)
    return pl.pallas_call(
        _ple_kernel,
        grid=(tokens // tm,),
        in_specs=[
            pl.BlockSpec((tm, D_MODEL), tok),
            pl.BlockSpec((tm, PLE_DIM), tok),
            pl.BlockSpec((1, D_MODEL), const),
            pl.BlockSpec((D_MODEL, D_MODEL), const),
            pl.BlockSpec((PLE_DIM, D_MODEL), const),
            pl.BlockSpec((1, D_MODEL), const),
        ],
        out_specs=pl.BlockSpec((tm, D_MODEL), tok),
        out_shape=jax.ShapeDtypeStruct((tokens, D_MODEL), F32),
        compiler_params=pltpu.CompilerParams(
            dimension_semantics=("arbitrary",), vmem_limit_bytes=VMEM_LIMIT),
        name="ple",
    )(h2, p2, g_ple, w_pg, w_pp, g_final)


def _rope_tables(seq):
    half = SWA_HEAD_DIM // 2
    pos = jnp.arange(seq, dtype=F32)
    inv = ROPE_THETA ** (-jnp.arange(0, SWA_HEAD_DIM, 2, dtype=F32) / SWA_HEAD_DIM)
    ang = pos[:, None] * inv[None, :]
    cos, sin = jnp.cos(ang), jnp.sin(ang)
    reps = LANES // half
    cos_t = jnp.tile(cos, (1, reps))
    sin_t = jnp.tile(jnp.concatenate([-sin, sin], axis=1), (1, reps // 2))
    return cos_t, sin_t


def _layer(h2d, p2d, g_mix, w_in, sinks, g_q, w_uq, g_kv, w_ukv, w_out, g_ffn, w_rg, b_rg,
           w_re, b_re, w_ei, w_eo, g_ple, w_pg, w_pp, g_final, cos_t, sin_t, batch, seq):
    assert SWA_HEAD_DIM == MLA_ROPE_DIM, "one rotary table serves both mixers"
    sizes = (SWA_HEADS * SWA_HEAD_DIM, SWA_KV_HEADS * SWA_HEAD_DIM, SWA_KV_HEADS * SWA_HEAD_DIM,
             MLA_Q_RANK, MLA_KV_RANK, MLA_ROPE_DIM, D_MODEL, D_MODEL)
    offs = [0]
    for s in sizes:
        offs.append(offs[-1] + s)
    seg = [w_in[:, offs[i]:offs[i + 1]] for i in range(len(sizes))]
    w_cat = jnp.concatenate(
        seg[:5] + [seg[6], seg[7], seg[5], jnp.zeros((D_MODEL, LANES - MLA_ROPE_DIM), w_in.dtype)],
        axis=1).astype(BF16)
    uq = w_uq.reshape(MLA_Q_RANK, MLA_HEADS, MLA_QK_DIM)
    w_uq_c = jnp.concatenate(
        [uq[:, :, :MLA_NOPE_DIM].reshape(MLA_Q_RANK, -1),
         uq[:, :, MLA_NOPE_DIM:].reshape(MLA_Q_RANK, -1)], axis=1).astype(BF16)
    ukv = w_ukv.reshape(MLA_KV_RANK, MLA_HEADS, MLA_NOPE_DIM + MLA_V_DIM)
    w_uk_c = ukv[:, :, :MLA_NOPE_DIM].reshape(MLA_KV_RANK, -1).astype(BF16)
    w_uvt = jnp.transpose(ukv[:, :, MLA_NOPE_DIM:], (1, 2, 0)).astype(BF16)

    q_a, k_a, v_a, g_a, g_b, q_m, k_m, v_m = _inproj(
        h2d, g_mix[None], w_cat, cos_t, sin_t, g_q[None], w_uq_c, g_kv[None], w_uk_c, w_uvt,
        batch, seq, tm=MLA_KV_CHUNK)

    sink_rows = jnp.repeat(sinks.astype(F32), SWA_WINDOW)[:, None]
    o_a = _swa(q_a, k_a, v_a, sink_rows, batch, seq, tq=512)
    o_b = _mla(q_m, k_m, v_m, batch, seq, tq=1024, tk=512)

    w_r = jnp.concatenate(
        [w_re, w_rg, jnp.zeros((D_MODEL, LANES - N_EXPERTS - N_GROUPS), F32)], axis=1)
    w_r_hi = w_r.astype(BF16)
    w_r_lo = (w_r - w_r_hi.astype(F32)).astype(BF16)
    w_r2 = jnp.stack([w_r_hi, w_r_lo])
    b_r = jnp.concatenate(
        [b_re, b_rg, jnp.zeros((LANES - N_EXPERTS - N_GROUPS,), F32)])[None]
    tokens = batch * seq
    h1, xn2, comb = _outproj(
        h2d, o_a.reshape(tokens, D_MODEL), o_b.reshape(tokens, D_MODEL), g_a, g_b,
        w_out.astype(BF16), g_ffn[None], w_r2, b_r, tm=512)

    h2 = _moe(h1, xn2, comb, w_ei.astype(BF16), w_eo.astype(BF16), tm=1024)
    return _ple(h2, p2d, g_ple[None], w_pg.astype(BF16), w_pp.astype(BF16), g_final[None], tm=512)


def kernel(x, p, g_mix, w_in, swa_sinks, mla_g_q, mla_w_uq, mla_g_kv, mla_w_ukv, w_out, g_ffn,
           w_router_group, b_router_group, w_router_expert, b_router_expert, w_expert_in,
           w_expert_out, g_ple, w_ple_gate, w_ple_proj, g_final):
    batch, seq, d = x.shape
    depth = p.shape[0]
    assert d == D_MODEL and depth == 1, "final RMSNorm is fused into the single layer"
    cos_t, sin_t = _rope_tables(seq)
    out = _layer(
        x.reshape(batch * seq, d), p[0].reshape(batch * seq, PLE_DIM), g_mix[0], w_in[0],
        swa_sinks[0], mla_g_q[0], mla_w_uq[0], mla_g_kv[0], mla_w_ukv[0], w_out[0], g_ffn[0],
        w_router_group[0], b_router_group[0], w_router_expert[0], b_router_expert[0],
        w_expert_in[0], w_expert_out[0], g_ple[0], w_ple_gate[0], w_ple_proj[0], g_final,
        cos_t, sin_t, batch, seq)
    return out.reshape(batch, seq, d)
```

```python
import functools
import math

import jax
import jax.numpy as jnp
from jax import lax
from jax.experimental import pallas as pl
from jax.experimental.pallas import tpu as pltpu

D_MODEL = 1024
PLE_DIM = 256
ROPE_THETA = 10000.0
EPS = 1e-6
NEG_INF = -1e30

SWA_HEAD_DIM = 64
SWA_HEADS = D_MODEL // SWA_HEAD_DIM
SWA_KV_HEADS = SWA_HEADS // 8
SWA_GROUP = SWA_HEADS // SWA_KV_HEADS
SWA_WINDOW = 128

MLA_NOPE_DIM = 128
MLA_ROPE_DIM = 64
MLA_V_DIM = 128
MLA_HEADS = D_MODEL // MLA_V_DIM
MLA_Q_RANK = 256
MLA_KV_RANK = 128
MLA_QK_DIM = MLA_NOPE_DIM + MLA_ROPE_DIM

N_GROUPS = 4
EXPERTS_PER_GROUP = 4
N_EXPERTS = N_GROUPS * EXPERTS_PER_GROUP
D_EXPERT = 256

LANES = 128
LOG2_E = 1.4426950408889634
MLA_KV_CHUNK = 512
VMEM_LIMIT = 56 * 1024 * 1024

_C_QA = 0
_C_KV = _C_QA + SWA_HEADS * SWA_HEAD_DIM
_C_CQ = _C_KV + 2 * SWA_KV_HEADS * SWA_HEAD_DIM
_C_CKVR = _C_CQ + MLA_Q_RANK
_C_GA = _C_CKVR + MLA_KV_RANK + LANES
_C_GB = _C_GA + D_MODEL
_C_END = _C_GB + D_MODEL

BF16 = jnp.bfloat16
F32 = jnp.float32


def _dot(a, b):
    return jnp.dot(a, b, preferred_element_type=F32)


def _dot_nt(a, b):
    return lax.dot_general(a, b, (((1,), (1,)), ((), ())), preferred_element_type=F32)


def _rms(x, g):
    r = lax.rsqrt(jnp.mean(x * x, axis=-1, keepdims=True) + EPS)
    return x * r * g


def _sigmoid(x):
    return 1.0 / (1.0 + jnp.exp(-x))


def _rope_lanes(x, cos, sin_signed):
    lane = lax.broadcasted_iota(jnp.int32, x.shape, 1)
    upper = (lane & (SWA_HEAD_DIM // 2)) != 0
    partner = jnp.where(upper, pltpu.roll(x, SWA_HEAD_DIM // 2, 1),
                        pltpu.roll(x, LANES - SWA_HEAD_DIM // 2, 1))
    return x * cos + partner * sin_signed


def _inproj_kernel(x_ref, g_ref, w_ref, cos_ref, sin_ref, gq_ref, wuq_ref, gkv_ref, wuk_ref,
                   wuvt_ref, qa_ref, ka_ref, va_ref, ga_ref, gb_ref, qm_ref, km_ref, vt_ref):
    xn = _rms(x_ref[...], g_ref[...]).astype(BF16)
    cos = cos_ref[...]
    sin = sin_ref[...]

    qa_scale = LOG2_E / math.sqrt(SWA_HEAD_DIM)
    qa = _dot(xn, w_ref[:, _C_QA:_C_KV])
    for j in range(SWA_HEADS // 2):
        q = (_rope_lanes(qa[:, j * LANES:(j + 1) * LANES], cos, sin) * qa_scale).astype(BF16)
        qa_ref[0, 2 * j] = q[:, :SWA_HEAD_DIM]
        qa_ref[0, 2 * j + 1] = q[:, SWA_HEAD_DIM:]
    kv = _dot(xn, w_ref[:, _C_KV:_C_CQ])
    k = _rope_lanes(kv[:, :LANES], cos, sin).astype(BF16)
    ka_ref[0, 0] = k[:, :SWA_HEAD_DIM]
    ka_ref[0, 1] = k[:, SWA_HEAD_DIM:]
    vt = kv[:, LANES:].T.astype(BF16)
    va_ref[0, 0] = vt[:SWA_HEAD_DIM]
    va_ref[0, 1] = vt[SWA_HEAD_DIM:]

    ga_ref[...] = _sigmoid(_dot(xn, w_ref[:, _C_GA:_C_GB])).astype(BF16)
    gb_ref[...] = _sigmoid(_dot(xn, w_ref[:, _C_GB:_C_END])).astype(BF16)

    cq = _rms(_dot(xn, w_ref[:, _C_CQ:_C_CKVR]), gq_ref[...]).astype(BF16)
    ckvr = _dot(xn, w_ref[:, _C_CKVR:_C_GA])
    ckv = _rms(ckvr[:, :MLA_KV_RANK], gkv_ref[...]).astype(BF16)
    kr = _rope_lanes(ckvr[:, MLA_KV_RANK:], cos, sin)[:, :MLA_ROPE_DIM].astype(BF16)
    qm_scale = LOG2_E / math.sqrt(MLA_QK_DIM)
    n_nope = MLA_HEADS * MLA_NOPE_DIM
    qn = (_dot(cq, wuq_ref[:, :n_nope]) * qm_scale).astype(BF16)
    kn = _dot(ckv, wuk_ref[...]).astype(BF16)
    qr = _dot(cq, wuq_ref[:, n_nope:])
    for h in range(MLA_HEADS):
        qm_ref[0, h, :, :MLA_NOPE_DIM] = qn[:, h * MLA_NOPE_DIM:(h + 1) * MLA_NOPE_DIM]
        km_ref[0, h, :, :MLA_NOPE_DIM] = kn[:, h * MLA_NOPE_DIM:(h + 1) * MLA_NOPE_DIM]
        km_ref[0, h, :, MLA_NOPE_DIM:] = kr
        vt_ref[0, h, 0] = _dot_nt(wuvt_ref[h], ckv).astype(BF16)
    for j in range(MLA_HEADS // 2):
        r = (_rope_lanes(qr[:, j * LANES:(j + 1) * LANES], cos, sin) * qm_scale).astype(BF16)
        qm_ref[0, 2 * j, :, MLA_NOPE_DIM:] = r[:, :MLA_ROPE_DIM]
        qm_ref[0, 2 * j + 1, :, MLA_NOPE_DIM:] = r[:, MLA_ROPE_DIM:]


def _inproj(x2, g_mix, w_cat, cos_t, sin_t, g_q, w_uq, g_kv, w_uk, w_uvt, batch, seq, tm):
    tokens = batch * seq
    nt = seq // tm
    const = lambda t: (0, 0)
    tok = lambda t: (t, 0)
    head = lambda t: (t // nt, 0, t % nt, 0)
    out_shape = (
        jax.ShapeDtypeStruct((batch, SWA_HEADS, seq, SWA_HEAD_DIM), BF16),
        jax.ShapeDtypeStruct((batch, SWA_KV_HEADS, seq, SWA_HEAD_DIM), BF16),
        jax.ShapeDtypeStruct((batch, SWA_KV_HEADS, SWA_HEAD_DIM, seq), BF16),
        jax.ShapeDtypeStruct((tokens, D_MODEL), BF16),
        jax.ShapeDtypeStruct((tokens, D_MODEL), BF16),
        jax.ShapeDtypeStruct((batch, MLA_HEADS, seq, MLA_QK_DIM), BF16),
        jax.ShapeDtypeStruct((batch, MLA_HEADS, seq, MLA_QK_DIM), BF16),
        jax.ShapeDtypeStruct((batch, MLA_HEADS, nt, MLA_V_DIM, tm), BF16),
    )
    return pl.pallas_call(
        _inproj_kernel,
        grid=(tokens // tm,),
        in_specs=[
            pl.BlockSpec((tm, D_MODEL), tok),
            pl.BlockSpec((1, D_MODEL), const),
            pl.BlockSpec((D_MODEL, _C_END), const, pipeline_mode=pl.Buffered(1)),
            pl.BlockSpec((tm, LANES), lambda t: (t % nt, 0)),
            pl.BlockSpec((tm, LANES), lambda t: (t % nt, 0)),
            pl.BlockSpec((1, MLA_Q_RANK), const),
            pl.BlockSpec(w_uq.shape, const, pipeline_mode=pl.Buffered(1)),
            pl.BlockSpec((1, MLA_KV_RANK), const),
            pl.BlockSpec(w_uk.shape, const, pipeline_mode=pl.Buffered(1)),
            pl.BlockSpec(w_uvt.shape, lambda t: (0, 0, 0), pipeline_mode=pl.Buffered(1)),
        ],
        out_specs=(
            pl.BlockSpec((1, SWA_HEADS, tm, SWA_HEAD_DIM), head),
            pl.BlockSpec((1, SWA_KV_HEADS, tm, SWA_HEAD_DIM), head),
            pl.BlockSpec((1, SWA_KV_HEADS, SWA_HEAD_DIM, tm), lambda t: (t // nt, 0, 0, t % nt)),
            pl.BlockSpec((tm, D_MODEL), tok),
            pl.BlockSpec((tm, D_MODEL), tok),
            pl.BlockSpec((1, MLA_HEADS, tm, MLA_QK_DIM), head),
            pl.BlockSpec((1, MLA_HEADS, tm, MLA_QK_DIM), head),
            pl.BlockSpec((1, MLA_HEADS, 1, MLA_V_DIM, tm), lambda t: (t // nt, 0, t % nt, 0, 0)),
        ),
        out_shape=out_shape,
        compiler_params=pltpu.CompilerParams(
            dimension_semantics=("arbitrary",), vmem_limit_bytes=VMEM_LIMIT),
        name="inproj",
    )(x2, g_mix, w_cat, cos_t, sin_t, g_q, w_uq, g_kv, w_uk, w_uvt)


def _swa_kernel(q_ref, k_ref, kp_ref, vt_ref, vtp_ref, sink_ref, o_ref, s_scr, bias_scr, *,
                n_sub):
    blk = SWA_WINDOW
    cols = SWA_GROUP * blk

    @pl.when((pl.program_id(0) == 0) & (pl.program_id(1) == 0))
    def _():
        kj = lax.broadcasted_iota(jnp.int32, (2 * blk, cols), 0)
        qi = lax.broadcasted_iota(jnp.int32, (2 * blk, cols), 1) & (blk - 1)
        dist = blk + qi - kj
        bias_scr[...] = jnp.where((dist >= 0) & (dist < SWA_WINDOW), 0.0, NEG_INF)

    key_row = lax.broadcasted_iota(jnp.int32, (2 * blk, 1), 0)
    no_prev = jnp.where((key_row < blk) & (pl.program_id(1) == 0), NEG_INF, 0.0)
    units = [(g, n) for g in range(SWA_KV_HEADS) for n in range(n_sub)]

    def scores(u, buf):
        g, n = u
        q = q_ref[0, g * SWA_GROUP:(g + 1) * SWA_GROUP, n * blk:(n + 1) * blk, :]
        k_prev = kp_ref[0, g] if n == 0 else k_ref[0, g, (n - 1) * blk:n * blk, :]
        kk = jnp.concatenate([k_prev, k_ref[0, g, n * blk:(n + 1) * blk, :]], axis=0)
        s_scr[buf] = _dot_nt(kk, q.reshape(cols, SWA_HEAD_DIM))

    def softmax_pv(u, buf):
        g, n = u
        sink = sink_ref[:, g * cols:(g + 1) * cols] * LOG2_E
        s = s_scr[buf] + bias_scr[...]
        if n == 0:
            s = s + no_prev
        m = jnp.maximum(jnp.max(s, axis=0, keepdims=True), sink)
        e = jnp.exp2(s - m)
        denom = jnp.sum(e, axis=0, keepdims=True) + jnp.exp2(sink - m)
        vt_prev = vtp_ref[0, g] if n == 0 else vt_ref[0, g, :, (n - 1) * blk:n * blk]
        vvt = jnp.concatenate([vt_prev, vt_ref[0, g, :, n * blk:(n + 1) * blk]], axis=1)
        o = (_dot(vvt, e.astype(BF16)) / denom).T
        for hh in range(SWA_GROUP):
            h = g * SWA_GROUP + hh
            o_ref[0, n * blk:(n + 1) * blk, h * SWA_HEAD_DIM:(h + 1) * SWA_HEAD_DIM] = (
                o[hh * blk:(hh + 1) * blk, :].astype(BF16))

    scores(units[0], 0)
    for idx, u in enumerate(units):
        if idx + 1 < len(units):
            scores(units[idx + 1], (idx + 1) % 2)
        softmax_pv(u, idx % 2)


def _swa(q_a, k_a, vt_a, sink_lanes, batch, seq, tq):
    n_sub = tq // SWA_WINDOW
    cur = lambda b, i: (b, 0, i, 0)
    prev = lambda b, i: (b, 0, jnp.maximum(i * n_sub - 1, 0), 0)
    return pl.pallas_call(
        functools.partial(_swa_kernel, n_sub=n_sub),
        grid=(batch, seq // tq),
        in_specs=[
            pl.BlockSpec((1, SWA_HEADS, tq, SWA_HEAD_DIM), cur),
            pl.BlockSpec((1, SWA_KV_HEADS, tq, SWA_HEAD_DIM), cur),
            pl.BlockSpec((1, SWA_KV_HEADS, SWA_WINDOW, SWA_HEAD_DIM), prev),
            pl.BlockSpec((1, SWA_KV_HEADS, SWA_HEAD_DIM, tq), lambda b, i: (b, 0, 0, i)),
            pl.BlockSpec((1, SWA_KV_HEADS, SWA_HEAD_DIM, SWA_WINDOW),
                         lambda b, i: (b, 0, 0, jnp.maximum(i * n_sub - 1, 0))),
            pl.BlockSpec(sink_lanes.shape, lambda b, i: (0, 0)),
        ],
        out_specs=pl.BlockSpec((1, tq, D_MODEL), lambda b, i: (b, i, 0)),
        out_shape=jax.ShapeDtypeStruct((batch, seq, D_MODEL), BF16),
        scratch_shapes=[pltpu.VMEM((2, 2 * SWA_WINDOW, SWA_GROUP * SWA_WINDOW), F32),
                        pltpu.VMEM((2 * SWA_WINDOW, SWA_GROUP * SWA_WINDOW), F32)],
        compiler_params=pltpu.CompilerParams(
            dimension_semantics=("arbitrary", "arbitrary"), vmem_limit_bytes=VMEM_LIMIT),
        name="swa",
    )(q_a, k_a, k_a, vt_a, vt_a, sink_lanes)


def _mla_kernel(q_ref, k_ref, vt_ref, o_ref, m_scr, l_scr, acc_scr, s_scr, *, tq, tk):
    assert tq == 2 * tk
    chunks = tk // MLA_KV_CHUNK
    n_q = q_ref.shape[2] // tq

    def q_block(i, carry):
        m_scr[...] = jnp.full(m_scr.shape, NEG_INF, F32)
        l_scr[...] = jnp.zeros(l_scr.shape, F32)
        acc_scr[...] = jnp.zeros(acc_scr.shape, F32)

        def scores(j, buf, q0=0):
            k = k_ref[0, 0, pl.ds(pl.multiple_of(j * tk, tk), tk), :]
            q = q_ref[0, 0, pl.ds(pl.multiple_of(i * tq + q0, tk), tq - q0), :]
            s_scr[buf, :, q0:] = _dot_nt(k, q)

        def softmax_pv(j, buf, q0=0, key_offset=None):
            s = s_scr[buf, :, q0:]
            if key_offset is not None:
                kpos = lax.broadcasted_iota(jnp.int32, s.shape, 0) + key_offset
                qpos = lax.broadcasted_iota(jnp.int32, s.shape, 1) + q0
                s = jnp.where(kpos <= qpos, s, NEG_INF)
            m_old = m_scr[:, q0:]
            m_new = jnp.maximum(m_old, jnp.max(s, axis=0, keepdims=True))
            alpha = jnp.exp2(m_old - m_new)
            p = jnp.exp2(s - m_new)
            l_scr[:, q0:] = alpha * l_scr[:, q0:] + jnp.sum(p, axis=0, keepdims=True)
            pb = p.astype(BF16)
            pv = _dot(vt_ref[0, 0, j * chunks], pb[:MLA_KV_CHUNK])
            for c in range(1, chunks):
                pv += _dot(vt_ref[0, 0, j * chunks + c],
                           pb[c * MLA_KV_CHUNK:(c + 1) * MLA_KV_CHUNK])
            acc_scr[:, q0:] = alpha * acc_scr[:, q0:] + pv
            m_scr[:, q0:] = m_new

        scores(0, 0)

        def pair(jj, c):
            t = 2 * jj
            scores(t + 1, 1)
            softmax_pv(t, 0)
            scores(t + 2, 0)
            softmax_pv(t + 1, 1)
            return c

        lax.fori_loop(0, i, pair, 0)
        scores(2 * i + 1, 1, q0=tk)
        softmax_pv(2 * i, 0, key_offset=0)
        softmax_pv(2 * i + 1, 1, q0=tk, key_offset=tk)

        o = acc_scr[...] / l_scr[...]
        o_ref[0, pl.ds(pl.multiple_of(i * tq, tq), tq), :] = o.T.astype(o_ref.dtype)
        return carry

    lax.fori_loop(0, n_q, q_block, 0)


def _mla(q_m, k_m, vt_m, batch, seq, tq, tk):
    return pl.pallas_call(
        functools.partial(_mla_kernel, tq=tq, tk=tk),
        grid=(batch, MLA_HEADS),
        in_specs=[
            pl.BlockSpec((1, 1, seq, MLA_QK_DIM), lambda b, h: (b, h, 0, 0)),
            pl.BlockSpec((1, 1, seq, MLA_QK_DIM), lambda b, h: (b, h, 0, 0)),
            pl.BlockSpec((1, 1, seq // MLA_KV_CHUNK, MLA_V_DIM, MLA_KV_CHUNK),
                         lambda b, h: (b, h, 0, 0, 0)),
        ],
        out_specs=pl.BlockSpec((1, seq, MLA_V_DIM), lambda b, h: (b, 0, h)),
        out_shape=jax.ShapeDtypeStruct((batch, seq, D_MODEL), BF16),
        scratch_shapes=[
            pltpu.VMEM((1, tq), F32),
            pltpu.VMEM((1, tq), F32),
            pltpu.VMEM((MLA_V_DIM, tq), F32),
            pltpu.VMEM((2, tk, tq), F32),
        ],
        compiler_params=pltpu.CompilerParams(
            dimension_semantics=("arbitrary", "arbitrary"), vmem_limit_bytes=VMEM_LIMIT),
        name="mla",
    )(q_m, k_m, vt_m)


def _outproj_kernel(x_ref, oa_ref, ob_ref, ga_ref, gb_ref, wo_ref, gf_ref, wr_ref, br_ref,
                    h_ref, xn_ref, comb_ref):
    merged = (ga_ref[...].astype(F32) * oa_ref[...].astype(F32)
              + gb_ref[...].astype(F32) * ob_ref[...].astype(F32)).astype(BF16)
    h = x_ref[...] + _dot(merged, wo_ref[...])
    h_ref[...] = h
    xn = _rms(h, gf_ref[...])
    xn_ref[...] = xn.astype(BF16)

    x_hi = xn.astype(BF16)
    x_lo = (xn - x_hi.astype(F32)).astype(BF16)
    lg = (_dot(x_hi, wr_ref[0]) + _dot(x_lo, wr_ref[0]) + _dot(x_hi, wr_ref[1])) + br_ref[...]

    lane = lax.broadcasted_iota(jnp.int32, lg.shape, 1)
    is_group = (lane >= N_EXPERTS) & (lane < N_EXPERTS + N_GROUPS)
    gl = jnp.where(is_group, lg, NEG_INF)
    g_max = jnp.max(gl, axis=-1, keepdims=True)
    g_sum = jnp.sum(jnp.where(is_group, jnp.exp(gl - g_max), 0.0), axis=-1, keepdims=True)
    g_w = 1.0 / g_sum
    big = jnp.int32(LANES)
    g_lane = jnp.min(jnp.where(is_group & (gl == g_max), lane, big), axis=-1, keepdims=True)
    gidx = g_lane - N_EXPERTS

    sel = (lane >= gidx * EXPERTS_PER_GROUP) & (lane < (gidx + 1) * EXPERTS_PER_GROUP)
    el = jnp.where(sel, lg, NEG_INF)
    e_max = jnp.max(el, axis=-1, keepdims=True)
    e_exp = jnp.where(sel, jnp.exp(el - e_max), 0.0)
    e_prob = e_exp / jnp.sum(e_exp, axis=-1, keepdims=True)
    p1 = jnp.max(e_prob, axis=-1, keepdims=True)
    i1 = jnp.min(jnp.where(sel & (e_prob == p1), lane, big), axis=-1, keepdims=True)
    rest = sel & (lane != i1)
    p2 = jnp.max(jnp.where(rest, e_prob, -1.0), axis=-1, keepdims=True)
    i2 = jnp.min(jnp.where(rest & (e_prob == p2), lane, big), axis=-1, keepdims=True)
    top_sum = p1 + p2
    comb = jnp.where(lane == i1, p1 / top_sum, jnp.where(lane == i2, p2 / top_sum, 0.0))
    comb_ref[...] = comb * g_w


def _outproj(x2, o_a, o_b, g_a, g_b, w_o, g_ffn, w_r, b_r, tm):
    tokens = x2.shape[0]
    tok = lambda t: (t, 0)
    const = lambda t: (0, 0)
    return pl.pallas_call(
        _outproj_kernel,
        grid=(tokens // tm,),
        in_specs=[
            pl.BlockSpec((tm, D_MODEL), tok),
            pl.BlockSpec((tm, D_MODEL), tok),
            pl.BlockSpec((tm, D_MODEL), tok),
            pl.BlockSpec((tm, D_MODEL), tok),
            pl.BlockSpec((tm, D_MODEL), tok),
            pl.BlockSpec((D_MODEL, D_MODEL), const),
            pl.BlockSpec((1, D_MODEL), const),
            pl.BlockSpec((2, D_MODEL, LANES), lambda t: (0, 0, 0)),
            pl.BlockSpec((1, LANES), const),
        ],
        out_specs=(
            pl.BlockSpec((tm, D_MODEL), tok),
            pl.BlockSpec((tm, D_MODEL), tok),
            pl.BlockSpec((tm, LANES), tok),
        ),
        out_shape=(
            jax.ShapeDtypeStruct((tokens, D_MODEL), F32),
            jax.ShapeDtypeStruct((tokens, D_MODEL), BF16),
            jax.ShapeDtypeStruct((tokens, LANES), F32),
        ),
        compiler_params=pltpu.CompilerParams(
            dimension_semantics=("arbitrary",), vmem_limit_bytes=VMEM_LIMIT),
        name="outproj",
    )(x2, o_a, o_b, g_a, g_b, w_o, g_ffn, w_r, b_r)


def _moe_kernel(h_ref, xn_ref, comb_ref, wi_ref, wo_ref, o_ref):
    e = pl.program_id(1)

    @pl.when(e == 0)
    def _():
        o_ref[...] = h_ref[...]

    comb = comb_ref[...]
    lane = lax.broadcasted_iota(jnp.int32, comb.shape, 1)
    c = jnp.sum(jnp.where(lane == e, comb, 0.0), axis=-1, keepdims=True)
    hid = _dot(xn_ref[...], wi_ref[0])
    gate = hid[:, :D_EXPERT]
    up = hid[:, D_EXPERT:]
    act = (gate * _sigmoid(gate)) * up * c
    o_ref[...] += _dot(act.astype(BF16), wo_ref[0])


def _moe(h1, xn, comb, w_ei, w_eo, tm):
    tokens = h1.shape[0]
    tok = lambda t, e: (t, 0)
    return pl.pallas_call(
        _moe_kernel,
        grid=(tokens // tm, N_EXPERTS),
        in_specs=[
            pl.BlockSpec((tm, D_MODEL), tok),
            pl.BlockSpec((tm, D_MODEL), tok),
            pl.BlockSpec((tm, LANES), tok),
            pl.BlockSpec((1, D_MODEL, 2 * D_EXPERT), lambda t, e: (e, 0, 0)),
            pl.BlockSpec((1, D_EXPERT, D_MODEL), lambda t, e: (e, 0, 0)),
        ],
        out_specs=pl.BlockSpec((tm, D_MODEL), tok),
        out_shape=jax.ShapeDtypeStruct((tokens, D_MODEL), F32),
        compiler_params=pltpu.CompilerParams(
            dimension_semantics=("arbitrary", "arbitrary"), vmem_limit_bytes=VMEM_LIMIT),
        name="moe",
    )(h1, xn, comb, w_ei, w_eo)


def _ple_kernel(h_ref, p_ref, gp_ref, wg_ref, wp_ref, gfin_ref, o_ref):
    h = h_ref[...]
    gate = _sigmoid(_dot(_rms(h, gp_ref[...]).astype(BF16), wg_ref[...]))
    h = h + gate * _dot(p_ref[...].astype(BF16), wp_ref[...])
    o_ref[...] = _rms(h, gfin_ref[...])


def _ple(h2, p2, g_ple, w_pg, w_pp, g_final, tm):
    tokens = h2.shape[0]
    tok = lambda t: (t, 0)
    const = lambda t: (0, 0)
    return pl.pallas_call(
        _ple_kernel,
        grid=(tokens // tm,),
        in_specs=[
            pl.BlockSpec((tm, D_MODEL), tok),
            pl.BlockSpec((tm, PLE_DIM), tok),
            pl.BlockSpec((1, D_MODEL), const),
            pl.BlockSpec((D_MODEL, D_MODEL), const),
            pl.BlockSpec((PLE_DIM, D_MODEL), const),
            pl.BlockSpec((1, D_MODEL), const),
        ],
        out_specs=pl.BlockSpec((tm, D_MODEL), tok),
        out_shape=jax.ShapeDtypeStruct((tokens, D_MODEL), F32),
        compiler_params=pltpu.CompilerParams(
            dimension_semantics=("arbitrary",), vmem_limit_bytes=VMEM_LIMIT),
        name="ple",
    )(h2, p2, g_ple, w_pg, w_pp, g_final)


def _rope_tables(seq):
    half = SWA_HEAD_DIM // 2
    pos = jnp.arange(seq, dtype=F32)
    inv = ROPE_THETA ** (-jnp.arange(0, SWA_HEAD_DIM, 2, dtype=F32) / SWA_HEAD_DIM)
    ang = pos[:, None] * inv[None, :]
    cos, sin = jnp.cos(ang), jnp.sin(ang)
    reps = LANES // half
    cos_t = jnp.tile(cos, (1, reps))
    sin_t = jnp.tile(jnp.concatenate([-sin, sin], axis=1), (1, reps // 2))
    return cos_t, sin_t


def _layer(h2d, p2d, g_mix, w_in, sinks, g_q, w_uq, g_kv, w_ukv, w_out, g_ffn, w_rg, b_rg,
           w_re, b_re, w_ei, w_eo, g_ple, w_pg, w_pp, g_final, cos_t, sin_t, batch, seq):
    assert SWA_HEAD_DIM == MLA_ROPE_DIM, "one rotary table serves both mixers"
    kr_end = _C_CKVR + MLA_KV_RANK + MLA_ROPE_DIM
    w_cat = jnp.concatenate(
        [w_in[:, :kr_end], jnp.zeros((D_MODEL, LANES - MLA_ROPE_DIM), w_in.dtype),
         w_in[:, kr_end:]], axis=1).astype(BF16)
    assert w_cat.shape[1] == _C_END
    uq = w_uq.reshape(MLA_Q_RANK, MLA_HEADS, MLA_QK_DIM)
    w_uq_c = jnp.concatenate(
        [uq[:, :, :MLA_NOPE_DIM].reshape(MLA_Q_RANK, -1),
         uq[:, :, MLA_NOPE_DIM:].reshape(MLA_Q_RANK, -1)], axis=1).astype(BF16)
    ukv = w_ukv.reshape(MLA_KV_RANK, MLA_HEADS, MLA_NOPE_DIM + MLA_V_DIM)
    w_uk_c = ukv[:, :, :MLA_NOPE_DIM].reshape(MLA_KV_RANK, -1).astype(BF16)
    w_uvt = jnp.transpose(ukv[:, :, MLA_NOPE_DIM:], (1, 2, 0)).astype(BF16)

    q_a, k_a, v_a, g_a, g_b, q_m, k_m, v_m = _inproj(
        h2d, g_mix[None], w_cat, cos_t, sin_t, g_q[None], w_uq_c, g_kv[None], w_uk_c, w_uvt,
        batch, seq, tm=MLA_KV_CHUNK)

    sink_lanes = jnp.repeat(sinks.astype(F32), SWA_WINDOW)[None]
    o_a = _swa(q_a, k_a, v_a, sink_lanes, batch, seq, tq=512)
    o_b = _mla(q_m, k_m, v_m, batch, seq, tq=1024, tk=512)

    w_r = jnp.concatenate(
        [w_re, w_rg, jnp.zeros((D_MODEL, LANES - N_EXPERTS - N_GROUPS), F32)], axis=1)
    w_r_hi = w_r.astype(BF16)
    w_r_lo = (w_r - w_r_hi.astype(F32)).astype(BF16)
    w_r2 = jnp.stack([w_r_hi, w_r_lo])
    b_r = jnp.concatenate(
        [b_re, b_rg, jnp.zeros((LANES - N_EXPERTS - N_GROUPS,), F32)])[None]
    tokens = batch * seq
    h1, xn2, comb = _outproj(
        h2d, o_a.reshape(tokens, D_MODEL), o_b.reshape(tokens, D_MODEL), g_a, g_b,
        w_out.astype(BF16), g_ffn[None], w_r2, b_r, tm=512)

    h2 = _moe(h1, xn2, comb, w_ei.astype(BF16), w_eo.astype(BF16), tm=1024)
    return _ple(h2, p2d, g_ple[None], w_pg.astype(BF16), w_pp.astype(BF16), g_final[None], tm=512)


def kernel(x, p, g_mix, w_in, swa_sinks, mla_g_q, mla_w_uq, mla_g_kv, mla_w_ukv, w_out, g_ffn,
           w_router_group, b_router_group, w_router_expert, b_router_expert, w_expert_in,
           w_expert_out, g_ple, w_ple_gate, w_ple_proj, g_final):
    batch, seq, d = x.shape
    depth = p.shape[0]
    assert d == D_MODEL and depth == 1, "final RMSNorm is fused into the single layer"
    cos_t, sin_t = _rope_tables(seq)
    out = _layer(
        x.reshape(batch * seq, d), p[0].reshape(batch * seq, PLE_DIM), g_mix[0], w_in[0],
        swa_sinks[0], mla_g_q[0], mla_w_uq[0], mla_g_kv[0], mla_w_ukv[0], w_out[0], g_ffn[0],
        w_router_group[0], b_router_group[0], w_router_expert[0], b_router_expert[0],
        w_expert_in[0], w_expert_out[0], g_ple[0], w_ple_gate[0], w_ple_proj[0], g_final,
        cos_t, sin_t, batch, seq)
    return out.reshape(batch, seq, d)
```

```python
import functools
import math

import jax
import jax.numpy as jnp
from jax import lax
from jax.experimental import pallas as pl
from jax.experimental.pallas import tpu as pltpu

D_MODEL = 1024
PLE_DIM = 256
ROPE_THETA = 10000.0
EPS = 1e-6
NEG_INF = -1e30

SWA_HEAD_DIM = 64
SWA_HEADS = D_MODEL // SWA_HEAD_DIM
SWA_KV_HEADS = SWA_HEADS // 8
SWA_GROUP = SWA_HEADS // SWA_KV_HEADS
SWA_WINDOW = 128

MLA_NOPE_DIM = 128
MLA_ROPE_DIM = 64
MLA_V_DIM = 128
MLA_HEADS = D_MODEL // MLA_V_DIM
MLA_Q_RANK = 256
MLA_KV_RANK = 128
MLA_QK_DIM = MLA_NOPE_DIM + MLA_ROPE_DIM

N_GROUPS = 4
EXPERTS_PER_GROUP = 4
N_EXPERTS = N_GROUPS * EXPERTS_PER_GROUP
D_EXPERT = 256

LANES = 128
LOG2_E = 1.4426950408889634
MLA_KV_CHUNK = 512
VMEM_LIMIT = 56 * 1024 * 1024

_C_QA = 0
_C_KV = _C_QA + SWA_HEADS * SWA_HEAD_DIM
_C_CQ = _C_KV + 2 * SWA_KV_HEADS * SWA_HEAD_DIM
_C_CKVR = _C_CQ + MLA_Q_RANK
_C_GA = _C_CKVR + MLA_KV_RANK + LANES
_C_GB = _C_GA + D_MODEL
_C_END = _C_GB + D_MODEL

BF16 = jnp.bfloat16
F32 = jnp.float32


def _dot(a, b):
    return jnp.dot(a, b, preferred_element_type=F32)


def _dot_nt(a, b):
    return lax.dot_general(a, b, (((1,), (1,)), ((), ())), preferred_element_type=F32)


def _rms(x, g):
    r = lax.rsqrt(jnp.mean(x * x, axis=-1, keepdims=True) + EPS)
    return x * r * g


def _sigmoid(x):
    return 1.0 / (1.0 + jnp.exp(-x))


def _rope_lanes(x, cos, sin_signed):
    lane = lax.broadcasted_iota(jnp.int32, x.shape, 1)
    upper = (lane & (SWA_HEAD_DIM // 2)) != 0
    partner = jnp.where(upper, pltpu.roll(x, SWA_HEAD_DIM // 2, 1),
                        pltpu.roll(x, LANES - SWA_HEAD_DIM // 2, 1))
    return x * cos + partner * sin_signed


def _inproj_kernel(x_ref, g_ref, w_ref, cos_ref, sin_ref, gq_ref, wuq_ref, gkv_ref, wuk_ref,
                   wuvt_ref, qa_ref, ka_ref, va_ref, ga_ref, gb_ref, qm_ref, km_ref, vt_ref):
    xn = _rms(x_ref[...], g_ref[...]).astype(BF16)
    cos = cos_ref[...]
    sin = sin_ref[...]

    qa_scale = LOG2_E / math.sqrt(SWA_HEAD_DIM)
    qa = _dot(xn, w_ref[:, _C_QA:_C_KV])
    for j in range(SWA_HEADS // 2):
        q = (_rope_lanes(qa[:, j * LANES:(j + 1) * LANES], cos, sin) * qa_scale).astype(BF16)
        qa_ref[0, 2 * j] = q[:, :SWA_HEAD_DIM]
        qa_ref[0, 2 * j + 1] = q[:, SWA_HEAD_DIM:]
    kv = _dot(xn, w_ref[:, _C_KV:_C_CQ])
    k = _rope_lanes(kv[:, :LANES], cos, sin).astype(BF16)
    ka_ref[0, 0] = k[:, :SWA_HEAD_DIM]
    ka_ref[0, 1] = k[:, SWA_HEAD_DIM:]
    vt = kv[:, LANES:].T.astype(BF16)
    va_ref[0, 0] = vt[:SWA_HEAD_DIM]
    va_ref[0, 1] = vt[SWA_HEAD_DIM:]

    ga_ref[...] = _sigmoid(_dot(xn, w_ref[:, _C_GA:_C_GB])).astype(BF16)
    gb_ref[...] = _sigmoid(_dot(xn, w_ref[:, _C_GB:_C_END])).astype(BF16)

    cq = _rms(_dot(xn, w_ref[:, _C_CQ:_C_CKVR]), gq_ref[...]).astype(BF16)
    ckvr = _dot(xn, w_ref[:, _C_CKVR:_C_GA])
    ckv = _rms(ckvr[:, :MLA_KV_RANK], gkv_ref[...]).astype(BF16)
    kr = _rope_lanes(ckvr[:, MLA_KV_RANK:], cos, sin)[:, :MLA_ROPE_DIM].astype(BF16)
    qm_scale = LOG2_E / math.sqrt(MLA_QK_DIM)
    n_nope = MLA_HEADS * MLA_NOPE_DIM
    qn = (_dot(cq, wuq_ref[:, :n_nope]) * qm_scale).astype(BF16)
    kn = _dot(ckv, wuk_ref[...]).astype(BF16)
    qr = _dot(cq, wuq_ref[:, n_nope:])
    for h in range(MLA_HEADS):
        qm_ref[0, h, :, :MLA_NOPE_DIM] = qn[:, h * MLA_NOPE_DIM:(h + 1) * MLA_NOPE_DIM]
        km_ref[0, h, :, :MLA_NOPE_DIM] = kn[:, h * MLA_NOPE_DIM:(h + 1) * MLA_NOPE_DIM]
        km_ref[0, h, :, MLA_NOPE_DIM:] = kr
        vt_ref[0, h, 0] = _dot_nt(wuvt_ref[h], ckv).astype(BF16)
    for j in range(MLA_HEADS // 2):
        r = (_rope_lanes(qr[:, j * LANES:(j + 1) * LANES], cos, sin) * qm_scale).astype(BF16)
        qm_ref[0, 2 * j, :, MLA_NOPE_DIM:] = r[:, :MLA_ROPE_DIM]
        qm_ref[0, 2 * j + 1, :, MLA_NOPE_DIM:] = r[:, MLA_ROPE_DIM:]


def _inproj(x2, g_mix, w_cat, cos_t, sin_t, g_q, w_uq, g_kv, w_uk, w_uvt, batch, seq, tm):
    tokens = batch * seq
    nt = seq // tm
    const = lambda t: (0, 0)
    tok = lambda t: (t, 0)
    head = lambda t: (t // nt, 0, t % nt, 0)
    out_shape = (
        jax.ShapeDtypeStruct((batch, SWA_HEADS, seq, SWA_HEAD_DIM), BF16),
        jax.ShapeDtypeStruct((batch, SWA_KV_HEADS, seq, SWA_HEAD_DIM), BF16),
        jax.ShapeDtypeStruct((batch, SWA_KV_HEADS, SWA_HEAD_DIM, seq), BF16),
        jax.ShapeDtypeStruct((tokens, D_MODEL), BF16),
        jax.ShapeDtypeStruct((tokens, D_MODEL), BF16),
        jax.ShapeDtypeStruct((batch, MLA_HEADS, seq, MLA_QK_DIM), BF16),
        jax.ShapeDtypeStruct((batch, MLA_HEADS, seq, MLA_QK_DIM), BF16),
        jax.ShapeDtypeStruct((batch, MLA_HEADS, nt, MLA_V_DIM, tm), BF16),
    )
    return pl.pallas_call(
        _inproj_kernel,
        grid=(tokens // tm,),
        in_specs=[
            pl.BlockSpec((tm, D_MODEL), tok),
            pl.BlockSpec((1, D_MODEL), const),
            pl.BlockSpec((D_MODEL, _C_END), const, pipeline_mode=pl.Buffered(1)),
            pl.BlockSpec((tm, LANES), lambda t: (t % nt, 0)),
            pl.BlockSpec((tm, LANES), lambda t: (t % nt, 0)),
            pl.BlockSpec((1, MLA_Q_RANK), const),
            pl.BlockSpec(w_uq.shape, const, pipeline_mode=pl.Buffered(1)),
            pl.BlockSpec((1, MLA_KV_RANK), const),
            pl.BlockSpec(w_uk.shape, const, pipeline_mode=pl.Buffered(1)),
            pl.BlockSpec(w_uvt.shape, lambda t: (0, 0, 0), pipeline_mode=pl.Buffered(1)),
        ],
        out_specs=(
            pl.BlockSpec((1, SWA_HEADS, tm, SWA_HEAD_DIM), head),
            pl.BlockSpec((1, SWA_KV_HEADS, tm, SWA_HEAD_DIM), head),
            pl.BlockSpec((1, SWA_KV_HEADS, SWA_HEAD_DIM, tm), lambda t: (t // nt, 0, 0, t % nt)),
            pl.BlockSpec((tm, D_MODEL), tok),
            pl.BlockSpec((tm, D_MODEL), tok),
            pl.BlockSpec((1, MLA_HEADS, tm, MLA_QK_DIM), head),
            pl.BlockSpec((1, MLA_HEADS, tm, MLA_QK_DIM), head),
            pl.BlockSpec((1, MLA_HEADS, 1, MLA_V_DIM, tm), lambda t: (t // nt, 0, t % nt, 0, 0)),
        ),
        out_shape=out_shape,
        compiler_params=pltpu.CompilerParams(
            dimension_semantics=("arbitrary",), vmem_limit_bytes=VMEM_LIMIT),
        name="inproj",
    )(x2, g_mix, w_cat, cos_t, sin_t, g_q, w_uq, g_kv, w_uk, w_uvt)


def _swa_kernel(q_ref, k_ref, kp_ref, vt_ref, vtp_ref, sink_ref, o_ref, s_scr, bias_scr, *,
                n_sub):
    blk = SWA_WINDOW
    cols = SWA_GROUP * blk

    @pl.when((pl.program_id(0) == 0) & (pl.program_id(1) == 0))
    def _():
        kj = lax.broadcasted_iota(jnp.int32, (2 * blk, cols), 0)
        qi = lax.broadcasted_iota(jnp.int32, (2 * blk, cols), 1) & (blk - 1)
        dist = blk + qi - kj
        bias_scr[...] = jnp.where((dist >= 0) & (dist < SWA_WINDOW), 0.0, NEG_INF)

    key_row = lax.broadcasted_iota(jnp.int32, (2 * blk, 1), 0)
    no_prev = jnp.where((key_row < blk) & (pl.program_id(1) == 0), NEG_INF, 0.0)
    units = [(g, n) for g in range(SWA_KV_HEADS) for n in range(n_sub)]

    def scores(u, buf):
        g, n = u
        q = q_ref[0, g * SWA_GROUP:(g + 1) * SWA_GROUP, n * blk:(n + 1) * blk, :]
        k_prev = kp_ref[0, g] if n == 0 else k_ref[0, g, (n - 1) * blk:n * blk, :]
        kk = jnp.concatenate([k_prev, k_ref[0, g, n * blk:(n + 1) * blk, :]], axis=0)
        s_scr[buf] = _dot_nt(kk, q.reshape(cols, SWA_HEAD_DIM))

    def softmax_pv(u, buf):
        g, n = u
        sink = sink_ref[:, g * cols:(g + 1) * cols] * LOG2_E
        s = s_scr[buf] + bias_scr[...]
        if n == 0:
            s = s + no_prev
        m = jnp.maximum(jnp.max(s, axis=0, keepdims=True), sink)
        e = jnp.exp2(s - m)
        denom = jnp.sum(e, axis=0, keepdims=True) + jnp.exp2(sink - m)
        vt_prev = vtp_ref[0, g] if n == 0 else vt_ref[0, g, :, (n - 1) * blk:n * blk]
        vvt = jnp.concatenate([vt_prev, vt_ref[0, g, :, n * blk:(n + 1) * blk]], axis=1)
        o = (_dot(vvt, e.astype(BF16)) / denom).T
        for hh in range(SWA_GROUP):
            h = g * SWA_GROUP + hh
            o_ref[0, n * blk:(n + 1) * blk, h * SWA_HEAD_DIM:(h + 1) * SWA_HEAD_DIM] = (
                o[hh * blk:(hh + 1) * blk, :].astype(BF16))

    scores(units[0], 0)
    for idx, u in enumerate(units):
        if idx + 1 < len(units):
            scores(units[idx + 1], (idx + 1) % 2)
        softmax_pv(u, idx % 2)


def _swa(q_a, k_a, vt_a, sink_lanes, batch, seq, tq):
    n_sub = tq // SWA_WINDOW
    cur = lambda b, i: (b, 0, i, 0)
    prev = lambda b, i: (b, 0, jnp.maximum(i * n_sub - 1, 0), 0)
    return pl.pallas_call(
        functools.partial(_swa_kernel, n_sub=n_sub),
        grid=(batch, seq // tq),
        in_specs=[
            pl.BlockSpec((1, SWA_HEADS, tq, SWA_HEAD_DIM), cur),
            pl.BlockSpec((1, SWA_KV_HEADS, tq, SWA_HEAD_DIM), cur),
            pl.BlockSpec((1, SWA_KV_HEADS, SWA_WINDOW, SWA_HEAD_DIM), prev),
            pl.BlockSpec((1, SWA_KV_HEADS, SWA_HEAD_DIM, tq), lambda b, i: (b, 0, 0, i)),
            pl.BlockSpec((1, SWA_KV_HEADS, SWA_HEAD_DIM, SWA_WINDOW),
                         lambda b, i: (b, 0, 0, jnp.maximum(i * n_sub - 1, 0))),
            pl.BlockSpec(sink_lanes.shape, lambda b, i: (0, 0)),
        ],
        out_specs=pl.BlockSpec((1, tq, D_MODEL), lambda b, i: (b, i, 0)),
        out_shape=jax.ShapeDtypeStruct((batch, seq, D_MODEL), BF16),
        scratch_shapes=[pltpu.VMEM((2, 2 * SWA_WINDOW, SWA_GROUP * SWA_WINDOW), F32),
                        pltpu.VMEM((2 * SWA_WINDOW, SWA_GROUP * SWA_WINDOW), F32)],
        compiler_params=pltpu.CompilerParams(
            dimension_semantics=("arbitrary", "arbitrary"), vmem_limit_bytes=VMEM_LIMIT),
        name="swa",
    )(q_a, k_a, k_a, vt_a, vt_a, sink_lanes)


def _mla_kernel(q_ref, k_ref, vt_ref, o_ref, m_scr, l_scr, acc_scr, s_scr, qt_scr, *, tq, tk):
    assert tq == 2 * tk
    chunks = tk // MLA_KV_CHUNK
    n_q = q_ref.shape[2] // tq

    def q_block(i, carry):
        m_scr[...] = jnp.full(m_scr.shape, NEG_INF, F32)
        l_scr[...] = jnp.zeros(l_scr.shape, F32)
        acc_scr[...] = jnp.zeros(acc_scr.shape, F32)

        q_blk = q_ref[0, 0, pl.ds(pl.multiple_of(i * tq, tq), tq), :]
        q_pad = jnp.concatenate(
            [q_blk, jnp.zeros((tq, qt_scr.shape[0] - MLA_QK_DIM), BF16)], axis=1)
        qt_scr[...] = q_pad.astype(F32).T.astype(BF16)

        def scores(j, buf, q0=0):
            k = k_ref[0, 0, pl.ds(pl.multiple_of(j * tk, tk), tk), :]
            s_scr[buf, :, q0:] = _dot(k, qt_scr[:MLA_QK_DIM, q0:])

        def softmax_pv(j, buf, q0=0, key_offset=None):
            s = s_scr[buf, :, q0:]
            if key_offset is not None:
                kpos = lax.broadcasted_iota(jnp.int32, s.shape, 0) + key_offset
                qpos = lax.broadcasted_iota(jnp.int32, s.shape, 1) + q0
                s = jnp.where(kpos <= qpos, s, NEG_INF)
            m_old = m_scr[:, q0:]
            m_new = jnp.maximum(m_old, jnp.max(s, axis=0, keepdims=True))
            alpha = jnp.exp2(m_old - m_new)
            p = jnp.exp2(s - m_new)
            l_scr[:, q0:] = alpha * l_scr[:, q0:] + jnp.sum(p, axis=0, keepdims=True)
            pb = p.astype(BF16)
            pv = _dot(vt_ref[0, 0, j * chunks], pb[:MLA_KV_CHUNK])
            for c in range(1, chunks):
                pv += _dot(vt_ref[0, 0, j * chunks + c],
                           pb[c * MLA_KV_CHUNK:(c + 1) * MLA_KV_CHUNK])
            acc_scr[:, q0:] = alpha * acc_scr[:, q0:] + pv
            m_scr[:, q0:] = m_new

        scores(0, 0)

        def pair(jj, c):
            t = 2 * jj
            scores(t + 1, 1)
            softmax_pv(t, 0)
            scores(t + 2, 0)
            softmax_pv(t + 1, 1)
            return c

        lax.fori_loop(0, i, pair, 0)
        scores(2 * i + 1, 1, q0=tk)
        softmax_pv(2 * i, 0, key_offset=0)
        softmax_pv(2 * i + 1, 1, q0=tk, key_offset=tk)

        o = acc_scr[...] / l_scr[...]
        o_ref[0, pl.ds(pl.multiple_of(i * tq, tq), tq), :] = o.T.astype(o_ref.dtype)
        return carry

    lax.fori_loop(0, n_q, q_block, 0)


def _mla(q_m, k_m, vt_m, batch, seq, tq, tk):
    return pl.pallas_call(
        functools.partial(_mla_kernel, tq=tq, tk=tk),
        grid=(batch, MLA_HEADS),
        in_specs=[
            pl.BlockSpec((1, 1, seq, MLA_QK_DIM), lambda b, h: (b, h, 0, 0)),
            pl.BlockSpec((1, 1, seq, MLA_QK_DIM), lambda b, h: (b, h, 0, 0)),
            pl.BlockSpec((1, 1, seq // MLA_KV_CHUNK, MLA_V_DIM, MLA_KV_CHUNK),
                         lambda b, h: (b, h, 0, 0, 0)),
        ],
        out_specs=pl.BlockSpec((1, seq, MLA_V_DIM), lambda b, h: (b, 0, h)),
        out_shape=jax.ShapeDtypeStruct((batch, seq, D_MODEL), BF16),
        scratch_shapes=[
            pltpu.VMEM((1, tq), F32),
            pltpu.VMEM((1, tq), F32),
            pltpu.VMEM((MLA_V_DIM, tq), F32),
            pltpu.VMEM((2, tk, tq), F32),
            pltpu.VMEM((2 * LANES, tq), BF16),
        ],
        compiler_params=pltpu.CompilerParams(
            dimension_semantics=("arbitrary", "arbitrary"), vmem_limit_bytes=VMEM_LIMIT),
        name="mla",
    )(q_m, k_m, vt_m)


def _outproj_kernel(x_ref, oa_ref, ob_ref, ga_ref, gb_ref, wo_ref, gf_ref, wr_ref, br_ref,
                    h_ref, xn_ref, comb_ref):
    merged = (ga_ref[...].astype(F32) * oa_ref[...].astype(F32)
              + gb_ref[...].astype(F32) * ob_ref[...].astype(F32)).astype(BF16)
    h = x_ref[...] + _dot(merged, wo_ref[...])
    h_ref[...] = h
    xn = _rms(h, gf_ref[...])
    xn_ref[...] = xn.astype(BF16)

    x_hi = xn.astype(BF16)
    x_lo = (xn - x_hi.astype(F32)).astype(BF16)
    lg = (_dot(x_hi, wr_ref[0]) + _dot(x_lo, wr_ref[0]) + _dot(x_hi, wr_ref[1])) + br_ref[...]

    lane = lax.broadcasted_iota(jnp.int32, lg.shape, 1)
    is_group = (lane >= N_EXPERTS) & (lane < N_EXPERTS + N_GROUPS)
    gl = jnp.where(is_group, lg, NEG_INF)
    g_max = jnp.max(gl, axis=-1, keepdims=True)
    g_sum = jnp.sum(jnp.where(is_group, jnp.exp(gl - g_max), 0.0), axis=-1, keepdims=True)
    g_w = 1.0 / g_sum
    big = jnp.int32(LANES)
    g_lane = jnp.min(jnp.where(is_group & (gl == g_max), lane, big), axis=-1, keepdims=True)
    gidx = g_lane - N_EXPERTS

    sel = (lane >= gidx * EXPERTS_PER_GROUP) & (lane < (gidx + 1) * EXPERTS_PER_GROUP)
    el = jnp.where(sel, lg, NEG_INF)
    e_max = jnp.max(el, axis=-1, keepdims=True)
    e_exp = jnp.where(sel, jnp.exp(el - e_max), 0.0)
    e_prob = e_exp / jnp.sum(e_exp, axis=-1, keepdims=True)
    p1 = jnp.max(e_prob, axis=-1, keepdims=True)
    i1 = jnp.min(jnp.where(sel & (e_prob == p1), lane, big), axis=-1, keepdims=True)
    rest = sel & (lane != i1)
    p2 = jnp.max(jnp.where(rest, e_prob, -1.0), axis=-1, keepdims=True)
    i2 = jnp.min(jnp.where(rest & (e_prob == p2), lane, big), axis=-1, keepdims=True)
    top_sum = p1 + p2
    comb = jnp.where(lane == i1, p1 / top_sum, jnp.where(lane == i2, p2 / top_sum, 0.0))
    comb_ref[...] = comb * g_w


def _outproj(x2, o_a, o_b, g_a, g_b, w_o, g_ffn, w_r, b_r, tm):
    tokens = x2.shape[0]
    tok = lambda t: (t, 0)
    const = lambda t: (0, 0)
    return pl.pallas_call(
        _outproj_kernel,
        grid=(tokens // tm,),
        in_specs=[
            pl.BlockSpec((tm, D_MODEL), tok),
            pl.BlockSpec((tm, D_MODEL), tok),
            pl.BlockSpec((tm, D_MODEL), tok),
            pl.BlockSpec((tm, D_MODEL), tok),
            pl.BlockSpec((tm, D_MODEL), tok),
            pl.BlockSpec((D_MODEL, D_MODEL), const),
            pl.BlockSpec((1, D_MODEL), const),
            pl.BlockSpec((2, D_MODEL, LANES), lambda t: (0, 0, 0)),
            pl.BlockSpec((1, LANES), const),
        ],
        out_specs=(
            pl.BlockSpec((tm, D_MODEL), tok),
            pl.BlockSpec((tm, D_MODEL), tok),
            pl.BlockSpec((tm, LANES), tok),
        ),
        out_shape=(
            jax.ShapeDtypeStruct((tokens, D_MODEL), F32),
            jax.ShapeDtypeStruct((tokens, D_MODEL), BF16),
            jax.ShapeDtypeStruct((tokens, LANES), F32),
        ),
        compiler_params=pltpu.CompilerParams(
            dimension_semantics=("arbitrary",), vmem_limit_bytes=VMEM_LIMIT),
        name="outproj",
    )(x2, o_a, o_b, g_a, g_b, w_o, g_ffn, w_r, b_r)


def _moe_kernel(h_ref, xn_ref, comb_ref, wi_ref, wo_ref, o_ref):
    e = pl.program_id(1)

    @pl.when(e == 0)
    def _():
        o_ref[...] = h_ref[...]

    comb = comb_ref[...]
    lane = lax.broadcasted_iota(jnp.int32, comb.shape, 1)
    c = jnp.sum(jnp.where(lane == e, comb, 0.0), axis=-1, keepdims=True)
    hid = _dot(xn_ref[...], wi_ref[0])
    gate = hid[:, :D_EXPERT]
    up = hid[:, D_EXPERT:]
    act = (gate * _sigmoid(gate)) * up * c
    o_ref[...] += _dot(act.astype(BF16), wo_ref[0])


def _moe(h1, xn, comb, w_ei, w_eo, tm):
    tokens = h1.shape[0]
    tok = lambda t, e: (t, 0)
    return pl.pallas_call(
        _moe_kernel,
        grid=(tokens // tm, N_EXPERTS),
        in_specs=[
            pl.BlockSpec((tm, D_MODEL), tok),
            pl.BlockSpec((tm, D_MODEL), tok),
            pl.BlockSpec((tm, LANES), tok),
            pl.BlockSpec((1, D_MODEL, 2 * D_EXPERT), lambda t, e: (e, 0, 0)),
            pl.BlockSpec((1, D_EXPERT, D_MODEL), lambda t, e: (e, 0, 0)),
        ],
        out_specs=pl.BlockSpec((tm, D_MODEL), tok),
        out_shape=jax.ShapeDtypeStruct((tokens, D_MODEL), F32),
        compiler_params=pltpu.CompilerParams(
            dimension_semantics=("arbitrary", "arbitrary"), vmem_limit_bytes=VMEM_LIMIT),
        name="moe",
    )(h1, xn, comb, w_ei, w_eo)


def _ple_kernel(h_ref, p_ref, gp_ref, wg_ref, wp_ref, gfin_ref, o_ref):
    h = h_ref[...]
    gate = _sigmoid(_dot(_rms(h, gp_ref[...]).astype(BF16), wg_ref[...]))
    h = h + gate * _dot(p_ref[...].astype(BF16), wp_ref[...])
    o_ref[...] = _rms(h, gfin_ref[...])


def _ple(h2, p2, g_ple, w_pg, w_pp, g_final, tm):
    tokens = h2.shape[0]
    tok = lambda t: (t, 0)
    const = lambda t: (0, 0)
    return pl.pallas_call(
        _ple_kernel,
        grid=(tokens // tm,),
        in_specs=[
            pl.BlockSpec((tm, D_MODEL), tok),
            pl.BlockSpec((tm, PLE_DIM), tok),
            pl.BlockSpec((1, D_MODEL), const),
            pl.BlockSpec((D_MODEL, D_MODEL), const),
            pl.BlockSpec((PLE_DIM, D_MODEL), const),
            pl.BlockSpec((1, D_MODEL), const),
        ],
        out_specs=pl.BlockSpec((tm, D_MODEL), tok),
        out_shape=jax.ShapeDtypeStruct((tokens, D_MODEL), F32),
        compiler_params=pltpu.CompilerParams(
            dimension_semantics=("arbitrary",), vmem_limit_bytes=VMEM_LIMIT),
        name="ple",
    )(h2, p2, g_ple, w_pg, w_pp, g_final)


def _rope_tables(seq):
    half = SWA_HEAD_DIM // 2
    pos = jnp.arange(seq, dtype=F32)
    inv = ROPE_THETA ** (-jnp.arange(0, SWA_HEAD_DIM, 2, dtype=F32) / SWA_HEAD_DIM)
    ang = pos[:, None] * inv[None, :]
    cos, sin = jnp.cos(ang), jnp.sin(ang)
    reps = LANES // half
    cos_t = jnp.tile(cos, (1, reps))
    sin_t = jnp.tile(jnp.concatenate([-sin, sin], axis=1), (1, reps // 2))
    return cos_t, sin_t


def _layer(h2d, p2d, g_mix, w_in, sinks, g_q, w_uq, g_kv, w_ukv, w_out, g_ffn, w_rg, b_rg,
           w_re, b_re, w_ei, w_eo, g_ple, w_pg, w_pp, g_final, cos_t, sin_t, batch, seq):
    assert SWA_HEAD_DIM == MLA_ROPE_DIM, "one rotary table serves both mixers"
    kr_end = _C_CKVR + MLA_KV_RANK + MLA_ROPE_DIM
    w_cat = jnp.concatenate(
        [w_in[:, :kr_end], jnp.zeros((D_MODEL, LANES - MLA_ROPE_DIM), w_in.dtype),
         w_in[:, kr_end:]], axis=1).astype(BF16)
    assert w_cat.shape[1] == _C_END
    uq = w_uq.reshape(MLA_Q_RANK, MLA_HEADS, MLA_QK_DIM)
    w_uq_c = jnp.concatenate(
        [uq[:, :, :MLA_NOPE_DIM].reshape(MLA_Q_RANK, -1),
         uq[:, :, MLA_NOPE_DIM:].reshape(MLA_Q_RANK, -1)], axis=1).astype(BF16)
    ukv = w_ukv.reshape(MLA_KV_RANK, MLA_HEADS, MLA_NOPE_DIM + MLA_V_DIM)
    w_uk_c = ukv[:, :, :MLA_NOPE_DIM].reshape(MLA_KV_RANK, -1).astype(BF16)
    w_uvt = jnp.transpose(ukv[:, :, MLA_NOPE_DIM:], (1, 2, 0)).astype(BF16)

    q_a, k_a, v_a, g_a, g_b, q_m, k_m, v_m = _inproj(
        h2d, g_mix[None], w_cat, cos_t, sin_t, g_q[None], w_uq_c, g_kv[None], w_uk_c, w_uvt,
        batch, seq, tm=MLA_KV_CHUNK)

    sink_lanes = jnp.repeat(sinks.astype(F32), SWA_WINDOW)[None]
    o_a = _swa(q_a, k_a, v_a, sink_lanes, batch, seq, tq=512)
    o_b = _mla(q_m, k_m, v_m, batch, seq, tq=1024, tk=512)

    w_r = jnp.concatenate(
        [w_re, w_rg, jnp.zeros((D_MODEL, LANES - N_EXPERTS - N_GROUPS), F32)], axis=1)
    w_r_hi = w_r.astype(BF16)
    w_r_lo = (w_r - w_r_hi.astype(F32)).astype(BF16)
    w_r2 = jnp.stack([w_r_hi, w_r_lo])
    b_r = jnp.concatenate(
        [b_re, b_rg, jnp.zeros((LANES - N_EXPERTS - N_GROUPS,), F32)])[None]
    tokens = batch * seq
    h1, xn2, comb = _outproj(
        h2d, o_a.reshape(tokens, D_MODEL), o_b.reshape(tokens, D_MODEL), g_a, g_b,
        w_out.astype(BF16), g_ffn[None], w_r2, b_r, tm=512)

    h2 = _moe(h1, xn2, comb, w_ei.astype(BF16), w_eo.astype(BF16), tm=1024)
    return _ple(h2, p2d, g_ple[None], w_pg.astype(BF16), w_pp.astype(BF16), g_final[None], tm=512)


def kernel(x, p, g_mix, w_in, swa_sinks, mla_g_q, mla_w_uq, mla_g_kv, mla_w_ukv, w_out, g_ffn,
           w_router_group, b_router_group, w_router_expert, b_router_expert, w_expert_in,
           w_expert_out, g_ple, w_ple_gate, w_ple_proj, g_final):
    batch, seq, d = x.shape
    depth = p.shape[0]
    assert d == D_MODEL and depth == 1, "final RMSNorm is fused into the single layer"
    cos_t, sin_t = _rope_tables(seq)
    out = _layer(
        x.reshape(batch * seq, d), p[0].reshape(batch * seq, PLE_DIM), g_mix[0], w_in[0],
        swa_sinks[0], mla_g_q[0], mla_w_uq[0], mla_g_kv[0], mla_w_ukv[0], w_out[0], g_ffn[0],
        w_router_group[0], b_router_group[0], w_router_expert[0], b_router_expert[0],
        w_expert_in[0], w_expert_out[0], g_ple[0], w_ple_gate[0], w_ple_proj[0], g_final,
        cos_t, sin_t, batch, seq)
    return out.reshape(batch, seq, d)
```

```python
import functools
import math

import jax
import jax.numpy as jnp
from jax import lax
from jax.experimental import pallas as pl
from jax.experimental.pallas import tpu as pltpu

D_MODEL = 1024
PLE_DIM = 256
ROPE_THETA = 10000.0
EPS = 1e-6
NEG_INF = -1e30

SWA_HEAD_DIM = 64
SWA_HEADS = D_MODEL // SWA_HEAD_DIM
SWA_KV_HEADS = SWA_HEADS // 8
SWA_GROUP = SWA_HEADS // SWA_KV_HEADS
SWA_WINDOW = 128

MLA_NOPE_DIM = 128
MLA_ROPE_DIM = 64
MLA_V_DIM = 128
MLA_HEADS = D_MODEL // MLA_V_DIM
MLA_Q_RANK = 256
MLA_KV_RANK = 128
MLA_QK_DIM = MLA_NOPE_DIM + MLA_ROPE_DIM

N_GROUPS = 4
EXPERTS_PER_GROUP = 4
N_EXPERTS = N_GROUPS * EXPERTS_PER_GROUP
D_EXPERT = 256

LANES = 128
LOG2_E = 1.4426950408889634
MLA_KV_CHUNK = 512
VMEM_LIMIT = 56 * 1024 * 1024

_C_QA = 0
_C_KV = _C_QA + SWA_HEADS * SWA_HEAD_DIM
_C_CQ = _C_KV + 2 * SWA_KV_HEADS * SWA_HEAD_DIM
_C_CKVR = _C_CQ + MLA_Q_RANK
_C_GA = _C_CKVR + MLA_KV_RANK + LANES
_C_GB = _C_GA + D_MODEL
_C_END = _C_GB + D_MODEL

BF16 = jnp.bfloat16
F32 = jnp.float32


def _dot(a, b):
    return jnp.dot(a, b, preferred_element_type=F32)


def _dot_nt(a, b):
    return lax.dot_general(a, b, (((1,), (1,)), ((), ())), preferred_element_type=F32)


def _rms(x, g):
    r = lax.rsqrt(jnp.mean(x * x, axis=-1, keepdims=True) + EPS)
    return x * r * g


def _sigmoid(x):
    return 1.0 / (1.0 + jnp.exp(-x))


def _rope_lanes(x, cos, sin_signed):
    lane = lax.broadcasted_iota(jnp.int32, x.shape, 1)
    upper = (lane & (SWA_HEAD_DIM // 2)) != 0
    partner = jnp.where(upper, pltpu.roll(x, SWA_HEAD_DIM // 2, 1),
                        pltpu.roll(x, LANES - SWA_HEAD_DIM // 2, 1))
    return x * cos + partner * sin_signed


def _inproj_kernel(x_ref, g_ref, w_ref, cos_ref, sin_ref, gq_ref, wuq_ref, gkv_ref, wuk_ref,
                   wuvt_ref, qa_ref, ka_ref, va_ref, ga_ref, gb_ref, qm_ref, km_ref, vt_ref):
    xn = _rms(x_ref[...], g_ref[...]).astype(BF16)
    cos = cos_ref[...]
    sin = sin_ref[...]

    qa_scale = LOG2_E / math.sqrt(SWA_HEAD_DIM)
    qa = _dot(xn, w_ref[:, _C_QA:_C_KV])
    for j in range(SWA_HEADS // 2):
        q = (_rope_lanes(qa[:, j * LANES:(j + 1) * LANES], cos, sin) * qa_scale).astype(BF16)
        qa_ref[0, 2 * j] = q[:, :SWA_HEAD_DIM]
        qa_ref[0, 2 * j + 1] = q[:, SWA_HEAD_DIM:]
    kv = _dot(xn, w_ref[:, _C_KV:_C_CQ])
    k = _rope_lanes(kv[:, :LANES], cos, sin).astype(BF16)
    ka_ref[0, 0] = k[:, :SWA_HEAD_DIM]
    ka_ref[0, 1] = k[:, SWA_HEAD_DIM:]
    vt = kv[:, LANES:].T.astype(BF16)
    va_ref[0, 0] = vt[:SWA_HEAD_DIM]
    va_ref[0, 1] = vt[SWA_HEAD_DIM:]

    ga_ref[...] = _sigmoid(_dot(xn, w_ref[:, _C_GA:_C_GB])).astype(BF16)
    gb_ref[...] = _sigmoid(_dot(xn, w_ref[:, _C_GB:_C_END])).astype(BF16)

    cq = _rms(_dot(xn, w_ref[:, _C_CQ:_C_CKVR]), gq_ref[...]).astype(BF16)
    ckvr = _dot(xn, w_ref[:, _C_CKVR:_C_GA])
    ckv = _rms(ckvr[:, :MLA_KV_RANK], gkv_ref[...]).astype(BF16)
    kr = _rope_lanes(ckvr[:, MLA_KV_RANK:], cos, sin)[:, :MLA_ROPE_DIM].astype(BF16)
    qm_scale = LOG2_E / math.sqrt(MLA_QK_DIM)
    n_nope = MLA_HEADS * MLA_NOPE_DIM
    qn = (_dot(cq, wuq_ref[:, :n_nope]) * qm_scale).astype(BF16)
    kn = _dot(ckv, wuk_ref[...]).astype(BF16)
    qr = _dot(cq, wuq_ref[:, n_nope:])
    for h in range(MLA_HEADS):
        qm_ref[0, h, :, :MLA_NOPE_DIM] = qn[:, h * MLA_NOPE_DIM:(h + 1) * MLA_NOPE_DIM]
        km_ref[0, h, :, :MLA_NOPE_DIM] = kn[:, h * MLA_NOPE_DIM:(h + 1) * MLA_NOPE_DIM]
        km_ref[0, h, :, MLA_NOPE_DIM:] = kr
        vt_ref[0, h, 0] = _dot_nt(wuvt_ref[h], ckv).astype(BF16)
    for j in range(MLA_HEADS // 2):
        r = (_rope_lanes(qr[:, j * LANES:(j + 1) * LANES], cos, sin) * qm_scale).astype(BF16)
        qm_ref[0, 2 * j, :, MLA_NOPE_DIM:] = r[:, :MLA_ROPE_DIM]
        qm_ref[0, 2 * j + 1, :, MLA_NOPE_DIM:] = r[:, MLA_ROPE_DIM:]


def _inproj(x2, g_mix, w_cat, cos_t, sin_t, g_q, w_uq, g_kv, w_uk, w_uvt, batch, seq, tm):
    tokens = batch * seq
    nt = seq // tm
    const = lambda t: (0, 0)
    tok = lambda t: (t, 0)
    head = lambda t: (t // nt, 0, t % nt, 0)
    out_shape = (
        jax.ShapeDtypeStruct((batch, SWA_HEADS, seq, SWA_HEAD_DIM), BF16),
        jax.ShapeDtypeStruct((batch, SWA_KV_HEADS, seq, SWA_HEAD_DIM), BF16),
        jax.ShapeDtypeStruct((batch, SWA_KV_HEADS, SWA_HEAD_DIM, seq), BF16),
        jax.ShapeDtypeStruct((tokens, D_MODEL), BF16),
        jax.ShapeDtypeStruct((tokens, D_MODEL), BF16),
        jax.ShapeDtypeStruct((batch, MLA_HEADS, seq, MLA_QK_DIM), BF16),
        jax.ShapeDtypeStruct((batch, MLA_HEADS, seq, MLA_QK_DIM), BF16),
        jax.ShapeDtypeStruct((batch, MLA_HEADS, nt, MLA_V_DIM, tm), BF16),
    )
    return pl.pallas_call(
        _inproj_kernel,
        grid=(tokens // tm,),
        in_specs=[
            pl.BlockSpec((tm, D_MODEL), tok),
            pl.BlockSpec((1, D_MODEL), const),
            pl.BlockSpec((D_MODEL, _C_END), const, pipeline_mode=pl.Buffered(1)),
            pl.BlockSpec((tm, LANES), lambda t: (t % nt, 0)),
            pl.BlockSpec((tm, LANES), lambda t: (t % nt, 0)),
            pl.BlockSpec((1, MLA_Q_RANK), const),
            pl.BlockSpec(w_uq.shape, const, pipeline_mode=pl.Buffered(1)),
            pl.BlockSpec((1, MLA_KV_RANK), const),
            pl.BlockSpec(w_uk.shape, const, pipeline_mode=pl.Buffered(1)),
            pl.BlockSpec(w_uvt.shape, lambda t: (0, 0, 0), pipeline_mode=pl.Buffered(1)),
        ],
        out_specs=(
            pl.BlockSpec((1, SWA_HEADS, tm, SWA_HEAD_DIM), head),
            pl.BlockSpec((1, SWA_KV_HEADS, tm, SWA_HEAD_DIM), head),
            pl.BlockSpec((1, SWA_KV_HEADS, SWA_HEAD_DIM, tm), lambda t: (t // nt, 0, 0, t % nt)),
            pl.BlockSpec((tm, D_MODEL), tok),
            pl.BlockSpec((tm, D_MODEL), tok),
            pl.BlockSpec((1, MLA_HEADS, tm, MLA_QK_DIM), head),
            pl.BlockSpec((1, MLA_HEADS, tm, MLA_QK_DIM), head),
            pl.BlockSpec((1, MLA_HEADS, 1, MLA_V_DIM, tm), lambda t: (t // nt, 0, t % nt, 0, 0)),
        ),
        out_shape=out_shape,
        compiler_params=pltpu.CompilerParams(
            dimension_semantics=("arbitrary",), vmem_limit_bytes=VMEM_LIMIT),
        name="inproj",
    )(x2, g_mix, w_cat, cos_t, sin_t, g_q, w_uq, g_kv, w_uk, w_uvt)


def _swa_kernel(q_ref, k_ref, kp_ref, vt_ref, vtp_ref, sink_ref, o_ref, s_scr, bias_scr, *,
                n_sub):
    blk = SWA_WINDOW
    cols = SWA_GROUP * blk

    @pl.when((pl.program_id(0) == 0) & (pl.program_id(1) == 0))
    def _():
        kj = lax.broadcasted_iota(jnp.int32, (2 * blk, cols), 0)
        qi = lax.broadcasted_iota(jnp.int32, (2 * blk, cols), 1) & (blk - 1)
        dist = blk + qi - kj
        bias_scr[...] = jnp.where((dist >= 0) & (dist < SWA_WINDOW), 0.0, NEG_INF)

    key_row = lax.broadcasted_iota(jnp.int32, (2 * blk, 1), 0)
    no_prev = jnp.where((key_row < blk) & (pl.program_id(1) == 0), NEG_INF, 0.0)
    units = [(g, n) for g in range(SWA_KV_HEADS) for n in range(n_sub)]

    def scores(u, buf):
        g, n = u
        q = q_ref[0, g * SWA_GROUP:(g + 1) * SWA_GROUP, n * blk:(n + 1) * blk, :]
        k_prev = kp_ref[0, g] if n == 0 else k_ref[0, g, (n - 1) * blk:n * blk, :]
        kk = jnp.concatenate([k_prev, k_ref[0, g, n * blk:(n + 1) * blk, :]], axis=0)
        s_scr[buf] = _dot_nt(kk, q.reshape(cols, SWA_HEAD_DIM))

    def softmax_pv(u, buf):
        g, n = u
        sink = sink_ref[:, g * cols:(g + 1) * cols] * LOG2_E
        s = s_scr[buf] + bias_scr[...]
        if n == 0:
            s = s + no_prev
        m = jnp.maximum(jnp.max(s, axis=0, keepdims=True), sink)
        e = jnp.exp2(s - m)
        denom = jnp.sum(e, axis=0, keepdims=True) + jnp.exp2(sink - m)
        vt_prev = vtp_ref[0, g] if n == 0 else vt_ref[0, g, :, (n - 1) * blk:n * blk]
        vvt = jnp.concatenate([vt_prev, vt_ref[0, g, :, n * blk:(n + 1) * blk]], axis=1)
        o = (_dot(vvt, e.astype(BF16)) / denom).T
        for hh in range(SWA_GROUP):
            h = g * SWA_GROUP + hh
            o_ref[0, n * blk:(n + 1) * blk, h * SWA_HEAD_DIM:(h + 1) * SWA_HEAD_DIM] = (
                o[hh * blk:(hh + 1) * blk, :].astype(BF16))

    scores(units[0], 0)
    for idx, u in enumerate(units):
        if idx + 1 < len(units):
            scores(units[idx + 1], (idx + 1) % 2)
        softmax_pv(u, idx % 2)


def _swa(q_a, k_a, vt_a, sink_lanes, batch, seq, tq):
    n_sub = tq // SWA_WINDOW
    cur = lambda b, i: (b, 0, i, 0)
    prev = lambda b, i: (b, 0, jnp.maximum(i * n_sub - 1, 0), 0)
    return pl.pallas_call(
        functools.partial(_swa_kernel, n_sub=n_sub),
        grid=(batch, seq // tq),
        in_specs=[
            pl.BlockSpec((1, SWA_HEADS, tq, SWA_HEAD_DIM), cur),
            pl.BlockSpec((1, SWA_KV_HEADS, tq, SWA_HEAD_DIM), cur),
            pl.BlockSpec((1, SWA_KV_HEADS, SWA_WINDOW, SWA_HEAD_DIM), prev),
            pl.BlockSpec((1, SWA_KV_HEADS, SWA_HEAD_DIM, tq), lambda b, i: (b, 0, 0, i)),
            pl.BlockSpec((1, SWA_KV_HEADS, SWA_HEAD_DIM, SWA_WINDOW),
                         lambda b, i: (b, 0, 0, jnp.maximum(i * n_sub - 1, 0))),
            pl.BlockSpec(sink_lanes.shape, lambda b, i: (0, 0)),
        ],
        out_specs=pl.BlockSpec((1, tq, D_MODEL), lambda b, i: (b, i, 0)),
        out_shape=jax.ShapeDtypeStruct((batch, seq, D_MODEL), BF16),
        scratch_shapes=[pltpu.VMEM((2, 2 * SWA_WINDOW, SWA_GROUP * SWA_WINDOW), F32),
                        pltpu.VMEM((2 * SWA_WINDOW, SWA_GROUP * SWA_WINDOW), F32)],
        compiler_params=pltpu.CompilerParams(
            dimension_semantics=("arbitrary", "arbitrary"), vmem_limit_bytes=VMEM_LIMIT),
        name="swa",
    )(q_a, k_a, k_a, vt_a, vt_a, sink_lanes)


def _mla_kernel(q_ref, k_ref, vt_ref, o_ref, m_scr, l_scr, acc_scr, s_scr, qt_scr, *, tq, tk):
    assert tq == 2 * tk
    chunks = tk // MLA_KV_CHUNK
    n_q = q_ref.shape[2] // tq

    def q_block(i, carry):
        m_scr[...] = jnp.full(m_scr.shape, NEG_INF, F32)
        l_scr[...] = jnp.zeros(l_scr.shape, F32)
        acc_scr[...] = jnp.zeros(acc_scr.shape, F32)

        q_blk = q_ref[0, 0, pl.ds(pl.multiple_of(i * tq, tq), tq), :]
        q_pad = jnp.concatenate(
            [q_blk, jnp.zeros((tq, qt_scr.shape[0] - MLA_QK_DIM), BF16)], axis=1)
        qt_scr[...] = q_pad.astype(F32).T.astype(BF16)

        def scores(j, buf, q0=0):
            k = k_ref[0, 0, pl.ds(pl.multiple_of(j * tk, tk), tk), :]
            s_scr[buf, :, q0:] = _dot(k, qt_scr[:MLA_QK_DIM, q0:])

        def softmax_pv(j, buf, q0=0, key_offset=None):
            s = s_scr[buf, :, q0:]
            if key_offset is not None:
                kpos = lax.broadcasted_iota(jnp.int32, s.shape, 0) + key_offset
                qpos = lax.broadcasted_iota(jnp.int32, s.shape, 1) + q0
                s = jnp.where(kpos <= qpos, s, NEG_INF)
            m_old = m_scr[:, q0:]
            m_new = jnp.maximum(m_old, jnp.max(s, axis=0, keepdims=True))
            alpha = jnp.exp2(m_old - m_new)
            p = jnp.exp2(s - m_new)
            l_scr[:, q0:] = alpha * l_scr[:, q0:] + jnp.sum(p, axis=0, keepdims=True)
            pb = p.astype(BF16)
            pv = _dot(vt_ref[0, 0, j * chunks], pb[:MLA_KV_CHUNK])
            for c in range(1, chunks):
                pv += _dot(vt_ref[0, 0, j * chunks + c],
                           pb[c * MLA_KV_CHUNK:(c + 1) * MLA_KV_CHUNK])
            acc_scr[:, q0:] = alpha * acc_scr[:, q0:] + pv
            m_scr[:, q0:] = m_new

        scores(0, 0)

        def pair(jj, c):
            t = 2 * jj
            scores(t + 1, 1)
            softmax_pv(t, 0)
            scores(t + 2, 0)
            softmax_pv(t + 1, 1)
            return c

        lax.fori_loop(0, i, pair, 0)
        scores(2 * i + 1, 1, q0=tk)
        softmax_pv(2 * i, 0, key_offset=0)
        softmax_pv(2 * i + 1, 1, q0=tk, key_offset=tk)

        o = acc_scr[...] / l_scr[...]
        o_ref[0, pl.ds(pl.multiple_of(i * tq, tq), tq), :] = o.T.astype(o_ref.dtype)
        return carry

    lax.fori_loop(0, n_q, q_block, 0)


def _mla(q_m, k_m, vt_m, batch, seq, tq, tk):
    return pl.pallas_call(
        functools.partial(_mla_kernel, tq=tq, tk=tk),
        grid=(batch, MLA_HEADS),
        in_specs=[
            pl.BlockSpec((1, 1, seq, MLA_QK_DIM), lambda b, h: (b, h, 0, 0)),
            pl.BlockSpec((1, 1, seq, MLA_QK_DIM), lambda b, h: (b, h, 0, 0)),
            pl.BlockSpec((1, 1, seq // MLA_KV_CHUNK, MLA_V_DIM, MLA_KV_CHUNK),
                         lambda b, h: (b, h, 0, 0, 0)),
        ],
        out_specs=pl.BlockSpec((1, seq, MLA_V_DIM), lambda b, h: (b, 0, h)),
        out_shape=jax.ShapeDtypeStruct((batch, seq, D_MODEL), BF16),
        scratch_shapes=[
            pltpu.VMEM((1, tq), F32),
            pltpu.VMEM((1, tq), F32),
            pltpu.VMEM((MLA_V_DIM, tq), F32),
            pltpu.VMEM((2, tk, tq), F32),
            pltpu.VMEM((2 * LANES, tq), BF16),
        ],
        compiler_params=pltpu.CompilerParams(
            dimension_semantics=("arbitrary", "arbitrary"), vmem_limit_bytes=VMEM_LIMIT),
        name="mla",
    )(q_m, k_m, vt_m)


def _outproj_kernel(x_ref, oa_ref, ob_ref, ga_ref, gb_ref, wo_ref, gf_ref, wr_ref, br_ref,
                    h_ref, xn_ref, comb_ref):
    merged = (ga_ref[...].astype(F32) * oa_ref[...].astype(F32)
              + gb_ref[...].astype(F32) * ob_ref[...].astype(F32)).astype(BF16)
    h = x_ref[...] + _dot(merged, wo_ref[...])
    h_ref[...] = h
    xn = _rms(h, gf_ref[...])
    xn_ref[...] = xn.astype(BF16)

    x_hi = xn.astype(BF16)
    x_lo = (xn - x_hi.astype(F32)).astype(BF16)
    lg = (_dot(x_hi, wr_ref[0]) + _dot(x_lo, wr_ref[0]) + _dot(x_hi, wr_ref[1])) + br_ref[...]

    lane = lax.broadcasted_iota(jnp.int32, lg.shape, 1)
    is_group = (lane >= N_EXPERTS) & (lane < N_EXPERTS + N_GROUPS)
    gl = jnp.where(is_group, lg, NEG_INF)
    g_max = jnp.max(gl, axis=-1, keepdims=True)
    g_sum = jnp.sum(jnp.where(is_group, jnp.exp(gl - g_max), 0.0), axis=-1, keepdims=True)
    g_w = 1.0 / g_sum
    big = jnp.int32(LANES)
    g_lane = jnp.min(jnp.where(is_group & (gl == g_max), lane, big), axis=-1, keepdims=True)
    gidx = g_lane - N_EXPERTS

    sel = (lane >= gidx * EXPERTS_PER_GROUP) & (lane < (gidx + 1) * EXPERTS_PER_GROUP)
    el = jnp.where(sel, lg, NEG_INF)
    e_max = jnp.max(el, axis=-1, keepdims=True)
    e_exp = jnp.where(sel, jnp.exp(el - e_max), 0.0)
    e_prob = e_exp / jnp.sum(e_exp, axis=-1, keepdims=True)
    p1 = jnp.max(e_prob, axis=-1, keepdims=True)
    i1 = jnp.min(jnp.where(sel & (e_prob == p1), lane, big), axis=-1, keepdims=True)
    rest = sel & (lane != i1)
    p2 = jnp.max(jnp.where(rest, e_prob, -1.0), axis=-1, keepdims=True)
    i2 = jnp.min(jnp.where(rest & (e_prob == p2), lane, big), axis=-1, keepdims=True)
    top_sum = p1 + p2
    comb = jnp.where(lane == i1, p1 / top_sum, jnp.where(lane == i2, p2 / top_sum, 0.0))
    comb_ref[...] = jnp.where(lane == N_EXPERTS, gidx.astype(F32), comb * g_w)


def _outproj(x2, o_a, o_b, g_a, g_b, w_o, g_ffn, w_r, b_r, tm):
    tokens = x2.shape[0]
    tok = lambda t: (t, 0)
    const = lambda t: (0, 0)
    return pl.pallas_call(
        _outproj_kernel,
        grid=(tokens // tm,),
        in_specs=[
            pl.BlockSpec((tm, D_MODEL), tok),
            pl.BlockSpec((tm, D_MODEL), tok),
            pl.BlockSpec((tm, D_MODEL), tok),
            pl.BlockSpec((tm, D_MODEL), tok),
            pl.BlockSpec((tm, D_MODEL), tok),
            pl.BlockSpec((D_MODEL, D_MODEL), const),
            pl.BlockSpec((1, D_MODEL), const),
            pl.BlockSpec((2, D_MODEL, LANES), lambda t: (0, 0, 0)),
            pl.BlockSpec((1, LANES), const),
        ],
        out_specs=(
            pl.BlockSpec((tm, D_MODEL), tok),
            pl.BlockSpec((tm, D_MODEL), tok),
            pl.BlockSpec((tm, LANES), tok),
        ),
        out_shape=(
            jax.ShapeDtypeStruct((tokens, D_MODEL), F32),
            jax.ShapeDtypeStruct((tokens, D_MODEL), BF16),
            jax.ShapeDtypeStruct((tokens, LANES), F32),
        ),
        compiler_params=pltpu.CompilerParams(
            dimension_semantics=("arbitrary",), vmem_limit_bytes=VMEM_LIMIT),
        name="outproj",
    )(x2, o_a, o_b, g_a, g_b, w_o, g_ffn, w_r, b_r)


def _moe_kernel(h_ref, xn_ref, comb_ref, wi_ref, wo_ref, o_ref, *, chunk):
    tm = h_ref.shape[0]
    o_ref[...] = h_ref[...]
    xn = xn_ref[...]
    comb = comb_ref[...]
    comb_hi = comb.astype(BF16)
    comb_lo = (comb - comb_hi.astype(F32)).astype(BF16)
    comb_hl = jnp.concatenate([comb_hi, comb_lo], axis=1)
    gid_col = comb[:, N_EXPERTS:N_EXPERTS + 1]
    gid_row = comb.T[N_EXPERTS:N_EXPERTS + 1, :]
    member_col = gid_col == lax.broadcasted_iota(jnp.int32, (tm, LANES), 1).astype(F32)
    member_row = gid_row == lax.broadcasted_iota(jnp.int32, (8, tm), 0).astype(F32)
    r_i = lax.broadcasted_iota(jnp.int32, (tm, tm), 0)
    c_i = lax.broadcasted_iota(jnp.int32, (tm, tm), 1)
    ranks_row = _dot(member_row.astype(BF16), (r_i < c_i).astype(BF16))
    ranks_col = _dot((r_i > c_i).astype(BF16), member_col.astype(BF16))
    pad = 2 * LANES
    never = -4.0 * tm
    slot_row = lax.broadcasted_iota(jnp.int32, (chunk, tm), 0).astype(F32)
    slot_col = lax.broadcasted_iota(jnp.int32, (tm, pad), 1).astype(F32)
    slot_col = jnp.where(slot_col < float(chunk), slot_col, never)

    for g in range(N_GROUPS):
        in_row = member_row[g:g + 1]
        rank_row = jnp.where(in_row, ranks_row[g:g + 1], -2.0 * never)
        rank_col = jnp.where(member_col[:, g:g + 1], ranks_col[:, g:g + 1], -2.0 * never)
        count = jnp.sum(in_row.astype(jnp.int32))
        n_chunks = (count + (chunk - 1)) // chunk

        def chunk_body(c, carry):
            base = (c * chunk).astype(F32)
            gather = jnp.where(rank_row - base == slot_row, 1.0, 0.0).astype(BF16)
            scatter = jnp.where(rank_col - base == slot_col, 1.0, 0.0).astype(BF16)
            xc = _dot(gather, xn).astype(BF16)
            wc = _dot(gather, comb_hl)
            wc = wc[:, :LANES] + wc[:, LANES:]
            yc = jnp.zeros((chunk, D_MODEL), F32)
            for j in range(EXPERTS_PER_GROUP):
                e = g * EXPERTS_PER_GROUP + j
                hid = _dot(xc, wi_ref[e])
                gate = hid[:, :D_EXPERT]
                act = (gate * _sigmoid(gate)) * hid[:, D_EXPERT:] * wc[:, e:e + 1]
                yc = yc + _dot(act.astype(BF16), wo_ref[e])
            yc = jnp.concatenate(
                [yc.astype(BF16), jnp.zeros((pad - chunk, D_MODEL), BF16)], axis=0)
            o_ref[...] += _dot(scatter, yc)
            return carry

        lax.fori_loop(0, n_chunks, chunk_body, 0)


def _moe(h1, xn, comb, w_ei, w_eo, tm, chunk):
    tokens = h1.shape[0]
    tok = lambda t: (t, 0)
    return pl.pallas_call(
        functools.partial(_moe_kernel, chunk=chunk),
        grid=(tokens // tm,),
        in_specs=[
            pl.BlockSpec((tm, D_MODEL), tok),
            pl.BlockSpec((tm, D_MODEL), tok),
            pl.BlockSpec((tm, LANES), tok),
            pl.BlockSpec(w_ei.shape, lambda t: (0, 0, 0), pipeline_mode=pl.Buffered(1)),
            pl.BlockSpec(w_eo.shape, lambda t: (0, 0, 0), pipeline_mode=pl.Buffered(1)),
        ],
        out_specs=pl.BlockSpec((tm, D_MODEL), tok),
        out_shape=jax.ShapeDtypeStruct((tokens, D_MODEL), F32),
        compiler_params=pltpu.CompilerParams(
            dimension_semantics=("arbitrary",), vmem_limit_bytes=VMEM_LIMIT),
        name="moe",
    )(h1, xn, comb, w_ei, w_eo)


def _ple_kernel(h_ref, p_ref, gp_ref, wg_ref, wp_ref, gfin_ref, o_ref):
    h = h_ref[...]
    gate = _sigmoid(_dot(_rms(h, gp_ref[...]).astype(BF16), wg_ref[...]))
    h = h + gate * _dot(p_ref[...].astype(BF16), wp_ref[...])
    o_ref[...] = _rms(h, gfin_ref[...])


def _ple(h2, p2, g_ple, w_pg, w_pp, g_final, tm):
    tokens = h2.shape[0]
    tok = lambda t: (t, 0)
    const = lambda t: (0, 0)
    return pl.pallas_call(
        _ple_kernel,
        grid=(tokens // tm,),
        in_specs=[
            pl.BlockSpec((tm, D_MODEL), tok),
            pl.BlockSpec((tm, PLE_DIM), tok),
            pl.BlockSpec((1, D_MODEL), const),
            pl.BlockSpec((D_MODEL, D_MODEL), const),
            pl.BlockSpec((PLE_DIM, D_MODEL), const),
            pl.BlockSpec((1, D_MODEL), const),
        ],
        out_specs=pl.BlockSpec((tm, D_MODEL), tok),
        out_shape=jax.ShapeDtypeStruct((tokens, D_MODEL), F32),
        compiler_params=pltpu.CompilerParams(
            dimension_semantics=("arbitrary",), vmem_limit_bytes=VMEM_LIMIT),
        name="ple",
    )(h2, p2, g_ple, w_pg, w_pp, g_final)


def _rope_tables(seq):
    half = SWA_HEAD_DIM // 2
    pos = jnp.arange(seq, dtype=F32)
    inv = ROPE_THETA ** (-jnp.arange(0, SWA_HEAD_DIM, 2, dtype=F32) / SWA_HEAD_DIM)
    ang = pos[:, None] * inv[None, :]
    cos, sin = jnp.cos(ang), jnp.sin(ang)
    reps = LANES // half
    cos_t = jnp.tile(cos, (1, reps))
    sin_t = jnp.tile(jnp.concatenate([-sin, sin], axis=1), (1, reps // 2))
    return cos_t, sin_t


def _layer(h2d, p2d, g_mix, w_in, sinks, g_q, w_uq, g_kv, w_ukv, w_out, g_ffn, w_rg, b_rg,
           w_re, b_re, w_ei, w_eo, g_ple, w_pg, w_pp, g_final, cos_t, sin_t, batch, seq):
    assert SWA_HEAD_DIM == MLA_ROPE_DIM, "one rotary table serves both mixers"
    kr_end = _C_CKVR + MLA_KV_RANK + MLA_ROPE_DIM
    w_cat = jnp.concatenate(
        [w_in[:, :kr_end], jnp.zeros((D_MODEL, LANES - MLA_ROPE_DIM), w_in.dtype),
         w_in[:, kr_end:]], axis=1).astype(BF16)
    assert w_cat.shape[1] == _C_END
    uq = w_uq.reshape(MLA_Q_RANK, MLA_HEADS, MLA_QK_DIM)
    w_uq_c = jnp.concatenate(
        [uq[:, :, :MLA_NOPE_DIM].reshape(MLA_Q_RANK, -1),
         uq[:, :, MLA_NOPE_DIM:].reshape(MLA_Q_RANK, -1)], axis=1).astype(BF16)
    ukv = w_ukv.reshape(MLA_KV_RANK, MLA_HEADS, MLA_NOPE_DIM + MLA_V_DIM)
    w_uk_c = ukv[:, :, :MLA_NOPE_DIM].reshape(MLA_KV_RANK, -1).astype(BF16)
    w_uvt = jnp.transpose(ukv[:, :, MLA_NOPE_DIM:], (1, 2, 0)).astype(BF16)

    q_a, k_a, v_a, g_a, g_b, q_m, k_m, v_m = _inproj(
        h2d, g_mix[None], w_cat, cos_t, sin_t, g_q[None], w_uq_c, g_kv[None], w_uk_c, w_uvt,
        batch, seq, tm=MLA_KV_CHUNK)

    sink_lanes = jnp.repeat(sinks.astype(F32), SWA_WINDOW)[None]
    o_a = _swa(q_a, k_a, v_a, sink_lanes, batch, seq, tq=512)
    o_b = _mla(q_m, k_m, v_m, batch, seq, tq=1024, tk=512)

    w_r = jnp.concatenate(
        [w_re, w_rg, jnp.zeros((D_MODEL, LANES - N_EXPERTS - N_GROUPS), F32)], axis=1)
    w_r_hi = w_r.astype(BF16)
    w_r_lo = (w_r - w_r_hi.astype(F32)).astype(BF16)
    w_r2 = jnp.stack([w_r_hi, w_r_lo])
    b_r = jnp.concatenate(
        [b_re, b_rg, jnp.zeros((LANES - N_EXPERTS - N_GROUPS,), F32)])[None]
    tokens = batch * seq
    h1, xn2, comb = _outproj(
        h2d, o_a.reshape(tokens, D_MODEL), o_b.reshape(tokens, D_MODEL), g_a, g_b,
        w_out.astype(BF16), g_ffn[None], w_r2, b_r, tm=512)

    h2 = _moe(h1, xn2, comb, w_ei.astype(BF16), w_eo.astype(BF16), tm=512, chunk=160)
    return _ple(h2, p2d, g_ple[None], w_pg.astype(BF16), w_pp.astype(BF16), g_final[None], tm=512)


def kernel(x, p, g_mix, w_in, swa_sinks, mla_g_q, mla_w_uq, mla_g_kv, mla_w_ukv, w_out, g_ffn,
           w_router_group, b_router_group, w_router_expert, b_router_expert, w_expert_in,
           w_expert_out, g_ple, w_ple_gate, w_ple_proj, g_final):
    batch, seq, d = x.shape
    depth = p.shape[0]
    assert d == D_MODEL and depth == 1, "final RMSNorm is fused into the single layer"
    cos_t, sin_t = _rope_tables(seq)
    out = _layer(
        x.reshape(batch * seq, d), p[0].reshape(batch * seq, PLE_DIM), g_mix[0], w_in[0],
        swa_sinks[0], mla_g_q[0], mla_w_uq[0], mla_g_kv[0], mla_w_ukv[0], w_out[0], g_ffn[0],
        w_router_group[0], b_router_group[0], w_router_expert[0], b_router_expert[0],
        w_expert_in[0], w_expert_out[0], g_ple[0], w_ple_gate[0], w_ple_proj[0], g_final,
        cos_t, sin_t, batch, seq)
    return out.reshape(batch, seq, d)
```

```python
import functools
import math

import jax
import jax.numpy as jnp
from jax import lax
from jax.experimental import pallas as pl
from jax.experimental.pallas import tpu as pltpu

D_MODEL = 1024
PLE_DIM = 256
ROPE_THETA = 10000.0
EPS = 1e-6
NEG_INF = -1e30

SWA_HEAD_DIM = 64
SWA_HEADS = D_MODEL // SWA_HEAD_DIM
SWA_KV_HEADS = SWA_HEADS // 8
SWA_GROUP = SWA_HEADS // SWA_KV_HEADS
SWA_WINDOW = 128

MLA_NOPE_DIM = 128
MLA_ROPE_DIM = 64
MLA_V_DIM = 128
MLA_HEADS = D_MODEL // MLA_V_DIM
MLA_Q_RANK = 256
MLA_KV_RANK = 128
MLA_QK_DIM = MLA_NOPE_DIM + MLA_ROPE_DIM

N_GROUPS = 4
EXPERTS_PER_GROUP = 4
N_EXPERTS = N_GROUPS * EXPERTS_PER_GROUP
D_EXPERT = 256

LANES = 128
LOG2_E = 1.4426950408889634
MLA_KV_CHUNK = 512
MLA_V_AUG = MLA_V_DIM + 16
VMEM_LIMIT = 56 * 1024 * 1024

_C_QA = 0
_C_KV = _C_QA + SWA_HEADS * SWA_HEAD_DIM
_C_CQ = _C_KV + 2 * SWA_KV_HEADS * SWA_HEAD_DIM
_C_CKVR = _C_CQ + MLA_Q_RANK
_C_GA = _C_CKVR + MLA_KV_RANK + LANES
_C_GB = _C_GA + D_MODEL
_C_END = _C_GB + D_MODEL

BF16 = jnp.bfloat16
F32 = jnp.float32


def _dot(a, b):
    return jnp.dot(a, b, preferred_element_type=F32)


def _dot_nt(a, b):
    return lax.dot_general(a, b, (((1,), (1,)), ((), ())), preferred_element_type=F32)


def _rms(x, g):
    r = lax.rsqrt(jnp.mean(x * x, axis=-1, keepdims=True) + EPS)
    return x * r * g


def _sigmoid(x):
    return 1.0 / (1.0 + jnp.exp(-x))


def _rope_lanes(x, cos, sin_signed):
    lane = lax.broadcasted_iota(jnp.int32, x.shape, 1)
    upper = (lane & (SWA_HEAD_DIM // 2)) != 0
    partner = jnp.where(upper, pltpu.roll(x, SWA_HEAD_DIM // 2, 1),
                        pltpu.roll(x, LANES - SWA_HEAD_DIM // 2, 1))
    return x * cos + partner * sin_signed


def _inproj_kernel(x_ref, g_ref, w_ref, cos_ref, sin_ref, gq_ref, wuq_ref, gkv_ref, wuk_ref,
                   wuvt_ref, qa_ref, ka_ref, va_ref, ga_ref, gb_ref, qm_ref, km_ref, vt_ref):
    xn = _rms(x_ref[...], g_ref[...]).astype(BF16)
    cos = cos_ref[...]
    sin = sin_ref[...]

    qa_scale = LOG2_E / math.sqrt(SWA_HEAD_DIM)
    qa = _dot(xn, w_ref[:, _C_QA:_C_KV])
    for j in range(SWA_HEADS // 2):
        q = (_rope_lanes(qa[:, j * LANES:(j + 1) * LANES], cos, sin) * qa_scale).astype(BF16)
        qa_ref[0, 2 * j] = q[:, :SWA_HEAD_DIM]
        qa_ref[0, 2 * j + 1] = q[:, SWA_HEAD_DIM:]
    kv = _dot(xn, w_ref[:, _C_KV:_C_CQ])
    k = _rope_lanes(kv[:, :LANES], cos, sin).astype(BF16)
    ka_ref[0, 0] = k[:, :SWA_HEAD_DIM]
    ka_ref[0, 1] = k[:, SWA_HEAD_DIM:]
    vt = kv[:, LANES:].T.astype(BF16)
    va_ref[0, 0] = vt[:SWA_HEAD_DIM]
    va_ref[0, 1] = vt[SWA_HEAD_DIM:]

    ga_ref[...] = _sigmoid(_dot(xn, w_ref[:, _C_GA:_C_GB])).astype(BF16)
    gb_ref[...] = _sigmoid(_dot(xn, w_ref[:, _C_GB:_C_END])).astype(BF16)

    cq = _rms(_dot(xn, w_ref[:, _C_CQ:_C_CKVR]), gq_ref[...]).astype(BF16)
    ckvr = _dot(xn, w_ref[:, _C_CKVR:_C_GA])
    ckv = _rms(ckvr[:, :MLA_KV_RANK], gkv_ref[...]).astype(BF16)
    kr = _rope_lanes(ckvr[:, MLA_KV_RANK:], cos, sin)[:, :MLA_ROPE_DIM].astype(BF16)
    qm_scale = LOG2_E / math.sqrt(MLA_QK_DIM)
    n_nope = MLA_HEADS * MLA_NOPE_DIM
    qn = (_dot(cq, wuq_ref[:, :n_nope]) * qm_scale).astype(BF16)
    kn = _dot(ckv, wuk_ref[...]).astype(BF16)
    qr = _dot(cq, wuq_ref[:, n_nope:])
    for h in range(MLA_HEADS):
        qm_ref[0, h, :, :MLA_NOPE_DIM] = qn[:, h * MLA_NOPE_DIM:(h + 1) * MLA_NOPE_DIM]
        km_ref[0, h, :, :MLA_NOPE_DIM] = kn[:, h * MLA_NOPE_DIM:(h + 1) * MLA_NOPE_DIM]
        km_ref[0, h, :, MLA_NOPE_DIM:] = kr
        vt_ref[0, h, 0, :MLA_V_DIM] = _dot_nt(wuvt_ref[h], ckv).astype(BF16)
        vt_ref[0, h, 0, MLA_V_DIM:] = jnp.ones((MLA_V_AUG - MLA_V_DIM, xn.shape[0]), BF16)
    for j in range(MLA_HEADS // 2):
        r = (_rope_lanes(qr[:, j * LANES:(j + 1) * LANES], cos, sin) * qm_scale).astype(BF16)
        qm_ref[0, 2 * j, :, MLA_NOPE_DIM:] = r[:, :MLA_ROPE_DIM]
        qm_ref[0, 2 * j + 1, :, MLA_NOPE_DIM:] = r[:, MLA_ROPE_DIM:]


def _inproj(x2, g_mix, w_cat, cos_t, sin_t, g_q, w_uq, g_kv, w_uk, w_uvt, batch, seq, tm):
    tokens = batch * seq
    nt = seq // tm
    const = lambda t: (0, 0)
    tok = lambda t: (t, 0)
    head = lambda t: (t // nt, 0, t % nt, 0)
    out_shape = (
        jax.ShapeDtypeStruct((batch, SWA_HEADS, seq, SWA_HEAD_DIM), BF16),
        jax.ShapeDtypeStruct((batch, SWA_KV_HEADS, seq, SWA_HEAD_DIM), BF16),
        jax.ShapeDtypeStruct((batch, SWA_KV_HEADS, SWA_HEAD_DIM, seq), BF16),
        jax.ShapeDtypeStruct((tokens, D_MODEL), BF16),
        jax.ShapeDtypeStruct((tokens, D_MODEL), BF16),
        jax.ShapeDtypeStruct((batch, MLA_HEADS, seq, MLA_QK_DIM), BF16),
        jax.ShapeDtypeStruct((batch, MLA_HEADS, seq, MLA_QK_DIM), BF16),
        jax.ShapeDtypeStruct((batch, MLA_HEADS, nt, MLA_V_AUG, tm), BF16),
    )
    return pl.pallas_call(
        _inproj_kernel,
        grid=(tokens // tm,),
        in_specs=[
            pl.BlockSpec((tm, D_MODEL), tok),
            pl.BlockSpec((1, D_MODEL), const),
            pl.BlockSpec((D_MODEL, _C_END), const, pipeline_mode=pl.Buffered(1)),
            pl.BlockSpec((tm, LANES), lambda t: (t % nt, 0)),
            pl.BlockSpec((tm, LANES), lambda t: (t % nt, 0)),
            pl.BlockSpec((1, MLA_Q_RANK), const),
            pl.BlockSpec(w_uq.shape, const, pipeline_mode=pl.Buffered(1)),
            pl.BlockSpec((1, MLA_KV_RANK), const),
            pl.BlockSpec(w_uk.shape, const, pipeline_mode=pl.Buffered(1)),
            pl.BlockSpec(w_uvt.shape, lambda t: (0, 0, 0), pipeline_mode=pl.Buffered(1)),
        ],
        out_specs=(
            pl.BlockSpec((1, SWA_HEADS, tm, SWA_HEAD_DIM), head),
            pl.BlockSpec((1, SWA_KV_HEADS, tm, SWA_HEAD_DIM), head),
            pl.BlockSpec((1, SWA_KV_HEADS, SWA_HEAD_DIM, tm), lambda t: (t // nt, 0, 0, t % nt)),
            pl.BlockSpec((tm, D_MODEL), tok),
            pl.BlockSpec((tm, D_MODEL), tok),
            pl.BlockSpec((1, MLA_HEADS, tm, MLA_QK_DIM), head),
            pl.BlockSpec((1, MLA_HEADS, tm, MLA_QK_DIM), head),
            pl.BlockSpec((1, MLA_HEADS, 1, MLA_V_AUG, tm), lambda t: (t // nt, 0, t % nt, 0, 0)),
        ),
        out_shape=out_shape,
        compiler_params=pltpu.CompilerParams(
            dimension_semantics=("arbitrary",), vmem_limit_bytes=VMEM_LIMIT),
        name="inproj",
    )(x2, g_mix, w_cat, cos_t, sin_t, g_q, w_uq, g_kv, w_uk, w_uvt)


def _swa_kernel(q_ref, k_ref, kp_ref, vt_ref, vtp_ref, sink_ref, o_ref, s_scr, bias_scr, *,
                n_sub):
    blk = SWA_WINDOW
    cols = SWA_GROUP * blk

    @pl.when((pl.program_id(0) == 0) & (pl.program_id(1) == 0))
    def _():
        kj = lax.broadcasted_iota(jnp.int32, (2 * blk, cols), 0)
        qi = lax.broadcasted_iota(jnp.int32, (2 * blk, cols), 1) & (blk - 1)
        dist = blk + qi - kj
        bias_scr[...] = jnp.where((dist >= 0) & (dist < SWA_WINDOW), 0.0, NEG_INF)

    key_row = lax.broadcasted_iota(jnp.int32, (2 * blk, 1), 0)
    no_prev = jnp.where((key_row < blk) & (pl.program_id(1) == 0), NEG_INF, 0.0)
    units = [(g, n) for g in range(SWA_KV_HEADS) for n in range(n_sub)]

    def scores(u, buf):
        g, n = u
        q = q_ref[0, g * SWA_GROUP:(g + 1) * SWA_GROUP, n * blk:(n + 1) * blk, :]
        k_prev = kp_ref[0, g] if n == 0 else k_ref[0, g, (n - 1) * blk:n * blk, :]
        kk = jnp.concatenate([k_prev, k_ref[0, g, n * blk:(n + 1) * blk, :]], axis=0)
        s_scr[buf] = _dot_nt(kk, q.reshape(cols, SWA_HEAD_DIM))

    def softmax_pv(u, buf):
        g, n = u
        sink = sink_ref[:, g * cols:(g + 1) * cols] * LOG2_E
        s = s_scr[buf] + bias_scr[...]
        if n == 0:
            s = s + no_prev
        m = jnp.maximum(jnp.max(s, axis=0, keepdims=True), sink)
        e = jnp.exp2(s - m)
        denom = jnp.sum(e, axis=0, keepdims=True) + jnp.exp2(sink - m)
        vt_prev = vtp_ref[0, g] if n == 0 else vt_ref[0, g, :, (n - 1) * blk:n * blk]
        vvt = jnp.concatenate([vt_prev, vt_ref[0, g, :, n * blk:(n + 1) * blk]], axis=1)
        o = (_dot(vvt, e.astype(BF16)) / denom).T
        for hh in range(SWA_GROUP):
            h = g * SWA_GROUP + hh
            o_ref[0, n * blk:(n + 1) * blk, h * SWA_HEAD_DIM:(h + 1) * SWA_HEAD_DIM] = (
                o[hh * blk:(hh + 1) * blk, :].astype(BF16))

    scores(units[0], 0)
    for idx, u in enumerate(units):
        if idx + 1 < len(units):
            scores(units[idx + 1], (idx + 1) % 2)
        softmax_pv(u, idx % 2)


def _swa(q_a, k_a, vt_a, sink_lanes, batch, seq, tq):
    n_sub = tq // SWA_WINDOW
    cur = lambda b, i: (b, 0, i, 0)
    prev = lambda b, i: (b, 0, jnp.maximum(i * n_sub - 1, 0), 0)
    return pl.pallas_call(
        functools.partial(_swa_kernel, n_sub=n_sub),
        grid=(batch, seq // tq),
        in_specs=[
            pl.BlockSpec((1, SWA_HEADS, tq, SWA_HEAD_DIM), cur),
            pl.BlockSpec((1, SWA_KV_HEADS, tq, SWA_HEAD_DIM), cur),
            pl.BlockSpec((1, SWA_KV_HEADS, SWA_WINDOW, SWA_HEAD_DIM), prev),
            pl.BlockSpec((1, SWA_KV_HEADS, SWA_HEAD_DIM, tq), lambda b, i: (b, 0, 0, i)),
            pl.BlockSpec((1, SWA_KV_HEADS, SWA_HEAD_DIM, SWA_WINDOW),
                         lambda b, i: (b, 0, 0, jnp.maximum(i * n_sub - 1, 0))),
            pl.BlockSpec(sink_lanes.shape, lambda b, i: (0, 0)),
        ],
        out_specs=pl.BlockSpec((1, tq, D_MODEL), lambda b, i: (b, i, 0)),
        out_shape=jax.ShapeDtypeStruct((batch, seq, D_MODEL), BF16),
        scratch_shapes=[pltpu.VMEM((2, 2 * SWA_WINDOW, SWA_GROUP * SWA_WINDOW), F32),
                        pltpu.VMEM((2 * SWA_WINDOW, SWA_GROUP * SWA_WINDOW), F32)],
        compiler_params=pltpu.CompilerParams(
            dimension_semantics=("arbitrary", "arbitrary"), vmem_limit_bytes=VMEM_LIMIT),
        name="swa",
    )(q_a, k_a, k_a, vt_a, vt_a, sink_lanes)


def _mla_kernel(q_ref, k_ref, vt_ref, o_ref, m_scr, acc_scr, s_scr, qt_scr, *, tq, tk):
    assert tq == 2 * tk
    chunks = tk // MLA_KV_CHUNK
    n_q = q_ref.shape[2] // tq

    def q_block(i, carry):
        m_scr[...] = jnp.full(m_scr.shape, NEG_INF, F32)
        acc_scr[...] = jnp.zeros(acc_scr.shape, F32)

        q_blk = q_ref[0, 0, pl.ds(pl.multiple_of(i * tq, tq), tq), :]
        q_pad = jnp.concatenate(
            [q_blk, jnp.zeros((tq, qt_scr.shape[0] - MLA_QK_DIM), BF16)], axis=1)
        qt_scr[...] = q_pad.astype(F32).T.astype(BF16)

        def scores(j, buf, q0=0):
            k = k_ref[0, 0, pl.ds(pl.multiple_of(j * tk, tk), tk), :]
            s_scr[buf, :, q0:] = _dot(k, qt_scr[:MLA_QK_DIM, q0:])

        def softmax_pv(j, buf, q0=0, key_offset=None):
            s = s_scr[buf, :, q0:]
            if key_offset is not None:
                kpos = lax.broadcasted_iota(jnp.int32, s.shape, 0) + key_offset
                qpos = lax.broadcasted_iota(jnp.int32, s.shape, 1) + q0
                s = jnp.where(kpos <= qpos, s, NEG_INF)
            m_old = m_scr[:, q0:]
            m_new = jnp.maximum(m_old, jnp.max(s, axis=0, keepdims=True))
            alpha = jnp.exp2(m_old - m_new)
            pb = jnp.exp2((s - m_new).astype(BF16))
            pv = _dot(vt_ref[0, 0, j * chunks], pb[:MLA_KV_CHUNK])
            for c in range(1, chunks):
                pv += _dot(vt_ref[0, 0, j * chunks + c],
                           pb[c * MLA_KV_CHUNK:(c + 1) * MLA_KV_CHUNK])
            acc_scr[:, q0:] = alpha * acc_scr[:, q0:] + pv
            m_scr[:, q0:] = m_new

        scores(0, 0)

        def pair(jj, c):
            t = 2 * jj
            scores(t + 1, 1)
            softmax_pv(t, 0)
            scores(t + 2, 0)
            softmax_pv(t + 1, 1)
            return c

        lax.fori_loop(0, i, pair, 0)
        scores(2 * i + 1, 1, q0=tk)
        softmax_pv(2 * i, 0, key_offset=0)
        softmax_pv(2 * i + 1, 1, q0=tk, key_offset=tk)

        o = acc_scr[:MLA_V_DIM] / acc_scr[MLA_V_DIM:MLA_V_DIM + 1]
        o_ref[0, pl.ds(pl.multiple_of(i * tq, tq), tq), :] = o.T.astype(o_ref.dtype)
        return carry

    lax.fori_loop(0, n_q, q_block, 0)


def _mla(q_m, k_m, vt_m, batch, seq, tq, tk):
    return pl.pallas_call(
        functools.partial(_mla_kernel, tq=tq, tk=tk),
        grid=(batch, MLA_HEADS),
        in_specs=[
            pl.BlockSpec((1, 1, seq, MLA_QK_DIM), lambda b, h: (b, h, 0, 0)),
            pl.BlockSpec((1, 1, seq, MLA_QK_DIM), lambda b, h: (b, h, 0, 0)),
            pl.BlockSpec((1, 1, seq // MLA_KV_CHUNK, MLA_V_AUG, MLA_KV_CHUNK),
                         lambda b, h: (b, h, 0, 0, 0)),
        ],
        out_specs=pl.BlockSpec((1, seq, MLA_V_DIM), lambda b, h: (b, 0, h)),
        out_shape=jax.ShapeDtypeStruct((batch, seq, D_MODEL), BF16),
        scratch_shapes=[
            pltpu.VMEM((1, tq), F32),
            pltpu.VMEM((MLA_V_AUG, tq), F32),
            pltpu.VMEM((2, tk, tq), F32),
            pltpu.VMEM((2 * LANES, tq), BF16),
        ],
        compiler_params=pltpu.CompilerParams(
            dimension_semantics=("arbitrary", "arbitrary"), vmem_limit_bytes=VMEM_LIMIT),
        name="mla",
    )(q_m, k_m, vt_m)


def _outproj_kernel(x_ref, oa_ref, ob_ref, ga_ref, gb_ref, wo_ref, gf_ref, wr_ref, br_ref,
                    h_ref, xn_ref, comb_ref):
    merged = (ga_ref[...].astype(F32) * oa_ref[...].astype(F32)
              + gb_ref[...].astype(F32) * ob_ref[...].astype(F32)).astype(BF16)
    h = x_ref[...] + _dot(merged, wo_ref[...])
    h_ref[...] = h
    xn = _rms(h, gf_ref[...])
    xn_ref[...] = xn.astype(BF16)

    x_hi = xn.astype(BF16)
    x_lo = (xn - x_hi.astype(F32)).astype(BF16)
    lg = (_dot(x_hi, wr_ref[0]) + _dot(x_lo, wr_ref[0]) + _dot(x_hi, wr_ref[1])) + br_ref[...]

    lane = lax.broadcasted_iota(jnp.int32, lg.shape, 1)
    is_group = (lane >= N_EXPERTS) & (lane < N_EXPERTS + N_GROUPS)
    gl = jnp.where(is_group, lg, NEG_INF)
    g_max = jnp.max(gl, axis=-1, keepdims=True)
    g_sum = jnp.sum(jnp.where(is_group, jnp.exp(gl - g_max), 0.0), axis=-1, keepdims=True)
    g_w = 1.0 / g_sum
    big = jnp.int32(LANES)
    g_lane = jnp.min(jnp.where(is_group & (gl == g_max), lane, big), axis=-1, keepdims=True)
    gidx = g_lane - N_EXPERTS

    sel = (lane >= gidx * EXPERTS_PER_GROUP) & (lane < (gidx + 1) * EXPERTS_PER_GROUP)
    el = jnp.where(sel, lg, NEG_INF)
    e_max = jnp.max(el, axis=-1, keepdims=True)
    e_exp = jnp.where(sel, jnp.exp(el - e_max), 0.0)
    e_prob = e_exp / jnp.sum(e_exp, axis=-1, keepdims=True)
    p1 = jnp.max(e_prob, axis=-1, keepdims=True)
    i1 = jnp.min(jnp.where(sel & (e_prob == p1), lane, big), axis=-1, keepdims=True)
    rest = sel & (lane != i1)
    p2 = jnp.max(jnp.where(rest, e_prob, -1.0), axis=-1, keepdims=True)
    i2 = jnp.min(jnp.where(rest & (e_prob == p2), lane, big), axis=-1, keepdims=True)
    top_sum = p1 + p2
    comb = jnp.where(lane == i1, p1 / top_sum, jnp.where(lane == i2, p2 / top_sum, 0.0))
    comb_ref[...] = jnp.where(lane == N_EXPERTS, gidx.astype(F32), comb * g_w)


def _outproj(x2, o_a, o_b, g_a, g_b, w_o, g_ffn, w_r, b_r, tm):
    tokens = x2.shape[0]
    tok = lambda t: (t, 0)
    const = lambda t: (0, 0)
    return pl.pallas_call(
        _outproj_kernel,
        grid=(tokens // tm,),
        in_specs=[
            pl.BlockSpec((tm, D_MODEL), tok),
            pl.BlockSpec((tm, D_MODEL), tok),
            pl.BlockSpec((tm, D_MODEL), tok),
            pl.BlockSpec((tm, D_MODEL), tok),
            pl.BlockSpec((tm, D_MODEL), tok),
            pl.BlockSpec((D_MODEL, D_MODEL), const),
            pl.BlockSpec((1, D_MODEL), const),
            pl.BlockSpec((2, D_MODEL, LANES), lambda t: (0, 0, 0)),
            pl.BlockSpec((1, LANES), const),
        ],
        out_specs=(
            pl.BlockSpec((tm, D_MODEL), tok),
            pl.BlockSpec((tm, D_MODEL), tok),
            pl.BlockSpec((tm, LANES), tok),
        ),
        out_shape=(
            jax.ShapeDtypeStruct((tokens, D_MODEL), F32),
            jax.ShapeDtypeStruct((tokens, D_MODEL), BF16),
            jax.ShapeDtypeStruct((tokens, LANES), F32),
        ),
        compiler_params=pltpu.CompilerParams(
            dimension_semantics=("arbitrary",), vmem_limit_bytes=VMEM_LIMIT),
        name="outproj",
    )(x2, o_a, o_b, g_a, g_b, w_o, g_ffn, w_r, b_r)


def _moe_kernel(h_ref, xn_ref, comb_ref, wi_ref, wo_ref, o_ref, *, chunk):
    tm = h_ref.shape[0]
    o_ref[...] = h_ref[...]
    xn = xn_ref[...]
    comb = comb_ref[...]
    comb_hi = comb.astype(BF16)
    comb_lo = (comb - comb_hi.astype(F32)).astype(BF16)
    comb_hl = jnp.concatenate([comb_hi, comb_lo], axis=1)
    gid_col = comb[:, N_EXPERTS:N_EXPERTS + 1]
    gid_row = comb.T[N_EXPERTS:N_EXPERTS + 1, :]
    member_col = gid_col == lax.broadcasted_iota(jnp.int32, (tm, LANES), 1).astype(F32)
    member_row = gid_row == lax.broadcasted_iota(jnp.int32, (8, tm), 0).astype(F32)
    r_i = lax.broadcasted_iota(jnp.int32, (tm, tm), 0)
    c_i = lax.broadcasted_iota(jnp.int32, (tm, tm), 1)
    ranks_row = _dot(member_row.astype(BF16), (r_i < c_i).astype(BF16))
    ranks_col = _dot((r_i > c_i).astype(BF16), member_col.astype(BF16))
    pad = 2 * LANES
    never = -4.0 * tm
    slot_row = lax.broadcasted_iota(jnp.int32, (chunk, tm), 0).astype(F32)
    slot_col = lax.broadcasted_iota(jnp.int32, (tm, pad), 1).astype(F32)
    slot_col = jnp.where(slot_col < float(chunk), slot_col, never)

    for g in range(N_GROUPS):
        in_row = member_row[g:g + 1]
        rank_row = jnp.where(in_row, ranks_row[g:g + 1], -2.0 * never)
        rank_col = jnp.where(member_col[:, g:g + 1], ranks_col[:, g:g + 1], -2.0 * never)
        count = jnp.sum(in_row.astype(jnp.int32))
        n_chunks = (count + (chunk - 1)) // chunk

        def chunk_body(c, carry):
            base = (c * chunk).astype(F32)
            gather = jnp.where(rank_row - base == slot_row, 1.0, 0.0).astype(BF16)
            scatter = jnp.where(rank_col - base == slot_col, 1.0, 0.0).astype(BF16)
            xc = _dot(gather, xn).astype(BF16)
            wc = _dot(gather, comb_hl)
            wc = wc[:, :LANES] + wc[:, LANES:]
            yc = jnp.zeros((chunk, D_MODEL), F32)
            for j in range(EXPERTS_PER_GROUP):
                e = g * EXPERTS_PER_GROUP + j
                hid = _dot(xc, wi_ref[e])
                gate = hid[:, :D_EXPERT]
                act = (gate * _sigmoid(gate)) * hid[:, D_EXPERT:] * wc[:, e:e + 1]
                yc = yc + _dot(act.astype(BF16), wo_ref[e])
            yc = jnp.concatenate(
                [yc.astype(BF16), jnp.zeros((pad - chunk, D_MODEL), BF16)], axis=0)
            o_ref[...] += _dot(scatter, yc)
            return carry

        lax.fori_loop(0, n_chunks, chunk_body, 0)


def _moe(h1, xn, comb, w_ei, w_eo, tm, chunk):
    tokens = h1.shape[0]
    tok = lambda t: (t, 0)
    return pl.pallas_call(
        functools.partial(_moe_kernel, chunk=chunk),
        grid=(tokens // tm,),
        in_specs=[
            pl.BlockSpec((tm, D_MODEL), tok),
            pl.BlockSpec((tm, D_MODEL), tok),
            pl.BlockSpec((tm, LANES), tok),
            pl.BlockSpec(w_ei.shape, lambda t: (0, 0, 0), pipeline_mode=pl.Buffered(1)),
            pl.BlockSpec(w_eo.shape, lambda t: (0, 0, 0), pipeline_mode=pl.Buffered(1)),
        ],
        out_specs=pl.BlockSpec((tm, D_MODEL), tok),
        out_shape=jax.ShapeDtypeStruct((tokens, D_MODEL), F32),
        compiler_params=pltpu.CompilerParams(
            dimension_semantics=("arbitrary",), vmem_limit_bytes=VMEM_LIMIT),
        name="moe",
    )(h1, xn, comb, w_ei, w_eo)


def _ple_kernel(h_ref, p_ref, gp_ref, wg_ref, wp_ref, gfin_ref, o_ref):
    h = h_ref[...]
    gate = _sigmoid(_dot(_rms(h, gp_ref[...]).astype(BF16), wg_ref[...]))
    h = h + gate * _dot(p_ref[...].astype(BF16), wp_ref[...])
    o_ref[...] = _rms(h, gfin_ref[...])


def _ple(h2, p2, g_ple, w_pg, w_pp, g_final, tm):
    tokens = h2.shape[0]
    tok = lambda t: (t, 0)
    const = lambda t: (0, 0)
    return pl.pallas_call(
        _ple_kernel,
        grid=(tokens // tm,),
        in_specs=[
            pl.BlockSpec((tm, D_MODEL), tok),
            pl.BlockSpec((tm, PLE_DIM), tok),
            pl.BlockSpec((1, D_MODEL), const),
            pl.BlockSpec((D_MODEL, D_MODEL), const),
            pl.BlockSpec((PLE_DIM, D_MODEL), const),
            pl.BlockSpec((1, D_MODEL), const),
        ],
        out_specs=pl.BlockSpec((tm, D_MODEL), tok),
        out_shape=jax.ShapeDtypeStruct((tokens, D_MODEL), F32),
        compiler_params=pltpu.CompilerParams(
            dimension_semantics=("arbitrary",), vmem_limit_bytes=VMEM_LIMIT),
        name="ple",
    )(h2, p2, g_ple, w_pg, w_pp, g_final)


def _rope_tables(seq):
    half = SWA_HEAD_DIM // 2
    pos = jnp.arange(seq, dtype=F32)
    inv = ROPE_THETA ** (-jnp.arange(0, SWA_HEAD_DIM, 2, dtype=F32) / SWA_HEAD_DIM)
    ang = pos[:, None] * inv[None, :]
    cos, sin = jnp.cos(ang), jnp.sin(ang)
    reps = LANES // half
    cos_t = jnp.tile(cos, (1, reps))
    sin_t = jnp.tile(jnp.concatenate([-sin, sin], axis=1), (1, reps // 2))
    return cos_t, sin_t


def _layer(h2d, p2d, g_mix, w_in, sinks, g_q, w_uq, g_kv, w_ukv, w_out, g_ffn, w_rg, b_rg,
           w_re, b_re, w_ei, w_eo, g_ple, w_pg, w_pp, g_final, cos_t, sin_t, batch, seq):
    assert SWA_HEAD_DIM == MLA_ROPE_DIM, "one rotary table serves both mixers"
    kr_end = _C_CKVR + MLA_KV_RANK + MLA_ROPE_DIM
    w_cat = jnp.concatenate(
        [w_in[:, :kr_end], jnp.zeros((D_MODEL, LANES - MLA_ROPE_DIM), w_in.dtype),
         w_in[:, kr_end:]], axis=1).astype(BF16)
    assert w_cat.shape[1] == _C_END
    uq = w_uq.reshape(MLA_Q_RANK, MLA_HEADS, MLA_QK_DIM)
    w_uq_c = jnp.concatenate(
        [uq[:, :, :MLA_NOPE_DIM].reshape(MLA_Q_RANK, -1),
         uq[:, :, MLA_NOPE_DIM:].reshape(MLA_Q_RANK, -1)], axis=1).astype(BF16)
    ukv = w_ukv.reshape(MLA_KV_RANK, MLA_HEADS, MLA_NOPE_DIM + MLA_V_DIM)
    w_uk_c = ukv[:, :, :MLA_NOPE_DIM].reshape(MLA_KV_RANK, -1).astype(BF16)
    w_uvt = jnp.transpose(ukv[:, :, MLA_NOPE_DIM:], (1, 2, 0)).astype(BF16)

    q_a, k_a, v_a, g_a, g_b, q_m, k_m, v_m = _inproj(
        h2d, g_mix[None], w_cat, cos_t, sin_t, g_q[None], w_uq_c, g_kv[None], w_uk_c, w_uvt,
        batch, seq, tm=MLA_KV_CHUNK)

    sink_lanes = jnp.repeat(sinks.astype(F32), SWA_WINDOW)[None]
    o_a = _swa(q_a, k_a, v_a, sink_lanes, batch, seq, tq=512)
    o_b = _mla(q_m, k_m, v_m, batch, seq, tq=1024, tk=512)

    w_r = jnp.concatenate(
        [w_re, w_rg, jnp.zeros((D_MODEL, LANES - N_EXPERTS - N_GROUPS), F32)], axis=1)
    w_r_hi = w_r.astype(BF16)
    w_r_lo = (w_r - w_r_hi.astype(F32)).astype(BF16)
    w_r2 = jnp.stack([w_r_hi, w_r_lo])
    b_r = jnp.concatenate(
        [b_re, b_rg, jnp.zeros((LANES - N_EXPERTS - N_GROUPS,), F32)])[None]
    tokens = batch * seq
    h1, xn2, comb = _outproj(
        h2d, o_a.reshape(tokens, D_MODEL), o_b.reshape(tokens, D_MODEL), g_a, g_b,
        w_out.astype(BF16), g_ffn[None], w_r2, b_r, tm=512)

    h2 = _moe(h1, xn2, comb, w_ei.astype(BF16), w_eo.astype(BF16), tm=512, chunk=160)
    return _ple(h2, p2d, g_ple[None], w_pg.astype(BF16), w_pp.astype(BF16), g_final[None], tm=512)


def kernel(x, p, g_mix, w_in, swa_sinks, mla_g_q, mla_w_uq, mla_g_kv, mla_w_ukv, w_out, g_ffn,
           w_router_group, b_router_group, w_router_expert, b_router_expert, w_expert_in,
           w_expert_out, g_ple, w_ple_gate, w_ple_proj, g_final):
    batch, seq, d = x.shape
    depth = p.shape[0]
    assert d == D_MODEL and depth == 1, "final RMSNorm is fused into the single layer"
    cos_t, sin_t = _rope_tables(seq)
    out = _layer(
        x.reshape(batch * seq, d), p[0].reshape(batch * seq, PLE_DIM), g_mix[0], w_in[0],
        swa_sinks[0], mla_g_q[0], mla_w_uq[0], mla_g_kv[0], mla_w_ukv[0], w_out[0], g_ffn[0],
        w_router_group[0], b_router_group[0], w_router_expert[0], b_router_expert[0],
        w_expert_in[0], w_expert_out[0], g_ple[0], w_ple_gate[0], w_ple_proj[0], g_final,
        cos_t, sin_t, batch, seq)
    return out.reshape(batch, seq, d)
```

```python
import functools
import math

import jax
import jax.numpy as jnp
from jax import lax
from jax.experimental import pallas as pl
from jax.experimental.pallas import tpu as pltpu

D_MODEL = 1024
PLE_DIM = 256
ROPE_THETA = 10000.0
EPS = 1e-6
NEG_INF = -1e30

SWA_HEAD_DIM = 64
SWA_HEADS = D_MODEL // SWA_HEAD_DIM
SWA_KV_HEADS = SWA_HEADS // 8
SWA_GROUP = SWA_HEADS // SWA_KV_HEADS
SWA_WINDOW = 128

MLA_NOPE_DIM = 128
MLA_ROPE_DIM = 64
MLA_V_DIM = 128
MLA_HEADS = D_MODEL // MLA_V_DIM
MLA_Q_RANK = 256
MLA_KV_RANK = 128
MLA_QK_DIM = MLA_NOPE_DIM + MLA_ROPE_DIM

N_GROUPS = 4
EXPERTS_PER_GROUP = 4
N_EXPERTS = N_GROUPS * EXPERTS_PER_GROUP
D_EXPERT = 256

LANES = 128
LOG2_E = 1.4426950408889634
MLA_KV_CHUNK = 512
VMEM_LIMIT = 56 * 1024 * 1024

_C_QA = 0
_C_KV = _C_QA + SWA_HEADS * SWA_HEAD_DIM
_C_CQ = _C_KV + 2 * SWA_KV_HEADS * SWA_HEAD_DIM
_C_CKVR = _C_CQ + MLA_Q_RANK
_C_GATES = _C_CKVR + MLA_KV_RANK + MLA_ROPE_DIM

BF16 = jnp.bfloat16
F32 = jnp.float32


def _dot(a, b):
    return jnp.dot(a, b, preferred_element_type=F32)


def _dot_nt(a, b):
    return lax.dot_general(a, b, (((1,), (1,)), ((), ())), preferred_element_type=F32)


def _rms(x, g):
    r = lax.rsqrt(jnp.mean(x * x, axis=-1, keepdims=True) + EPS)
    return x * r * g


def _sigmoid(x):
    return 1.0 / (1.0 + jnp.exp(-x))


def _rope_lanes(x, cos, sin_signed):
    lane = lax.broadcasted_iota(jnp.int32, x.shape, 1)
    upper = (lane & (SWA_HEAD_DIM // 2)) != 0
    partner = jnp.where(upper, pltpu.roll(x, SWA_HEAD_DIM // 2, 1),
                        pltpu.roll(x, LANES - SWA_HEAD_DIM // 2, 1))
    return x * cos + partner * sin_signed


def _inproj_kernel(x_ref, g_ref, w_ref, wg_ref, cos_ref, sin_ref, gq_ref, wuq_ref, gkv_ref,
                   wuk_ref, wuvt_ref, qa_ref, ka_ref, va_ref, ga_ref, gb_ref, qm_ref, km_ref,
                   vt_ref):
    xn = _rms(x_ref[...], g_ref[...]).astype(BF16)
    cos = cos_ref[...]
    sin = sin_ref[...]

    qa_scale = LOG2_E / math.sqrt(SWA_HEAD_DIM)
    qa = _dot(xn, w_ref[:, _C_QA:_C_KV])
    for j in range(SWA_HEADS // 2):
        q = (_rope_lanes(qa[:, j * LANES:(j + 1) * LANES], cos, sin) * qa_scale).astype(BF16)
        qa_ref[0, 2 * j] = q[:, :SWA_HEAD_DIM]
        qa_ref[0, 2 * j + 1] = q[:, SWA_HEAD_DIM:]
    kv = _dot(xn, w_ref[:, _C_KV:_C_CQ])
    k = _rope_lanes(kv[:, :LANES], cos, sin).astype(BF16)
    ka_ref[0, 0] = k[:, :SWA_HEAD_DIM]
    ka_ref[0, 1] = k[:, SWA_HEAD_DIM:]
    vt = kv[:, LANES:].T.astype(BF16)
    va_ref[0, 0] = vt[:SWA_HEAD_DIM]
    va_ref[0, 1] = vt[SWA_HEAD_DIM:]

    ga_ref[...] = _sigmoid(_dot(xn, wg_ref[:, :D_MODEL])).astype(BF16)
    gb_ref[...] = _sigmoid(_dot(xn, wg_ref[:, D_MODEL:])).astype(BF16)

    cq = _rms(_dot(xn, w_ref[:, _C_CQ:_C_CKVR]), gq_ref[...]).astype(BF16)
    ckvr = _dot(xn, w_ref[:, _C_CKVR:_C_GATES])
    ckv = _rms(ckvr[:, :MLA_KV_RANK], gkv_ref[...]).astype(BF16)
    kr = jnp.concatenate(
        [ckvr[:, MLA_KV_RANK:], jnp.zeros((xn.shape[0], LANES - MLA_ROPE_DIM), F32)], axis=1)
    kr = _rope_lanes(kr, cos, sin)[:, :MLA_ROPE_DIM].astype(BF16)
    qm_scale = LOG2_E / math.sqrt(MLA_QK_DIM)
    n_nope = MLA_HEADS * MLA_NOPE_DIM
    qn = (_dot(cq, wuq_ref[:, :n_nope]) * qm_scale).astype(BF16)
    kn = _dot(ckv, wuk_ref[...]).astype(BF16)
    qr = _dot(cq, wuq_ref[:, n_nope:])
    for h in range(MLA_HEADS):
        qm_ref[0, h, :, :MLA_NOPE_DIM] = qn[:, h * MLA_NOPE_DIM:(h + 1) * MLA_NOPE_DIM]
        km_ref[0, h, :, :MLA_NOPE_DIM] = kn[:, h * MLA_NOPE_DIM:(h + 1) * MLA_NOPE_DIM]
        km_ref[0, h, :, MLA_NOPE_DIM:] = kr
        vt_ref[0, h, 0] = _dot_nt(wuvt_ref[h], ckv).astype(BF16)
    for j in range(MLA_HEADS // 2):
        r = (_rope_lanes(qr[:, j * LANES:(j + 1) * LANES], cos, sin) * qm_scale).astype(BF16)
        qm_ref[0, 2 * j, :, MLA_NOPE_DIM:] = r[:, :MLA_ROPE_DIM]
        qm_ref[0, 2 * j + 1, :, MLA_NOPE_DIM:] = r[:, MLA_ROPE_DIM:]


def _inproj(x2, g_mix, w_main, w_gates, cos_t, sin_t, g_q, w_uq, g_kv, w_uk, w_uvt, batch, seq,
            tm):
    tokens = batch * seq
    nt = seq // tm
    const = lambda t: (0, 0)
    tok = lambda t: (t, 0)
    head = lambda t: (t // nt, 0, t % nt, 0)
    out_shape = (
        jax.ShapeDtypeStruct((batch, SWA_HEADS, seq, SWA_HEAD_DIM), BF16),
        jax.ShapeDtypeStruct((batch, SWA_KV_HEADS, seq, SWA_HEAD_DIM), BF16),
        jax.ShapeDtypeStruct((batch, SWA_KV_HEADS, SWA_HEAD_DIM, seq), BF16),
        jax.ShapeDtypeStruct((tokens, D_MODEL), BF16),
        jax.ShapeDtypeStruct((tokens, D_MODEL), BF16),
        jax.ShapeDtypeStruct((batch, MLA_HEADS, seq, MLA_QK_DIM), BF16),
        jax.ShapeDtypeStruct((batch, MLA_HEADS, seq, MLA_QK_DIM), BF16),
        jax.ShapeDtypeStruct((batch, MLA_HEADS, nt, MLA_V_DIM, tm), BF16),
    )
    return pl.pallas_call(
        _inproj_kernel,
        grid=(tokens // tm,),
        in_specs=[
            pl.BlockSpec((tm, D_MODEL), tok),
            pl.BlockSpec((1, D_MODEL), const),
            pl.BlockSpec(w_main.shape, const, pipeline_mode=pl.Buffered(1)),
            pl.BlockSpec(w_gates.shape, const, pipeline_mode=pl.Buffered(1)),
            pl.BlockSpec((tm, LANES), lambda t: (t % nt, 0)),
            pl.BlockSpec((tm, LANES), lambda t: (t % nt, 0)),
            pl.BlockSpec((1, MLA_Q_RANK), const),
            pl.BlockSpec(w_uq.shape, const, pipeline_mode=pl.Buffered(1)),
            pl.BlockSpec((1, MLA_KV_RANK), const),
            pl.BlockSpec(w_uk.shape, const, pipeline_mode=pl.Buffered(1)),
            pl.BlockSpec(w_uvt.shape, lambda t: (0, 0, 0), pipeline_mode=pl.Buffered(1)),
        ],
        out_specs=(
            pl.BlockSpec((1, SWA_HEADS, tm, SWA_HEAD_DIM), head),
            pl.BlockSpec((1, SWA_KV_HEADS, tm, SWA_HEAD_DIM), head),
            pl.BlockSpec((1, SWA_KV_HEADS, SWA_HEAD_DIM, tm), lambda t: (t // nt, 0, 0, t % nt)),
            pl.BlockSpec((tm, D_MODEL), tok),
            pl.BlockSpec((tm, D_MODEL), tok),
            pl.BlockSpec((1, MLA_HEADS, tm, MLA_QK_DIM), head),
            pl.BlockSpec((1, MLA_HEADS, tm, MLA_QK_DIM), head),
            pl.BlockSpec((1, MLA_HEADS, 1, MLA_V_DIM, tm), lambda t: (t // nt, 0, t % nt, 0, 0)),
        ),
        out_shape=out_shape,
        compiler_params=pltpu.CompilerParams(
            dimension_semantics=("arbitrary",), vmem_limit_bytes=VMEM_LIMIT),
        name="inproj",
    )(x2, g_mix, w_main, w_gates, cos_t, sin_t, g_q, w_uq, g_kv, w_uk, w_uvt)


def _swa_kernel(q_ref, k_ref, kp_ref, vt_ref, vtp_ref, sink_ref, o_ref, s_scr, bias_scr, *,
                n_sub):
    blk = SWA_WINDOW
    cols = SWA_GROUP * blk

    @pl.when((pl.program_id(0) == 0) & (pl.program_id(1) == 0))
    def _():
        kj = lax.broadcasted_iota(jnp.int32, (2 * blk, cols), 0)
        qi = lax.broadcasted_iota(jnp.int32, (2 * blk, cols), 1) & (blk - 1)
        dist = blk + qi - kj
        bias_scr[...] = jnp.where((dist >= 0) & (dist < SWA_WINDOW), 0.0, NEG_INF)

    key_row = lax.broadcasted_iota(jnp.int32, (2 * blk, 1), 0)
    no_prev = jnp.where((key_row < blk) & (pl.program_id(1) == 0), NEG_INF, 0.0)
    units = [(g, n) for g in range(SWA_KV_HEADS) for n in range(n_sub)]

    def scores(u, buf):
        g, n = u
        q = q_ref[0, g * SWA_GROUP:(g + 1) * SWA_GROUP, n * blk:(n + 1) * blk, :]
        k_prev = kp_ref[0, g] if n == 0 else k_ref[0, g, (n - 1) * blk:n * blk, :]
        kk = jnp.concatenate([k_prev, k_ref[0, g, n * blk:(n + 1) * blk, :]], axis=0)
        s_scr[buf] = _dot_nt(kk, q.reshape(cols, SWA_HEAD_DIM))

    def softmax_pv(u, buf):
        g, n = u
        sink = sink_ref[:, g * cols:(g + 1) * cols] * LOG2_E
        s = s_scr[buf] + bias_scr[...]
        if n == 0:
            s = s + no_prev
        m = jnp.maximum(jnp.max(s, axis=0, keepdims=True), sink)
        e = jnp.exp2(s - m)
        denom = jnp.sum(e, axis=0, keepdims=True) + jnp.exp2(sink - m)
        vt_prev = vtp_ref[0, g] if n == 0 else vt_ref[0, g, :, (n - 1) * blk:n * blk]
        vvt = jnp.concatenate([vt_prev, vt_ref[0, g, :, n * blk:(n + 1) * blk]], axis=1)
        o = (_dot(vvt, e.astype(BF16)) / denom).T
        for hh in range(SWA_GROUP):
            h = g * SWA_GROUP + hh
            o_ref[0, n * blk:(n + 1) * blk, h * SWA_HEAD_DIM:(h + 1) * SWA_HEAD_DIM] = (
                o[hh * blk:(hh + 1) * blk, :].astype(BF16))

    scores(units[0], 0)
    for idx, u in enumerate(units):
        if idx + 1 < len(units):
            scores(units[idx + 1], (idx + 1) % 2)
        softmax_pv(u, idx % 2)


def _swa(q_a, k_a, vt_a, sink_lanes, batch, seq, tq):
    n_sub = tq // SWA_WINDOW
    cur = lambda b, i: (b, 0, i, 0)
    prev = lambda b, i: (b, 0, jnp.maximum(i * n_sub - 1, 0), 0)
    return pl.pallas_call(
        functools.partial(_swa_kernel, n_sub=n_sub),
        grid=(batch, seq // tq),
        in_specs=[
            pl.BlockSpec((1, SWA_HEADS, tq, SWA_HEAD_DIM), cur),
            pl.BlockSpec((1, SWA_KV_HEADS, tq, SWA_HEAD_DIM), cur),
            pl.BlockSpec((1, SWA_KV_HEADS, SWA_WINDOW, SWA_HEAD_DIM), prev),
            pl.BlockSpec((1, SWA_KV_HEADS, SWA_HEAD_DIM, tq), lambda b, i: (b, 0, 0, i)),
            pl.BlockSpec((1, SWA_KV_HEADS, SWA_HEAD_DIM, SWA_WINDOW),
                         lambda b, i: (b, 0, 0, jnp.maximum(i * n_sub - 1, 0))),
            pl.BlockSpec(sink_lanes.shape, lambda b, i: (0, 0)),
        ],
        out_specs=pl.BlockSpec((1, tq, D_MODEL), lambda b, i: (b, i, 0)),
        out_shape=jax.ShapeDtypeStruct((batch, seq, D_MODEL), BF16),
        scratch_shapes=[pltpu.VMEM((2, 2 * SWA_WINDOW, SWA_GROUP * SWA_WINDOW), F32),
                        pltpu.VMEM((2 * SWA_WINDOW, SWA_GROUP * SWA_WINDOW), F32)],
        compiler_params=pltpu.CompilerParams(
            dimension_semantics=("arbitrary", "arbitrary"), vmem_limit_bytes=VMEM_LIMIT),
        name="swa",
    )(q_a, k_a, k_a, vt_a, vt_a, sink_lanes)


def _mla_kernel(q_ref, k_ref, vt_ref, o_ref, m_scr, l_scr, acc_scr, s_scr, qt_scr, *, tq, tk):
    assert tq == 2 * tk
    chunks = tk // MLA_KV_CHUNK
    n_q = q_ref.shape[2] // tq

    def q_block(i, carry):
        m_scr[...] = jnp.full(m_scr.shape, NEG_INF, F32)
        l_scr[...] = jnp.zeros(l_scr.shape, F32)
        acc_scr[...] = jnp.zeros(acc_scr.shape, F32)

        q_blk = q_ref[0, 0, pl.ds(pl.multiple_of(i * tq, tq), tq), :]
        q_pad = jnp.concatenate(
            [q_blk, jnp.zeros((tq, qt_scr.shape[0] - MLA_QK_DIM), BF16)], axis=1)
        qt_scr[...] = q_pad.astype(F32).T.astype(BF16)

        def scores(j, buf, q0=0):
            k = k_ref[0, 0, pl.ds(pl.multiple_of(j * tk, tk), tk), :]
            s_scr[buf, :, q0:] = _dot(k, qt_scr[:MLA_QK_DIM, q0:])

        def softmax_pv(j, buf, q0=0, key_offset=None):
            s = s_scr[buf, :, q0:]
            if key_offset is not None:
                kpos = lax.broadcasted_iota(jnp.int32, s.shape, 0) + key_offset
                qpos = lax.broadcasted_iota(jnp.int32, s.shape, 1) + q0
                s = jnp.where(kpos <= qpos, s, NEG_INF)
            m_old = m_scr[:, q0:]
            m_new = jnp.maximum(m_old, jnp.max(s, axis=0, keepdims=True))
            alpha = jnp.exp2(m_old - m_new)
            p = jnp.exp2(s - m_new)
            l_scr[:, q0:] = alpha * l_scr[:, q0:] + jnp.sum(p, axis=0, keepdims=True)
            pb = p.astype(BF16)
            pv = _dot(vt_ref[0, 0, j * chunks], pb[:MLA_KV_CHUNK])
            for c in range(1, chunks):
                pv += _dot(vt_ref[0, 0, j * chunks + c],
                           pb[c * MLA_KV_CHUNK:(c + 1) * MLA_KV_CHUNK])
            acc_scr[:, q0:] = alpha * acc_scr[:, q0:] + pv
            m_scr[:, q0:] = m_new

        scores(0, 0)

        def pair(jj, c):
            t = 2 * jj
            scores(t + 1, 1)
            softmax_pv(t, 0)
            scores(t + 2, 0)
            softmax_pv(t + 1, 1)
            return c

        lax.fori_loop(0, i, pair, 0)
        scores(2 * i + 1, 1, q0=tk)
        softmax_pv(2 * i, 0, key_offset=0)
        softmax_pv(2 * i + 1, 1, q0=tk, key_offset=tk)

        o = acc_scr[...] / l_scr[...]
        o_ref[0, pl.ds(pl.multiple_of(i * tq, tq), tq), :] = o.T.astype(o_ref.dtype)
        return carry

    lax.fori_loop(0, n_q, q_block, 0)


def _mla(q_m, k_m, vt_m, batch, seq, tq, tk):
    return pl.pallas_call(
        functools.partial(_mla_kernel, tq=tq, tk=tk),
        grid=(batch, MLA_HEADS),
        in_specs=[
            pl.BlockSpec((1, 1, seq, MLA_QK_DIM), lambda b, h: (b, h, 0, 0)),
            pl.BlockSpec((1, 1, seq, MLA_QK_DIM), lambda b, h: (b, h, 0, 0)),
            pl.BlockSpec((1, 1, seq // MLA_KV_CHUNK, MLA_V_DIM, MLA_KV_CHUNK),
                         lambda b, h: (b, h, 0, 0, 0)),
        ],
        out_specs=pl.BlockSpec((1, seq, MLA_V_DIM), lambda b, h: (b, 0, h)),
        out_shape=jax.ShapeDtypeStruct((batch, seq, D_MODEL), BF16),
        scratch_shapes=[
            pltpu.VMEM((1, tq), F32),
            pltpu.VMEM((1, tq), F32),
            pltpu.VMEM((MLA_V_DIM, tq), F32),
            pltpu.VMEM((2, tk, tq), F32),
            pltpu.VMEM((2 * LANES, tq), BF16),
        ],
        compiler_params=pltpu.CompilerParams(
            dimension_semantics=("arbitrary", "arbitrary"), vmem_limit_bytes=VMEM_LIMIT),
        name="mla",
    )(q_m, k_m, vt_m)


def _outproj_kernel(x_ref, oa_ref, ob_ref, ga_ref, gb_ref, wo_ref, gf_ref, wr_ref, br_ref,
                    h_ref, xn_ref, comb_ref):
    merged = (ga_ref[...].astype(F32) * oa_ref[...].astype(F32)
              + gb_ref[...].astype(F32) * ob_ref[...].astype(F32)).astype(BF16)
    h = x_ref[...] + _dot(merged, wo_ref[...])
    h_ref[...] = h
    xn = _rms(h, gf_ref[...])
    xn_ref[...] = xn.astype(BF16)

    x_hi = xn.astype(BF16)
    x_lo = (xn - x_hi.astype(F32)).astype(BF16)
    lg = (_dot(x_hi, wr_ref[0]) + _dot(x_lo, wr_ref[0]) + _dot(x_hi, wr_ref[1])) + br_ref[...]

    lane = lax.broadcasted_iota(jnp.int32, lg.shape, 1)
    is_group = (lane >= N_EXPERTS) & (lane < N_EXPERTS + N_GROUPS)
    gl = jnp.where(is_group, lg, NEG_INF)
    g_max = jnp.max(gl, axis=-1, keepdims=True)
    g_sum = jnp.sum(jnp.where(is_group, jnp.exp(gl - g_max), 0.0), axis=-1, keepdims=True)
    g_w = 1.0 / g_sum
    big = jnp.int32(LANES)
    g_lane = jnp.min(jnp.where(is_group & (gl == g_max), lane, big), axis=-1, keepdims=True)
    gidx = g_lane - N_EXPERTS

    sel = (lane >= gidx * EXPERTS_PER_GROUP) & (lane < (gidx + 1) * EXPERTS_PER_GROUP)
    el = jnp.where(sel, lg, NEG_INF)
    e_max = jnp.max(el, axis=-1, keepdims=True)
    e_exp = jnp.where(sel, jnp.exp(el - e_max), 0.0)
    e_prob = e_exp / jnp.sum(e_exp, axis=-1, keepdims=True)
    p1 = jnp.max(e_prob, axis=-1, keepdims=True)
    i1 = jnp.min(jnp.where(sel & (e_prob == p1), lane, big), axis=-1, keepdims=True)
    rest = sel & (lane != i1)
    p2 = jnp.max(jnp.where(rest, e_prob, -1.0), axis=-1, keepdims=True)
    i2 = jnp.min(jnp.where(rest & (e_prob == p2), lane, big), axis=-1, keepdims=True)
    top_sum = p1 + p2
    comb = jnp.where(lane == i1, p1 / top_sum, jnp.where(lane == i2, p2 / top_sum, 0.0))
    comb_ref[...] = jnp.where(lane == N_EXPERTS, gidx.astype(F32), comb * g_w)


def _outproj(x2, o_a, o_b, g_a, g_b, w_o, g_ffn, w_r, b_r, tm):
    tokens = x2.shape[0]
    tok = lambda t: (t, 0)
    const = lambda t: (0, 0)
    return pl.pallas_call(
        _outproj_kernel,
        grid=(tokens // tm,),
        in_specs=[
            pl.BlockSpec((tm, D_MODEL), tok),
            pl.BlockSpec((tm, D_MODEL), tok),
            pl.BlockSpec((tm, D_MODEL), tok),
            pl.BlockSpec((tm, D_MODEL), tok),
            pl.BlockSpec((tm, D_MODEL), tok),
            pl.BlockSpec((D_MODEL, D_MODEL), const),
            pl.BlockSpec((1, D_MODEL), const),
            pl.BlockSpec((2, D_MODEL, LANES), lambda t: (0, 0, 0)),
            pl.BlockSpec((1, LANES), const),
        ],
        out_specs=(
            pl.BlockSpec((tm, D_MODEL), tok),
            pl.BlockSpec((tm, D_MODEL), tok),
            pl.BlockSpec((tm, LANES), tok),
        ),
        out_shape=(
            jax.ShapeDtypeStruct((tokens, D_MODEL), F32),
            jax.ShapeDtypeStruct((tokens, D_MODEL), BF16),
            jax.ShapeDtypeStruct((tokens, LANES), F32),
        ),
        compiler_params=pltpu.CompilerParams(
            dimension_semantics=("arbitrary",), vmem_limit_bytes=VMEM_LIMIT),
        name="outproj",
    )(x2, o_a, o_b, g_a, g_b, w_o, g_ffn, w_r, b_r)


def _moe_kernel(h_ref, xn_ref, comb_ref, wi_ref, wo_ref, p_ref, gp_ref, wg_ref, wp_ref, gfin_ref,
                o_ref, *, chunk):
    tm = h_ref.shape[0]
    o_ref[...] = h_ref[...]
    xn = xn_ref[...]
    comb = comb_ref[...]
    comb_hi = comb.astype(BF16)
    comb_lo = (comb - comb_hi.astype(F32)).astype(BF16)
    comb_hl = jnp.concatenate([comb_hi, comb_lo], axis=1)
    gid_col = comb[:, N_EXPERTS:N_EXPERTS + 1]
    gid_row = comb.T[N_EXPERTS:N_EXPERTS + 1, :]
    member_col = gid_col == lax.broadcasted_iota(jnp.int32, (tm, LANES), 1).astype(F32)
    member_row = gid_row == lax.broadcasted_iota(jnp.int32, (8, tm), 0).astype(F32)
    r_i = lax.broadcasted_iota(jnp.int32, (tm, tm), 0)
    c_i = lax.broadcasted_iota(jnp.int32, (tm, tm), 1)
    ranks_row = _dot(member_row.astype(BF16), (r_i < c_i).astype(BF16))
    ranks_col = _dot((r_i > c_i).astype(BF16), member_col.astype(BF16))
    pad = 2 * LANES
    never = -4.0 * tm
    slot_row = lax.broadcasted_iota(jnp.int32, (chunk, tm), 0).astype(F32)
    slot_col = lax.broadcasted_iota(jnp.int32, (tm, pad), 1).astype(F32)
    slot_col = jnp.where(slot_col < float(chunk), slot_col, never)

    for g in range(N_GROUPS):
        in_row = member_row[g:g + 1]
        rank_row = jnp.where(in_row, ranks_row[g:g + 1], -2.0 * never)
        rank_col = jnp.where(member_col[:, g:g + 1], ranks_col[:, g:g + 1], -2.0 * never)
        count = jnp.sum(in_row.astype(jnp.int32))
        n_chunks = (count + (chunk - 1)) // chunk

        def chunk_body(c, carry):
            base = (c * chunk).astype(F32)
            gather = jnp.where(rank_row - base == slot_row, 1.0, 0.0).astype(BF16)
            scatter = jnp.where(rank_col - base == slot_col, 1.0, 0.0).astype(BF16)
            xc = _dot(gather, xn).astype(BF16)
            wc = _dot(gather, comb_hl)
            wc = wc[:, :LANES] + wc[:, LANES:]
            yc = jnp.zeros((chunk, D_MODEL), F32)
            for j in range(EXPERTS_PER_GROUP):
                e = g * EXPERTS_PER_GROUP + j
                hid = _dot(xc, wi_ref[e])
                gate = hid[:, :D_EXPERT]
                act = (gate * _sigmoid(gate)) * hid[:, D_EXPERT:] * wc[:, e:e + 1]
                yc = yc + _dot(act.astype(BF16), wo_ref[e])
            yc = jnp.concatenate(
                [yc.astype(BF16), jnp.zeros((pad - chunk, D_MODEL), BF16)], axis=0)
            o_ref[...] += _dot(scatter, yc)
            return carry

        lax.fori_loop(0, n_chunks, chunk_body, 0)

    h = o_ref[...]
    gate = _sigmoid(_dot(_rms(h, gp_ref[...]).astype(BF16), wg_ref[...]))
    h = h + gate * _dot(p_ref[...].astype(BF16), wp_ref[...])
    o_ref[...] = _rms(h, gfin_ref[...])


def _moe_ple(h1, xn, comb, w_ei, w_eo, p2, g_ple, w_pg, w_pp, g_final, tm, chunk):
    tokens = h1.shape[0]
    tok = lambda t: (t, 0)
    const = lambda t: (0, 0)
    return pl.pallas_call(
        functools.partial(_moe_kernel, chunk=chunk),
        grid=(tokens // tm,),
        in_specs=[
            pl.BlockSpec((tm, D_MODEL), tok),
            pl.BlockSpec((tm, D_MODEL), tok),
            pl.BlockSpec((tm, LANES), tok),
            pl.BlockSpec(w_ei.shape, lambda t: (0, 0, 0), pipeline_mode=pl.Buffered(1)),
            pl.BlockSpec(w_eo.shape, lambda t: (0, 0, 0), pipeline_mode=pl.Buffered(1)),
            pl.BlockSpec((tm, PLE_DIM), tok),
            pl.BlockSpec((1, D_MODEL), const),
            pl.BlockSpec((D_MODEL, D_MODEL), const, pipeline_mode=pl.Buffered(1)),
            pl.BlockSpec((PLE_DIM, D_MODEL), const, pipeline_mode=pl.Buffered(1)),
            pl.BlockSpec((1, D_MODEL), const),
        ],
        out_specs=pl.BlockSpec((tm, D_MODEL), tok),
        out_shape=jax.ShapeDtypeStruct((tokens, D_MODEL), F32),
        compiler_params=pltpu.CompilerParams(
            dimension_semantics=("arbitrary",), vmem_limit_bytes=VMEM_LIMIT),
        name="moe_ple",
    )(h1, xn, comb, w_ei, w_eo, p2, g_ple, w_pg, w_pp, g_final)


def _rope_tables(seq):
    half = SWA_HEAD_DIM // 2
    lane = jnp.arange(LANES)
    pos = jnp.arange(seq, dtype=F32)
    inv = ROPE_THETA ** (-(2 * (lane % half)).astype(F32) / SWA_HEAD_DIM)
    ang = pos[:, None] * inv[None, :]
    upper = (lane % SWA_HEAD_DIM) >= half
    return jnp.cos(ang), jnp.where(upper[None, :], jnp.sin(ang), -jnp.sin(ang))


def _layer(h2d, p2d, g_mix, w_in, sinks, g_q, w_uq, g_kv, w_ukv, w_out, g_ffn, w_rg, b_rg,
           w_re, b_re, w_ei, w_eo, g_ple, w_pg, w_pp, g_final, cos_t, sin_t, batch, seq):
    assert SWA_HEAD_DIM == MLA_ROPE_DIM, "one rotary table serves both mixers"
    w_main = w_in[:, :_C_GATES].astype(BF16)
    w_gates = w_in[:, _C_GATES:].astype(BF16)
    assert w_gates.shape[1] == 2 * D_MODEL
    uq = w_uq.reshape(MLA_Q_RANK, MLA_HEADS, MLA_QK_DIM)
    w_uq_c = jnp.concatenate(
        [uq[:, :, :MLA_NOPE_DIM].reshape(MLA_Q_RANK, -1),
         uq[:, :, MLA_NOPE_DIM:].reshape(MLA_Q_RANK, -1)], axis=1).astype(BF16)
    ukv = w_ukv.reshape(MLA_KV_RANK, MLA_HEADS, MLA_NOPE_DIM + MLA_V_DIM)
    w_uk_c = ukv[:, :, :MLA_NOPE_DIM].reshape(MLA_KV_RANK, -1).astype(BF16)
    w_uvt = jnp.transpose(ukv[:, :, MLA_NOPE_DIM:], (1, 2, 0)).astype(BF16)

    q_a, k_a, v_a, g_a, g_b, q_m, k_m, v_m = _inproj(
        h2d, g_mix[None], w_main, w_gates, cos_t, sin_t, g_q[None], w_uq_c, g_kv[None], w_uk_c,
        w_uvt,
        batch, seq, tm=MLA_KV_CHUNK)

    sink_lanes = jnp.repeat(sinks.astype(F32), SWA_WINDOW)[None]
    o_a = _swa(q_a, k_a, v_a, sink_lanes, batch, seq, tq=512)
    o_b = _mla(q_m, k_m, v_m, batch, seq, tq=1024, tk=512)

    w_r = jnp.concatenate(
        [w_re, w_rg, jnp.zeros((D_MODEL, LANES - N_EXPERTS - N_GROUPS), F32)], axis=1)
    w_r_hi = w_r.astype(BF16)
    w_r_lo = (w_r - w_r_hi.astype(F32)).astype(BF16)
    w_r2 = jnp.stack([w_r_hi, w_r_lo])
    b_r = jnp.concatenate(
        [b_re, b_rg, jnp.zeros((LANES - N_EXPERTS - N_GROUPS,), F32)])[None]
    tokens = batch * seq
    h1, xn2, comb = _outproj(
        h2d, o_a.reshape(tokens, D_MODEL), o_b.reshape(tokens, D_MODEL), g_a, g_b,
        w_out.astype(BF16), g_ffn[None], w_r2, b_r, tm=512)

    return _moe_ple(h1, xn2, comb, w_ei.astype(BF16), w_eo.astype(BF16), p2d, g_ple[None],
                    w_pg.astype(BF16), w_pp.astype(BF16), g_final[None], tm=512, chunk=160)


def kernel(x, p, g_mix, w_in, swa_sinks, mla_g_q, mla_w_uq, mla_g_kv, mla_w_ukv, w_out, g_ffn,
           w_router_group, b_router_group, w_router_expert, b_router_expert, w_expert_in,
           w_expert_out, g_ple, w_ple_gate, w_ple_proj, g_final):
    batch, seq, d = x.shape
    depth = p.shape[0]
    assert d == D_MODEL and depth == 1, "final RMSNorm is fused into the single layer"
    cos_t, sin_t = _rope_tables(seq)
    out = _layer(
        x.reshape(batch * seq, d), p[0].reshape(batch * seq, PLE_DIM), g_mix[0], w_in[0],
        swa_sinks[0], mla_g_q[0], mla_w_uq[0], mla_g_kv[0], mla_w_ukv[0], w_out[0], g_ffn[0],
        w_router_group[0], b_router_group[0], w_router_expert[0], b_router_expert[0],
        w_expert_in[0], w_expert_out[0], g_ple[0], w_ple_gate[0], w_ple_proj[0], g_final,
        cos_t, sin_t, batch, seq)
    return out.reshape(batch, seq, d)
```

```python
import functools
import math

import jax
import jax.numpy as jnp
from jax import lax
from jax.experimental import pallas as pl
from jax.experimental.pallas import tpu as pltpu

D_MODEL = 1024
PLE_DIM = 256
ROPE_THETA = 10000.0
EPS = 1e-6
NEG_INF = -1e30

SWA_HEAD_DIM = 64
SWA_HEADS = D_MODEL // SWA_HEAD_DIM
SWA_KV_HEADS = SWA_HEADS // 8
SWA_GROUP = SWA_HEADS // SWA_KV_HEADS
SWA_WINDOW = 128

MLA_NOPE_DIM = 128
MLA_ROPE_DIM = 64
MLA_V_DIM = 128
MLA_HEADS = D_MODEL // MLA_V_DIM
MLA_Q_RANK = 256
MLA_KV_RANK = 128
MLA_QK_DIM = MLA_NOPE_DIM + MLA_ROPE_DIM

N_GROUPS = 4
EXPERTS_PER_GROUP = 4
N_EXPERTS = N_GROUPS * EXPERTS_PER_GROUP
D_EXPERT = 256

LANES = 128
LOG2_E = 1.4426950408889634
MLA_KV_CHUNK = 512
VMEM_LIMIT = 56 * 1024 * 1024

_C_QA = 0
_C_KV = _C_QA + SWA_HEADS * SWA_HEAD_DIM
_C_CQ = _C_KV + 2 * SWA_KV_HEADS * SWA_HEAD_DIM
_C_CKVR = _C_CQ + MLA_Q_RANK
_C_GATES = _C_CKVR + MLA_KV_RANK + MLA_ROPE_DIM

BF16 = jnp.bfloat16
F32 = jnp.float32


def _dot(a, b):
    return jnp.dot(a, b, preferred_element_type=F32)


def _dot_nt(a, b):
    return lax.dot_general(a, b, (((1,), (1,)), ((), ())), preferred_element_type=F32)


def _rms(x, g):
    r = lax.rsqrt(jnp.mean(x * x, axis=-1, keepdims=True) + EPS)
    return x * r * g


def _sigmoid(x):
    return 1.0 / (1.0 + jnp.exp(-x))


def _rope_lanes(x, cos, sin_signed):
    lane = lax.broadcasted_iota(jnp.int32, x.shape, 1)
    upper = (lane & (SWA_HEAD_DIM // 2)) != 0
    partner = jnp.where(upper, pltpu.roll(x, SWA_HEAD_DIM // 2, 1),
                        pltpu.roll(x, LANES - SWA_HEAD_DIM // 2, 1))
    return x * cos + partner * sin_signed


def _inproj_kernel(x_ref, g_ref, w_ref, wg_ref, cos_ref, sin_ref, cost_ref, sint_ref, gq_ref,
                   wuqt_ref, gkv_ref, wuk_ref, wuvt_ref, qa_ref, ka_ref, va_ref, ga_ref, gb_ref,
                   qm_ref, km_ref, vt_ref):
    xn = _rms(x_ref[...], g_ref[...]).astype(BF16)
    cos = cos_ref[...]
    sin = sin_ref[...]

    qa_scale = LOG2_E / math.sqrt(SWA_HEAD_DIM)
    qa = _dot(xn, w_ref[:, _C_QA:_C_KV])
    for j in range(SWA_HEADS // 2):
        q = (_rope_lanes(qa[:, j * LANES:(j + 1) * LANES], cos, sin) * qa_scale).astype(BF16)
        qa_ref[0, 2 * j] = q[:, :SWA_HEAD_DIM]
        qa_ref[0, 2 * j + 1] = q[:, SWA_HEAD_DIM:]
    kv = _dot(xn, w_ref[:, _C_KV:_C_CQ])
    k = _rope_lanes(kv[:, :LANES], cos, sin).astype(BF16)
    ka_ref[0, 0] = k[:, :SWA_HEAD_DIM]
    ka_ref[0, 1] = k[:, SWA_HEAD_DIM:]
    vt = kv[:, LANES:].T.astype(BF16)
    va_ref[0, 0] = vt[:SWA_HEAD_DIM]
    va_ref[0, 1] = vt[SWA_HEAD_DIM:]

    ga_ref[...] = _sigmoid(_dot(xn, wg_ref[:, :D_MODEL])).astype(BF16)
    gb_ref[...] = _sigmoid(_dot(xn, wg_ref[:, D_MODEL:])).astype(BF16)

    cq = _rms(_dot(xn, w_ref[:, _C_CQ:_C_CKVR]), gq_ref[...])
    ckvr = _dot(xn, w_ref[:, _C_CKVR:_C_GATES])
    ckv = _rms(ckvr[:, :MLA_KV_RANK], gkv_ref[...])
    kr = jnp.concatenate(
        [ckvr[:, MLA_KV_RANK:], jnp.zeros((xn.shape[0], LANES - MLA_ROPE_DIM), F32)], axis=1)
    kr = _rope_lanes(kr, cos, sin)[:, :MLA_ROPE_DIM].astype(BF16)
    qm_scale = LOG2_E / math.sqrt(MLA_QK_DIM)
    kn = _dot(ckv.astype(BF16), wuk_ref[...]).astype(BF16)
    cq_t = cq.T.astype(BF16)
    ckv_t = ckv.T.astype(BF16)
    cos_h, sin_h = cost_ref[...], sint_ref[...]
    half = MLA_ROPE_DIM // 2
    for h in range(MLA_HEADS):
        km_ref[0, h, :, :MLA_NOPE_DIM] = kn[:, h * MLA_NOPE_DIM:(h + 1) * MLA_NOPE_DIM]
        km_ref[0, h, :, MLA_NOPE_DIM:] = kr
        vt_ref[0, h, 0] = _dot(wuvt_ref[h], ckv_t).astype(BF16)
        q_t = _dot(wuqt_ref[h], cq_t)
        x1 = q_t[MLA_NOPE_DIM:MLA_NOPE_DIM + half]
        x2 = q_t[MLA_NOPE_DIM + half:]
        q_t = jnp.concatenate(
            [q_t[:MLA_NOPE_DIM], x1 * cos_h - x2 * sin_h, x2 * cos_h + x1 * sin_h], axis=0)
        qm_ref[0, h, 0] = (q_t * qm_scale).astype(BF16)


def _inproj(x2, g_mix, w_main, w_gates, cos_t, sin_t, cos_h, sin_h, g_q, w_uqt, g_kv, w_uk, w_uvt,
            batch, seq, tm):
    tokens = batch * seq
    nt = seq // tm
    const = lambda t: (0, 0)
    tok = lambda t: (t, 0)
    head = lambda t: (t // nt, 0, t % nt, 0)
    out_shape = (
        jax.ShapeDtypeStruct((batch, SWA_HEADS, seq, SWA_HEAD_DIM), BF16),
        jax.ShapeDtypeStruct((batch, SWA_KV_HEADS, seq, SWA_HEAD_DIM), BF16),
        jax.ShapeDtypeStruct((batch, SWA_KV_HEADS, SWA_HEAD_DIM, seq), BF16),
        jax.ShapeDtypeStruct((tokens, D_MODEL), BF16),
        jax.ShapeDtypeStruct((tokens, D_MODEL), BF16),
        jax.ShapeDtypeStruct((batch, MLA_HEADS, nt, MLA_QK_DIM, tm), BF16),
        jax.ShapeDtypeStruct((batch, MLA_HEADS, seq, MLA_QK_DIM), BF16),
        jax.ShapeDtypeStruct((batch, MLA_HEADS, nt, MLA_V_DIM, tm), BF16),
    )
    return pl.pallas_call(
        _inproj_kernel,
        grid=(tokens // tm,),
        in_specs=[
            pl.BlockSpec((tm, D_MODEL), tok),
            pl.BlockSpec((1, D_MODEL), const),
            pl.BlockSpec(w_main.shape, const, pipeline_mode=pl.Buffered(1)),
            pl.BlockSpec(w_gates.shape, const, pipeline_mode=pl.Buffered(1)),
            pl.BlockSpec((tm, LANES), lambda t: (t % nt, 0)),
            pl.BlockSpec((tm, LANES), lambda t: (t % nt, 0)),
            pl.BlockSpec((MLA_ROPE_DIM // 2, tm), lambda t: (0, t % nt)),
            pl.BlockSpec((MLA_ROPE_DIM // 2, tm), lambda t: (0, t % nt)),
            pl.BlockSpec((1, MLA_Q_RANK), const),
            pl.BlockSpec(w_uqt.shape, lambda t: (0, 0, 0), pipeline_mode=pl.Buffered(1)),
            pl.BlockSpec((1, MLA_KV_RANK), const),
            pl.BlockSpec(w_uk.shape, const, pipeline_mode=pl.Buffered(1)),
            pl.BlockSpec(w_uvt.shape, lambda t: (0, 0, 0), pipeline_mode=pl.Buffered(1)),
        ],
        out_specs=(
            pl.BlockSpec((1, SWA_HEADS, tm, SWA_HEAD_DIM), head),
            pl.BlockSpec((1, SWA_KV_HEADS, tm, SWA_HEAD_DIM), head),
            pl.BlockSpec((1, SWA_KV_HEADS, SWA_HEAD_DIM, tm), lambda t: (t // nt, 0, 0, t % nt)),
            pl.BlockSpec((tm, D_MODEL), tok),
            pl.BlockSpec((tm, D_MODEL), tok),
            pl.BlockSpec((1, MLA_HEADS, 1, MLA_QK_DIM, tm), lambda t: (t // nt, 0, t % nt, 0, 0)),
            pl.BlockSpec((1, MLA_HEADS, tm, MLA_QK_DIM), head),
            pl.BlockSpec((1, MLA_HEADS, 1, MLA_V_DIM, tm), lambda t: (t // nt, 0, t % nt, 0, 0)),
        ),
        out_shape=out_shape,
        compiler_params=pltpu.CompilerParams(
            dimension_semantics=("arbitrary",), vmem_limit_bytes=VMEM_LIMIT),
        name="inproj",
    )(x2, g_mix, w_main, w_gates, cos_t, sin_t, cos_h, sin_h, g_q, w_uqt, g_kv, w_uk, w_uvt)


def _swa_kernel(q_ref, k_ref, kp_ref, vt_ref, vtp_ref, sink_ref, o_ref, s_scr, bias_scr, *,
                n_sub):
    blk = SWA_WINDOW
    cols = SWA_GROUP * blk

    @pl.when((pl.program_id(0) == 0) & (pl.program_id(1) == 0))
    def _():
        kj = lax.broadcasted_iota(jnp.int32, (2 * blk, cols), 0)
        qi = lax.broadcasted_iota(jnp.int32, (2 * blk, cols), 1) & (blk - 1)
        dist = blk + qi - kj
        bias_scr[...] = jnp.where((dist >= 0) & (dist < SWA_WINDOW), 0.0, NEG_INF)

    key_row = lax.broadcasted_iota(jnp.int32, (2 * blk, 1), 0)
    no_prev = jnp.where((key_row < blk) & (pl.program_id(1) == 0), NEG_INF, 0.0)
    units = [(g, n) for g in range(SWA_KV_HEADS) for n in range(n_sub)]

    def scores(u, buf):
        g, n = u
        q = q_ref[0, g * SWA_GROUP:(g + 1) * SWA_GROUP, n * blk:(n + 1) * blk, :]
        k_prev = kp_ref[0, g] if n == 0 else k_ref[0, g, (n - 1) * blk:n * blk, :]
        kk = jnp.concatenate([k_prev, k_ref[0, g, n * blk:(n + 1) * blk, :]], axis=0)
        s_scr[buf] = _dot_nt(kk, q.reshape(cols, SWA_HEAD_DIM))

    def softmax_pv(u, buf):
        g, n = u
        sink = sink_ref[:, g * cols:(g + 1) * cols] * LOG2_E
        s = s_scr[buf] + bias_scr[...]
        if n == 0:
            s = s + no_prev
        m = jnp.maximum(jnp.max(s, axis=0, keepdims=True), sink)
        e = jnp.exp2(s - m)
        denom = jnp.sum(e, axis=0, keepdims=True) + jnp.exp2(sink - m)
        vt_prev = vtp_ref[0, g] if n == 0 else vt_ref[0, g, :, (n - 1) * blk:n * blk]
        vvt = jnp.concatenate([vt_prev, vt_ref[0, g, :, n * blk:(n + 1) * blk]], axis=1)
        o = (_dot(vvt, e.astype(BF16)) / denom).T
        for hh in range(SWA_GROUP):
            h = g * SWA_GROUP + hh
            o_ref[0, n * blk:(n + 1) * blk, h * SWA_HEAD_DIM:(h + 1) * SWA_HEAD_DIM] = (
                o[hh * blk:(hh + 1) * blk, :].astype(BF16))

    scores(units[0], 0)
    for idx, u in enumerate(units):
        if idx + 1 < len(units):
            scores(units[idx + 1], (idx + 1) % 2)
        softmax_pv(u, idx % 2)


def _swa(q_a, k_a, vt_a, sink_lanes, batch, seq, tq):
    n_sub = tq // SWA_WINDOW
    cur = lambda b, i: (b, 0, i, 0)
    prev = lambda b, i: (b, 0, jnp.maximum(i * n_sub - 1, 0), 0)
    return pl.pallas_call(
        functools.partial(_swa_kernel, n_sub=n_sub),
        grid=(batch, seq // tq),
        in_specs=[
            pl.BlockSpec((1, SWA_HEADS, tq, SWA_HEAD_DIM), cur),
            pl.BlockSpec((1, SWA_KV_HEADS, tq, SWA_HEAD_DIM), cur),
            pl.BlockSpec((1, SWA_KV_HEADS, SWA_WINDOW, SWA_HEAD_DIM), prev),
            pl.BlockSpec((1, SWA_KV_HEADS, SWA_HEAD_DIM, tq), lambda b, i: (b, 0, 0, i)),
            pl.BlockSpec((1, SWA_KV_HEADS, SWA_HEAD_DIM, SWA_WINDOW),
                         lambda b, i: (b, 0, 0, jnp.maximum(i * n_sub - 1, 0))),
            pl.BlockSpec(sink_lanes.shape, lambda b, i: (0, 0)),
        ],
        out_specs=pl.BlockSpec((1, tq, D_MODEL), lambda b, i: (b, i, 0)),
        out_shape=jax.ShapeDtypeStruct((batch, seq, D_MODEL), BF16),
        scratch_shapes=[pltpu.VMEM((2, 2 * SWA_WINDOW, SWA_GROUP * SWA_WINDOW), F32),
                        pltpu.VMEM((2 * SWA_WINDOW, SWA_GROUP * SWA_WINDOW), F32)],
        compiler_params=pltpu.CompilerParams(
            dimension_semantics=("arbitrary", "arbitrary"), vmem_limit_bytes=VMEM_LIMIT),
        name="swa",
    )(q_a, k_a, k_a, vt_a, vt_a, sink_lanes)


def _mla_kernel(qt_ref, k_ref, vt_ref, o_ref, m_scr, l_scr, acc_scr, s_scr, *, tq, tk):
    assert tq == 2 * tk
    chunks = tk // MLA_KV_CHUNK
    q_chunks = tq // MLA_KV_CHUNK
    n_q = qt_ref.shape[2] // q_chunks

    def q_block(i, carry):
        m_scr[...] = jnp.full(m_scr.shape, NEG_INF, F32)
        l_scr[...] = jnp.zeros(l_scr.shape, F32)
        acc_scr[...] = jnp.zeros(acc_scr.shape, F32)

        def scores(j, buf, q0=0):
            k = k_ref[0, 0, pl.ds(pl.multiple_of(j * tk, tk), tk), :]
            for c in range(q0 // MLA_KV_CHUNK, q_chunks):
                s_scr[buf, :, c * MLA_KV_CHUNK:(c + 1) * MLA_KV_CHUNK] = _dot(
                    k, qt_ref[0, 0, i * q_chunks + c])

        def softmax_pv(j, buf, q0=0, key_offset=None):
            s = s_scr[buf, :, q0:]
            if key_offset is not None:
                kpos = lax.broadcasted_iota(jnp.int32, s.shape, 0) + key_offset
                qpos = lax.broadcasted_iota(jnp.int32, s.shape, 1) + q0
                s = jnp.where(kpos <= qpos, s, NEG_INF)
            m_old = m_scr[:, q0:]
            m_new = jnp.maximum(m_old, jnp.max(s, axis=0, keepdims=True))
            alpha = jnp.exp2(m_old - m_new)
            p = jnp.exp2(s - m_new)
            l_scr[:, q0:] = alpha * l_scr[:, q0:] + jnp.sum(p, axis=0, keepdims=True)
            pb = p.astype(BF16)
            pv = _dot(vt_ref[0, 0, j * chunks], pb[:MLA_KV_CHUNK])
            for c in range(1, chunks):
                pv += _dot(vt_ref[0, 0, j * chunks + c],
                           pb[c * MLA_KV_CHUNK:(c + 1) * MLA_KV_CHUNK])
            acc_scr[:, q0:] = alpha * acc_scr[:, q0:] + pv
            m_scr[:, q0:] = m_new

        scores(0, 0)

        def pair(jj, c):
            t = 2 * jj
            scores(t + 1, 1)
            softmax_pv(t, 0)
            scores(t + 2, 0)
            softmax_pv(t + 1, 1)
            return c

        lax.fori_loop(0, i, pair, 0)
        scores(2 * i + 1, 1, q0=tk)
        softmax_pv(2 * i, 0, key_offset=0)
        softmax_pv(2 * i + 1, 1, q0=tk, key_offset=tk)

        o = acc_scr[...] / l_scr[...]
        o_ref[0, pl.ds(pl.multiple_of(i * tq, tq), tq), :] = o.T.astype(o_ref.dtype)
        return carry

    lax.fori_loop(0, n_q, q_block, 0)


def _mla(q_m, k_m, vt_m, batch, seq, tq, tk):
    return pl.pallas_call(
        functools.partial(_mla_kernel, tq=tq, tk=tk),
        grid=(batch, MLA_HEADS),
        in_specs=[
            pl.BlockSpec((1, 1, seq // MLA_KV_CHUNK, MLA_QK_DIM, MLA_KV_CHUNK),
                         lambda b, h: (b, h, 0, 0, 0)),
            pl.BlockSpec((1, 1, seq, MLA_QK_DIM), lambda b, h: (b, h, 0, 0)),
            pl.BlockSpec((1, 1, seq // MLA_KV_CHUNK, MLA_V_DIM, MLA_KV_CHUNK),
                         lambda b, h: (b, h, 0, 0, 0)),
        ],
        out_specs=pl.BlockSpec((1, seq, MLA_V_DIM), lambda b, h: (b, 0, h)),
        out_shape=jax.ShapeDtypeStruct((batch, seq, D_MODEL), BF16),
        scratch_shapes=[
            pltpu.VMEM((1, tq), F32),
            pltpu.VMEM((1, tq), F32),
            pltpu.VMEM((MLA_V_DIM, tq), F32),
            pltpu.VMEM((2, tk, tq), F32),
        ],
        compiler_params=pltpu.CompilerParams(
            dimension_semantics=("arbitrary", "arbitrary"), vmem_limit_bytes=VMEM_LIMIT),
        name="mla",
    )(q_m, k_m, vt_m)


def _outproj_kernel(x_ref, oa_ref, ob_ref, ga_ref, gb_ref, wo_ref, gf_ref, wr_ref, br_ref,
                    h_ref, xn_ref, comb_ref):
    merged = (ga_ref[...].astype(F32) * oa_ref[...].astype(F32)
              + gb_ref[...].astype(F32) * ob_ref[...].astype(F32)).astype(BF16)
    h = x_ref[...] + _dot(merged, wo_ref[...])
    h_ref[...] = h
    xn = _rms(h, gf_ref[...])
    xn_ref[...] = xn.astype(BF16)

    x_hi = xn.astype(BF16)
    x_lo = (xn - x_hi.astype(F32)).astype(BF16)
    lg = (_dot(x_hi, wr_ref[0]) + _dot(x_lo, wr_ref[0]) + _dot(x_hi, wr_ref[1])) + br_ref[...]

    lane = lax.broadcasted_iota(jnp.int32, lg.shape, 1)
    is_group = (lane >= N_EXPERTS) & (lane < N_EXPERTS + N_GROUPS)
    gl = jnp.where(is_group, lg, NEG_INF)
    g_max = jnp.max(gl, axis=-1, keepdims=True)
    g_sum = jnp.sum(jnp.where(is_group, jnp.exp(gl - g_max), 0.0), axis=-1, keepdims=True)
    g_w = 1.0 / g_sum
    big = jnp.int32(LANES)
    g_lane = jnp.min(jnp.where(is_group & (gl == g_max), lane, big), axis=-1, keepdims=True)
    gidx = g_lane - N_EXPERTS

    sel = (lane >= gidx * EXPERTS_PER_GROUP) & (lane < (gidx + 1) * EXPERTS_PER_GROUP)
    el = jnp.where(sel, lg, NEG_INF)
    e_max = jnp.max(el, axis=-1, keepdims=True)
    e_exp = jnp.where(sel, jnp.exp(el - e_max), 0.0)
    e_prob = e_exp / jnp.sum(e_exp, axis=-1, keepdims=True)
    p1 = jnp.max(e_prob, axis=-1, keepdims=True)
    i1 = jnp.min(jnp.where(sel & (e_prob == p1), lane, big), axis=-1, keepdims=True)
    rest = sel & (lane != i1)
    p2 = jnp.max(jnp.where(rest, e_prob, -1.0), axis=-1, keepdims=True)
    i2 = jnp.min(jnp.where(rest & (e_prob == p2), lane, big), axis=-1, keepdims=True)
    top_sum = p1 + p2
    comb = jnp.where(lane == i1, p1 / top_sum, jnp.where(lane == i2, p2 / top_sum, 0.0))
    comb_ref[...] = jnp.where(lane == N_EXPERTS, gidx.astype(F32), comb * g_w)


def _outproj(x2, o_a, o_b, g_a, g_b, w_o, g_ffn, w_r, b_r, tm):
    tokens = x2.shape[0]
    tok = lambda t: (t, 0)
    const = lambda t: (0, 0)
    return pl.pallas_call(
        _outproj_kernel,
        grid=(tokens // tm,),
        in_specs=[
            pl.BlockSpec((tm, D_MODEL), tok),
            pl.BlockSpec((tm, D_MODEL), tok),
            pl.BlockSpec((tm, D_MODEL), tok),
            pl.BlockSpec((tm, D_MODEL), tok),
            pl.BlockSpec((tm, D_MODEL), tok),
            pl.BlockSpec((D_MODEL, D_MODEL), const),
            pl.BlockSpec((1, D_MODEL), const),
            pl.BlockSpec((2, D_MODEL, LANES), lambda t: (0, 0, 0)),
            pl.BlockSpec((1, LANES), const),
        ],
        out_specs=(
            pl.BlockSpec((tm, D_MODEL), tok),
            pl.BlockSpec((tm, D_MODEL), tok),
            pl.BlockSpec((tm, LANES), tok),
        ),
        out_shape=(
            jax.ShapeDtypeStruct((tokens, D_MODEL), F32),
            jax.ShapeDtypeStruct((tokens, D_MODEL), BF16),
            jax.ShapeDtypeStruct((tokens, LANES), F32),
        ),
        compiler_params=pltpu.CompilerParams(
            dimension_semantics=("arbitrary",), vmem_limit_bytes=VMEM_LIMIT),
        name="outproj",
    )(x2, o_a, o_b, g_a, g_b, w_o, g_ffn, w_r, b_r)


def _moe_kernel(h_ref, xn_ref, comb_ref, wi_ref, wo_ref, p_ref, gp_ref, wg_ref, wp_ref, gfin_ref,
                o_ref, *, chunk):
    tm = h_ref.shape[0]
    o_ref[...] = h_ref[...]
    xn = xn_ref[...]
    comb = comb_ref[...]
    comb_hi = comb.astype(BF16)
    comb_lo = (comb - comb_hi.astype(F32)).astype(BF16)
    comb_hl = jnp.concatenate([comb_hi, comb_lo], axis=1)
    gid_col = comb[:, N_EXPERTS:N_EXPERTS + 1]
    gid_row = comb.T[N_EXPERTS:N_EXPERTS + 1, :]
    member_col = gid_col == lax.broadcasted_iota(jnp.int32, (tm, LANES), 1).astype(F32)
    member_row = gid_row == lax.broadcasted_iota(jnp.int32, (8, tm), 0).astype(F32)
    r_i = lax.broadcasted_iota(jnp.int32, (tm, tm), 0)
    c_i = lax.broadcasted_iota(jnp.int32, (tm, tm), 1)
    ranks_row = _dot(member_row.astype(BF16), (r_i < c_i).astype(BF16))
    ranks_col = _dot((r_i > c_i).astype(BF16), member_col.astype(BF16))
    pad = 2 * LANES
    never = -4.0 * tm
    slot_row = lax.broadcasted_iota(jnp.int32, (chunk, tm), 0).astype(F32)
    slot_col = lax.broadcasted_iota(jnp.int32, (tm, pad), 1).astype(F32)
    slot_col = jnp.where(slot_col < float(chunk), slot_col, never)

    for g in range(N_GROUPS):
        in_row = member_row[g:g + 1]
        rank_row = jnp.where(in_row, ranks_row[g:g + 1], -2.0 * never)
        rank_col = jnp.where(member_col[:, g:g + 1], ranks_col[:, g:g + 1], -2.0 * never)
        count = jnp.sum(in_row.astype(jnp.int32))
        n_chunks = (count + (chunk - 1)) // chunk

        def chunk_body(c, carry):
            base = (c * chunk).astype(F32)
            gather = jnp.where(rank_row - base == slot_row, 1.0, 0.0).astype(BF16)
            scatter = jnp.where(rank_col - base == slot_col, 1.0, 0.0).astype(BF16)
            xc = _dot(gather, xn).astype(BF16)
            wc = _dot(gather, comb_hl)
            wc = wc[:, :LANES] + wc[:, LANES:]
            yc = jnp.zeros((chunk, D_MODEL), F32)
            for j in range(EXPERTS_PER_GROUP):
                e = g * EXPERTS_PER_GROUP + j
                hid = _dot(xc, wi_ref[e])
                gate = hid[:, :D_EXPERT]
                act = (gate * _sigmoid(gate)) * hid[:, D_EXPERT:] * wc[:, e:e + 1]
                yc = yc + _dot(act.astype(BF16), wo_ref[e])
            yc = jnp.concatenate(
                [yc.astype(BF16), jnp.zeros((pad - chunk, D_MODEL), BF16)], axis=0)
            o_ref[...] += _dot(scatter, yc)
            return carry

        lax.fori_loop(0, n_chunks, chunk_body, 0)

    h = o_ref[...]
    gate = _sigmoid(_dot(_rms(h, gp_ref[...]).astype(BF16), wg_ref[...]))
    h = h + gate * _dot(p_ref[...].astype(BF16), wp_ref[...])
    o_ref[...] = _rms(h, gfin_ref[...])


def _moe_ple(h1, xn, comb, w_ei, w_eo, p2, g_ple, w_pg, w_pp, g_final, tm, chunk):
    tokens = h1.shape[0]
    tok = lambda t: (t, 0)
    const = lambda t: (0, 0)
    return pl.pallas_call(
        functools.partial(_moe_kernel, chunk=chunk),
        grid=(tokens // tm,),
        in_specs=[
            pl.BlockSpec((tm, D_MODEL), tok),
            pl.BlockSpec((tm, D_MODEL), tok),
            pl.BlockSpec((tm, LANES), tok),
            pl.BlockSpec(w_ei.shape, lambda t: (0, 0, 0), pipeline_mode=pl.Buffered(1)),
            pl.BlockSpec(w_eo.shape, lambda t: (0, 0, 0), pipeline_mode=pl.Buffered(1)),
            pl.BlockSpec((tm, PLE_DIM), tok),
            pl.BlockSpec((1, D_MODEL), const),
            pl.BlockSpec((D_MODEL, D_MODEL), const, pipeline_mode=pl.Buffered(1)),
            pl.BlockSpec((PLE_DIM, D_MODEL), const, pipeline_mode=pl.Buffered(1)),
            pl.BlockSpec((1, D_MODEL), const),
        ],
        out_specs=pl.BlockSpec((tm, D_MODEL), tok),
        out_shape=jax.ShapeDtypeStruct((tokens, D_MODEL), F32),
        compiler_params=pltpu.CompilerParams(
            dimension_semantics=("arbitrary",), vmem_limit_bytes=VMEM_LIMIT),
        name="moe_ple",
    )(h1, xn, comb, w_ei, w_eo, p2, g_ple, w_pg, w_pp, g_final)


def _rope_tables(seq):
    half = SWA_HEAD_DIM // 2
    pos = jnp.arange(seq, dtype=F32)
    inv = ROPE_THETA ** (-jnp.arange(0, SWA_HEAD_DIM, 2, dtype=F32) / SWA_HEAD_DIM)
    ang = pos[:, None] * inv[None, :]
    cos, sin = jnp.cos(ang), jnp.sin(ang)
    reps = LANES // SWA_HEAD_DIM
    cos_t = jnp.broadcast_to(cos[:, None, :], (seq, 2 * reps, half)).reshape(seq, LANES)
    sin_t = jnp.broadcast_to(
        jnp.stack([-sin, sin], axis=1)[:, None], (seq, reps, 2, half)).reshape(seq, LANES)
    return cos_t, sin_t, cos.T, sin.T


def _layer(h2d, p2d, g_mix, w_in, sinks, g_q, w_uq, g_kv, w_ukv, w_out, g_ffn, w_rg, b_rg,
           w_re, b_re, w_ei, w_eo, g_ple, w_pg, w_pp, g_final, rope, batch, seq):
    cos_t, sin_t, cos_h, sin_h = rope
    assert SWA_HEAD_DIM == MLA_ROPE_DIM, "one rotary table serves both mixers"
    w_main = w_in[:, :_C_GATES].astype(BF16)
    w_gates = w_in[:, _C_GATES:].astype(BF16)
    assert w_gates.shape[1] == 2 * D_MODEL
    w_uqt = jnp.transpose(
        w_uq.reshape(MLA_Q_RANK, MLA_HEADS, MLA_QK_DIM), (1, 2, 0)).astype(BF16)
    ukv = w_ukv.reshape(MLA_KV_RANK, MLA_HEADS, MLA_NOPE_DIM + MLA_V_DIM)
    w_uk_c = ukv[:, :, :MLA_NOPE_DIM].reshape(MLA_KV_RANK, -1).astype(BF16)
    w_uvt = jnp.transpose(ukv[:, :, MLA_NOPE_DIM:], (1, 2, 0)).astype(BF16)

    q_a, k_a, v_a, g_a, g_b, q_m, k_m, v_m = _inproj(
        h2d, g_mix[None], w_main, w_gates, cos_t, sin_t, cos_h, sin_h, g_q[None], w_uqt,
        g_kv[None], w_uk_c, w_uvt, batch, seq, tm=MLA_KV_CHUNK)

    sink_lanes = jnp.repeat(sinks.astype(F32), SWA_WINDOW)[None]
    o_a = _swa(q_a, k_a, v_a, sink_lanes, batch, seq, tq=512)
    o_b = _mla(q_m, k_m, v_m, batch, seq, tq=1024, tk=512)

    w_r = jnp.concatenate(
        [w_re, w_rg, jnp.zeros((D_MODEL, LANES - N_EXPERTS - N_GROUPS), F32)], axis=1)
    w_r_hi = w_r.astype(BF16)
    w_r_lo = (w_r - w_r_hi.astype(F32)).astype(BF16)
    w_r2 = jnp.stack([w_r_hi, w_r_lo])
    b_r = jnp.concatenate(
        [b_re, b_rg, jnp.zeros((LANES - N_EXPERTS - N_GROUPS,), F32)])[None]
    tokens = batch * seq
    h1, xn2, comb = _outproj(
        h2d, o_a.reshape(tokens, D_MODEL), o_b.reshape(tokens, D_MODEL), g_a, g_b,
        w_out.astype(BF16), g_ffn[None], w_r2, b_r, tm=512)

    return _moe_ple(h1, xn2, comb, w_ei.astype(BF16), w_eo.astype(BF16), p2d, g_ple[None],
                    w_pg.astype(BF16), w_pp.astype(BF16), g_final[None], tm=512, chunk=160)


def kernel(x, p, g_mix, w_in, swa_sinks, mla_g_q, mla_w_uq, mla_g_kv, mla_w_ukv, w_out, g_ffn,
           w_router_group, b_router_group, w_router_expert, b_router_expert, w_expert_in,
           w_expert_out, g_ple, w_ple_gate, w_ple_proj, g_final):
    batch, seq, d = x.shape
    depth = p.shape[0]
    assert d == D_MODEL and depth == 1, "final RMSNorm is fused into the single layer"
    rope = _rope_tables(seq)
    out = _layer(
        x.reshape(batch * seq, d), p[0].reshape(batch * seq, PLE_DIM), g_mix[0], w_in[0],
        swa_sinks[0], mla_g_q[0], mla_w_uq[0], mla_g_kv[0], mla_w_ukv[0], w_out[0], g_ffn[0],
        w_router_group[0], b_router_group[0], w_router_expert[0], b_router_expert[0],
        w_expert_in[0], w_expert_out[0], g_ple[0], w_ple_gate[0], w_ple_proj[0], g_final,
        rope, batch, seq)
    return out.reshape(batch, seq, d)
```

```python
import functools
import math

import jax
import jax.numpy as jnp
from jax import lax
from jax.experimental import pallas as pl
from jax.experimental.pallas import tpu as pltpu

D_MODEL = 1024
PLE_DIM = 256
ROPE_THETA = 10000.0
EPS = 1e-6
NEG_INF = -1e30

SWA_HEAD_DIM = 64
SWA_HEADS = D_MODEL // SWA_HEAD_DIM
SWA_KV_HEADS = SWA_HEADS // 8
SWA_GROUP = SWA_HEADS // SWA_KV_HEADS
SWA_WINDOW = 128

MLA_NOPE_DIM = 128
MLA_ROPE_DIM = 64
MLA_V_DIM = 128
MLA_HEADS = D_MODEL // MLA_V_DIM
MLA_Q_RANK = 256
MLA_KV_RANK = 128
MLA_QK_DIM = MLA_NOPE_DIM + MLA_ROPE_DIM

N_GROUPS = 4
EXPERTS_PER_GROUP = 4
N_EXPERTS = N_GROUPS * EXPERTS_PER_GROUP
D_EXPERT = 256

LANES = 128
LOG2_E = 1.4426950408889634
MLA_KV_CHUNK = 512
VMEM_LIMIT = 56 * 1024 * 1024

_C_QA = 0
_C_KV = _C_QA + SWA_HEADS * SWA_HEAD_DIM
_C_CQ = _C_KV + 2 * SWA_KV_HEADS * SWA_HEAD_DIM
_C_CKVR = _C_CQ + MLA_Q_RANK
_C_GATES = _C_CKVR + MLA_KV_RANK + MLA_ROPE_DIM

BF16 = jnp.bfloat16
F32 = jnp.float32


def _dot(a, b):
    return jnp.dot(a, b, preferred_element_type=F32)


def _dot_nt(a, b):
    return lax.dot_general(a, b, (((1,), (1,)), ((), ())), preferred_element_type=F32)


def _rms(x, g):
    r = lax.rsqrt(jnp.mean(x * x, axis=-1, keepdims=True) + EPS)
    return x * r * g


def _sigmoid(x):
    return 1.0 / (1.0 + jnp.exp(-x))


def _rope_lanes(x, cos, sin_signed):
    lane = lax.broadcasted_iota(jnp.int32, x.shape, 1)
    upper = (lane & (SWA_HEAD_DIM // 2)) != 0
    partner = jnp.where(upper, pltpu.roll(x, SWA_HEAD_DIM // 2, 1),
                        pltpu.roll(x, LANES - SWA_HEAD_DIM // 2, 1))
    return x * cos + partner * sin_signed


def _inproj_kernel(x_ref, g_ref, w_ref, wg_ref, cos_ref, sin_ref, cost_ref, sint_ref, gq_ref,
                   wuqt_ref, gkv_ref, wuk_ref, wuvt_ref, qa_ref, ka_ref, va_ref, ga_ref, gb_ref,
                   qm_ref, km_ref, vt_ref):
    xn = _rms(x_ref[...], g_ref[...]).astype(BF16)
    cos = cos_ref[...]
    sin = sin_ref[...]

    qa_scale = LOG2_E / math.sqrt(SWA_HEAD_DIM)
    qa = _dot(xn, w_ref[:, _C_QA:_C_KV])
    for j in range(SWA_HEADS // 2):
        q = (_rope_lanes(qa[:, j * LANES:(j + 1) * LANES], cos, sin) * qa_scale).astype(BF16)
        qa_ref[0, 2 * j] = q[:, :SWA_HEAD_DIM]
        qa_ref[0, 2 * j + 1] = q[:, SWA_HEAD_DIM:]
    kv = _dot(xn, w_ref[:, _C_KV:_C_CQ])
    k = _rope_lanes(kv[:, :LANES], cos, sin).astype(BF16)
    ka_ref[0, 0] = k[:, :SWA_HEAD_DIM]
    ka_ref[0, 1] = k[:, SWA_HEAD_DIM:]
    vt = kv[:, LANES:].T.astype(BF16)
    va_ref[0, 0] = vt[:SWA_HEAD_DIM]
    va_ref[0, 1] = vt[SWA_HEAD_DIM:]

    ga_ref[...] = _sigmoid(_dot(xn, wg_ref[:, :D_MODEL])).astype(BF16)
    gb_ref[...] = _sigmoid(_dot(xn, wg_ref[:, D_MODEL:])).astype(BF16)

    cq = _rms(_dot(xn, w_ref[:, _C_CQ:_C_CKVR]), gq_ref[...])
    ckvr = _dot(xn, w_ref[:, _C_CKVR:_C_GATES])
    ckv = _rms(ckvr[:, :MLA_KV_RANK], gkv_ref[...])
    kr = jnp.concatenate(
        [ckvr[:, MLA_KV_RANK:], jnp.zeros((xn.shape[0], LANES - MLA_ROPE_DIM), F32)], axis=1)
    kr = _rope_lanes(kr, cos, sin)[:, :MLA_ROPE_DIM].astype(BF16)
    qm_scale = LOG2_E / math.sqrt(MLA_QK_DIM)
    kn = _dot(ckv.astype(BF16), wuk_ref[...]).astype(BF16)
    cq_t = cq.T.astype(BF16)
    ckv_t = ckv.T.astype(BF16)
    cos_h, sin_h = cost_ref[...], sint_ref[...]
    half = MLA_ROPE_DIM // 2
    for h in range(MLA_HEADS):
        km_ref[0, h, :, :MLA_NOPE_DIM] = kn[:, h * MLA_NOPE_DIM:(h + 1) * MLA_NOPE_DIM]
        km_ref[0, h, :, MLA_NOPE_DIM:] = kr
        vt_ref[0, h, 0] = _dot(wuvt_ref[h], ckv_t).astype(BF16)
        q_t = _dot(wuqt_ref[h], cq_t)
        x1 = q_t[MLA_NOPE_DIM:MLA_NOPE_DIM + half]
        x2 = q_t[MLA_NOPE_DIM + half:]
        q_t = jnp.concatenate(
            [q_t[:MLA_NOPE_DIM], x1 * cos_h - x2 * sin_h, x2 * cos_h + x1 * sin_h], axis=0)
        qm_ref[0, h, 0] = (q_t * qm_scale).astype(BF16)


def _inproj(x2, g_mix, w_main, w_gates, cos_t, sin_t, cos_h, sin_h, g_q, w_uqt, g_kv, w_uk, w_uvt,
            batch, seq, tm):
    tokens = batch * seq
    nt = seq // tm
    const = lambda t: (0, 0)
    tok = lambda t: (t, 0)
    head = lambda t: (t // nt, 0, t % nt, 0)
    out_shape = (
        jax.ShapeDtypeStruct((batch, SWA_HEADS, seq, SWA_HEAD_DIM), BF16),
        jax.ShapeDtypeStruct((batch, SWA_KV_HEADS, seq, SWA_HEAD_DIM), BF16),
        jax.ShapeDtypeStruct((batch, SWA_KV_HEADS, SWA_HEAD_DIM, seq), BF16),
        jax.ShapeDtypeStruct((tokens, D_MODEL), BF16),
        jax.ShapeDtypeStruct((tokens, D_MODEL), BF16),
        jax.ShapeDtypeStruct((batch, MLA_HEADS, nt, MLA_QK_DIM, tm), BF16),
        jax.ShapeDtypeStruct((batch, MLA_HEADS, seq, MLA_QK_DIM), BF16),
        jax.ShapeDtypeStruct((batch, MLA_HEADS, nt, MLA_V_DIM, tm), BF16),
    )
    return pl.pallas_call(
        _inproj_kernel,
        grid=(tokens // tm,),
        in_specs=[
            pl.BlockSpec((tm, D_MODEL), tok),
            pl.BlockSpec((1, D_MODEL), const),
            pl.BlockSpec(w_main.shape, const, pipeline_mode=pl.Buffered(1)),
            pl.BlockSpec(w_gates.shape, const, pipeline_mode=pl.Buffered(1)),
            pl.BlockSpec((tm, LANES), lambda t: (t % nt, 0)),
            pl.BlockSpec((tm, LANES), lambda t: (t % nt, 0)),
            pl.BlockSpec((MLA_ROPE_DIM // 2, tm), lambda t: (0, t % nt)),
            pl.BlockSpec((MLA_ROPE_DIM // 2, tm), lambda t: (0, t % nt)),
            pl.BlockSpec((1, MLA_Q_RANK), const),
            pl.BlockSpec(w_uqt.shape, lambda t: (0, 0, 0), pipeline_mode=pl.Buffered(1)),
            pl.BlockSpec((1, MLA_KV_RANK), const),
            pl.BlockSpec(w_uk.shape, const, pipeline_mode=pl.Buffered(1)),
            pl.BlockSpec(w_uvt.shape, lambda t: (0, 0, 0), pipeline_mode=pl.Buffered(1)),
        ],
        out_specs=(
            pl.BlockSpec((1, SWA_HEADS, tm, SWA_HEAD_DIM), head),
            pl.BlockSpec((1, SWA_KV_HEADS, tm, SWA_HEAD_DIM), head),
            pl.BlockSpec((1, SWA_KV_HEADS, SWA_HEAD_DIM, tm), lambda t: (t // nt, 0, 0, t % nt)),
            pl.BlockSpec((tm, D_MODEL), tok),
            pl.BlockSpec((tm, D_MODEL), tok),
            pl.BlockSpec((1, MLA_HEADS, 1, MLA_QK_DIM, tm), lambda t: (t // nt, 0, t % nt, 0, 0)),
            pl.BlockSpec((1, MLA_HEADS, tm, MLA_QK_DIM), head),
            pl.BlockSpec((1, MLA_HEADS, 1, MLA_V_DIM, tm), lambda t: (t // nt, 0, t % nt, 0, 0)),
        ),
        out_shape=out_shape,
        compiler_params=pltpu.CompilerParams(
            dimension_semantics=("arbitrary",), vmem_limit_bytes=VMEM_LIMIT),
        name="inproj",
    )(x2, g_mix, w_main, w_gates, cos_t, sin_t, cos_h, sin_h, g_q, w_uqt, g_kv, w_uk, w_uvt)


def _swa_kernel(q_ref, k_ref, kp_ref, vt_ref, vtp_ref, sink_ref, o_ref, s_scr, bias_scr, *,
                n_sub):
    blk = SWA_WINDOW
    cols = SWA_GROUP * blk

    @pl.when((pl.program_id(0) == 0) & (pl.program_id(1) == 0))
    def _():
        kj = lax.broadcasted_iota(jnp.int32, (2 * blk, cols), 0)
        qi = lax.broadcasted_iota(jnp.int32, (2 * blk, cols), 1) & (blk - 1)
        dist = blk + qi - kj
        bias_scr[...] = jnp.where((dist >= 0) & (dist < SWA_WINDOW), 0.0, NEG_INF)

    key_row = lax.broadcasted_iota(jnp.int32, (2 * blk, 1), 0)
    no_prev = jnp.where((key_row < blk) & (pl.program_id(1) == 0), NEG_INF, 0.0)
    units = [(g, n) for g in range(SWA_KV_HEADS) for n in range(n_sub)]

    def scores(u, buf):
        g, n = u
        q = q_ref[0, g * SWA_GROUP:(g + 1) * SWA_GROUP, n * blk:(n + 1) * blk, :]
        k_prev = kp_ref[0, g] if n == 0 else k_ref[0, g, (n - 1) * blk:n * blk, :]
        kk = jnp.concatenate([k_prev, k_ref[0, g, n * blk:(n + 1) * blk, :]], axis=0)
        s_scr[buf] = _dot_nt(kk, q.reshape(cols, SWA_HEAD_DIM))

    def softmax_pv(u, buf):
        g, n = u
        sink = sink_ref[:, g * cols:(g + 1) * cols] * LOG2_E
        s = s_scr[buf] + bias_scr[...]
        if n == 0:
            s = s + no_prev
        m = jnp.maximum(jnp.max(s, axis=0, keepdims=True), sink)
        e = jnp.exp2(s - m)
        denom = jnp.sum(e, axis=0, keepdims=True) + jnp.exp2(sink - m)
        vt_prev = vtp_ref[0, g] if n == 0 else vt_ref[0, g, :, (n - 1) * blk:n * blk]
        vvt = jnp.concatenate([vt_prev, vt_ref[0, g, :, n * blk:(n + 1) * blk]], axis=1)
        o = (_dot(vvt, e.astype(BF16)) / denom).T
        for hh in range(SWA_GROUP):
            h = g * SWA_GROUP + hh
            o_ref[0, n * blk:(n + 1) * blk, h * SWA_HEAD_DIM:(h + 1) * SWA_HEAD_DIM] = (
                o[hh * blk:(hh + 1) * blk, :].astype(BF16))

    scores(units[0], 0)
    for idx, u in enumerate(units):
        if idx + 1 < len(units):
            scores(units[idx + 1], (idx + 1) % 2)
        softmax_pv(u, idx % 2)


def _swa(q_a, k_a, vt_a, sink_lanes, batch, seq, tq):
    n_sub = tq // SWA_WINDOW
    cur = lambda b, i: (b, 0, i, 0)
    prev = lambda b, i: (b, 0, jnp.maximum(i * n_sub - 1, 0), 0)
    return pl.pallas_call(
        functools.partial(_swa_kernel, n_sub=n_sub),
        grid=(batch, seq // tq),
        in_specs=[
            pl.BlockSpec((1, SWA_HEADS, tq, SWA_HEAD_DIM), cur),
            pl.BlockSpec((1, SWA_KV_HEADS, tq, SWA_HEAD_DIM), cur),
            pl.BlockSpec((1, SWA_KV_HEADS, SWA_WINDOW, SWA_HEAD_DIM), prev),
            pl.BlockSpec((1, SWA_KV_HEADS, SWA_HEAD_DIM, tq), lambda b, i: (b, 0, 0, i)),
            pl.BlockSpec((1, SWA_KV_HEADS, SWA_HEAD_DIM, SWA_WINDOW),
                         lambda b, i: (b, 0, 0, jnp.maximum(i * n_sub - 1, 0))),
            pl.BlockSpec(sink_lanes.shape, lambda b, i: (0, 0)),
        ],
        out_specs=pl.BlockSpec((1, tq, D_MODEL), lambda b, i: (b, i, 0)),
        out_shape=jax.ShapeDtypeStruct((batch, seq, D_MODEL), BF16),
        scratch_shapes=[pltpu.VMEM((2, 2 * SWA_WINDOW, SWA_GROUP * SWA_WINDOW), F32),
                        pltpu.VMEM((2 * SWA_WINDOW, SWA_GROUP * SWA_WINDOW), F32)],
        compiler_params=pltpu.CompilerParams(
            dimension_semantics=("arbitrary", "arbitrary"), vmem_limit_bytes=VMEM_LIMIT),
        name="swa",
    )(q_a, k_a, k_a, vt_a, vt_a, sink_lanes)


def _mla_kernel(qt_ref, k_ref, vt_ref, o_ref, m_scr, l_scr, acc_scr, s_scr, *, tq, tk):
    assert tq == 2 * tk
    chunks = tk // MLA_KV_CHUNK
    q_chunks = tq // MLA_KV_CHUNK
    n_q = qt_ref.shape[2] // q_chunks

    def q_block(i, carry):
        m_scr[...] = jnp.full(m_scr.shape, NEG_INF, F32)
        l_scr[...] = jnp.zeros(l_scr.shape, F32)
        acc_scr[...] = jnp.zeros(acc_scr.shape, F32)

        def scores(j, buf, q0=0):
            k = k_ref[0, 0, pl.ds(pl.multiple_of(j * tk, tk), tk), :]
            for c in range(q0 // MLA_KV_CHUNK, q_chunks):
                s_scr[buf, :, c * MLA_KV_CHUNK:(c + 1) * MLA_KV_CHUNK] = _dot(
                    k, qt_ref[0, 0, i * q_chunks + c])

        def softmax_pv(j, buf, q0=0, key_offset=None):
            s = s_scr[buf, :, q0:]
            if key_offset is not None:
                kpos = lax.broadcasted_iota(jnp.int32, s.shape, 0) + key_offset
                qpos = lax.broadcasted_iota(jnp.int32, s.shape, 1) + q0
                s = jnp.where(kpos <= qpos, s, NEG_INF)
            m_old = m_scr[:, q0:]
            m_new = jnp.maximum(m_old, jnp.max(s, axis=0, keepdims=True))
            alpha = jnp.exp2(m_old - m_new)
            p = jnp.exp2(s - m_new)
            l_scr[:, q0:] = alpha * l_scr[:, q0:] + jnp.sum(p, axis=0, keepdims=True)
            pb = p.astype(BF16)
            pv = _dot(vt_ref[0, 0, j * chunks], pb[:MLA_KV_CHUNK])
            for c in range(1, chunks):
                pv += _dot(vt_ref[0, 0, j * chunks + c],
                           pb[c * MLA_KV_CHUNK:(c + 1) * MLA_KV_CHUNK])
            acc_scr[:, q0:] = alpha * acc_scr[:, q0:] + pv
            m_scr[:, q0:] = m_new

        scores(0, 0)

        def pair(jj, c):
            t = 2 * jj
            scores(t + 1, 1)
            softmax_pv(t, 0)
            scores(t + 2, 0)
            softmax_pv(t + 1, 1)
            return c

        lax.fori_loop(0, i, pair, 0)
        scores(2 * i + 1, 1, q0=tk)
        softmax_pv(2 * i, 0, key_offset=0)
        softmax_pv(2 * i + 1, 1, q0=tk, key_offset=tk)

        o = acc_scr[...] / l_scr[...]
        o_ref[0, pl.ds(pl.multiple_of(i * tq, tq), tq), :] = o.T.astype(o_ref.dtype)
        return carry

    lax.fori_loop(0, n_q, q_block, 0)


def _mla(q_m, k_m, vt_m, batch, seq, tq, tk):
    return pl.pallas_call(
        functools.partial(_mla_kernel, tq=tq, tk=tk),
        grid=(batch, MLA_HEADS),
        in_specs=[
            pl.BlockSpec((1, 1, seq // MLA_KV_CHUNK, MLA_QK_DIM, MLA_KV_CHUNK),
                         lambda b, h: (b, h, 0, 0, 0)),
            pl.BlockSpec((1, 1, seq, MLA_QK_DIM), lambda b, h: (b, h, 0, 0)),
            pl.BlockSpec((1, 1, seq // MLA_KV_CHUNK, MLA_V_DIM, MLA_KV_CHUNK),
                         lambda b, h: (b, h, 0, 0, 0)),
        ],
        out_specs=pl.BlockSpec((1, seq, MLA_V_DIM), lambda b, h: (b, 0, h)),
        out_shape=jax.ShapeDtypeStruct((batch, seq, D_MODEL), BF16),
        scratch_shapes=[
            pltpu.VMEM((1, tq), F32),
            pltpu.VMEM((1, tq), F32),
            pltpu.VMEM((MLA_V_DIM, tq), F32),
            pltpu.VMEM((2, tk, tq), F32),
        ],
        compiler_params=pltpu.CompilerParams(
            dimension_semantics=("arbitrary", "arbitrary"), vmem_limit_bytes=VMEM_LIMIT),
        name="mla",
    )(q_m, k_m, vt_m)


def _outproj_kernel(x_ref, oa_ref, ob_ref, ga_ref, gb_ref, wo_ref, gf_ref, wr_ref, br_ref,
                    h_ref, xn_ref, comb_ref):
    merged = (ga_ref[...].astype(F32) * oa_ref[...].astype(F32)
              + gb_ref[...].astype(F32) * ob_ref[...].astype(F32)).astype(BF16)
    h = x_ref[...] + _dot(merged, wo_ref[...])
    h_ref[...] = h
    xn = _rms(h, gf_ref[...])
    xn_ref[...] = xn.astype(BF16)

    x_hi = xn.astype(BF16)
    x_lo = (xn - x_hi.astype(F32)).astype(BF16)
    lg = (_dot(x_hi, wr_ref[0]) + _dot(x_lo, wr_ref[0]) + _dot(x_hi, wr_ref[1])) + br_ref[...]

    lane = lax.broadcasted_iota(jnp.int32, lg.shape, 1)
    is_group = (lane >= N_EXPERTS) & (lane < N_EXPERTS + N_GROUPS)
    gl = jnp.where(is_group, lg, NEG_INF)
    g_max = jnp.max(gl, axis=-1, keepdims=True)
    g_sum = jnp.sum(jnp.where(is_group, jnp.exp(gl - g_max), 0.0), axis=-1, keepdims=True)
    g_w = 1.0 / g_sum
    big = jnp.int32(LANES)
    g_lane = jnp.min(jnp.where(is_group & (gl == g_max), lane, big), axis=-1, keepdims=True)
    gidx = g_lane - N_EXPERTS

    sel = (lane >= gidx * EXPERTS_PER_GROUP) & (lane < (gidx + 1) * EXPERTS_PER_GROUP)
    el = jnp.where(sel, lg, NEG_INF)
    e_max = jnp.max(el, axis=-1, keepdims=True)
    e_exp = jnp.where(sel, jnp.exp(el - e_max), 0.0)
    e_prob = e_exp / jnp.sum(e_exp, axis=-1, keepdims=True)
    p1 = jnp.max(e_prob, axis=-1, keepdims=True)
    i1 = jnp.min(jnp.where(sel & (e_prob == p1), lane, big), axis=-1, keepdims=True)
    rest = sel & (lane != i1)
    p2 = jnp.max(jnp.where(rest, e_prob, -1.0), axis=-1, keepdims=True)
    i2 = jnp.min(jnp.where(rest & (e_prob == p2), lane, big), axis=-1, keepdims=True)
    top_sum = p1 + p2
    comb = jnp.where(lane == i1, p1 / top_sum, jnp.where(lane == i2, p2 / top_sum, 0.0))
    comb_ref[...] = jnp.where(lane == N_EXPERTS, gidx.astype(F32), comb * g_w)


def _outproj(x2, o_a, o_b, g_a, g_b, w_o, g_ffn, w_r, b_r, tm):
    tokens = x2.shape[0]
    tok = lambda t: (t, 0)
    const = lambda t: (0, 0)
    return pl.pallas_call(
        _outproj_kernel,
        grid=(tokens // tm,),
        in_specs=[
            pl.BlockSpec((tm, D_MODEL), tok),
            pl.BlockSpec((tm, D_MODEL), tok),
            pl.BlockSpec((tm, D_MODEL), tok),
            pl.BlockSpec((tm, D_MODEL), tok),
            pl.BlockSpec((tm, D_MODEL), tok),
            pl.BlockSpec((D_MODEL, D_MODEL), const),
            pl.BlockSpec((1, D_MODEL), const),
            pl.BlockSpec((2, D_MODEL, LANES), lambda t: (0, 0, 0)),
            pl.BlockSpec((1, LANES), const),
        ],
        out_specs=(
            pl.BlockSpec((tm, D_MODEL), tok),
            pl.BlockSpec((tm, D_MODEL), tok),
            pl.BlockSpec((tm, LANES), tok),
        ),
        out_shape=(
            jax.ShapeDtypeStruct((tokens, D_MODEL), F32),
            jax.ShapeDtypeStruct((tokens, D_MODEL), BF16),
            jax.ShapeDtypeStruct((tokens, LANES), F32),
        ),
        compiler_params=pltpu.CompilerParams(
            dimension_semantics=("arbitrary",), vmem_limit_bytes=VMEM_LIMIT),
        name="outproj",
    )(x2, o_a, o_b, g_a, g_b, w_o, g_ffn, w_r, b_r)


def _moe_kernel(h_ref, xn_ref, comb_ref, wi_ref, wo_ref, p_ref, gp_ref, wg_ref, wp_ref, gfin_ref,
                o_ref, *, chunk):
    tm = h_ref.shape[0]
    o_ref[...] = h_ref[...]
    xn = xn_ref[...]
    comb = comb_ref[...]
    comb_hi = comb.astype(BF16)
    comb_lo = (comb - comb_hi.astype(F32)).astype(BF16)
    comb_hl = jnp.concatenate([comb_hi, comb_lo], axis=1)
    gid_col = comb[:, N_EXPERTS:N_EXPERTS + 1]
    gid_row = comb.T[N_EXPERTS:N_EXPERTS + 1, :]
    member_col = gid_col == lax.broadcasted_iota(jnp.int32, (tm, LANES), 1).astype(F32)
    member_row = gid_row == lax.broadcasted_iota(jnp.int32, (8, tm), 0).astype(F32)
    r_i = lax.broadcasted_iota(jnp.int32, (tm, tm), 0)
    c_i = lax.broadcasted_iota(jnp.int32, (tm, tm), 1)
    ranks_row = _dot(member_row.astype(BF16), (r_i < c_i).astype(BF16))
    ranks_col = _dot((r_i > c_i).astype(BF16), member_col.astype(BF16))
    pad = 2 * LANES
    never = -4.0 * tm
    slot_row = lax.broadcasted_iota(jnp.int32, (chunk, tm), 0).astype(F32)
    slot_col = lax.broadcasted_iota(jnp.int32, (tm, pad), 1).astype(F32)
    slot_col = jnp.where(slot_col < float(chunk), slot_col, never)

    groups = range(N_GROUPS)
    in_rows = [member_row[g:g + 1] for g in groups]
    rank_rows = [jnp.where(in_rows[g], ranks_row[g:g + 1], -2.0 * never) for g in groups]
    rank_cols = [jnp.where(member_col[:, g:g + 1], ranks_col[:, g:g + 1], -2.0 * never)
                 for g in groups]

    def one_hots(g, base):
        gather = jnp.where(rank_rows[g] - base == slot_row, 1.0, 0.0).astype(BF16)
        scatter = jnp.where(rank_cols[g] - base == slot_col, 1.0, 0.0).astype(BF16)
        return gather, scatter

    def compact(gather):
        xc = _dot(gather, xn).astype(BF16)
        wc = _dot(gather, comb_hl)
        return xc, wc[:, :LANES] + wc[:, LANES:]

    def experts_in(g, xc):
        return [_dot(xc, wi_ref[g * EXPERTS_PER_GROUP + j]) for j in range(EXPERTS_PER_GROUP)]

    def activate(g, hids, wc):
        acts = []
        for j, hid in enumerate(hids):
            e = g * EXPERTS_PER_GROUP + j
            gate = hid[:, :D_EXPERT]
            act = (gate * _sigmoid(gate)) * hid[:, D_EXPERT:] * wc[:, e:e + 1]
            acts.append(act.astype(BF16))
        return acts

    def experts_out(g, acts):
        yc = _dot(acts[0], wo_ref[g * EXPERTS_PER_GROUP])
        for j in range(1, EXPERTS_PER_GROUP):
            yc = yc + _dot(acts[j], wo_ref[g * EXPERTS_PER_GROUP + j])
        return jnp.concatenate(
            [yc.astype(BF16), jnp.zeros((pad - chunk, D_MODEL), BF16)], axis=0)

    hots = [one_hots(g, 0.0) for g in groups]
    compacted = [compact(hots[g][0]) for g in groups]
    hidden = [experts_in(g, compacted[g][0]) for g in groups]
    acts = [activate(g, hidden[g], compacted[g][1]) for g in groups]
    outs = [experts_out(g, acts[g]) for g in groups]
    o_ref[...] += _dot(jnp.concatenate([hots[g][1] for g in groups], axis=1),
                       jnp.concatenate(outs, axis=0))

    for g in groups:
        count = jnp.sum(in_rows[g].astype(jnp.int32))
        n_chunks = (count + (chunk - 1)) // chunk

        def chunk_body(c, carry, g=g):
            gather, scatter = one_hots(g, (c * chunk).astype(F32))
            xc, wc = compact(gather)
            yc = experts_out(g, activate(g, experts_in(g, xc), wc))
            o_ref[...] += _dot(scatter, yc)
            return carry

        lax.fori_loop(1, n_chunks, chunk_body, 0)

    h = o_ref[...]
    gate = _sigmoid(_dot(_rms(h, gp_ref[...]).astype(BF16), wg_ref[...]))
    h = h + gate * _dot(p_ref[...].astype(BF16), wp_ref[...])
    o_ref[...] = _rms(h, gfin_ref[...])


def _moe_ple(h1, xn, comb, w_ei, w_eo, p2, g_ple, w_pg, w_pp, g_final, tm, chunk):
    tokens = h1.shape[0]
    tok = lambda t: (t, 0)
    const = lambda t: (0, 0)
    return pl.pallas_call(
        functools.partial(_moe_kernel, chunk=chunk),
        grid=(tokens // tm,),
        in_specs=[
            pl.BlockSpec((tm, D_MODEL), tok),
            pl.BlockSpec((tm, D_MODEL), tok),
            pl.BlockSpec((tm, LANES), tok),
            pl.BlockSpec(w_ei.shape, lambda t: (0, 0, 0), pipeline_mode=pl.Buffered(1)),
            pl.BlockSpec(w_eo.shape, lambda t: (0, 0, 0), pipeline_mode=pl.Buffered(1)),
            pl.BlockSpec((tm, PLE_DIM), tok),
            pl.BlockSpec((1, D_MODEL), const),
            pl.BlockSpec((D_MODEL, D_MODEL), const, pipeline_mode=pl.Buffered(1)),
            pl.BlockSpec((PLE_DIM, D_MODEL), const, pipeline_mode=pl.Buffered(1)),
            pl.BlockSpec((1, D_MODEL), const),
        ],
        out_specs=pl.BlockSpec((tm, D_MODEL), tok),
        out_shape=jax.ShapeDtypeStruct((tokens, D_MODEL), F32),
        compiler_params=pltpu.CompilerParams(
            dimension_semantics=("arbitrary",), vmem_limit_bytes=VMEM_LIMIT),
        name="moe_ple",
    )(h1, xn, comb, w_ei, w_eo, p2, g_ple, w_pg, w_pp, g_final)


def _rope_tables(seq):
    half = SWA_HEAD_DIM // 2
    pos = jnp.arange(seq, dtype=F32)
    inv = ROPE_THETA ** (-jnp.arange(0, SWA_HEAD_DIM, 2, dtype=F32) / SWA_HEAD_DIM)
    ang = pos[:, None] * inv[None, :]
    cos, sin = jnp.cos(ang), jnp.sin(ang)
    reps = LANES // SWA_HEAD_DIM
    cos_t = jnp.broadcast_to(cos[:, None, :], (seq, 2 * reps, half)).reshape(seq, LANES)
    sin_t = jnp.broadcast_to(
        jnp.stack([-sin, sin], axis=1)[:, None], (seq, reps, 2, half)).reshape(seq, LANES)
    return cos_t, sin_t, cos.T, sin.T


def _layer(h2d, p2d, g_mix, w_in, sinks, g_q, w_uq, g_kv, w_ukv, w_out, g_ffn, w_rg, b_rg,
           w_re, b_re, w_ei, w_eo, g_ple, w_pg, w_pp, g_final, rope, batch, seq):
    cos_t, sin_t, cos_h, sin_h = rope
    assert SWA_HEAD_DIM == MLA_ROPE_DIM, "one rotary table serves both mixers"
    w_main = w_in[:, :_C_GATES].astype(BF16)
    w_gates = w_in[:, _C_GATES:].astype(BF16)
    assert w_gates.shape[1] == 2 * D_MODEL
    w_uqt = jnp.transpose(
        w_uq.reshape(MLA_Q_RANK, MLA_HEADS, MLA_QK_DIM), (1, 2, 0)).astype(BF16)
    ukv = w_ukv.reshape(MLA_KV_RANK, MLA_HEADS, MLA_NOPE_DIM + MLA_V_DIM)
    w_uk_c = ukv[:, :, :MLA_NOPE_DIM].reshape(MLA_KV_RANK, -1).astype(BF16)
    w_uvt = jnp.transpose(ukv[:, :, MLA_NOPE_DIM:], (1, 2, 0)).astype(BF16)

    q_a, k_a, v_a, g_a, g_b, q_m, k_m, v_m = _inproj(
        h2d, g_mix[None], w_main, w_gates, cos_t, sin_t, cos_h, sin_h, g_q[None], w_uqt,
        g_kv[None], w_uk_c, w_uvt, batch, seq, tm=MLA_KV_CHUNK)

    sink_lanes = jnp.repeat(sinks.astype(F32), SWA_WINDOW)[None]
    o_a = _swa(q_a, k_a, v_a, sink_lanes, batch, seq, tq=512)
    o_b = _mla(q_m, k_m, v_m, batch, seq, tq=1024, tk=512)

    w_r = jnp.concatenate(
        [w_re, w_rg, jnp.zeros((D_MODEL, LANES - N_EXPERTS - N_GROUPS), F32)], axis=1)
    w_r_hi = w_r.astype(BF16)
    w_r_lo = (w_r - w_r_hi.astype(F32)).astype(BF16)
    w_r2 = jnp.stack([w_r_hi, w_r_lo])
    b_r = jnp.concatenate(
        [b_re, b_rg, jnp.zeros((LANES - N_EXPERTS - N_GROUPS,), F32)])[None]
    tokens = batch * seq
    h1, xn2, comb = _outproj(
        h2d, o_a.reshape(tokens, D_MODEL), o_b.reshape(tokens, D_MODEL), g_a, g_b,
        w_out.astype(BF16), g_ffn[None], w_r2, b_r, tm=512)

    return _moe_ple(h1, xn2, comb, w_ei.astype(BF16), w_eo.astype(BF16), p2d, g_ple[None],
                    w_pg.astype(BF16), w_pp.astype(BF16), g_final[None], tm=512, chunk=160)


def kernel(x, p, g_mix, w_in, swa_sinks, mla_g_q, mla_w_uq, mla_g_kv, mla_w_ukv, w_out, g_ffn,
           w_router_group, b_router_group, w_router_expert, b_router_expert, w_expert_in,
           w_expert_out, g_ple, w_ple_gate, w_ple_proj, g_final):
    batch, seq, d = x.shape
    depth = p.shape[0]
    assert d == D_MODEL and depth == 1, "final RMSNorm is fused into the single layer"
    rope = _rope_tables(seq)
    out = _layer(
        x.reshape(batch * seq, d), p[0].reshape(batch * seq, PLE_DIM), g_mix[0], w_in[0],
        swa_sinks[0], mla_g_q[0], mla_w_uq[0], mla_g_kv[0], mla_w_ukv[0], w_out[0], g_ffn[0],
        w_router_group[0], b_router_group[0], w_router_expert[0], b_router_expert[0],
        w_expert_in[0], w_expert_out[0], g_ple[0], w_ple_gate[0], w_ple_proj[0], g_final,
        rope, batch, seq)
    return out.reshape(batch, seq, d)
```

```python
import functools
import math

import jax
import jax.numpy as jnp
from jax import lax
from jax.experimental import pallas as pl
from jax.experimental.pallas import tpu as pltpu

D_MODEL = 1024
PLE_DIM = 256
ROPE_THETA = 10000.0
EPS = 1e-6
NEG_INF = -1e30

SWA_HEAD_DIM = 64
SWA_HEADS = D_MODEL // SWA_HEAD_DIM
SWA_KV_HEADS = SWA_HEADS // 8
SWA_GROUP = SWA_HEADS // SWA_KV_HEADS
SWA_WINDOW = 128

MLA_NOPE_DIM = 128
MLA_ROPE_DIM = 64
MLA_V_DIM = 128
MLA_HEADS = D_MODEL // MLA_V_DIM
MLA_Q_RANK = 256
MLA_KV_RANK = 128
MLA_QK_DIM = MLA_NOPE_DIM + MLA_ROPE_DIM

N_GROUPS = 4
EXPERTS_PER_GROUP = 4
N_EXPERTS = N_GROUPS * EXPERTS_PER_GROUP
D_EXPERT = 256

LANES = 128
LOG2_E = 1.4426950408889634
MLA_KV_CHUNK = 512
VMEM_LIMIT = 56 * 1024 * 1024

_C_QA = 0
_C_KV = _C_QA + SWA_HEADS * SWA_HEAD_DIM
_C_CQ = _C_KV + 2 * SWA_KV_HEADS * SWA_HEAD_DIM
_C_CKVR = _C_CQ + MLA_Q_RANK
_C_GATES = _C_CKVR + MLA_KV_RANK + MLA_ROPE_DIM

BF16 = jnp.bfloat16
F32 = jnp.float32


def _dot(a, b):
    return jnp.dot(a, b, preferred_element_type=F32)


def _dot_nt(a, b):
    return lax.dot_general(a, b, (((1,), (1,)), ((), ())), preferred_element_type=F32)


def _rms(x, g):
    r = lax.rsqrt(jnp.mean(x * x, axis=-1, keepdims=True) + EPS)
    return x * r * g


def _sigmoid(x):
    return 1.0 / (1.0 + jnp.exp(-x))


def _rope_lanes(x, cos, sin_signed):
    lane = lax.broadcasted_iota(jnp.int32, x.shape, 1)
    upper = (lane & (SWA_HEAD_DIM // 2)) != 0
    partner = jnp.where(upper, pltpu.roll(x, SWA_HEAD_DIM // 2, 1),
                        pltpu.roll(x, LANES - SWA_HEAD_DIM // 2, 1))
    return x * cos + partner * sin_signed


def _inproj_kernel(x_ref, g_ref, w_ref, wg_ref, cos_ref, sin_ref, cost_ref, sint_ref, gq_ref,
                   wuqt_ref, gkv_ref, wuk_ref, wuvt_ref, qa_ref, ka_ref, va_ref, ga_ref, gb_ref,
                   qm_ref, km_ref, vt_ref):
    xn = _rms(x_ref[...], g_ref[...]).astype(BF16)
    cos = cos_ref[...]
    sin = sin_ref[...]

    qa_scale = LOG2_E / math.sqrt(SWA_HEAD_DIM)
    qa = _dot(xn, w_ref[:, _C_QA:_C_KV])
    for j in range(SWA_HEADS // 2):
        q = (_rope_lanes(qa[:, j * LANES:(j + 1) * LANES], cos, sin) * qa_scale).astype(BF16)
        qa_ref[0, 2 * j] = q[:, :SWA_HEAD_DIM]
        qa_ref[0, 2 * j + 1] = q[:, SWA_HEAD_DIM:]
    kv = _dot(xn, w_ref[:, _C_KV:_C_CQ])
    k = _rope_lanes(kv[:, :LANES], cos, sin).astype(BF16)
    ka_ref[0, 0] = k[:, :SWA_HEAD_DIM]
    ka_ref[0, 1] = k[:, SWA_HEAD_DIM:]
    vt = kv[:, LANES:].T.astype(BF16)
    va_ref[0, 0] = vt[:SWA_HEAD_DIM]
    va_ref[0, 1] = vt[SWA_HEAD_DIM:]

    ga_ref[...] = _sigmoid(_dot(xn, wg_ref[:, :D_MODEL])).astype(BF16)
    gb_ref[...] = _sigmoid(_dot(xn, wg_ref[:, D_MODEL:])).astype(BF16)

    cq = _rms(_dot(xn, w_ref[:, _C_CQ:_C_CKVR]), gq_ref[...])
    ckvr = _dot(xn, w_ref[:, _C_CKVR:_C_GATES])
    ckv = _rms(ckvr[:, :MLA_KV_RANK], gkv_ref[...])
    kr = jnp.concatenate(
        [ckvr[:, MLA_KV_RANK:], jnp.zeros((xn.shape[0], LANES - MLA_ROPE_DIM), F32)], axis=1)
    kr = _rope_lanes(kr, cos, sin)[:, :MLA_ROPE_DIM].astype(BF16)
    qm_scale = LOG2_E / math.sqrt(MLA_QK_DIM)
    kn = _dot(ckv.astype(BF16), wuk_ref[...]).astype(BF16)
    cq_t = cq.T.astype(BF16)
    ckv_t = ckv.T.astype(BF16)
    cos_h, sin_h = cost_ref[...], sint_ref[...]
    half = MLA_ROPE_DIM // 2
    for h in range(MLA_HEADS):
        km_ref[0, h, :, :MLA_NOPE_DIM] = kn[:, h * MLA_NOPE_DIM:(h + 1) * MLA_NOPE_DIM]
        km_ref[0, h, :, MLA_NOPE_DIM:] = kr
        vt_ref[0, h, 0] = _dot(wuvt_ref[h], ckv_t).astype(BF16)
        q_t = _dot(wuqt_ref[h], cq_t)
        x1 = q_t[MLA_NOPE_DIM:MLA_NOPE_DIM + half]
        x2 = q_t[MLA_NOPE_DIM + half:]
        q_t = jnp.concatenate(
            [q_t[:MLA_NOPE_DIM], x1 * cos_h - x2 * sin_h, x2 * cos_h + x1 * sin_h], axis=0)
        qm_ref[0, h, 0] = (q_t * qm_scale).astype(BF16)


def _inproj(x2, g_mix, w_main, w_gates, cos_t, sin_t, cos_h, sin_h, g_q, w_uqt, g_kv, w_uk, w_uvt,
            batch, seq, tm):
    tokens = batch * seq
    nt = seq // tm
    const = lambda t: (0, 0)
    tok = lambda t: (t, 0)
    head = lambda t: (t // nt, 0, t % nt, 0)
    out_shape = (
        jax.ShapeDtypeStruct((batch, SWA_HEADS, seq, SWA_HEAD_DIM), BF16),
        jax.ShapeDtypeStruct((batch, SWA_KV_HEADS, seq, SWA_HEAD_DIM), BF16),
        jax.ShapeDtypeStruct((batch, SWA_KV_HEADS, SWA_HEAD_DIM, seq), BF16),
        jax.ShapeDtypeStruct((tokens, D_MODEL), BF16),
        jax.ShapeDtypeStruct((tokens, D_MODEL), BF16),
        jax.ShapeDtypeStruct((batch, MLA_HEADS, nt, MLA_QK_DIM, tm), BF16),
        jax.ShapeDtypeStruct((batch, MLA_HEADS, seq, MLA_QK_DIM), BF16),
        jax.ShapeDtypeStruct((batch, MLA_HEADS, nt, MLA_V_DIM, tm), BF16),
    )
    return pl.pallas_call(
        _inproj_kernel,
        grid=(tokens // tm,),
        in_specs=[
            pl.BlockSpec((tm, D_MODEL), tok),
            pl.BlockSpec((1, D_MODEL), const),
            pl.BlockSpec(w_main.shape, const, pipeline_mode=pl.Buffered(1)),
            pl.BlockSpec(w_gates.shape, const, pipeline_mode=pl.Buffered(1)),
            pl.BlockSpec((tm, LANES), lambda t: (t % nt, 0)),
            pl.BlockSpec((tm, LANES), lambda t: (t % nt, 0)),
            pl.BlockSpec((MLA_ROPE_DIM // 2, tm), lambda t: (0, t % nt)),
            pl.BlockSpec((MLA_ROPE_DIM // 2, tm), lambda t: (0, t % nt)),
            pl.BlockSpec((1, MLA_Q_RANK), const),
            pl.BlockSpec(w_uqt.shape, lambda t: (0, 0, 0), pipeline_mode=pl.Buffered(1)),
            pl.BlockSpec((1, MLA_KV_RANK), const),
            pl.BlockSpec(w_uk.shape, const, pipeline_mode=pl.Buffered(1)),
            pl.BlockSpec(w_uvt.shape, lambda t: (0, 0, 0), pipeline_mode=pl.Buffered(1)),
        ],
        out_specs=(
            pl.BlockSpec((1, SWA_HEADS, tm, SWA_HEAD_DIM), head),
            pl.BlockSpec((1, SWA_KV_HEADS, tm, SWA_HEAD_DIM), head),
            pl.BlockSpec((1, SWA_KV_HEADS, SWA_HEAD_DIM, tm), lambda t: (t // nt, 0, 0, t % nt)),
            pl.BlockSpec((tm, D_MODEL), tok),
            pl.BlockSpec((tm, D_MODEL), tok),
            pl.BlockSpec((1, MLA_HEADS, 1, MLA_QK_DIM, tm), lambda t: (t // nt, 0, t % nt, 0, 0)),
            pl.BlockSpec((1, MLA_HEADS, tm, MLA_QK_DIM), head),
            pl.BlockSpec((1, MLA_HEADS, 1, MLA_V_DIM, tm), lambda t: (t // nt, 0, t % nt, 0, 0)),
        ),
        out_shape=out_shape,
        compiler_params=pltpu.CompilerParams(
            dimension_semantics=("arbitrary",), vmem_limit_bytes=VMEM_LIMIT),
        name="inproj",
    )(x2, g_mix, w_main, w_gates, cos_t, sin_t, cos_h, sin_h, g_q, w_uqt, g_kv, w_uk, w_uvt)


def _swa_kernel(q_ref, k_ref, kp_ref, vt_ref, vtp_ref, sink_ref, o_ref, s_scr, bias_scr, *,
                n_sub):
    blk = SWA_WINDOW
    cols = SWA_GROUP * blk

    @pl.when((pl.program_id(0) == 0) & (pl.program_id(1) == 0))
    def _():
        kj = lax.broadcasted_iota(jnp.int32, (2 * blk, cols), 0)
        qi = lax.broadcasted_iota(jnp.int32, (2 * blk, cols), 1) & (blk - 1)
        dist = blk + qi - kj
        bias_scr[...] = jnp.where((dist >= 0) & (dist < SWA_WINDOW), 0.0, NEG_INF)

    key_row = lax.broadcasted_iota(jnp.int32, (2 * blk, 1), 0)
    no_prev = jnp.where((key_row < blk) & (pl.program_id(1) == 0), NEG_INF, 0.0)
    units = [(g, n) for g in range(SWA_KV_HEADS) for n in range(n_sub)]

    def scores(u, buf):
        g, n = u
        q = q_ref[0, g * SWA_GROUP:(g + 1) * SWA_GROUP, n * blk:(n + 1) * blk, :]
        k_prev = kp_ref[0, g] if n == 0 else k_ref[0, g, (n - 1) * blk:n * blk, :]
        kk = jnp.concatenate([k_prev, k_ref[0, g, n * blk:(n + 1) * blk, :]], axis=0)
        s_scr[buf] = _dot_nt(kk, q.reshape(cols, SWA_HEAD_DIM))

    def softmax_pv(u, buf):
        g, n = u
        sink = sink_ref[:, g * cols:(g + 1) * cols] * LOG2_E
        s = s_scr[buf] + bias_scr[...]
        if n == 0:
            s = s + no_prev
        m = jnp.maximum(jnp.max(s, axis=0, keepdims=True), sink)
        e = jnp.exp2(s - m)
        denom = jnp.sum(e, axis=0, keepdims=True) + jnp.exp2(sink - m)
        vt_prev = vtp_ref[0, g] if n == 0 else vt_ref[0, g, :, (n - 1) * blk:n * blk]
        vvt = jnp.concatenate([vt_prev, vt_ref[0, g, :, n * blk:(n + 1) * blk]], axis=1)
        o = (_dot(vvt, e.astype(BF16)) / denom).T
        for hh in range(SWA_GROUP):
            h = g * SWA_GROUP + hh
            o_ref[0, n * blk:(n + 1) * blk, h * SWA_HEAD_DIM:(h + 1) * SWA_HEAD_DIM] = (
                o[hh * blk:(hh + 1) * blk, :].astype(BF16))

    scores(units[0], 0)
    for idx, u in enumerate(units):
        if idx + 1 < len(units):
            scores(units[idx + 1], (idx + 1) % 2)
        softmax_pv(u, idx % 2)


def _swa(q_a, k_a, vt_a, sink_lanes, batch, seq, tq):
    n_sub = tq // SWA_WINDOW
    cur = lambda b, i: (b, 0, i, 0)
    prev = lambda b, i: (b, 0, jnp.maximum(i * n_sub - 1, 0), 0)
    return pl.pallas_call(
        functools.partial(_swa_kernel, n_sub=n_sub),
        grid=(batch, seq // tq),
        in_specs=[
            pl.BlockSpec((1, SWA_HEADS, tq, SWA_HEAD_DIM), cur),
            pl.BlockSpec((1, SWA_KV_HEADS, tq, SWA_HEAD_DIM), cur),
            pl.BlockSpec((1, SWA_KV_HEADS, SWA_WINDOW, SWA_HEAD_DIM), prev),
            pl.BlockSpec((1, SWA_KV_HEADS, SWA_HEAD_DIM, tq), lambda b, i: (b, 0, 0, i)),
            pl.BlockSpec((1, SWA_KV_HEADS, SWA_HEAD_DIM, SWA_WINDOW),
                         lambda b, i: (b, 0, 0, jnp.maximum(i * n_sub - 1, 0))),
            pl.BlockSpec(sink_lanes.shape, lambda b, i: (0, 0)),
        ],
        out_specs=pl.BlockSpec((1, tq, D_MODEL), lambda b, i: (b, i, 0)),
        out_shape=jax.ShapeDtypeStruct((batch, seq, D_MODEL), BF16),
        scratch_shapes=[pltpu.VMEM((2, 2 * SWA_WINDOW, SWA_GROUP * SWA_WINDOW), F32),
                        pltpu.VMEM((2 * SWA_WINDOW, SWA_GROUP * SWA_WINDOW), F32)],
        compiler_params=pltpu.CompilerParams(
            dimension_semantics=("arbitrary", "arbitrary"), vmem_limit_bytes=VMEM_LIMIT),
        name="swa",
    )(q_a, k_a, k_a, vt_a, vt_a, sink_lanes)


def _mla_kernel(qt_ref, k_ref, vt_ref, o_ref, m_scr, l_scr, acc_scr, s_scr, *, tq, tk):
    assert tq == 2 * tk
    chunks = tk // MLA_KV_CHUNK
    q_chunks = tq // MLA_KV_CHUNK
    n_q = qt_ref.shape[2] // q_chunks

    def q_block(i, carry):
        m_scr[...] = jnp.full(m_scr.shape, NEG_INF, F32)
        l_scr[...] = jnp.zeros(l_scr.shape, F32)
        acc_scr[...] = jnp.zeros(acc_scr.shape, F32)

        def scores(j, buf, q0=0, blk=i):
            k = k_ref[0, 0, pl.ds(pl.multiple_of(j * tk, tk), tk), :]
            for c in range(q0 // MLA_KV_CHUNK, q_chunks):
                s_scr[buf, :, c * MLA_KV_CHUNK:(c + 1) * MLA_KV_CHUNK] = _dot(
                    k, qt_ref[0, 0, blk * q_chunks + c])

        def softmax_pv(j, buf, q0=0, key_offset=None):
            s = s_scr[buf, :, q0:]
            if key_offset is not None:
                kpos = lax.broadcasted_iota(jnp.int32, s.shape, 0) + key_offset
                qpos = lax.broadcasted_iota(jnp.int32, s.shape, 1) + q0
                s = jnp.where(kpos <= qpos, s, NEG_INF)
            m_old = m_scr[:, q0:]
            m_new = jnp.maximum(m_old, jnp.max(s, axis=0, keepdims=True))
            alpha = jnp.exp2(m_old - m_new)
            p = jnp.exp2(s - m_new)
            l_scr[:, q0:] = alpha * l_scr[:, q0:] + jnp.sum(p, axis=0, keepdims=True)
            pb = p.astype(BF16)
            pv = _dot(vt_ref[0, 0, j * chunks], pb[:MLA_KV_CHUNK])
            for c in range(1, chunks):
                pv += _dot(vt_ref[0, 0, j * chunks + c],
                           pb[c * MLA_KV_CHUNK:(c + 1) * MLA_KV_CHUNK])
            acc_scr[:, q0:] = alpha * acc_scr[:, q0:] + pv
            m_scr[:, q0:] = m_new

        @pl.when(i == 0)
        def _():
            scores(0, 0)

        def pair(jj, c):
            t = 2 * jj
            scores(t + 1, 1)
            softmax_pv(t, 0)
            scores(t + 2, 0)
            softmax_pv(t + 1, 1)
            return c

        lax.fori_loop(0, i, pair, 0)
        scores(2 * i + 1, 1, q0=tk)
        softmax_pv(2 * i, 0, key_offset=0)
        scores(0, 0, blk=jnp.minimum(i + 1, n_q - 1))
        softmax_pv(2 * i + 1, 1, q0=tk, key_offset=tk)

        o = acc_scr[...] / l_scr[...]
        o_ref[0, pl.ds(pl.multiple_of(i * tq, tq), tq), :] = o.T.astype(o_ref.dtype)
        return carry

    lax.fori_loop(0, n_q, q_block, 0)


def _mla(q_m, k_m, vt_m, batch, seq, tq, tk):
    return pl.pallas_call(
        functools.partial(_mla_kernel, tq=tq, tk=tk),
        grid=(batch, MLA_HEADS),
        in_specs=[
            pl.BlockSpec((1, 1, seq // MLA_KV_CHUNK, MLA_QK_DIM, MLA_KV_CHUNK),
                         lambda b, h: (b, h, 0, 0, 0)),
            pl.BlockSpec((1, 1, seq, MLA_QK_DIM), lambda b, h: (b, h, 0, 0)),
            pl.BlockSpec((1, 1, seq // MLA_KV_CHUNK, MLA_V_DIM, MLA_KV_CHUNK),
                         lambda b, h: (b, h, 0, 0, 0)),
        ],
        out_specs=pl.BlockSpec((1, seq, MLA_V_DIM), lambda b, h: (b, 0, h)),
        out_shape=jax.ShapeDtypeStruct((batch, seq, D_MODEL), BF16),
        scratch_shapes=[
            pltpu.VMEM((1, tq), F32),
            pltpu.VMEM((1, tq), F32),
            pltpu.VMEM((MLA_V_DIM, tq), F32),
            pltpu.VMEM((2, tk, tq), F32),
        ],
        compiler_params=pltpu.CompilerParams(
            dimension_semantics=("arbitrary", "arbitrary"), vmem_limit_bytes=VMEM_LIMIT),
        name="mla",
    )(q_m, k_m, vt_m)


def _outproj_kernel(x_ref, oa_ref, ob_ref, ga_ref, gb_ref, wo_ref, gf_ref, wr_ref, br_ref,
                    h_ref, xn_ref, comb_ref):
    merged = (ga_ref[...].astype(F32) * oa_ref[...].astype(F32)
              + gb_ref[...].astype(F32) * ob_ref[...].astype(F32)).astype(BF16)
    h = x_ref[...] + _dot(merged, wo_ref[...])
    h_ref[...] = h
    xn = _rms(h, gf_ref[...])
    xn_ref[...] = xn.astype(BF16)

    tm = xn.shape[0]
    x_hi = xn.astype(BF16)
    x_lo = (xn - x_hi.astype(F32)).astype(BF16)
    prod = _dot(jnp.concatenate([x_hi, x_lo], axis=0), wr_ref[...])
    lg = (prod[:tm, :LANES] + prod[:tm, LANES:] + prod[tm:, :LANES] + prod[tm:, LANES:]
          + br_ref[...])
    lg_t = lg.T
    big = jnp.int32(LANES)
    g_row = lax.broadcasted_iota(jnp.int32, (8, tm), 0)
    gl = jnp.where(g_row < N_GROUPS, lg_t[N_EXPERTS:N_EXPERTS + 8], NEG_INF)
    g_max = jnp.max(gl, axis=0, keepdims=True)
    g_w = 1.0 / jnp.sum(jnp.exp(gl - g_max), axis=0, keepdims=True)
    gidx = jnp.min(jnp.where(gl == g_max, g_row, big), axis=0, keepdims=True)

    row = lax.broadcasted_iota(jnp.int32, (N_EXPERTS, tm), 0)
    sel = (row >= gidx * EXPERTS_PER_GROUP) & (row < (gidx + 1) * EXPERTS_PER_GROUP)
    el = jnp.where(sel, lg_t[:N_EXPERTS], NEG_INF)
    e_exp = jnp.exp(el - jnp.max(el, axis=0, keepdims=True))
    e_prob = e_exp / jnp.sum(e_exp, axis=0, keepdims=True)
    p1 = jnp.max(e_prob, axis=0, keepdims=True)
    i1 = jnp.min(jnp.where(sel & (e_prob == p1), row, big), axis=0, keepdims=True)
    rest = sel & (row != i1)
    p2 = jnp.max(jnp.where(rest, e_prob, -1.0), axis=0, keepdims=True)
    i2 = jnp.min(jnp.where(rest & (e_prob == p2), row, big), axis=0, keepdims=True)
    top_sum = p1 + p2
    comb = jnp.where(row == i1, p1 / top_sum, jnp.where(row == i2, p2 / top_sum, 0.0)) * g_w
    gid_rows = jnp.where(g_row == 0, gidx.astype(F32), 0.0)
    comb_t = jnp.concatenate(
        [comb, gid_rows, jnp.zeros((LANES - N_EXPERTS - 8, tm), F32)], axis=0)
    comb_ref[...] = comb_t.T


def _outproj(x2, o_a, o_b, g_a, g_b, w_o, g_ffn, w_r, b_r, tm):
    tokens = x2.shape[0]
    tok = lambda t: (t, 0)
    const = lambda t: (0, 0)
    return pl.pallas_call(
        _outproj_kernel,
        grid=(tokens // tm,),
        in_specs=[
            pl.BlockSpec((tm, D_MODEL), tok),
            pl.BlockSpec((tm, D_MODEL), tok),
            pl.BlockSpec((tm, D_MODEL), tok),
            pl.BlockSpec((tm, D_MODEL), tok),
            pl.BlockSpec((tm, D_MODEL), tok),
            pl.BlockSpec((D_MODEL, D_MODEL), const),
            pl.BlockSpec((1, D_MODEL), const),
            pl.BlockSpec((D_MODEL, 2 * LANES), const),
            pl.BlockSpec((1, LANES), const),
        ],
        out_specs=(
            pl.BlockSpec((tm, D_MODEL), tok),
            pl.BlockSpec((tm, D_MODEL), tok),
            pl.BlockSpec((tm, LANES), tok),
        ),
        out_shape=(
            jax.ShapeDtypeStruct((tokens, D_MODEL), F32),
            jax.ShapeDtypeStruct((tokens, D_MODEL), BF16),
            jax.ShapeDtypeStruct((tokens, LANES), F32),
        ),
        compiler_params=pltpu.CompilerParams(
            dimension_semantics=("arbitrary",), vmem_limit_bytes=VMEM_LIMIT),
        name="outproj",
    )(x2, o_a, o_b, g_a, g_b, w_o, g_ffn, w_r, b_r)


def _moe_kernel(h_ref, xn_ref, comb_ref, wi_ref, wo_ref, p_ref, gp_ref, wg_ref, wp_ref, gfin_ref,
                o_ref, *, chunk):
    tm = h_ref.shape[0]
    o_ref[...] = h_ref[...]
    xn = xn_ref[...]
    comb = comb_ref[...]
    comb_hi = comb.astype(BF16)
    comb_lo = (comb - comb_hi.astype(F32)).astype(BF16)
    comb_hl = jnp.concatenate([comb_hi, comb_lo], axis=1)
    gid_col = comb[:, N_EXPERTS:N_EXPERTS + 1]
    gid_row = comb.T[N_EXPERTS:N_EXPERTS + 1, :]
    member_col = gid_col == lax.broadcasted_iota(jnp.int32, (tm, LANES), 1).astype(F32)
    member_row = gid_row == lax.broadcasted_iota(jnp.int32, (8, tm), 0).astype(F32)
    r_i = lax.broadcasted_iota(jnp.int32, (tm, tm), 0)
    c_i = lax.broadcasted_iota(jnp.int32, (tm, tm), 1)
    ranks_row = _dot(member_row.astype(BF16), (r_i < c_i).astype(BF16))
    ranks_col = _dot((r_i > c_i).astype(BF16), member_col.astype(BF16))
    pad = 2 * LANES
    never = -4.0 * tm
    slot_row = lax.broadcasted_iota(jnp.int32, (chunk, tm), 0).astype(F32)
    slot_col = lax.broadcasted_iota(jnp.int32, (tm, pad), 1).astype(F32)
    slot_col = jnp.where(slot_col < float(chunk), slot_col, never)

    groups = range(N_GROUPS)
    in_rows = [member_row[g:g + 1] for g in groups]
    rank_rows = [jnp.where(in_rows[g], ranks_row[g:g + 1], -2.0 * never) for g in groups]
    rank_cols = [jnp.where(member_col[:, g:g + 1], ranks_col[:, g:g + 1], -2.0 * never)
                 for g in groups]

    def one_hots(g, base):
        gather = jnp.where(rank_rows[g] - base == slot_row, 1.0, 0.0).astype(BF16)
        scatter = jnp.where(rank_cols[g] - base == slot_col, 1.0, 0.0).astype(BF16)
        return gather, scatter

    def compact(gather):
        xc = _dot(gather, xn).astype(BF16)
        wc = _dot(gather, comb_hl)
        return xc, wc[:, :LANES] + wc[:, LANES:]

    def experts_in(g, xc):
        return [_dot(xc, wi_ref[g * EXPERTS_PER_GROUP + j]) for j in range(EXPERTS_PER_GROUP)]

    def activate(g, hids, wc):
        acts = []
        for j, hid in enumerate(hids):
            e = g * EXPERTS_PER_GROUP + j
            gate = hid[:, :D_EXPERT]
            act = (gate * _sigmoid(gate)) * hid[:, D_EXPERT:] * wc[:, e:e + 1]
            acts.append(act.astype(BF16))
        return acts

    def experts_out(g, acts):
        yc = _dot(acts[0], wo_ref[g * EXPERTS_PER_GROUP])
        for j in range(1, EXPERTS_PER_GROUP):
            yc = yc + _dot(acts[j], wo_ref[g * EXPERTS_PER_GROUP + j])
        return jnp.concatenate(
            [yc.astype(BF16), jnp.zeros((pad - chunk, D_MODEL), BF16)], axis=0)

    hots = [one_hots(g, 0.0) for g in groups]
    compacted = [compact(hots[g][0]) for g in groups]
    hidden = [experts_in(g, compacted[g][0]) for g in groups]
    acts = [activate(g, hidden[g], compacted[g][1]) for g in groups]
    outs = [experts_out(g, acts[g]) for g in groups]
    o_ref[...] += _dot(jnp.concatenate([hots[g][1] for g in groups], axis=1),
                       jnp.concatenate(outs, axis=0))

    for g in groups:
        count = jnp.sum(in_rows[g].astype(jnp.int32))
        n_chunks = (count + (chunk - 1)) // chunk

        def chunk_body(c, carry, g=g):
            gather, scatter = one_hots(g, (c * chunk).astype(F32))
            xc, wc = compact(gather)
            yc = experts_out(g, activate(g, experts_in(g, xc), wc))
            o_ref[...] += _dot(scatter, yc)
            return carry

        lax.fori_loop(1, n_chunks, chunk_body, 0)

    h = o_ref[...]
    gate = _sigmoid(_dot(_rms(h, gp_ref[...]).astype(BF16), wg_ref[...]))
    h = h + gate * _dot(p_ref[...].astype(BF16), wp_ref[...])
    o_ref[...] = _rms(h, gfin_ref[...])


def _moe_ple(h1, xn, comb, w_ei, w_eo, p2, g_ple, w_pg, w_pp, g_final, tm, chunk):
    tokens = h1.shape[0]
    tok = lambda t: (t, 0)
    const = lambda t: (0, 0)
    return pl.pallas_call(
        functools.partial(_moe_kernel, chunk=chunk),
        grid=(tokens // tm,),
        in_specs=[
            pl.BlockSpec((tm, D_MODEL), tok),
            pl.BlockSpec((tm, D_MODEL), tok),
            pl.BlockSpec((tm, LANES), tok),
            pl.BlockSpec(w_ei.shape, lambda t: (0, 0, 0), pipeline_mode=pl.Buffered(1)),
            pl.BlockSpec(w_eo.shape, lambda t: (0, 0, 0), pipeline_mode=pl.Buffered(1)),
            pl.BlockSpec((tm, PLE_DIM), tok),
            pl.BlockSpec((1, D_MODEL), const),
            pl.BlockSpec((D_MODEL, D_MODEL), const, pipeline_mode=pl.Buffered(1)),
            pl.BlockSpec((PLE_DIM, D_MODEL), const, pipeline_mode=pl.Buffered(1)),
            pl.BlockSpec((1, D_MODEL), const),
        ],
        out_specs=pl.BlockSpec((tm, D_MODEL), tok),
        out_shape=jax.ShapeDtypeStruct((tokens, D_MODEL), F32),
        compiler_params=pltpu.CompilerParams(
            dimension_semantics=("arbitrary",), vmem_limit_bytes=VMEM_LIMIT),
        name="moe_ple",
    )(h1, xn, comb, w_ei, w_eo, p2, g_ple, w_pg, w_pp, g_final)


def _rope_tables(seq):
    half = SWA_HEAD_DIM // 2
    pos = jnp.arange(seq, dtype=F32)
    inv = ROPE_THETA ** (-jnp.arange(0, SWA_HEAD_DIM, 2, dtype=F32) / SWA_HEAD_DIM)
    ang = pos[:, None] * inv[None, :]
    cos, sin = jnp.cos(ang), jnp.sin(ang)
    reps = LANES // SWA_HEAD_DIM
    cos_t = jnp.broadcast_to(cos[:, None, :], (seq, 2 * reps, half)).reshape(seq, LANES)
    sin_t = jnp.broadcast_to(
        jnp.stack([-sin, sin], axis=1)[:, None], (seq, reps, 2, half)).reshape(seq, LANES)
    return cos_t, sin_t, cos.T, sin.T


def _layer(h2d, p2d, g_mix, w_in, sinks, g_q, w_uq, g_kv, w_ukv, w_out, g_ffn, w_rg, b_rg,
           w_re, b_re, w_ei, w_eo, g_ple, w_pg, w_pp, g_final, rope, batch, seq):
    cos_t, sin_t, cos_h, sin_h = rope
    assert SWA_HEAD_DIM == MLA_ROPE_DIM, "one rotary table serves both mixers"
    w_main = w_in[:, :_C_GATES].astype(BF16)
    w_gates = w_in[:, _C_GATES:].astype(BF16)
    assert w_gates.shape[1] == 2 * D_MODEL
    w_uqt = jnp.transpose(
        w_uq.reshape(MLA_Q_RANK, MLA_HEADS, MLA_QK_DIM), (1, 2, 0)).astype(BF16)
    ukv = w_ukv.reshape(MLA_KV_RANK, MLA_HEADS, MLA_NOPE_DIM + MLA_V_DIM)
    w_uk_c = ukv[:, :, :MLA_NOPE_DIM].reshape(MLA_KV_RANK, -1).astype(BF16)
    w_uvt = jnp.transpose(ukv[:, :, MLA_NOPE_DIM:], (1, 2, 0)).astype(BF16)

    q_a, k_a, v_a, g_a, g_b, q_m, k_m, v_m = _inproj(
        h2d, g_mix[None], w_main, w_gates, cos_t, sin_t, cos_h, sin_h, g_q[None], w_uqt,
        g_kv[None], w_uk_c, w_uvt, batch, seq, tm=MLA_KV_CHUNK)

    sink_lanes = jnp.repeat(sinks.astype(F32), SWA_WINDOW)[None]
    o_a = _swa(q_a, k_a, v_a, sink_lanes, batch, seq, tq=512)
    o_b = _mla(q_m, k_m, v_m, batch, seq, tq=1024, tk=512)

    w_r = jnp.concatenate(
        [w_re, w_rg, jnp.zeros((D_MODEL, LANES - N_EXPERTS - N_GROUPS), F32)], axis=1)
    w_r_hi = w_r.astype(BF16)
    w_r_lo = (w_r - w_r_hi.astype(F32)).astype(BF16)
    w_r2 = jnp.concatenate([w_r_hi, w_r_lo], axis=1)
    b_r = jnp.concatenate(
        [b_re, b_rg, jnp.zeros((LANES - N_EXPERTS - N_GROUPS,), F32)])[None]
    tokens = batch * seq
    h1, xn2, comb = _outproj(
        h2d, o_a.reshape(tokens, D_MODEL), o_b.reshape(tokens, D_MODEL), g_a, g_b,
        w_out.astype(BF16), g_ffn[None], w_r2, b_r, tm=512)

    return _moe_ple(h1, xn2, comb, w_ei.astype(BF16), w_eo.astype(BF16), p2d, g_ple[None],
                    w_pg.astype(BF16), w_pp.astype(BF16), g_final[None], tm=512, chunk=160)


def kernel(x, p, g_mix, w_in, swa_sinks, mla_g_q, mla_w_uq, mla_g_kv, mla_w_ukv, w_out, g_ffn,
           w_router_group, b_router_group, w_router_expert, b_router_expert, w_expert_in,
           w_expert_out, g_ple, w_ple_gate, w_ple_proj, g_final):
    batch, seq, d = x.shape
    depth = p.shape[0]
    assert d == D_MODEL and depth == 1, "final RMSNorm is fused into the single layer"
    rope = _rope_tables(seq)
    out = _layer(
        x.reshape(batch * seq, d), p[0].reshape(batch * seq, PLE_DIM), g_mix[0], w_in[0],
        swa_sinks[0], mla_g_q[0], mla_w_uq[0], mla_g_kv[0], mla_w_ukv[0], w_out[0], g_ffn[0],
        w_router_group[0], b_router_group[0], w_router_expert[0], b_router_expert[0],
        w_expert_in[0], w_expert_out[0], g_ple[0], w_ple_gate[0], w_ple_proj[0], g_final,
        rope, batch, seq)
    return out.reshape(batch, seq, d)
```

```python
import functools
import math

import jax
import jax.numpy as jnp
from jax import lax
from jax.experimental import pallas as pl
from jax.experimental.pallas import tpu as pltpu

D_MODEL = 1024
PLE_DIM = 256
ROPE_THETA = 10000.0
EPS = 1e-6
NEG_INF = -1e30

SWA_HEAD_DIM = 64
SWA_HEADS = D_MODEL // SWA_HEAD_DIM
SWA_KV_HEADS = SWA_HEADS // 8
SWA_GROUP = SWA_HEADS // SWA_KV_HEADS
SWA_WINDOW = 128

MLA_NOPE_DIM = 128
MLA_ROPE_DIM = 64
MLA_V_DIM = 128
MLA_HEADS = D_MODEL // MLA_V_DIM
MLA_Q_RANK = 256
MLA_KV_RANK = 128
MLA_QK_DIM = MLA_NOPE_DIM + MLA_ROPE_DIM

N_GROUPS = 4
EXPERTS_PER_GROUP = 4
N_EXPERTS = N_GROUPS * EXPERTS_PER_GROUP
D_EXPERT = 256

LANES = 128
LOG2_E = 1.4426950408889634
MLA_KV_CHUNK = 512
VMEM_LIMIT = 56 * 1024 * 1024

_C_QA = 0
_C_KV = _C_QA + SWA_HEADS * SWA_HEAD_DIM
_C_CQ = _C_KV + 2 * SWA_KV_HEADS * SWA_HEAD_DIM
_C_CKVR = _C_CQ + MLA_Q_RANK
_C_GATES = _C_CKVR + MLA_KV_RANK + MLA_ROPE_DIM

BF16 = jnp.bfloat16
F32 = jnp.float32


def _dot(a, b):
    return jnp.dot(a, b, preferred_element_type=F32)


def _dot_nt(a, b):
    return lax.dot_general(a, b, (((1,), (1,)), ((), ())), preferred_element_type=F32)


def _rms(x, g):
    r = lax.rsqrt(jnp.mean(x * x, axis=-1, keepdims=True) + EPS)
    return x * r * g


def _sigmoid(x):
    return 1.0 / (1.0 + jnp.exp(-x))


def _rope_lanes(x, cos, sin_signed):
    lane = lax.broadcasted_iota(jnp.int32, x.shape, 1)
    upper = (lane & (SWA_HEAD_DIM // 2)) != 0
    partner = jnp.where(upper, pltpu.roll(x, SWA_HEAD_DIM // 2, 1),
                        pltpu.roll(x, LANES - SWA_HEAD_DIM // 2, 1))
    return x * cos + partner * sin_signed


def _inproj_kernel(x_ref, g_ref, w_ref, wg_ref, cos_ref, sin_ref, cost_ref, sint_ref, gq_ref,
                   wuqt_ref, gkv_ref, wuk_ref, wuvt_ref, qa_ref, ka_ref, va_ref, ga_ref, gb_ref,
                   qm_ref, km_ref, vt_ref):
    xn = _rms(x_ref[...], g_ref[...]).astype(BF16)
    cos = cos_ref[...]
    sin = sin_ref[...]

    qa_scale = LOG2_E / math.sqrt(SWA_HEAD_DIM)
    qa = _dot(xn, w_ref[:, _C_QA:_C_KV])
    for j in range(SWA_HEADS // 2):
        q = (_rope_lanes(qa[:, j * LANES:(j + 1) * LANES], cos, sin) * qa_scale).astype(BF16)
        qa_ref[0, 2 * j] = q[:, :SWA_HEAD_DIM]
        qa_ref[0, 2 * j + 1] = q[:, SWA_HEAD_DIM:]
    kv = _dot(xn, w_ref[:, _C_KV:_C_CQ])
    k = _rope_lanes(kv[:, :LANES], cos, sin).astype(BF16)
    ka_ref[0, 0] = k[:, :SWA_HEAD_DIM]
    ka_ref[0, 1] = k[:, SWA_HEAD_DIM:]
    vt = kv[:, LANES:].T.astype(BF16)
    va_ref[0, 0] = vt[:SWA_HEAD_DIM]
    va_ref[0, 1] = vt[SWA_HEAD_DIM:]

    ga_ref[...] = _sigmoid(_dot(xn, wg_ref[:, :D_MODEL])).astype(BF16)
    gb_ref[...] = _sigmoid(_dot(xn, wg_ref[:, D_MODEL:])).astype(BF16)

    cq = _rms(_dot(xn, w_ref[:, _C_CQ:_C_CKVR]), gq_ref[...])
    ckvr = _dot(xn, w_ref[:, _C_CKVR:_C_GATES])
    ckv = _rms(ckvr[:, :MLA_KV_RANK], gkv_ref[...])
    kr = jnp.concatenate(
        [ckvr[:, MLA_KV_RANK:], jnp.zeros((xn.shape[0], LANES - MLA_ROPE_DIM), F32)], axis=1)
    kr = _rope_lanes(kr, cos, sin)[:, :MLA_ROPE_DIM].astype(BF16)
    qm_scale = LOG2_E / math.sqrt(MLA_QK_DIM)
    kn = _dot(ckv.astype(BF16), wuk_ref[...]).astype(BF16)
    cq_t = cq.T.astype(BF16)
    ckv_t = ckv.T.astype(BF16)
    cos_h, sin_h = cost_ref[...], sint_ref[...]
    half = MLA_ROPE_DIM // 2
    for h in range(MLA_HEADS):
        km_ref[0, h, :, :MLA_NOPE_DIM] = kn[:, h * MLA_NOPE_DIM:(h + 1) * MLA_NOPE_DIM]
        km_ref[0, h, :, MLA_NOPE_DIM:] = kr
        vt_ref[0, h, 0] = _dot(wuvt_ref[h], ckv_t).astype(BF16)
        q_t = _dot(wuqt_ref[h], cq_t)
        x1 = q_t[MLA_NOPE_DIM:MLA_NOPE_DIM + half]
        x2 = q_t[MLA_NOPE_DIM + half:]
        q_t = jnp.concatenate(
            [q_t[:MLA_NOPE_DIM], x1 * cos_h - x2 * sin_h, x2 * cos_h + x1 * sin_h], axis=0)
        qm_ref[0, h, 0] = (q_t * qm_scale).astype(BF16)


def _inproj(x2, g_mix, w_main, w_gates, cos_t, sin_t, cos_h, sin_h, g_q, w_uqt, g_kv, w_uk, w_uvt,
            batch, seq, tm):
    tokens = batch * seq
    nt = seq // tm
    const = lambda t: (0, 0)
    tok = lambda t: (t, 0)
    head = lambda t: (t // nt, 0, t % nt, 0)
    out_shape = (
        jax.ShapeDtypeStruct((batch, SWA_HEADS, seq, SWA_HEAD_DIM), BF16),
        jax.ShapeDtypeStruct((batch, SWA_KV_HEADS, seq, SWA_HEAD_DIM), BF16),
        jax.ShapeDtypeStruct((batch, SWA_KV_HEADS, SWA_HEAD_DIM, seq), BF16),
        jax.ShapeDtypeStruct((tokens, D_MODEL), BF16),
        jax.ShapeDtypeStruct((tokens, D_MODEL), BF16),
        jax.ShapeDtypeStruct((batch, MLA_HEADS, nt, MLA_QK_DIM, tm), BF16),
        jax.ShapeDtypeStruct((batch, MLA_HEADS, seq, MLA_QK_DIM), BF16),
        jax.ShapeDtypeStruct((batch, MLA_HEADS, nt, MLA_V_DIM, tm), BF16),
    )
    return pl.pallas_call(
        _inproj_kernel,
        grid=(tokens // tm,),
        in_specs=[
            pl.BlockSpec((tm, D_MODEL), tok),
            pl.BlockSpec((1, D_MODEL), const),
            pl.BlockSpec(w_main.shape, const, pipeline_mode=pl.Buffered(1)),
            pl.BlockSpec(w_gates.shape, const, pipeline_mode=pl.Buffered(1)),
            pl.BlockSpec((tm, LANES), lambda t: (t % nt, 0)),
            pl.BlockSpec((tm, LANES), lambda t: (t % nt, 0)),
            pl.BlockSpec((MLA_ROPE_DIM // 2, tm), lambda t: (0, t % nt)),
            pl.BlockSpec((MLA_ROPE_DIM // 2, tm), lambda t: (0, t % nt)),
            pl.BlockSpec((1, MLA_Q_RANK), const),
            pl.BlockSpec(w_uqt.shape, lambda t: (0, 0, 0), pipeline_mode=pl.Buffered(1)),
            pl.BlockSpec((1, MLA_KV_RANK), const),
            pl.BlockSpec(w_uk.shape, const, pipeline_mode=pl.Buffered(1)),
            pl.BlockSpec(w_uvt.shape, lambda t: (0, 0, 0), pipeline_mode=pl.Buffered(1)),
        ],
        out_specs=(
            pl.BlockSpec((1, SWA_HEADS, tm, SWA_HEAD_DIM), head),
            pl.BlockSpec((1, SWA_KV_HEADS, tm, SWA_HEAD_DIM), head),
            pl.BlockSpec((1, SWA_KV_HEADS, SWA_HEAD_DIM, tm), lambda t: (t // nt, 0, 0, t % nt)),
            pl.BlockSpec((tm, D_MODEL), tok),
            pl.BlockSpec((tm, D_MODEL), tok),
            pl.BlockSpec((1, MLA_HEADS, 1, MLA_QK_DIM, tm), lambda t: (t // nt, 0, t % nt, 0, 0)),
            pl.BlockSpec((1, MLA_HEADS, tm, MLA_QK_DIM), head),
            pl.BlockSpec((1, MLA_HEADS, 1, MLA_V_DIM, tm), lambda t: (t // nt, 0, t % nt, 0, 0)),
        ),
        out_shape=out_shape,
        compiler_params=pltpu.CompilerParams(
            dimension_semantics=("arbitrary",), vmem_limit_bytes=VMEM_LIMIT),
        name="inproj",
    )(x2, g_mix, w_main, w_gates, cos_t, sin_t, cos_h, sin_h, g_q, w_uqt, g_kv, w_uk, w_uvt)


def _swa_kernel(q_ref, k_ref, kp_ref, vt_ref, vtp_ref, sink_ref, o_ref, s_scr, bias_scr, *,
                n_sub):
    blk = SWA_WINDOW
    cols = SWA_GROUP * blk

    @pl.when((pl.program_id(0) == 0) & (pl.program_id(1) == 0))
    def _():
        kj = lax.broadcasted_iota(jnp.int32, (2 * blk, cols), 0)
        qi = lax.broadcasted_iota(jnp.int32, (2 * blk, cols), 1) & (blk - 1)
        dist = blk + qi - kj
        bias_scr[...] = jnp.where((dist >= 0) & (dist < SWA_WINDOW), 0.0, NEG_INF)

    key_row = lax.broadcasted_iota(jnp.int32, (2 * blk, 1), 0)
    no_prev = jnp.where((key_row < blk) & (pl.program_id(1) == 0), NEG_INF, 0.0)
    units = [(g, n) for g in range(SWA_KV_HEADS) for n in range(n_sub)]

    def scores(u, buf):
        g, n = u
        q = q_ref[0, g * SWA_GROUP:(g + 1) * SWA_GROUP, n * blk:(n + 1) * blk, :]
        k_prev = kp_ref[0, g] if n == 0 else k_ref[0, g, (n - 1) * blk:n * blk, :]
        kk = jnp.concatenate([k_prev, k_ref[0, g, n * blk:(n + 1) * blk, :]], axis=0)
        s_scr[buf] = _dot_nt(kk, q.reshape(cols, SWA_HEAD_DIM))

    def softmax_pv(u, buf):
        g, n = u
        sink = sink_ref[:, g * cols:(g + 1) * cols] * LOG2_E
        s = s_scr[buf] + bias_scr[...]
        if n == 0:
            s = s + no_prev
        m = jnp.maximum(jnp.max(s, axis=0, keepdims=True), sink)
        e = jnp.exp2(s - m)
        denom = jnp.sum(e, axis=0, keepdims=True) + jnp.exp2(sink - m)
        vt_prev = vtp_ref[0, g] if n == 0 else vt_ref[0, g, :, (n - 1) * blk:n * blk]
        vvt = jnp.concatenate([vt_prev, vt_ref[0, g, :, n * blk:(n + 1) * blk]], axis=1)
        o = (_dot(vvt, e.astype(BF16)) / denom).T
        for hh in range(SWA_GROUP):
            h = g * SWA_GROUP + hh
            o_ref[0, n * blk:(n + 1) * blk, h * SWA_HEAD_DIM:(h + 1) * SWA_HEAD_DIM] = (
                o[hh * blk:(hh + 1) * blk, :].astype(BF16))

    scores(units[0], 0)
    for idx, u in enumerate(units):
        if idx + 1 < len(units):
            scores(units[idx + 1], (idx + 1) % 2)
        softmax_pv(u, idx % 2)


def _swa(q_a, k_a, vt_a, sink_lanes, batch, seq, tq):
    n_sub = tq // SWA_WINDOW
    cur = lambda b, i: (b, 0, i, 0)
    prev = lambda b, i: (b, 0, jnp.maximum(i * n_sub - 1, 0), 0)
    return pl.pallas_call(
        functools.partial(_swa_kernel, n_sub=n_sub),
        grid=(batch, seq // tq),
        in_specs=[
            pl.BlockSpec((1, SWA_HEADS, tq, SWA_HEAD_DIM), cur),
            pl.BlockSpec((1, SWA_KV_HEADS, tq, SWA_HEAD_DIM), cur),
            pl.BlockSpec((1, SWA_KV_HEADS, SWA_WINDOW, SWA_HEAD_DIM), prev),
            pl.BlockSpec((1, SWA_KV_HEADS, SWA_HEAD_DIM, tq), lambda b, i: (b, 0, 0, i)),
            pl.BlockSpec((1, SWA_KV_HEADS, SWA_HEAD_DIM, SWA_WINDOW),
                         lambda b, i: (b, 0, 0, jnp.maximum(i * n_sub - 1, 0))),
            pl.BlockSpec(sink_lanes.shape, lambda b, i: (0, 0)),
        ],
        out_specs=pl.BlockSpec((1, tq, D_MODEL), lambda b, i: (b, i, 0)),
        out_shape=jax.ShapeDtypeStruct((batch, seq, D_MODEL), BF16),
        scratch_shapes=[pltpu.VMEM((2, 2 * SWA_WINDOW, SWA_GROUP * SWA_WINDOW), F32),
                        pltpu.VMEM((2 * SWA_WINDOW, SWA_GROUP * SWA_WINDOW), F32)],
        compiler_params=pltpu.CompilerParams(
            dimension_semantics=("arbitrary", "arbitrary"), vmem_limit_bytes=VMEM_LIMIT),
        name="swa",
    )(q_a, k_a, k_a, vt_a, vt_a, sink_lanes)


def _mla_kernel(qt_ref, k_ref, vt_ref, o_ref, m_scr, l_scr, acc_scr, s_scr, *, tq, tk):
    assert tq == 2 * tk
    chunks = tk // MLA_KV_CHUNK
    q_chunks = tq // MLA_KV_CHUNK
    n_q = qt_ref.shape[2] // q_chunks

    def q_block(i, carry):
        m_scr[...] = jnp.full(m_scr.shape, NEG_INF, F32)
        l_scr[...] = jnp.zeros(l_scr.shape, F32)
        acc_scr[...] = jnp.zeros(acc_scr.shape, F32)

        def scores(j, buf, q0=0, blk=i):
            k = k_ref[0, 0, pl.ds(pl.multiple_of(j * tk, tk), tk), :]
            for c in range(q0 // MLA_KV_CHUNK, q_chunks):
                s_scr[buf, :, c * MLA_KV_CHUNK:(c + 1) * MLA_KV_CHUNK] = _dot(
                    k, qt_ref[0, 0, blk * q_chunks + c])

        def softmax_pv(j, buf, q0=0, key_offset=None):
            s = s_scr[buf, :, q0:]
            if key_offset is not None:
                kpos = lax.broadcasted_iota(jnp.int32, s.shape, 0) + key_offset
                qpos = lax.broadcasted_iota(jnp.int32, s.shape, 1) + q0
                s = jnp.where(kpos <= qpos, s, NEG_INF)
            m_old = m_scr[:, q0:]
            m_new = jnp.maximum(m_old, jnp.max(s, axis=0, keepdims=True))
            alpha = jnp.exp2(m_old - m_new)
            p = jnp.exp2(s - m_new)
            l_scr[:, q0:] = alpha * l_scr[:, q0:] + jnp.sum(p, axis=0, keepdims=True)
            pb = p.astype(BF16)
            pv = _dot(vt_ref[0, 0, j * chunks], pb[:MLA_KV_CHUNK])
            for c in range(1, chunks):
                pv += _dot(vt_ref[0, 0, j * chunks + c],
                           pb[c * MLA_KV_CHUNK:(c + 1) * MLA_KV_CHUNK])
            acc_scr[:, q0:] = alpha * acc_scr[:, q0:] + pv
            m_scr[:, q0:] = m_new

        @pl.when(i == 0)
        def _():
            scores(0, 0)

        def pair(jj, c):
            t = 2 * jj
            scores(t + 1, 1)
            softmax_pv(t, 0)
            scores(t + 2, 0)
            softmax_pv(t + 1, 1)
            return c

        lax.fori_loop(0, i, pair, 0)
        scores(2 * i + 1, 1, q0=tk)
        softmax_pv(2 * i, 0, key_offset=0)
        scores(0, 0, blk=jnp.minimum(i + 1, n_q - 1))
        softmax_pv(2 * i + 1, 1, q0=tk, key_offset=tk)

        o = acc_scr[...] / l_scr[...]
        o_ref[0, pl.ds(pl.multiple_of(i * tq, tq), tq), :] = o.T.astype(o_ref.dtype)
        return carry

    lax.fori_loop(0, n_q, q_block, 0)


def _mla(q_m, k_m, vt_m, batch, seq, tq, tk):
    return pl.pallas_call(
        functools.partial(_mla_kernel, tq=tq, tk=tk),
        grid=(batch, MLA_HEADS),
        in_specs=[
            pl.BlockSpec((1, 1, seq // MLA_KV_CHUNK, MLA_QK_DIM, MLA_KV_CHUNK),
                         lambda b, h: (b, h, 0, 0, 0)),
            pl.BlockSpec((1, 1, seq, MLA_QK_DIM), lambda b, h: (b, h, 0, 0)),
            pl.BlockSpec((1, 1, seq // MLA_KV_CHUNK, MLA_V_DIM, MLA_KV_CHUNK),
                         lambda b, h: (b, h, 0, 0, 0)),
        ],
        out_specs=pl.BlockSpec((1, seq, MLA_V_DIM), lambda b, h: (b, 0, h)),
        out_shape=jax.ShapeDtypeStruct((batch, seq, D_MODEL), BF16),
        scratch_shapes=[
            pltpu.VMEM((1, tq), F32),
            pltpu.VMEM((1, tq), F32),
            pltpu.VMEM((MLA_V_DIM, tq), F32),
            pltpu.VMEM((2, tk, tq), F32),
        ],
        compiler_params=pltpu.CompilerParams(
            dimension_semantics=("arbitrary", "arbitrary"), vmem_limit_bytes=VMEM_LIMIT),
        name="mla",
    )(q_m, k_m, vt_m)


def _mix_and_route(x_ref, oa_ref, ob_ref, ga_ref, gb_ref, wo_ref, gf_ref, wr_ref, br_ref):
    merged = (ga_ref[...].astype(F32) * oa_ref[...].astype(F32)
              + gb_ref[...].astype(F32) * ob_ref[...].astype(F32)).astype(BF16)
    h = x_ref[...] + _dot(merged, wo_ref[...])
    xn = _rms(h, gf_ref[...])

    tm = xn.shape[0]
    x_hi = xn.astype(BF16)
    x_lo = (xn - x_hi.astype(F32)).astype(BF16)
    prod = _dot(jnp.concatenate([x_hi, x_lo], axis=0), wr_ref[...])
    lg = (prod[:tm, :LANES] + prod[:tm, LANES:] + prod[tm:, :LANES] + prod[tm:, LANES:]
          + br_ref[...])
    lg_t = lg.T
    big = jnp.int32(LANES)
    g_row = lax.broadcasted_iota(jnp.int32, (8, tm), 0)
    gl = jnp.where(g_row < N_GROUPS, lg_t[N_EXPERTS:N_EXPERTS + 8], NEG_INF)
    g_max = jnp.max(gl, axis=0, keepdims=True)
    g_w = 1.0 / jnp.sum(jnp.exp(gl - g_max), axis=0, keepdims=True)
    gidx = jnp.min(jnp.where(gl == g_max, g_row, big), axis=0, keepdims=True)

    row = lax.broadcasted_iota(jnp.int32, (N_EXPERTS, tm), 0)
    sel = (row >= gidx * EXPERTS_PER_GROUP) & (row < (gidx + 1) * EXPERTS_PER_GROUP)
    el = jnp.where(sel, lg_t[:N_EXPERTS], NEG_INF)
    e_exp = jnp.exp(el - jnp.max(el, axis=0, keepdims=True))
    e_prob = e_exp / jnp.sum(e_exp, axis=0, keepdims=True)
    p1 = jnp.max(e_prob, axis=0, keepdims=True)
    i1 = jnp.min(jnp.where(sel & (e_prob == p1), row, big), axis=0, keepdims=True)
    rest = sel & (row != i1)
    p2 = jnp.max(jnp.where(rest, e_prob, -1.0), axis=0, keepdims=True)
    i2 = jnp.min(jnp.where(rest & (e_prob == p2), row, big), axis=0, keepdims=True)
    top_sum = p1 + p2
    comb = jnp.where(row == i1, p1 / top_sum, jnp.where(row == i2, p2 / top_sum, 0.0)) * g_w
    gid_rows = jnp.where(g_row == 0, gidx.astype(F32), 0.0)
    comb_t = jnp.concatenate(
        [comb, gid_rows, jnp.zeros((LANES - N_EXPERTS - 8, tm), F32)], axis=0)
    return h, x_hi, comb_t


def _block_tail_kernel(x_ref, oa_ref, ob_ref, ga_ref, gb_ref, wout_ref, gf_ref, wr_ref, br_ref,
                       wi_ref, wo_ref, p_ref, gp_ref, wg_ref, wp_ref, gfin_ref, o_ref, *, chunk):
    tm = x_ref.shape[0]
    h, xn, comb_t = _mix_and_route(
        x_ref, oa_ref, ob_ref, ga_ref, gb_ref, wout_ref, gf_ref, wr_ref, br_ref)
    o_ref[...] = h
    comb = comb_t.T
    comb_hi = comb.astype(BF16)
    comb_lo = (comb - comb_hi.astype(F32)).astype(BF16)
    comb_hl = jnp.concatenate([comb_hi, comb_lo], axis=1)
    gid_col = comb[:, N_EXPERTS:N_EXPERTS + 1]
    gid_row = comb_t[N_EXPERTS:N_EXPERTS + 1, :]
    member_col = gid_col == lax.broadcasted_iota(jnp.int32, (tm, LANES), 1).astype(F32)
    member_row = gid_row == lax.broadcasted_iota(jnp.int32, (8, tm), 0).astype(F32)
    r_i = lax.broadcasted_iota(jnp.int32, (tm, tm), 0)
    c_i = lax.broadcasted_iota(jnp.int32, (tm, tm), 1)
    ranks_row = _dot(member_row.astype(BF16), (r_i < c_i).astype(BF16))
    ranks_col = _dot((r_i > c_i).astype(BF16), member_col.astype(BF16))
    pad = 2 * LANES
    never = -4.0 * tm
    slot_row = lax.broadcasted_iota(jnp.int32, (chunk, tm), 0).astype(F32)
    slot_col = lax.broadcasted_iota(jnp.int32, (tm, pad), 1).astype(F32)
    slot_col = jnp.where(slot_col < float(chunk), slot_col, never)

    groups = range(N_GROUPS)
    in_rows = [member_row[g:g + 1] for g in groups]
    rank_rows = [jnp.where(in_rows[g], ranks_row[g:g + 1], -2.0 * never) for g in groups]
    rank_cols = [jnp.where(member_col[:, g:g + 1], ranks_col[:, g:g + 1], -2.0 * never)
                 for g in groups]

    def one_hots(g, base):
        gather = jnp.where(rank_rows[g] - base == slot_row, 1.0, 0.0).astype(BF16)
        scatter = jnp.where(rank_cols[g] - base == slot_col, 1.0, 0.0).astype(BF16)
        return gather, scatter

    def compact(gather):
        xc = _dot(gather, xn).astype(BF16)
        wc = _dot(gather, comb_hl)
        return xc, wc[:, :LANES] + wc[:, LANES:]

    def experts_in(g, xc):
        return [_dot(xc, wi_ref[g * EXPERTS_PER_GROUP + j]) for j in range(EXPERTS_PER_GROUP)]

    def activate(g, hids, wc):
        acts = []
        for j, hid in enumerate(hids):
            e = g * EXPERTS_PER_GROUP + j
            gate = hid[:, :D_EXPERT]
            act = (gate * _sigmoid(gate)) * hid[:, D_EXPERT:] * wc[:, e:e + 1]
            acts.append(act.astype(BF16))
        return acts

    def experts_out(g, acts):
        yc = _dot(acts[0], wo_ref[g * EXPERTS_PER_GROUP])
        for j in range(1, EXPERTS_PER_GROUP):
            yc = yc + _dot(acts[j], wo_ref[g * EXPERTS_PER_GROUP + j])
        return jnp.concatenate(
            [yc.astype(BF16), jnp.zeros((pad - chunk, D_MODEL), BF16)], axis=0)

    hots = [one_hots(g, 0.0) for g in groups]
    compacted = [compact(hots[g][0]) for g in groups]
    hidden = [experts_in(g, compacted[g][0]) for g in groups]
    acts = [activate(g, hidden[g], compacted[g][1]) for g in groups]
    outs = [experts_out(g, acts[g]) for g in groups]
    o_ref[...] += _dot(jnp.concatenate([hots[g][1] for g in groups], axis=1),
                       jnp.concatenate(outs, axis=0))

    for g in groups:
        count = jnp.sum(in_rows[g].astype(jnp.int32))
        n_chunks = (count + (chunk - 1)) // chunk

        def chunk_body(c, carry, g=g):
            gather, scatter = one_hots(g, (c * chunk).astype(F32))
            xc, wc = compact(gather)
            yc = experts_out(g, activate(g, experts_in(g, xc), wc))
            o_ref[...] += _dot(scatter, yc)
            return carry

        lax.fori_loop(1, n_chunks, chunk_body, 0)

    h = o_ref[...]
    gate = _sigmoid(_dot(_rms(h, gp_ref[...]).astype(BF16), wg_ref[...]))
    h = h + gate * _dot(p_ref[...].astype(BF16), wp_ref[...])
    o_ref[...] = _rms(h, gfin_ref[...])


def _block_tail(x2, o_a, o_b, g_a, g_b, w_o, g_ffn, w_r, b_r, w_ei, w_eo, p2, g_ple, w_pg, w_pp,
                g_final, tm, chunk):
    tokens = x2.shape[0]
    tok = lambda t: (t, 0)
    const = lambda t: (0, 0)
    return pl.pallas_call(
        functools.partial(_block_tail_kernel, chunk=chunk),
        grid=(tokens // tm,),
        in_specs=[
            pl.BlockSpec((tm, D_MODEL), tok),
            pl.BlockSpec((tm, D_MODEL), tok),
            pl.BlockSpec((tm, D_MODEL), tok),
            pl.BlockSpec((tm, D_MODEL), tok),
            pl.BlockSpec((tm, D_MODEL), tok),
            pl.BlockSpec((D_MODEL, D_MODEL), const, pipeline_mode=pl.Buffered(1)),
            pl.BlockSpec((1, D_MODEL), const),
            pl.BlockSpec((D_MODEL, 2 * LANES), const, pipeline_mode=pl.Buffered(1)),
            pl.BlockSpec((1, LANES), const),
            pl.BlockSpec(w_ei.shape, lambda t: (0, 0, 0), pipeline_mode=pl.Buffered(1)),
            pl.BlockSpec(w_eo.shape, lambda t: (0, 0, 0), pipeline_mode=pl.Buffered(1)),
            pl.BlockSpec((tm, PLE_DIM), tok),
            pl.BlockSpec((1, D_MODEL), const),
            pl.BlockSpec((D_MODEL, D_MODEL), const, pipeline_mode=pl.Buffered(1)),
            pl.BlockSpec((PLE_DIM, D_MODEL), const, pipeline_mode=pl.Buffered(1)),
            pl.BlockSpec((1, D_MODEL), const),
        ],
        out_specs=pl.BlockSpec((tm, D_MODEL), tok),
        out_shape=jax.ShapeDtypeStruct((tokens, D_MODEL), F32),
        compiler_params=pltpu.CompilerParams(
            dimension_semantics=("arbitrary",), vmem_limit_bytes=VMEM_LIMIT),
        name="block_tail",
    )(x2, o_a, o_b, g_a, g_b, w_o, g_ffn, w_r, b_r, w_ei, w_eo, p2, g_ple, w_pg, w_pp, g_final)


def _rope_tables(seq):
    half = SWA_HEAD_DIM // 2
    pos = jnp.arange(seq, dtype=F32)
    inv = ROPE_THETA ** (-jnp.arange(0, SWA_HEAD_DIM, 2, dtype=F32) / SWA_HEAD_DIM)
    ang = pos[:, None] * inv[None, :]
    cos, sin = jnp.cos(ang), jnp.sin(ang)
    reps = LANES // SWA_HEAD_DIM
    cos_t = jnp.broadcast_to(cos[:, None, :], (seq, 2 * reps, half)).reshape(seq, LANES)
    sin_t = jnp.broadcast_to(
        jnp.stack([-sin, sin], axis=1)[:, None], (seq, reps, 2, half)).reshape(seq, LANES)
    return cos_t, sin_t, cos.T, sin.T


def _layer(h2d, p2d, g_mix, w_in, sinks, g_q, w_uq, g_kv, w_ukv, w_out, g_ffn, w_rg, b_rg,
           w_re, b_re, w_ei, w_eo, g_ple, w_pg, w_pp, g_final, rope, batch, seq):
    cos_t, sin_t, cos_h, sin_h = rope
    assert SWA_HEAD_DIM == MLA_ROPE_DIM, "one rotary table serves both mixers"
    w_main = w_in[:, :_C_GATES].astype(BF16)
    w_gates = w_in[:, _C_GATES:].astype(BF16)
    assert w_gates.shape[1] == 2 * D_MODEL
    w_uqt = jnp.transpose(
        w_uq.reshape(MLA_Q_RANK, MLA_HEADS, MLA_QK_DIM), (1, 2, 0)).astype(BF16)
    ukv = w_ukv.reshape(MLA_KV_RANK, MLA_HEADS, MLA_NOPE_DIM + MLA_V_DIM)
    w_uk_c = ukv[:, :, :MLA_NOPE_DIM].reshape(MLA_KV_RANK, -1).astype(BF16)
    w_uvt = jnp.transpose(ukv[:, :, MLA_NOPE_DIM:], (1, 2, 0)).astype(BF16)

    q_a, k_a, v_a, g_a, g_b, q_m, k_m, v_m = _inproj(
        h2d, g_mix[None], w_main, w_gates, cos_t, sin_t, cos_h, sin_h, g_q[None], w_uqt,
        g_kv[None], w_uk_c, w_uvt, batch, seq, tm=MLA_KV_CHUNK)

    sink_lanes = jnp.repeat(sinks.astype(F32), SWA_WINDOW)[None]
    o_a = _swa(q_a, k_a, v_a, sink_lanes, batch, seq, tq=512)
    o_b = _mla(q_m, k_m, v_m, batch, seq, tq=1024, tk=512)

    w_r = jnp.concatenate(
        [w_re, w_rg, jnp.zeros((D_MODEL, LANES - N_EXPERTS - N_GROUPS), F32)], axis=1)
    w_r_hi = w_r.astype(BF16)
    w_r_lo = (w_r - w_r_hi.astype(F32)).astype(BF16)
    w_r2 = jnp.concatenate([w_r_hi, w_r_lo], axis=1)
    b_r = jnp.concatenate(
        [b_re, b_rg, jnp.zeros((LANES - N_EXPERTS - N_GROUPS,), F32)])[None]
    tokens = batch * seq
    return _block_tail(
        h2d, o_a.reshape(tokens, D_MODEL), o_b.reshape(tokens, D_MODEL), g_a, g_b,
        w_out.astype(BF16), g_ffn[None], w_r2, b_r, w_ei.astype(BF16), w_eo.astype(BF16), p2d,
        g_ple[None], w_pg.astype(BF16), w_pp.astype(BF16), g_final[None], tm=512, chunk=160)


def kernel(x, p, g_mix, w_in, swa_sinks, mla_g_q, mla_w_uq, mla_g_kv, mla_w_ukv, w_out, g_ffn,
           w_router_group, b_router_group, w_router_expert, b_router_expert, w_expert_in,
           w_expert_out, g_ple, w_ple_gate, w_ple_proj, g_final):
    batch, seq, d = x.shape
    depth = p.shape[0]
    assert d == D_MODEL and depth == 1, "final RMSNorm is fused into the single layer"
    rope = _rope_tables(seq)
    out = _layer(
        x.reshape(batch * seq, d), p[0].reshape(batch * seq, PLE_DIM), g_mix[0], w_in[0],
        swa_sinks[0], mla_g_q[0], mla_w_uq[0], mla_g_kv[0], mla_w_ukv[0], w_out[0], g_ffn[0],
        w_router_group[0], b_router_group[0], w_router_expert[0], b_router_expert[0],
        w_expert_in[0], w_expert_out[0], g_ple[0], w_ple_gate[0], w_ple_proj[0], g_final,
        rope, batch, seq)
    return out.reshape(batch, seq, d)
```

```python
import functools
import math

import jax
import jax.numpy as jnp
from jax import lax
from jax.experimental import pallas as pl
from jax.experimental.pallas import tpu as pltpu

D_MODEL = 1024
PLE_DIM = 256
ROPE_THETA = 10000.0
EPS = 1e-6
NEG_INF = -1e30

SWA_HEAD_DIM = 64
SWA_HEADS = D_MODEL // SWA_HEAD_DIM
SWA_KV_HEADS = SWA_HEADS // 8
SWA_GROUP = SWA_HEADS // SWA_KV_HEADS
SWA_WINDOW = 128

MLA_NOPE_DIM = 128
MLA_ROPE_DIM = 64
MLA_V_DIM = 128
MLA_HEADS = D_MODEL // MLA_V_DIM
MLA_Q_RANK = 256
MLA_KV_RANK = 128
MLA_QK_DIM = MLA_NOPE_DIM + MLA_ROPE_DIM
MLA_QK_PAD = 256

N_GROUPS = 4
EXPERTS_PER_GROUP = 4
N_EXPERTS = N_GROUPS * EXPERTS_PER_GROUP
D_EXPERT = 256

LANES = 128
LOG2_E = 1.4426950408889634
MLA_KV_CHUNK = 512
VMEM_LIMIT = 56 * 1024 * 1024

_C_QA = 0
_C_KV = _C_QA + SWA_HEADS * SWA_HEAD_DIM
_C_CQ = _C_KV + 2 * SWA_KV_HEADS * SWA_HEAD_DIM
_C_CKVR = _C_CQ + MLA_Q_RANK
_C_GATES = _C_CKVR + MLA_KV_RANK + MLA_ROPE_DIM

BF16 = jnp.bfloat16
F32 = jnp.float32


def _dot(a, b):
    return jnp.dot(a, b, preferred_element_type=F32)


def _dot_nt(a, b):
    return lax.dot_general(a, b, (((1,), (1,)), ((), ())), preferred_element_type=F32)


def _rms(x, g):
    r = lax.rsqrt(jnp.mean(x * x, axis=-1, keepdims=True) + EPS)
    return x * r * g


def _sigmoid(x):
    return 1.0 / (1.0 + jnp.exp(-x))


def _rope_lanes(x, cos, sin_signed):
    lane = lax.broadcasted_iota(jnp.int32, x.shape, 1)
    upper = (lane & (SWA_HEAD_DIM // 2)) != 0
    partner = jnp.where(upper, pltpu.roll(x, SWA_HEAD_DIM // 2, 1),
                        pltpu.roll(x, LANES - SWA_HEAD_DIM // 2, 1))
    return x * cos + partner * sin_signed


def _inproj_kernel(x_ref, g_ref, w_ref, wg_ref, cos_ref, sin_ref, cost_ref, sint_ref, gq_ref,
                   wuqt_ref, gkv_ref, wuk_ref, wuvt_ref, qa_ref, ka_ref, va_ref, ga_ref, gb_ref,
                   qm_ref, km_ref, vt_ref):
    xn = _rms(x_ref[...], g_ref[...]).astype(BF16)
    cos = cos_ref[...]
    sin = sin_ref[...]

    qa_scale = LOG2_E / math.sqrt(SWA_HEAD_DIM)
    qa = _dot(xn, w_ref[:, _C_QA:_C_KV])
    for j in range(SWA_HEADS // 2):
        q = (_rope_lanes(qa[:, j * LANES:(j + 1) * LANES], cos, sin) * qa_scale).astype(BF16)
        qa_ref[0, 2 * j] = q[:, :SWA_HEAD_DIM]
        qa_ref[0, 2 * j + 1] = q[:, SWA_HEAD_DIM:]
    kv = _dot(xn, w_ref[:, _C_KV:_C_CQ])
    k = _rope_lanes(kv[:, :LANES], cos, sin).astype(BF16)
    ka_ref[0, 0] = k[:, :SWA_HEAD_DIM]
    ka_ref[0, 1] = k[:, SWA_HEAD_DIM:]
    vt = kv[:, LANES:].T.astype(BF16)
    va_ref[0, 0] = vt[:SWA_HEAD_DIM]
    va_ref[0, 1] = vt[SWA_HEAD_DIM:]

    ga_ref[...] = _sigmoid(_dot(xn, wg_ref[:, :D_MODEL])).astype(BF16)
    gb_ref[...] = _sigmoid(_dot(xn, wg_ref[:, D_MODEL:])).astype(BF16)

    cq = _rms(_dot(xn, w_ref[:, _C_CQ:_C_CKVR]), gq_ref[...])
    ckvr = _dot(xn, w_ref[:, _C_CKVR:_C_GATES])
    ckv = _rms(ckvr[:, :MLA_KV_RANK], gkv_ref[...])
    kr = jnp.concatenate(
        [ckvr[:, MLA_KV_RANK:], jnp.zeros((xn.shape[0], LANES - MLA_ROPE_DIM), F32)], axis=1)
    kr = _rope_lanes(kr, cos, sin)[:, :MLA_ROPE_DIM].astype(BF16)
    qm_scale = LOG2_E / math.sqrt(MLA_QK_DIM)
    kn = _dot(ckv.astype(BF16), wuk_ref[...]).astype(BF16)
    cq_t = cq.T.astype(BF16)
    ckv_t = ckv.T.astype(BF16)
    cos_h, sin_h = cost_ref[...], sint_ref[...]
    half = MLA_ROPE_DIM // 2
    tm = xn.shape[0]
    kr_pad = jnp.concatenate(
        [kr, jnp.zeros((tm, MLA_QK_PAD - MLA_QK_DIM), BF16)], axis=1)
    for h in range(MLA_HEADS):
        km_ref[0, h, :, :MLA_NOPE_DIM] = kn[:, h * MLA_NOPE_DIM:(h + 1) * MLA_NOPE_DIM]
        km_ref[0, h, :, MLA_NOPE_DIM:] = kr_pad
        vt_ref[0, h, 0] = _dot(wuvt_ref[h], ckv_t).astype(BF16)
        q_t = _dot(wuqt_ref[h], cq_t)
        x1 = q_t[MLA_NOPE_DIM:MLA_NOPE_DIM + half]
        x2 = q_t[MLA_NOPE_DIM + half:]
        q_t = jnp.concatenate(
            [q_t[:MLA_NOPE_DIM], x1 * cos_h - x2 * sin_h, x2 * cos_h + x1 * sin_h,
             jnp.zeros((MLA_QK_PAD - MLA_QK_DIM, tm), F32)], axis=0)
        qm_ref[0, h, 0] = (q_t * qm_scale).astype(BF16)


def _inproj(x2, g_mix, w_main, w_gates, cos_t, sin_t, cos_h, sin_h, g_q, w_uqt, g_kv, w_uk, w_uvt,
            batch, seq, tm):
    tokens = batch * seq
    nt = seq // tm
    const = lambda t: (0, 0)
    tok = lambda t: (t, 0)
    head = lambda t: (t // nt, 0, t % nt, 0)
    out_shape = (
        jax.ShapeDtypeStruct((batch, SWA_HEADS, seq, SWA_HEAD_DIM), BF16),
        jax.ShapeDtypeStruct((batch, SWA_KV_HEADS, seq, SWA_HEAD_DIM), BF16),
        jax.ShapeDtypeStruct((batch, SWA_KV_HEADS, SWA_HEAD_DIM, seq), BF16),
        jax.ShapeDtypeStruct((tokens, D_MODEL), BF16),
        jax.ShapeDtypeStruct((tokens, D_MODEL), BF16),
        jax.ShapeDtypeStruct((batch, MLA_HEADS, nt, MLA_QK_PAD, tm), BF16),
        jax.ShapeDtypeStruct((batch, MLA_HEADS, seq, MLA_QK_PAD), BF16),
        jax.ShapeDtypeStruct((batch, MLA_HEADS, nt, MLA_V_DIM, tm), BF16),
    )
    return pl.pallas_call(
        _inproj_kernel,
        grid=(tokens // tm,),
        in_specs=[
            pl.BlockSpec((tm, D_MODEL), tok),
            pl.BlockSpec((1, D_MODEL), const),
            pl.BlockSpec(w_main.shape, const, pipeline_mode=pl.Buffered(1)),
            pl.BlockSpec(w_gates.shape, const, pipeline_mode=pl.Buffered(1)),
            pl.BlockSpec((tm, LANES), lambda t: (t % nt, 0)),
            pl.BlockSpec((tm, LANES), lambda t: (t % nt, 0)),
            pl.BlockSpec((MLA_ROPE_DIM // 2, tm), lambda t: (0, t % nt)),
            pl.BlockSpec((MLA_ROPE_DIM // 2, tm), lambda t: (0, t % nt)),
            pl.BlockSpec((1, MLA_Q_RANK), const),
            pl.BlockSpec(w_uqt.shape, lambda t: (0, 0, 0), pipeline_mode=pl.Buffered(1)),
            pl.BlockSpec((1, MLA_KV_RANK), const),
            pl.BlockSpec(w_uk.shape, const, pipeline_mode=pl.Buffered(1)),
            pl.BlockSpec(w_uvt.shape, lambda t: (0, 0, 0), pipeline_mode=pl.Buffered(1)),
        ],
        out_specs=(
            pl.BlockSpec((1, SWA_HEADS, tm, SWA_HEAD_DIM), head),
            pl.BlockSpec((1, SWA_KV_HEADS, tm, SWA_HEAD_DIM), head),
            pl.BlockSpec((1, SWA_KV_HEADS, SWA_HEAD_DIM, tm), lambda t: (t // nt, 0, 0, t % nt)),
            pl.BlockSpec((tm, D_MODEL), tok),
            pl.BlockSpec((tm, D_MODEL), tok),
            pl.BlockSpec((1, MLA_HEADS, 1, MLA_QK_PAD, tm), lambda t: (t // nt, 0, t % nt, 0, 0)),
            pl.BlockSpec((1, MLA_HEADS, tm, MLA_QK_PAD), head),
            pl.BlockSpec((1, MLA_HEADS, 1, MLA_V_DIM, tm), lambda t: (t // nt, 0, t % nt, 0, 0)),
        ),
        out_shape=out_shape,
        compiler_params=pltpu.CompilerParams(
            dimension_semantics=("arbitrary",), vmem_limit_bytes=VMEM_LIMIT),
        name="inproj",
    )(x2, g_mix, w_main, w_gates, cos_t, sin_t, cos_h, sin_h, g_q, w_uqt, g_kv, w_uk, w_uvt)


def _swa_kernel(q_ref, k_ref, kp_ref, vt_ref, vtp_ref, sink_ref, o_ref, s_scr, bias_scr, *,
                n_sub):
    blk = SWA_WINDOW
    cols = SWA_GROUP * blk

    @pl.when((pl.program_id(0) == 0) & (pl.program_id(1) == 0))
    def _():
        kj = lax.broadcasted_iota(jnp.int32, (2 * blk, cols), 0)
        qi = lax.broadcasted_iota(jnp.int32, (2 * blk, cols), 1) & (blk - 1)
        dist = blk + qi - kj
        bias_scr[...] = jnp.where((dist >= 0) & (dist < SWA_WINDOW), 0.0, NEG_INF)

    key_row = lax.broadcasted_iota(jnp.int32, (2 * blk, 1), 0)
    no_prev = jnp.where((key_row < blk) & (pl.program_id(1) == 0), NEG_INF, 0.0)
    units = [(g, n) for g in range(SWA_KV_HEADS) for n in range(n_sub)]

    def scores(u, buf):
        g, n = u
        q = q_ref[0, g * SWA_GROUP:(g + 1) * SWA_GROUP, n * blk:(n + 1) * blk, :]
        k_prev = kp_ref[0, g] if n == 0 else k_ref[0, g, (n - 1) * blk:n * blk, :]
        kk = jnp.concatenate([k_prev, k_ref[0, g, n * blk:(n + 1) * blk, :]], axis=0)
        s_scr[buf] = _dot_nt(kk, q.reshape(cols, SWA_HEAD_DIM))

    def softmax_pv(u, buf):
        g, n = u
        sink = sink_ref[:, g * cols:(g + 1) * cols] * LOG2_E
        s = s_scr[buf] + bias_scr[...]
        if n == 0:
            s = s + no_prev
        m = jnp.maximum(jnp.max(s, axis=0, keepdims=True), sink)
        e = jnp.exp2(s - m)
        denom = jnp.sum(e, axis=0, keepdims=True) + jnp.exp2(sink - m)
        vt_prev = vtp_ref[0, g] if n == 0 else vt_ref[0, g, :, (n - 1) * blk:n * blk]
        vvt = jnp.concatenate([vt_prev, vt_ref[0, g, :, n * blk:(n + 1) * blk]], axis=1)
        o = (_dot(vvt, e.astype(BF16)) / denom).T
        for hh in range(SWA_GROUP):
            h = g * SWA_GROUP + hh
            o_ref[0, n * blk:(n + 1) * blk, h * SWA_HEAD_DIM:(h + 1) * SWA_HEAD_DIM] = (
                o[hh * blk:(hh + 1) * blk, :].astype(BF16))

    scores(units[0], 0)
    for idx, u in enumerate(units):
        if idx + 1 < len(units):
            scores(units[idx + 1], (idx + 1) % 2)
        softmax_pv(u, idx % 2)


def _swa(q_a, k_a, vt_a, sink_lanes, batch, seq, tq):
    n_sub = tq // SWA_WINDOW
    cur = lambda b, i: (b, 0, i, 0)
    prev = lambda b, i: (b, 0, jnp.maximum(i * n_sub - 1, 0), 0)
    return pl.pallas_call(
        functools.partial(_swa_kernel, n_sub=n_sub),
        grid=(batch, seq // tq),
        in_specs=[
            pl.BlockSpec((1, SWA_HEADS, tq, SWA_HEAD_DIM), cur),
            pl.BlockSpec((1, SWA_KV_HEADS, tq, SWA_HEAD_DIM), cur),
            pl.BlockSpec((1, SWA_KV_HEADS, SWA_WINDOW, SWA_HEAD_DIM), prev),
            pl.BlockSpec((1, SWA_KV_HEADS, SWA_HEAD_DIM, tq), lambda b, i: (b, 0, 0, i)),
            pl.BlockSpec((1, SWA_KV_HEADS, SWA_HEAD_DIM, SWA_WINDOW),
                         lambda b, i: (b, 0, 0, jnp.maximum(i * n_sub - 1, 0))),
            pl.BlockSpec(sink_lanes.shape, lambda b, i: (0, 0)),
        ],
        out_specs=pl.BlockSpec((1, tq, D_MODEL), lambda b, i: (b, i, 0)),
        out_shape=jax.ShapeDtypeStruct((batch, seq, D_MODEL), BF16),
        scratch_shapes=[pltpu.VMEM((2, 2 * SWA_WINDOW, SWA_GROUP * SWA_WINDOW), F32),
                        pltpu.VMEM((2 * SWA_WINDOW, SWA_GROUP * SWA_WINDOW), F32)],
        compiler_params=pltpu.CompilerParams(
            dimension_semantics=("arbitrary", "arbitrary"), vmem_limit_bytes=VMEM_LIMIT),
        name="swa",
    )(q_a, k_a, k_a, vt_a, vt_a, sink_lanes)


def _mla_kernel(qt_ref, k_ref, vt_ref, o_ref, m_scr, l_scr, acc_scr, s_scr, *, tq, tk):
    assert tq == 2 * tk
    chunks = tk // MLA_KV_CHUNK
    q_chunks = tq // MLA_KV_CHUNK
    n_q = qt_ref.shape[2] // q_chunks

    def q_block(i, carry):
        m_scr[...] = jnp.full(m_scr.shape, NEG_INF, F32)
        l_scr[...] = jnp.zeros(l_scr.shape, F32)
        acc_scr[...] = jnp.zeros(acc_scr.shape, F32)

        def scores(j, buf, q0=0, blk=i):
            k = k_ref[0, 0, pl.ds(pl.multiple_of(j * tk, tk), tk), :]
            for c in range(q0 // MLA_KV_CHUNK, q_chunks):
                s_scr[buf, :, c * MLA_KV_CHUNK:(c + 1) * MLA_KV_CHUNK] = _dot(
                    k, qt_ref[0, 0, blk * q_chunks + c])

        def softmax_pv(j, buf, q0=0, key_offset=None):
            s = s_scr[buf, :, q0:]
            if key_offset is not None:
                kpos = lax.broadcasted_iota(jnp.int32, s.shape, 0) + key_offset
                qpos = lax.broadcasted_iota(jnp.int32, s.shape, 1) + q0
                s = jnp.where(kpos <= qpos, s, NEG_INF)
            m_old = m_scr[:, q0:]
            m_new = jnp.maximum(m_old, jnp.max(s, axis=0, keepdims=True))
            alpha = jnp.exp2(m_old - m_new)
            p = jnp.exp2(s - m_new)
            l_scr[:, q0:] = alpha * l_scr[:, q0:] + jnp.sum(p, axis=0, keepdims=True)
            pb = p.astype(BF16)
            pv = _dot(vt_ref[0, 0, j * chunks], pb[:MLA_KV_CHUNK])
            for c in range(1, chunks):
                pv += _dot(vt_ref[0, 0, j * chunks + c],
                           pb[c * MLA_KV_CHUNK:(c + 1) * MLA_KV_CHUNK])
            acc_scr[:, q0:] = alpha * acc_scr[:, q0:] + pv
            m_scr[:, q0:] = m_new

        @pl.when(i == 0)
        def _():
            scores(0, 0)

        def pair(jj, c):
            t = 2 * jj
            scores(t + 1, 1)
            softmax_pv(t, 0)
            scores(t + 2, 0)
            softmax_pv(t + 1, 1)
            return c

        lax.fori_loop(0, i, pair, 0)
        scores(2 * i + 1, 1, q0=tk)
        softmax_pv(2 * i, 0, key_offset=0)
        scores(0, 0, blk=jnp.minimum(i + 1, n_q - 1))
        softmax_pv(2 * i + 1, 1, q0=tk, key_offset=tk)

        o = acc_scr[...] / l_scr[...]
        o_ref[0, pl.ds(pl.multiple_of(i * tq, tq), tq), :] = o.T.astype(o_ref.dtype)
        return carry

    lax.fori_loop(0, n_q, q_block, 0)


def _mla(q_m, k_m, vt_m, batch, seq, tq, tk):
    return pl.pallas_call(
        functools.partial(_mla_kernel, tq=tq, tk=tk),
        grid=(batch, MLA_HEADS),
        in_specs=[
            pl.BlockSpec((1, 1, seq // MLA_KV_CHUNK, MLA_QK_PAD, MLA_KV_CHUNK),
                         lambda b, h: (b, h, 0, 0, 0)),
            pl.BlockSpec((1, 1, seq, MLA_QK_PAD), lambda b, h: (b, h, 0, 0)),
            pl.BlockSpec((1, 1, seq // MLA_KV_CHUNK, MLA_V_DIM, MLA_KV_CHUNK),
                         lambda b, h: (b, h, 0, 0, 0)),
        ],
        out_specs=pl.BlockSpec((1, seq, MLA_V_DIM), lambda b, h: (b, 0, h)),
        out_shape=jax.ShapeDtypeStruct((batch, seq, D_MODEL), BF16),
        scratch_shapes=[
            pltpu.VMEM((1, tq), F32),
            pltpu.VMEM((1, tq), F32),
            pltpu.VMEM((MLA_V_DIM, tq), F32),
            pltpu.VMEM((2, tk, tq), F32),
        ],
        compiler_params=pltpu.CompilerParams(
            dimension_semantics=("arbitrary", "arbitrary"), vmem_limit_bytes=VMEM_LIMIT),
        name="mla",
    )(q_m, k_m, vt_m)


def _mix_and_route(x_ref, oa_ref, ob_ref, ga_ref, gb_ref, wo_ref, gf_ref, wr_ref, br_ref):
    merged = (ga_ref[...].astype(F32) * oa_ref[...].astype(F32)
              + gb_ref[...].astype(F32) * ob_ref[...].astype(F32)).astype(BF16)
    h = x_ref[...] + _dot(merged, wo_ref[...])
    xn = _rms(h, gf_ref[...])

    tm = xn.shape[0]
    x_hi = xn.astype(BF16)
    x_lo = (xn - x_hi.astype(F32)).astype(BF16)
    prod = _dot(jnp.concatenate([x_hi, x_lo], axis=0), wr_ref[...])
    lg = (prod[:tm, :LANES] + prod[:tm, LANES:] + prod[tm:, :LANES] + prod[tm:, LANES:]
          + br_ref[...])
    lg_t = lg.T
    big = jnp.int32(LANES)
    g_row = lax.broadcasted_iota(jnp.int32, (8, tm), 0)
    gl = jnp.where(g_row < N_GROUPS, lg_t[N_EXPERTS:N_EXPERTS + 8], NEG_INF)
    g_max = jnp.max(gl, axis=0, keepdims=True)
    g_w = 1.0 / jnp.sum(jnp.exp(gl - g_max), axis=0, keepdims=True)
    gidx = jnp.min(jnp.where(gl == g_max, g_row, big), axis=0, keepdims=True)

    row = lax.broadcasted_iota(jnp.int32, (N_EXPERTS, tm), 0)
    sel = (row >= gidx * EXPERTS_PER_GROUP) & (row < (gidx + 1) * EXPERTS_PER_GROUP)
    el = jnp.where(sel, lg_t[:N_EXPERTS], NEG_INF)
    e_exp = jnp.exp(el - jnp.max(el, axis=0, keepdims=True))
    e_prob = e_exp / jnp.sum(e_exp, axis=0, keepdims=True)
    p1 = jnp.max(e_prob, axis=0, keepdims=True)
    i1 = jnp.min(jnp.where(sel & (e_prob == p1), row, big), axis=0, keepdims=True)
    rest = sel & (row != i1)
    p2 = jnp.max(jnp.where(rest, e_prob, -1.0), axis=0, keepdims=True)
    i2 = jnp.min(jnp.where(rest & (e_prob == p2), row, big), axis=0, keepdims=True)
    top_sum = p1 + p2
    comb = jnp.where(row == i1, p1 / top_sum, jnp.where(row == i2, p2 / top_sum, 0.0)) * g_w
    gid_rows = jnp.where(g_row == 0, gidx.astype(F32), 0.0)
    comb_t = jnp.concatenate(
        [comb, gid_rows, jnp.zeros((LANES - N_EXPERTS - 8, tm), F32)], axis=0)
    return h, x_hi, comb_t


def _block_tail_kernel(x_ref, oa_ref, ob_ref, ga_ref, gb_ref, wout_ref, gf_ref, wr_ref, br_ref,
                       wi_ref, wo_ref, p_ref, gp_ref, wg_ref, wp_ref, gfin_ref, o_ref, *, chunk):
    tm = x_ref.shape[0]
    h, xn, comb_t = _mix_and_route(
        x_ref, oa_ref, ob_ref, ga_ref, gb_ref, wout_ref, gf_ref, wr_ref, br_ref)
    o_ref[...] = h
    comb = comb_t.T
    comb_hi = comb.astype(BF16)
    comb_lo = (comb - comb_hi.astype(F32)).astype(BF16)
    comb_hl = jnp.concatenate([comb_hi, comb_lo], axis=1)
    gid_col = comb[:, N_EXPERTS:N_EXPERTS + 1]
    gid_row = comb_t[N_EXPERTS:N_EXPERTS + 1, :]
    member_col = gid_col == lax.broadcasted_iota(jnp.int32, (tm, LANES), 1).astype(F32)
    member_row = gid_row == lax.broadcasted_iota(jnp.int32, (8, tm), 0).astype(F32)
    r_i = lax.broadcasted_iota(jnp.int32, (tm, tm), 0)
    c_i = lax.broadcasted_iota(jnp.int32, (tm, tm), 1)
    ranks_row = _dot(member_row.astype(BF16), (r_i < c_i).astype(BF16))
    ranks_col = _dot((r_i > c_i).astype(BF16), member_col.astype(BF16))
    pad = 2 * LANES
    never = -4.0 * tm
    slot_row = lax.broadcasted_iota(jnp.int32, (chunk, tm), 0).astype(F32)
    slot_col = lax.broadcasted_iota(jnp.int32, (tm, pad), 1).astype(F32)
    slot_col = jnp.where(slot_col < float(chunk), slot_col, never)

    groups = range(N_GROUPS)
    in_rows = [member_row[g:g + 1] for g in groups]
    rank_rows = [jnp.where(in_rows[g], ranks_row[g:g + 1], -2.0 * never) for g in groups]
    rank_cols = [jnp.where(member_col[:, g:g + 1], ranks_col[:, g:g + 1], -2.0 * never)
                 for g in groups]

    def one_hots(g, base):
        gather = jnp.where(rank_rows[g] - base == slot_row, 1.0, 0.0).astype(BF16)
        scatter = jnp.where(rank_cols[g] - base == slot_col, 1.0, 0.0).astype(BF16)
        return gather, scatter

    def compact(gather):
        xc = _dot(gather, xn).astype(BF16)
        wc = _dot(gather, comb_hl)
        return xc, wc[:, :LANES] + wc[:, LANES:]

    def experts_in(g, xc):
        return [_dot(xc, wi_ref[g * EXPERTS_PER_GROUP + j]) for j in range(EXPERTS_PER_GROUP)]

    def activate(g, hids, wc):
        acts = []
        for j, hid in enumerate(hids):
            e = g * EXPERTS_PER_GROUP + j
            gate = hid[:, :D_EXPERT]
            act = (gate * _sigmoid(gate)) * hid[:, D_EXPERT:] * wc[:, e:e + 1]
            acts.append(act.astype(BF16))
        return acts

    def experts_out(g, acts):
        yc = _dot(acts[0], wo_ref[g * EXPERTS_PER_GROUP])
        for j in range(1, EXPERTS_PER_GROUP):
            yc = yc + _dot(acts[j], wo_ref[g * EXPERTS_PER_GROUP + j])
        return jnp.concatenate(
            [yc.astype(BF16), jnp.zeros((pad - chunk, D_MODEL), BF16)], axis=0)

    hots = [one_hots(g, 0.0) for g in groups]
    compacted = [compact(hots[g][0]) for g in groups]
    hidden = [experts_in(g, compacted[g][0]) for g in groups]
    acts = [activate(g, hidden[g], compacted[g][1]) for g in groups]
    outs = [experts_out(g, acts[g]) for g in groups]
    o_ref[...] += _dot(jnp.concatenate([hots[g][1] for g in groups], axis=1),
                       jnp.concatenate(outs, axis=0))

    for g in groups:
        count = jnp.sum(in_rows[g].astype(jnp.int32))
        n_chunks = (count + (chunk - 1)) // chunk

        def chunk_body(c, carry, g=g):
            gather, scatter = one_hots(g, (c * chunk).astype(F32))
            xc, wc = compact(gather)
            yc = experts_out(g, activate(g, experts_in(g, xc), wc))
            o_ref[...] += _dot(scatter, yc)
            return carry

        lax.fori_loop(1, n_chunks, chunk_body, 0)

    h = o_ref[...]
    gate = _sigmoid(_dot(_rms(h, gp_ref[...]).astype(BF16), wg_ref[...]))
    h = h + gate * _dot(p_ref[...].astype(BF16), wp_ref[...])
    o_ref[...] = _rms(h, gfin_ref[...])


def _block_tail(x2, o_a, o_b, g_a, g_b, w_o, g_ffn, w_r, b_r, w_ei, w_eo, p2, g_ple, w_pg, w_pp,
                g_final, tm, chunk):
    tokens = x2.shape[0]
    tok = lambda t: (t, 0)
    const = lambda t: (0, 0)
    return pl.pallas_call(
        functools.partial(_block_tail_kernel, chunk=chunk),
        grid=(tokens // tm,),
        in_specs=[
            pl.BlockSpec((tm, D_MODEL), tok),
            pl.BlockSpec((tm, D_MODEL), tok),
            pl.BlockSpec((tm, D_MODEL), tok),
            pl.BlockSpec((tm, D_MODEL), tok),
            pl.BlockSpec((tm, D_MODEL), tok),
            pl.BlockSpec((D_MODEL, D_MODEL), const, pipeline_mode=pl.Buffered(1)),
            pl.BlockSpec((1, D_MODEL), const),
            pl.BlockSpec((D_MODEL, 2 * LANES), const, pipeline_mode=pl.Buffered(1)),
            pl.BlockSpec((1, LANES), const),
            pl.BlockSpec(w_ei.shape, lambda t: (0, 0, 0), pipeline_mode=pl.Buffered(1)),
            pl.BlockSpec(w_eo.shape, lambda t: (0, 0, 0), pipeline_mode=pl.Buffered(1)),
            pl.BlockSpec((tm, PLE_DIM), tok),
            pl.BlockSpec((1, D_MODEL), const),
            pl.BlockSpec((D_MODEL, D_MODEL), const, pipeline_mode=pl.Buffered(1)),
            pl.BlockSpec((PLE_DIM, D_MODEL), const, pipeline_mode=pl.Buffered(1)),
            pl.BlockSpec((1, D_MODEL), const),
        ],
        out_specs=pl.BlockSpec((tm, D_MODEL), tok),
        out_shape=jax.ShapeDtypeStruct((tokens, D_MODEL), F32),
        compiler_params=pltpu.CompilerParams(
            dimension_semantics=("arbitrary",), vmem_limit_bytes=VMEM_LIMIT),
        name="block_tail",
    )(x2, o_a, o_b, g_a, g_b, w_o, g_ffn, w_r, b_r, w_ei, w_eo, p2, g_ple, w_pg, w_pp, g_final)


def _rope_tables(seq):
    half = SWA_HEAD_DIM // 2
    pos = jnp.arange(seq, dtype=F32)
    inv = ROPE_THETA ** (-jnp.arange(0, SWA_HEAD_DIM, 2, dtype=F32) / SWA_HEAD_DIM)
    ang = pos[:, None] * inv[None, :]
    cos, sin = jnp.cos(ang), jnp.sin(ang)
    reps = LANES // SWA_HEAD_DIM
    cos_t = jnp.broadcast_to(cos[:, None, :], (seq, 2 * reps, half)).reshape(seq, LANES)
    sin_t = jnp.broadcast_to(
        jnp.stack([-sin, sin], axis=1)[:, None], (seq, reps, 2, half)).reshape(seq, LANES)
    return cos_t, sin_t, cos.T, sin.T


def _layer(h2d, p2d, g_mix, w_in, sinks, g_q, w_uq, g_kv, w_ukv, w_out, g_ffn, w_rg, b_rg,
           w_re, b_re, w_ei, w_eo, g_ple, w_pg, w_pp, g_final, rope, batch, seq):
    cos_t, sin_t, cos_h, sin_h = rope
    assert SWA_HEAD_DIM == MLA_ROPE_DIM, "one rotary table serves both mixers"
    w_main = w_in[:, :_C_GATES].astype(BF16)
    w_gates = w_in[:, _C_GATES:].astype(BF16)
    assert w_gates.shape[1] == 2 * D_MODEL
    w_uqt = jnp.transpose(
        w_uq.reshape(MLA_Q_RANK, MLA_HEADS, MLA_QK_DIM), (1, 2, 0)).astype(BF16)
    ukv = w_ukv.reshape(MLA_KV_RANK, MLA_HEADS, MLA_NOPE_DIM + MLA_V_DIM)
    w_uk_c = ukv[:, :, :MLA_NOPE_DIM].reshape(MLA_KV_RANK, -1).astype(BF16)
    w_uvt = jnp.transpose(ukv[:, :, MLA_NOPE_DIM:], (1, 2, 0)).astype(BF16)

    q_a, k_a, v_a, g_a, g_b, q_m, k_m, v_m = _inproj(
        h2d, g_mix[None], w_main, w_gates, cos_t, sin_t, cos_h, sin_h, g_q[None], w_uqt,
        g_kv[None], w_uk_c, w_uvt, batch, seq, tm=MLA_KV_CHUNK)

    sink_lanes = jnp.repeat(sinks.astype(F32), SWA_WINDOW)[None]
    o_a = _swa(q_a, k_a, v_a, sink_lanes, batch, seq, tq=512)
    o_b = _mla(q_m, k_m, v_m, batch, seq, tq=1024, tk=512)

    w_r = jnp.concatenate(
        [w_re, w_rg, jnp.zeros((D_MODEL, LANES - N_EXPERTS - N_GROUPS), F32)], axis=1)
    w_r_hi = w_r.astype(BF16)
    w_r_lo = (w_r - w_r_hi.astype(F32)).astype(BF16)
    w_r2 = jnp.concatenate([w_r_hi, w_r_lo], axis=1)
    b_r = jnp.concatenate(
        [b_re, b_rg, jnp.zeros((LANES - N_EXPERTS - N_GROUPS,), F32)])[None]
    tokens = batch * seq
    return _block_tail(
        h2d, o_a.reshape(tokens, D_MODEL), o_b.reshape(tokens, D_MODEL), g_a, g_b,
        w_out.astype(BF16), g_ffn[None], w_r2, b_r, w_ei.astype(BF16), w_eo.astype(BF16), p2d,
        g_ple[None], w_pg.astype(BF16), w_pp.astype(BF16), g_final[None], tm=512, chunk=160)


def kernel(x, p, g_mix, w_in, swa_sinks, mla_g_q, mla_w_uq, mla_g_kv, mla_w_ukv, w_out, g_ffn,
           w_router_group, b_router_group, w_router_expert, b_router_expert, w_expert_in,
           w_expert_out, g_ple, w_ple_gate, w_ple_proj, g_final):
    batch, seq, d = x.shape
    depth = p.shape[0]
    assert d == D_MODEL and depth == 1, "final RMSNorm is fused into the single layer"
    rope = _rope_tables(seq)
    out = _layer(
        x.reshape(batch * seq, d), p[0].reshape(batch * seq, PLE_DIM), g_mix[0], w_in[0],
        swa_sinks[0], mla_g_q[0], mla_w_uq[0], mla_g_kv[0], mla_w_ukv[0], w_out[0], g_ffn[0],
        w_router_group[0], b_router_group[0], w_router_expert[0], b_router_expert[0],
        w_expert_in[0], w_expert_out[0], g_ple[0], w_ple_gate[0], w_ple_proj[0], g_final,
        rope, batch, seq)
    return out.reshape(batch, seq, d)
```

```python
import functools
import math

import jax
import jax.numpy as jnp
from jax import lax
from jax.experimental import pallas as pl
from jax.experimental.pallas import tpu as pltpu

D_MODEL = 1024
PLE_DIM = 256
ROPE_THETA = 10000.0
EPS = 1e-6
NEG_INF = -1e30

SWA_HEAD_DIM = 64
SWA_HEADS = D_MODEL // SWA_HEAD_DIM
SWA_KV_HEADS = SWA_HEADS // 8
SWA_GROUP = SWA_HEADS // SWA_KV_HEADS
SWA_WINDOW = 128

MLA_NOPE_DIM = 128
MLA_ROPE_DIM = 64
MLA_V_DIM = 128
MLA_HEADS = D_MODEL // MLA_V_DIM
MLA_Q_RANK = 256
MLA_KV_RANK = 128
MLA_QK_DIM = MLA_NOPE_DIM + MLA_ROPE_DIM

N_GROUPS = 4
EXPERTS_PER_GROUP = 4
N_EXPERTS = N_GROUPS * EXPERTS_PER_GROUP
D_EXPERT = 256

LANES = 128
LOG2_E = 1.4426950408889634
MLA_KV_CHUNK = 512
VMEM_LIMIT = 56 * 1024 * 1024

_C_QA = 0
_C_KV = _C_QA + SWA_HEADS * SWA_HEAD_DIM
_C_CQ = _C_KV + 2 * SWA_KV_HEADS * SWA_HEAD_DIM
_C_CKVR = _C_CQ + MLA_Q_RANK
_C_GATES = _C_CKVR + MLA_KV_RANK + MLA_ROPE_DIM

BF16 = jnp.bfloat16
F32 = jnp.float32


def _dot(a, b):
    return jnp.dot(a, b, preferred_element_type=F32)


def _dot_nt(a, b):
    return lax.dot_general(a, b, (((1,), (1,)), ((), ())), preferred_element_type=F32)


def _rms(x, g):
    r = lax.rsqrt(jnp.mean(x * x, axis=-1, keepdims=True) + EPS)
    return x * r * g


def _sigmoid(x):
    return 1.0 / (1.0 + jnp.exp(-x))


def _rope_lanes(x, cos, sin_signed):
    lane = lax.broadcasted_iota(jnp.int32, x.shape, 1)
    upper = (lane & (SWA_HEAD_DIM // 2)) != 0
    partner = jnp.where(upper, pltpu.roll(x, SWA_HEAD_DIM // 2, 1),
                        pltpu.roll(x, LANES - SWA_HEAD_DIM // 2, 1))
    return x * cos + partner * sin_signed


def _inproj_kernel(x_ref, g_ref, w_ref, wg_ref, cos_ref, sin_ref, cost_ref, sint_ref, gq_ref,
                   wuqt_ref, gkv_ref, wuk_ref, wuvt_ref, qa_ref, ka_ref, va_ref, ga_ref, gb_ref,
                   qm_ref, km_ref, vt_ref):
    xn = _rms(x_ref[...], g_ref[...]).astype(BF16)
    cos = cos_ref[...]
    sin = sin_ref[...]
    tm = xn.shape[0]

    cq = _rms(_dot(xn, w_ref[:, _C_CQ:_C_CKVR]), gq_ref[...])
    ckvr = _dot(xn, w_ref[:, _C_CKVR:_C_GATES])
    ckv = _rms(ckvr[:, :MLA_KV_RANK], gkv_ref[...])
    kr = jnp.concatenate(
        [ckvr[:, MLA_KV_RANK:], jnp.zeros((tm, LANES - MLA_ROPE_DIM), F32)], axis=1)
    kr = _rope_lanes(kr, cos, sin)[:, :MLA_ROPE_DIM].astype(BF16)
    qm_scale = LOG2_E / math.sqrt(MLA_QK_DIM)
    cq_t = cq.T.astype(BF16)
    ckv_t = ckv.T.astype(BF16)
    kn = _dot(ckv.astype(BF16), wuk_ref[...]).astype(BF16)
    vt_all = _dot(wuvt_ref[...].reshape(MLA_HEADS * MLA_V_DIM, MLA_KV_RANK), ckv_t)
    qt_all = _dot(wuqt_ref[...].reshape(MLA_HEADS * MLA_QK_DIM, MLA_Q_RANK), cq_t)

    qa = _dot(xn, w_ref[:, _C_QA:_C_KV])
    kv = _dot(xn, w_ref[:, _C_KV:_C_CQ])
    gate_a = _dot(xn, wg_ref[:, :D_MODEL])
    gate_b = _dot(xn, wg_ref[:, D_MODEL:])

    cos_h, sin_h = cost_ref[...], sint_ref[...]
    half = MLA_ROPE_DIM // 2
    for h in range(MLA_HEADS):
        km_ref[0, h, :, :MLA_NOPE_DIM] = kn[:, h * MLA_NOPE_DIM:(h + 1) * MLA_NOPE_DIM]
        km_ref[0, h, :, MLA_NOPE_DIM:] = kr
        vt_ref[0, h, 0] = vt_all[h * MLA_V_DIM:(h + 1) * MLA_V_DIM].astype(BF16)
        q_t = qt_all[h * MLA_QK_DIM:(h + 1) * MLA_QK_DIM]
        x1 = q_t[MLA_NOPE_DIM:MLA_NOPE_DIM + half]
        x2 = q_t[MLA_NOPE_DIM + half:]
        q_t = jnp.concatenate(
            [q_t[:MLA_NOPE_DIM], x1 * cos_h - x2 * sin_h, x2 * cos_h + x1 * sin_h], axis=0)
        qm_ref[0, h, 0] = (q_t * qm_scale).astype(BF16)

    qa_scale = LOG2_E / math.sqrt(SWA_HEAD_DIM)
    for j in range(SWA_HEADS // 2):
        q = (_rope_lanes(qa[:, j * LANES:(j + 1) * LANES], cos, sin) * qa_scale).astype(BF16)
        qa_ref[0, 2 * j] = q[:, :SWA_HEAD_DIM]
        qa_ref[0, 2 * j + 1] = q[:, SWA_HEAD_DIM:]
    k = _rope_lanes(kv[:, :LANES], cos, sin).astype(BF16)
    ka_ref[0, 0] = k[:, :SWA_HEAD_DIM]
    ka_ref[0, 1] = k[:, SWA_HEAD_DIM:]
    vt = kv[:, LANES:].T.astype(BF16)
    va_ref[0, 0] = vt[:SWA_HEAD_DIM]
    va_ref[0, 1] = vt[SWA_HEAD_DIM:]

    ga_ref[...] = _sigmoid(gate_a).astype(BF16)
    gb_ref[...] = _sigmoid(gate_b).astype(BF16)


def _inproj(x2, g_mix, w_main, w_gates, cos_t, sin_t, cos_h, sin_h, g_q, w_uqt, g_kv, w_uk, w_uvt,
            batch, seq, tm):
    tokens = batch * seq
    nt = seq // tm
    const = lambda t: (0, 0)
    tok = lambda t: (t, 0)
    head = lambda t: (t // nt, 0, t % nt, 0)
    out_shape = (
        jax.ShapeDtypeStruct((batch, SWA_HEADS, seq, SWA_HEAD_DIM), BF16),
        jax.ShapeDtypeStruct((batch, SWA_KV_HEADS, seq, SWA_HEAD_DIM), BF16),
        jax.ShapeDtypeStruct((batch, SWA_KV_HEADS, SWA_HEAD_DIM, seq), BF16),
        jax.ShapeDtypeStruct((tokens, D_MODEL), BF16),
        jax.ShapeDtypeStruct((tokens, D_MODEL), BF16),
        jax.ShapeDtypeStruct((batch, MLA_HEADS, nt, MLA_QK_DIM, tm), BF16),
        jax.ShapeDtypeStruct((batch, MLA_HEADS, seq, MLA_QK_DIM), BF16),
        jax.ShapeDtypeStruct((batch, MLA_HEADS, nt, MLA_V_DIM, tm), BF16),
    )
    return pl.pallas_call(
        _inproj_kernel,
        grid=(tokens // tm,),
        in_specs=[
            pl.BlockSpec((tm, D_MODEL), tok),
            pl.BlockSpec((1, D_MODEL), const),
            pl.BlockSpec(w_main.shape, const, pipeline_mode=pl.Buffered(1)),
            pl.BlockSpec(w_gates.shape, const, pipeline_mode=pl.Buffered(1)),
            pl.BlockSpec((tm, LANES), lambda t: (t % nt, 0)),
            pl.BlockSpec((tm, LANES), lambda t: (t % nt, 0)),
            pl.BlockSpec((MLA_ROPE_DIM // 2, tm), lambda t: (0, t % nt)),
            pl.BlockSpec((MLA_ROPE_DIM // 2, tm), lambda t: (0, t % nt)),
            pl.BlockSpec((1, MLA_Q_RANK), const),
            pl.BlockSpec(w_uqt.shape, lambda t: (0, 0, 0), pipeline_mode=pl.Buffered(1)),
            pl.BlockSpec((1, MLA_KV_RANK), const),
            pl.BlockSpec(w_uk.shape, const, pipeline_mode=pl.Buffered(1)),
            pl.BlockSpec(w_uvt.shape, lambda t: (0, 0, 0), pipeline_mode=pl.Buffered(1)),
        ],
        out_specs=(
            pl.BlockSpec((1, SWA_HEADS, tm, SWA_HEAD_DIM), head),
            pl.BlockSpec((1, SWA_KV_HEADS, tm, SWA_HEAD_DIM), head),
            pl.BlockSpec((1, SWA_KV_HEADS, SWA_HEAD_DIM, tm), lambda t: (t // nt, 0, 0, t % nt)),
            pl.BlockSpec((tm, D_MODEL), tok),
            pl.BlockSpec((tm, D_MODEL), tok),
            pl.BlockSpec((1, MLA_HEADS, 1, MLA_QK_DIM, tm), lambda t: (t // nt, 0, t % nt, 0, 0)),
            pl.BlockSpec((1, MLA_HEADS, tm, MLA_QK_DIM), head),
            pl.BlockSpec((1, MLA_HEADS, 1, MLA_V_DIM, tm), lambda t: (t // nt, 0, t % nt, 0, 0)),
        ),
        out_shape=out_shape,
        compiler_params=pltpu.CompilerParams(
            dimension_semantics=("arbitrary",), vmem_limit_bytes=VMEM_LIMIT),
        name="inproj",
    )(x2, g_mix, w_main, w_gates, cos_t, sin_t, cos_h, sin_h, g_q, w_uqt, g_kv, w_uk, w_uvt)


def _swa_kernel(q_ref, k_ref, kp_ref, vt_ref, vtp_ref, sink_ref, o_ref, s_scr, bias_scr, *,
                n_sub):
    blk = SWA_WINDOW
    cols = SWA_GROUP * blk

    @pl.when((pl.program_id(0) == 0) & (pl.program_id(1) == 0))
    def _():
        kj = lax.broadcasted_iota(jnp.int32, (2 * blk, cols), 0)
        qi = lax.broadcasted_iota(jnp.int32, (2 * blk, cols), 1) & (blk - 1)
        dist = blk + qi - kj
        bias_scr[...] = jnp.where((dist >= 0) & (dist < SWA_WINDOW), 0.0, NEG_INF)

    key_row = lax.broadcasted_iota(jnp.int32, (2 * blk, 1), 0)
    no_prev = jnp.where((key_row < blk) & (pl.program_id(1) == 0), NEG_INF, 0.0)
    units = [(g, n) for g in range(SWA_KV_HEADS) for n in range(n_sub)]

    def scores(u, buf):
        g, n = u
        q = q_ref[0, g * SWA_GROUP:(g + 1) * SWA_GROUP, n * blk:(n + 1) * blk, :]
        k_prev = kp_ref[0, g] if n == 0 else k_ref[0, g, (n - 1) * blk:n * blk, :]
        kk = jnp.concatenate([k_prev, k_ref[0, g, n * blk:(n + 1) * blk, :]], axis=0)
        s_scr[buf] = _dot_nt(kk, q.reshape(cols, SWA_HEAD_DIM))

    def softmax_pv(u, buf):
        g, n = u
        sink = sink_ref[:, g * cols:(g + 1) * cols] * LOG2_E
        s = s_scr[buf] + bias_scr[...]
        if n == 0:
            s = s + no_prev
        m = jnp.maximum(jnp.max(s, axis=0, keepdims=True), sink)
        e = jnp.exp2(s - m)
        denom = jnp.sum(e, axis=0, keepdims=True) + jnp.exp2(sink - m)
        vt_prev = vtp_ref[0, g] if n == 0 else vt_ref[0, g, :, (n - 1) * blk:n * blk]
        vvt = jnp.concatenate([vt_prev, vt_ref[0, g, :, n * blk:(n + 1) * blk]], axis=1)
        o = (_dot(vvt, e.astype(BF16)) / denom).T
        for hh in range(SWA_GROUP):
            h = g * SWA_GROUP + hh
            o_ref[0, n * blk:(n + 1) * blk, h * SWA_HEAD_DIM:(h + 1) * SWA_HEAD_DIM] = (
                o[hh * blk:(hh + 1) * blk, :].astype(BF16))

    scores(units[0], 0)
    for idx, u in enumerate(units):
        if idx + 1 < len(units):
            scores(units[idx + 1], (idx + 1) % 2)
        softmax_pv(u, idx % 2)


def _swa(q_a, k_a, vt_a, sink_lanes, batch, seq, tq):
    n_sub = tq // SWA_WINDOW
    cur = lambda b, i: (b, 0, i, 0)
    prev = lambda b, i: (b, 0, jnp.maximum(i * n_sub - 1, 0), 0)
    return pl.pallas_call(
        functools.partial(_swa_kernel, n_sub=n_sub),
        grid=(batch, seq // tq),
        in_specs=[
            pl.BlockSpec((1, SWA_HEADS, tq, SWA_HEAD_DIM), cur),
            pl.BlockSpec((1, SWA_KV_HEADS, tq, SWA_HEAD_DIM), cur),
            pl.BlockSpec((1, SWA_KV_HEADS, SWA_WINDOW, SWA_HEAD_DIM), prev),
            pl.BlockSpec((1, SWA_KV_HEADS, SWA_HEAD_DIM, tq), lambda b, i: (b, 0, 0, i)),
            pl.BlockSpec((1, SWA_KV_HEADS, SWA_HEAD_DIM, SWA_WINDOW),
                         lambda b, i: (b, 0, 0, jnp.maximum(i * n_sub - 1, 0))),
            pl.BlockSpec(sink_lanes.shape, lambda b, i: (0, 0)),
        ],
        out_specs=pl.BlockSpec((1, tq, D_MODEL), lambda b, i: (b, i, 0)),
        out_shape=jax.ShapeDtypeStruct((batch, seq, D_MODEL), BF16),
        scratch_shapes=[pltpu.VMEM((2, 2 * SWA_WINDOW, SWA_GROUP * SWA_WINDOW), F32),
                        pltpu.VMEM((2 * SWA_WINDOW, SWA_GROUP * SWA_WINDOW), F32)],
        compiler_params=pltpu.CompilerParams(
            dimension_semantics=("arbitrary", "arbitrary"), vmem_limit_bytes=VMEM_LIMIT),
        name="swa",
    )(q_a, k_a, k_a, vt_a, vt_a, sink_lanes)


def _mla_kernel(qt_ref, k_ref, vt_ref, o_ref, m_scr, l_scr, acc_scr, s_scr, *, tq, tk):
    ratio = tq // tk
    assert tq == ratio * tk and ratio % 2 == 0
    chunks = tk // MLA_KV_CHUNK
    q_chunks = tq // MLA_KV_CHUNK
    n_q = qt_ref.shape[2] // q_chunks

    def q_block(i, carry):
        m_scr[...] = jnp.full(m_scr.shape, NEG_INF, F32)
        l_scr[...] = jnp.zeros(l_scr.shape, F32)
        acc_scr[...] = jnp.zeros(acc_scr.shape, F32)

        def scores(j, buf, q0=0, blk=i):
            k = k_ref[0, 0, pl.ds(pl.multiple_of(j * tk, tk), tk), :]
            for c in range(q0 // MLA_KV_CHUNK, q_chunks):
                s_scr[buf, :, c * MLA_KV_CHUNK:(c + 1) * MLA_KV_CHUNK] = _dot(
                    k, qt_ref[0, 0, blk * q_chunks + c])

        def softmax_pv(j, buf, q0=0, key_offset=None):
            s = s_scr[buf, :, q0:]
            if key_offset is not None:
                kpos = lax.broadcasted_iota(jnp.int32, s.shape, 0) + key_offset
                qpos = lax.broadcasted_iota(jnp.int32, s.shape, 1) + q0
                s = jnp.where(kpos <= qpos, s, NEG_INF)
            m_old = m_scr[:, q0:]
            m_new = jnp.maximum(m_old, jnp.max(s, axis=0, keepdims=True))
            alpha = jnp.exp2(m_old - m_new)
            p = jnp.exp2(s - m_new)
            l_scr[:, q0:] = alpha * l_scr[:, q0:] + jnp.sum(p, axis=0, keepdims=True)
            pb = p.astype(BF16)
            pv = _dot(vt_ref[0, 0, j * chunks], pb[:MLA_KV_CHUNK])
            for c in range(1, chunks):
                pv += _dot(vt_ref[0, 0, j * chunks + c],
                           pb[c * MLA_KV_CHUNK:(c + 1) * MLA_KV_CHUNK])
            acc_scr[:, q0:] = alpha * acc_scr[:, q0:] + pv
            m_scr[:, q0:] = m_new

        @pl.when(i == 0)
        def _():
            scores(0, 0)

        def pair(jj, c):
            t = 2 * jj
            scores(t + 1, 1)
            softmax_pv(t, 0)
            scores(t + 2, 0)
            softmax_pv(t + 1, 1)
            return c

        lax.fori_loop(0, (ratio // 2) * i, pair, 0)
        for d in range(ratio):
            if d + 1 < ratio:
                scores(ratio * i + d + 1, (d + 1) % 2, q0=(d + 1) * tk)
            else:
                scores(0, 0, blk=jnp.minimum(i + 1, n_q - 1))
            softmax_pv(ratio * i + d, d % 2, q0=d * tk, key_offset=d * tk)

        o = acc_scr[...] / l_scr[...]
        o_ref[0, pl.ds(pl.multiple_of(i * tq, tq), tq), :] = o.T.astype(o_ref.dtype)
        return carry

    lax.fori_loop(0, n_q, q_block, 0)


def _mla(q_m, k_m, vt_m, batch, seq, tq, tk):
    return pl.pallas_call(
        functools.partial(_mla_kernel, tq=tq, tk=tk),
        grid=(batch, MLA_HEADS),
        in_specs=[
            pl.BlockSpec((1, 1, seq // MLA_KV_CHUNK, MLA_QK_DIM, MLA_KV_CHUNK),
                         lambda b, h: (b, h, 0, 0, 0)),
            pl.BlockSpec((1, 1, seq, MLA_QK_DIM), lambda b, h: (b, h, 0, 0)),
            pl.BlockSpec((1, 1, seq // MLA_KV_CHUNK, MLA_V_DIM, MLA_KV_CHUNK),
                         lambda b, h: (b, h, 0, 0, 0)),
        ],
        out_specs=pl.BlockSpec((1, seq, MLA_V_DIM), lambda b, h: (b, 0, h)),
        out_shape=jax.ShapeDtypeStruct((batch, seq, D_MODEL), BF16),
        scratch_shapes=[
            pltpu.VMEM((1, tq), F32),
            pltpu.VMEM((1, tq), F32),
            pltpu.VMEM((MLA_V_DIM, tq), F32),
            pltpu.VMEM((2, tk, tq), F32),
        ],
        compiler_params=pltpu.CompilerParams(
            dimension_semantics=("arbitrary", "arbitrary"), vmem_limit_bytes=VMEM_LIMIT),
        name="mla",
    )(q_m, k_m, vt_m)


def _mix_and_route(x_ref, oa_ref, ob_ref, ga_ref, gb_ref, wo_ref, gf_ref, wr_ref, br_ref):
    merged = (ga_ref[...].astype(F32) * oa_ref[...].astype(F32)
              + gb_ref[...].astype(F32) * ob_ref[...].astype(F32)).astype(BF16)
    h = x_ref[...] + _dot(merged, wo_ref[...])
    xn = _rms(h, gf_ref[...])

    tm = xn.shape[0]
    x_hi = xn.astype(BF16)
    x_lo = (xn - x_hi.astype(F32)).astype(BF16)
    prod = _dot(jnp.concatenate([x_hi, x_lo], axis=0), wr_ref[...])
    lg = (prod[:tm, :LANES] + prod[:tm, LANES:] + prod[tm:, :LANES] + prod[tm:, LANES:]
          + br_ref[...])
    lg_t = lg.T
    big = jnp.int32(LANES)
    g_row = lax.broadcasted_iota(jnp.int32, (8, tm), 0)
    gl = jnp.where(g_row < N_GROUPS, lg_t[N_EXPERTS:N_EXPERTS + 8], NEG_INF)
    g_max = jnp.max(gl, axis=0, keepdims=True)
    g_w = 1.0 / jnp.sum(jnp.exp(gl - g_max), axis=0, keepdims=True)
    gidx = jnp.min(jnp.where(gl == g_max, g_row, big), axis=0, keepdims=True)

    row = lax.broadcasted_iota(jnp.int32, (N_EXPERTS, tm), 0)
    sel = (row >= gidx * EXPERTS_PER_GROUP) & (row < (gidx + 1) * EXPERTS_PER_GROUP)
    el = jnp.where(sel, lg_t[:N_EXPERTS], NEG_INF)
    e_exp = jnp.exp(el - jnp.max(el, axis=0, keepdims=True))
    e_prob = e_exp / jnp.sum(e_exp, axis=0, keepdims=True)
    p1 = jnp.max(e_prob, axis=0, keepdims=True)
    i1 = jnp.min(jnp.where(sel & (e_prob == p1), row, big), axis=0, keepdims=True)
    rest = sel & (row != i1)
    p2 = jnp.max(jnp.where(rest, e_prob, -1.0), axis=0, keepdims=True)
    i2 = jnp.min(jnp.where(rest & (e_prob == p2), row, big), axis=0, keepdims=True)
    top_sum = p1 + p2
    comb = jnp.where(row == i1, p1 / top_sum, jnp.where(row == i2, p2 / top_sum, 0.0)) * g_w
    gid_rows = jnp.where(g_row == 0, gidx.astype(F32), 0.0)
    comb_t = jnp.concatenate(
        [comb, gid_rows, jnp.zeros((LANES - N_EXPERTS - 8, tm), F32)], axis=0)
    return h, x_hi, comb_t


def _block_tail_kernel(x_ref, oa_ref, ob_ref, ga_ref, gb_ref, wout_ref, gf_ref, wr_ref, br_ref,
                       wi_ref, wo_ref, p_ref, gp_ref, wg_ref, wp_ref, gfin_ref, o_ref, *, chunk):
    tm = x_ref.shape[0]
    h, xn, comb_t = _mix_and_route(
        x_ref, oa_ref, ob_ref, ga_ref, gb_ref, wout_ref, gf_ref, wr_ref, br_ref)
    o_ref[...] = h
    comb = comb_t.T
    comb_hi = comb.astype(BF16)
    comb_lo = (comb - comb_hi.astype(F32)).astype(BF16)
    comb_hl = jnp.concatenate([comb_hi, comb_lo], axis=1)
    gid_col = comb[:, N_EXPERTS:N_EXPERTS + 1]
    gid_row = comb_t[N_EXPERTS:N_EXPERTS + 1, :]
    member_col = gid_col == lax.broadcasted_iota(jnp.int32, (tm, LANES), 1).astype(F32)
    member_row = gid_row == lax.broadcasted_iota(jnp.int32, (8, tm), 0).astype(F32)
    r_i = lax.broadcasted_iota(jnp.int32, (tm, tm), 0)
    c_i = lax.broadcasted_iota(jnp.int32, (tm, tm), 1)
    ranks_row = _dot(member_row.astype(BF16), (r_i < c_i).astype(BF16))
    ranks_col = _dot((r_i > c_i).astype(BF16), member_col.astype(BF16))
    pad = 2 * LANES
    never = -4.0 * tm
    slot_row = lax.broadcasted_iota(jnp.int32, (chunk, tm), 0).astype(F32)
    slot_col = lax.broadcasted_iota(jnp.int32, (tm, pad), 1).astype(F32)
    slot_col = jnp.where(slot_col < float(chunk), slot_col, never)

    groups = range(N_GROUPS)
    in_rows = [member_row[g:g + 1] for g in groups]
    rank_rows = [jnp.where(in_rows[g], ranks_row[g:g + 1], -2.0 * never) for g in groups]
    rank_cols = [jnp.where(member_col[:, g:g + 1], ranks_col[:, g:g + 1], -2.0 * never)
                 for g in groups]

    def one_hots(g, base):
        gather = jnp.where(rank_rows[g] - base == slot_row, 1.0, 0.0).astype(BF16)
        scatter = jnp.where(rank_cols[g] - base == slot_col, 1.0, 0.0).astype(BF16)
        return gather, scatter

    def compact(gather):
        xc = _dot(gather, xn).astype(BF16)
        wc = _dot(gather, comb_hl)
        return xc, wc[:, :LANES] + wc[:, LANES:]

    def experts_in(g, xc):
        return [_dot(xc, wi_ref[g * EXPERTS_PER_GROUP + j]) for j in range(EXPERTS_PER_GROUP)]

    def activate(g, hids, wc):
        acts = []
        for j, hid in enumerate(hids):
            e = g * EXPERTS_PER_GROUP + j
            gate = hid[:, :D_EXPERT]
            act = (gate * _sigmoid(gate)) * hid[:, D_EXPERT:] * wc[:, e:e + 1]
            acts.append(act.astype(BF16))
        return acts

    def experts_out(g, acts):
        yc = _dot(acts[0], wo_ref[g * EXPERTS_PER_GROUP])
        for j in range(1, EXPERTS_PER_GROUP):
            yc = yc + _dot(acts[j], wo_ref[g * EXPERTS_PER_GROUP + j])
        return jnp.concatenate(
            [yc.astype(BF16), jnp.zeros((pad - chunk, D_MODEL), BF16)], axis=0)

    hots = [one_hots(g, 0.0) for g in groups]
    compacted = [compact(hots[g][0]) for g in groups]
    hidden = [experts_in(g, compacted[g][0]) for g in groups]
    acts = [activate(g, hidden[g], compacted[g][1]) for g in groups]
    outs = [experts_out(g, acts[g]) for g in groups]
    o_ref[...] += _dot(jnp.concatenate([hots[g][1] for g in groups], axis=1),
                       jnp.concatenate(outs, axis=0))

    for g in groups:
        count = jnp.sum(in_rows[g].astype(jnp.int32))
        n_chunks = (count + (chunk - 1)) // chunk

        def chunk_body(c, carry, g=g):
            gather, scatter = one_hots(g, (c * chunk).astype(F32))
            xc, wc = compact(gather)
            yc = experts_out(g, activate(g, experts_in(g, xc), wc))
            o_ref[...] += _dot(scatter, yc)
            return carry

        lax.fori_loop(1, n_chunks, chunk_body, 0)

    h = o_ref[...]
    gate = _sigmoid(_dot(_rms(h, gp_ref[...]).astype(BF16), wg_ref[...]))
    h = h + gate * _dot(p_ref[...].astype(BF16), wp_ref[...])
    o_ref[...] = _rms(h, gfin_ref[...])


def _block_tail(x2, o_a, o_b, g_a, g_b, w_o, g_ffn, w_r, b_r, w_ei, w_eo, p2, g_ple, w_pg, w_pp,
                g_final, tm, chunk):
    tokens = x2.shape[0]
    tok = lambda t: (t, 0)
    const = lambda t: (0, 0)
    return pl.pallas_call(
        functools.partial(_block_tail_kernel, chunk=chunk),
        grid=(tokens // tm,),
        in_specs=[
            pl.BlockSpec((tm, D_MODEL), tok),
            pl.BlockSpec((tm, D_MODEL), tok),
            pl.BlockSpec((tm, D_MODEL), tok),
            pl.BlockSpec((tm, D_MODEL), tok),
            pl.BlockSpec((tm, D_MODEL), tok),
            pl.BlockSpec((D_MODEL, D_MODEL), const, pipeline_mode=pl.Buffered(1)),
            pl.BlockSpec((1, D_MODEL), const),
            pl.BlockSpec((D_MODEL, 2 * LANES), const, pipeline_mode=pl.Buffered(1)),
            pl.BlockSpec((1, LANES), const),
            pl.BlockSpec(w_ei.shape, lambda t: (0, 0, 0), pipeline_mode=pl.Buffered(1)),
            pl.BlockSpec(w_eo.shape, lambda t: (0, 0, 0), pipeline_mode=pl.Buffered(1)),
            pl.BlockSpec((tm, PLE_DIM), tok),
            pl.BlockSpec((1, D_MODEL), const),
            pl.BlockSpec((D_MODEL, D_MODEL), const, pipeline_mode=pl.Buffered(1)),
            pl.BlockSpec((PLE_DIM, D_MODEL), const, pipeline_mode=pl.Buffered(1)),
            pl.BlockSpec((1, D_MODEL), const),
        ],
        out_specs=pl.BlockSpec((tm, D_MODEL), tok),
        out_shape=jax.ShapeDtypeStruct((tokens, D_MODEL), F32),
        compiler_params=pltpu.CompilerParams(
            dimension_semantics=("arbitrary",), vmem_limit_bytes=VMEM_LIMIT),
        name="block_tail",
    )(x2, o_a, o_b, g_a, g_b, w_o, g_ffn, w_r, b_r, w_ei, w_eo, p2, g_ple, w_pg, w_pp, g_final)


def _rope_tables(seq):
    half = SWA_HEAD_DIM // 2
    pos = jnp.arange(seq, dtype=F32)
    inv = ROPE_THETA ** (-jnp.arange(0, SWA_HEAD_DIM, 2, dtype=F32) / SWA_HEAD_DIM)
    ang = pos[:, None] * inv[None, :]
    cos, sin = jnp.cos(ang), jnp.sin(ang)
    reps = LANES // SWA_HEAD_DIM
    cos_t = jnp.broadcast_to(cos[:, None, :], (seq, 2 * reps, half)).reshape(seq, LANES)
    sin_t = jnp.broadcast_to(
        jnp.stack([-sin, sin], axis=1)[:, None], (seq, reps, 2, half)).reshape(seq, LANES)
    return cos_t, sin_t, cos.T, sin.T


def _layer(h2d, p2d, g_mix, w_in, sinks, g_q, w_uq, g_kv, w_ukv, w_out, g_ffn, w_rg, b_rg,
           w_re, b_re, w_ei, w_eo, g_ple, w_pg, w_pp, g_final, rope, batch, seq):
    cos_t, sin_t, cos_h, sin_h = rope
    assert SWA_HEAD_DIM == MLA_ROPE_DIM, "one rotary table serves both mixers"
    w_main = w_in[:, :_C_GATES].astype(BF16)
    w_gates = w_in[:, _C_GATES:].astype(BF16)
    assert w_gates.shape[1] == 2 * D_MODEL
    w_uqt = jnp.transpose(
        w_uq.reshape(MLA_Q_RANK, MLA_HEADS, MLA_QK_DIM), (1, 2, 0)).astype(BF16)
    ukv = w_ukv.reshape(MLA_KV_RANK, MLA_HEADS, MLA_NOPE_DIM + MLA_V_DIM)
    w_uk_c = ukv[:, :, :MLA_NOPE_DIM].reshape(MLA_KV_RANK, -1).astype(BF16)
    w_uvt = jnp.transpose(ukv[:, :, MLA_NOPE_DIM:], (1, 2, 0)).astype(BF16)

    q_a, k_a, v_a, g_a, g_b, q_m, k_m, v_m = _inproj(
        h2d, g_mix[None], w_main, w_gates, cos_t, sin_t, cos_h, sin_h, g_q[None], w_uqt,
        g_kv[None], w_uk_c, w_uvt, batch, seq, tm=MLA_KV_CHUNK)

    sink_lanes = jnp.repeat(sinks.astype(F32), SWA_WINDOW)[None]
    o_a = _swa(q_a, k_a, v_a, sink_lanes, batch, seq, tq=512)
    o_b = _mla(q_m, k_m, v_m, batch, seq, tq=1024, tk=512)

    w_r = jnp.concatenate(
        [w_re, w_rg, jnp.zeros((D_MODEL, LANES - N_EXPERTS - N_GROUPS), F32)], axis=1)
    w_r_hi = w_r.astype(BF16)
    w_r_lo = (w_r - w_r_hi.astype(F32)).astype(BF16)
    w_r2 = jnp.concatenate([w_r_hi, w_r_lo], axis=1)
    b_r = jnp.concatenate(
        [b_re, b_rg, jnp.zeros((LANES - N_EXPERTS - N_GROUPS,), F32)])[None]
    tokens = batch * seq
    return _block_tail(
        h2d, o_a.reshape(tokens, D_MODEL), o_b.reshape(tokens, D_MODEL), g_a, g_b,
        w_out.astype(BF16), g_ffn[None], w_r2, b_r, w_ei.astype(BF16), w_eo.astype(BF16), p2d,
        g_ple[None], w_pg.astype(BF16), w_pp.astype(BF16), g_final[None], tm=512, chunk=160)


def kernel(x, p, g_mix, w_in, swa_sinks, mla_g_q, mla_w_uq, mla_g_kv, mla_w_ukv, w_out, g_ffn,
           w_router_group, b_router_group, w_router_expert, b_router_expert, w_expert_in,
           w_expert_out, g_ple, w_ple_gate, w_ple_proj, g_final):
    batch, seq, d = x.shape
    depth = p.shape[0]
    assert d == D_MODEL and depth == 1, "final RMSNorm is fused into the single layer"
    rope = _rope_tables(seq)
    out = _layer(
        x.reshape(batch * seq, d), p[0].reshape(batch * seq, PLE_DIM), g_mix[0], w_in[0],
        swa_sinks[0], mla_g_q[0], mla_w_uq[0], mla_g_kv[0], mla_w_ukv[0], w_out[0], g_ffn[0],
        w_router_group[0], b_router_group[0], w_router_expert[0], b_router_expert[0],
        w_expert_in[0], w_expert_out[0], g_ple[0], w_ple_gate[0], w_ple_proj[0], g_final,
        rope, batch, seq)
    return out.reshape(batch, seq, d)
```

```python
import functools
import math

import jax
import jax.numpy as jnp
from jax import lax
from jax.experimental import pallas as pl
from jax.experimental.pallas import tpu as pltpu

D_MODEL = 1024
PLE_DIM = 256
ROPE_THETA = 10000.0
EPS = 1e-6
NEG_INF = -1e30

SWA_HEAD_DIM = 64
SWA_HEADS = D_MODEL // SWA_HEAD_DIM
SWA_KV_HEADS = SWA_HEADS // 8
SWA_GROUP = SWA_HEADS // SWA_KV_HEADS
SWA_WINDOW = 128

MLA_NOPE_DIM = 128
MLA_ROPE_DIM = 64
MLA_V_DIM = 128
MLA_HEADS = D_MODEL // MLA_V_DIM
MLA_Q_RANK = 256
MLA_KV_RANK = 128
MLA_QK_DIM = MLA_NOPE_DIM + MLA_ROPE_DIM

N_GROUPS = 4
EXPERTS_PER_GROUP = 4
N_EXPERTS = N_GROUPS * EXPERTS_PER_GROUP
D_EXPERT = 256

LANES = 128
LOG2_E = 1.4426950408889634
VMEM_LIMIT = 56 * 1024 * 1024

MLA_KV_CHUNK = 512
SWA_TQ = 512
SWA_HEADS_PER_UNIT = 4
MLA_TQ = 1024
MLA_TK = 512
TAIL_TM = 512
MOE_CHUNK = 160

_C_QA = 0
_C_KV = _C_QA + SWA_HEADS * SWA_HEAD_DIM
_C_CQ = _C_KV + 2 * SWA_KV_HEADS * SWA_HEAD_DIM
_C_CKVR = _C_CQ + MLA_Q_RANK
_C_GATES = _C_CKVR + MLA_KV_RANK + MLA_ROPE_DIM

BF16 = jnp.bfloat16
F32 = jnp.float32


def _dot(a, b):
    return jnp.dot(a, b, preferred_element_type=F32)


def _dot_nt(a, b):
    return lax.dot_general(a, b, (((1,), (1,)), ((), ())), preferred_element_type=F32)


def _rms(x, g):
    r = lax.rsqrt(jnp.mean(x * x, axis=-1, keepdims=True) + EPS)
    return x * r * g


def _sigmoid(x):
    return 1.0 / (1.0 + jnp.exp(-x))


def _rope_lanes(x, cos, sin_signed):
    lane = lax.broadcasted_iota(jnp.int32, x.shape, 1)
    upper = (lane & (SWA_HEAD_DIM // 2)) != 0
    partner = jnp.where(upper, pltpu.roll(x, SWA_HEAD_DIM // 2, 1),
                        pltpu.roll(x, LANES - SWA_HEAD_DIM // 2, 1))
    return x * cos + partner * sin_signed


def _inproj_kernel(x_ref, g_ref, w_ref, wg_ref, cos_ref, sin_ref, cost_ref, sint_ref, gq_ref,
                   wuqt_ref, gkv_ref, wuk_ref, wuvt_ref, qa_ref, ka_ref, va_ref, ga_ref, gb_ref,
                   qm_ref, km_ref, vt_ref):
    xn = _rms(x_ref[...], g_ref[...]).astype(BF16)
    cos = cos_ref[...]
    sin = sin_ref[...]
    tm = xn.shape[0]

    cq = _rms(_dot(xn, w_ref[:, _C_CQ:_C_CKVR]), gq_ref[...])
    ckvr = _dot(xn, w_ref[:, _C_CKVR:_C_GATES])
    ckv = _rms(ckvr[:, :MLA_KV_RANK], gkv_ref[...])
    kr = jnp.concatenate(
        [ckvr[:, MLA_KV_RANK:], jnp.zeros((tm, LANES - MLA_ROPE_DIM), F32)], axis=1)
    kr = _rope_lanes(kr, cos, sin)[:, :MLA_ROPE_DIM].astype(BF16)
    qm_scale = LOG2_E / math.sqrt(MLA_QK_DIM)
    cq_t = cq.T.astype(BF16)
    ckv_t = ckv.T.astype(BF16)
    kn = _dot(ckv.astype(BF16), wuk_ref[...]).astype(BF16)
    vt_all = _dot(wuvt_ref[...].reshape(MLA_HEADS * MLA_V_DIM, MLA_KV_RANK), ckv_t)
    qt_all = _dot(wuqt_ref[...].reshape(MLA_HEADS * MLA_QK_DIM, MLA_Q_RANK), cq_t)

    qa = _dot(xn, w_ref[:, _C_QA:_C_KV])
    kv = _dot(xn, w_ref[:, _C_KV:_C_CQ])
    gate_a = _dot(xn, wg_ref[:, :D_MODEL])
    gate_b = _dot(xn, wg_ref[:, D_MODEL:])

    cos_h, sin_h = cost_ref[...], sint_ref[...]
    half = MLA_ROPE_DIM // 2
    for h in range(MLA_HEADS):
        km_ref[0, h, :, :MLA_NOPE_DIM] = kn[:, h * MLA_NOPE_DIM:(h + 1) * MLA_NOPE_DIM]
        km_ref[0, h, :, MLA_NOPE_DIM:] = kr
        vt_ref[0, h, 0] = vt_all[h * MLA_V_DIM:(h + 1) * MLA_V_DIM].astype(BF16)
        q_t = qt_all[h * MLA_QK_DIM:(h + 1) * MLA_QK_DIM]
        x1 = q_t[MLA_NOPE_DIM:MLA_NOPE_DIM + half]
        x2 = q_t[MLA_NOPE_DIM + half:]
        q_t = jnp.concatenate(
            [q_t[:MLA_NOPE_DIM], x1 * cos_h - x2 * sin_h, x2 * cos_h + x1 * sin_h], axis=0)
        qm_ref[0, h, 0] = (q_t * qm_scale).astype(BF16)

    qa_scale = LOG2_E / math.sqrt(SWA_HEAD_DIM)
    for j in range(SWA_HEADS // 2):
        q = (_rope_lanes(qa[:, j * LANES:(j + 1) * LANES], cos, sin) * qa_scale).astype(BF16)
        qa_ref[0, 2 * j] = q[:, :SWA_HEAD_DIM]
        qa_ref[0, 2 * j + 1] = q[:, SWA_HEAD_DIM:]
    k = _rope_lanes(kv[:, :LANES], cos, sin).astype(BF16)
    ka_ref[0, 0] = k[:, :SWA_HEAD_DIM]
    ka_ref[0, 1] = k[:, SWA_HEAD_DIM:]
    vt = kv[:, LANES:].T.astype(BF16)
    va_ref[0, 0] = vt[:SWA_HEAD_DIM]
    va_ref[0, 1] = vt[SWA_HEAD_DIM:]

    ga_ref[...] = _sigmoid(gate_a).astype(BF16)
    gb_ref[...] = _sigmoid(gate_b).astype(BF16)


def _inproj(x2, g_mix, w_main, w_gates, cos_t, sin_t, cos_h, sin_h, g_q, w_uqt, g_kv, w_uk, w_uvt,
            batch, seq, tm):
    tokens = batch * seq
    nt = seq // tm
    const = lambda t: (0, 0)
    tok = lambda t: (t, 0)
    head = lambda t: (t // nt, 0, t % nt, 0)
    out_shape = (
        jax.ShapeDtypeStruct((batch, SWA_HEADS, seq, SWA_HEAD_DIM), BF16),
        jax.ShapeDtypeStruct((batch, SWA_KV_HEADS, seq, SWA_HEAD_DIM), BF16),
        jax.ShapeDtypeStruct((batch, SWA_KV_HEADS, SWA_HEAD_DIM, seq), BF16),
        jax.ShapeDtypeStruct((tokens, D_MODEL), BF16),
        jax.ShapeDtypeStruct((tokens, D_MODEL), BF16),
        jax.ShapeDtypeStruct((batch, MLA_HEADS, nt, MLA_QK_DIM, tm), BF16),
        jax.ShapeDtypeStruct((batch, MLA_HEADS, seq, MLA_QK_DIM), BF16),
        jax.ShapeDtypeStruct((batch, MLA_HEADS, nt, MLA_V_DIM, tm), BF16),
    )
    return pl.pallas_call(
        _inproj_kernel,
        grid=(tokens // tm,),
        in_specs=[
            pl.BlockSpec((tm, D_MODEL), tok),
            pl.BlockSpec((1, D_MODEL), const),
            pl.BlockSpec(w_main.shape, const, pipeline_mode=pl.Buffered(1)),
            pl.BlockSpec(w_gates.shape, const, pipeline_mode=pl.Buffered(1)),
            pl.BlockSpec((tm, LANES), lambda t: (t % nt, 0)),
            pl.BlockSpec((tm, LANES), lambda t: (t % nt, 0)),
            pl.BlockSpec((MLA_ROPE_DIM // 2, tm), lambda t: (0, t % nt)),
            pl.BlockSpec((MLA_ROPE_DIM // 2, tm), lambda t: (0, t % nt)),
            pl.BlockSpec((1, MLA_Q_RANK), const),
            pl.BlockSpec(w_uqt.shape, lambda t: (0, 0, 0), pipeline_mode=pl.Buffered(1)),
            pl.BlockSpec((1, MLA_KV_RANK), const),
            pl.BlockSpec(w_uk.shape, const, pipeline_mode=pl.Buffered(1)),
            pl.BlockSpec(w_uvt.shape, lambda t: (0, 0, 0), pipeline_mode=pl.Buffered(1)),
        ],
        out_specs=(
            pl.BlockSpec((1, SWA_HEADS, tm, SWA_HEAD_DIM), head),
            pl.BlockSpec((1, SWA_KV_HEADS, tm, SWA_HEAD_DIM), head),
            pl.BlockSpec((1, SWA_KV_HEADS, SWA_HEAD_DIM, tm), lambda t: (t // nt, 0, 0, t % nt)),
            pl.BlockSpec((tm, D_MODEL), tok),
            pl.BlockSpec((tm, D_MODEL), tok),
            pl.BlockSpec((1, MLA_HEADS, 1, MLA_QK_DIM, tm), lambda t: (t // nt, 0, t % nt, 0, 0)),
            pl.BlockSpec((1, MLA_HEADS, tm, MLA_QK_DIM), head),
            pl.BlockSpec((1, MLA_HEADS, 1, MLA_V_DIM, tm), lambda t: (t // nt, 0, t % nt, 0, 0)),
        ),
        out_shape=out_shape,
        compiler_params=pltpu.CompilerParams(
            dimension_semantics=("arbitrary",), vmem_limit_bytes=VMEM_LIMIT),
        name="inproj",
    )(x2, g_mix, w_main, w_gates, cos_t, sin_t, cos_h, sin_h, g_q, w_uqt, g_kv, w_uk, w_uvt)


def _swa_kernel(q_ref, k_ref, kp_ref, vt_ref, vtp_ref, sink_ref, o_ref, s_scr, bias_scr, *,
                n_sub):
    blk = SWA_WINDOW
    cols = s_scr.shape[2]
    hpu = cols // blk

    @pl.when((pl.program_id(0) == 0) & (pl.program_id(1) == 0))
    def _():
        kj = lax.broadcasted_iota(jnp.int32, (2 * blk, cols), 0)
        qi = lax.broadcasted_iota(jnp.int32, (2 * blk, cols), 1) & (blk - 1)
        dist = blk + qi - kj
        bias_scr[...] = jnp.where((dist >= 0) & (dist < SWA_WINDOW), 0.0, NEG_INF)

    key_row = lax.broadcasted_iota(jnp.int32, (2 * blk, 1), 0)
    no_prev = jnp.where((key_row < blk) & (pl.program_id(1) == 0), NEG_INF, 0.0)
    units = [(g, h0, n) for g in range(SWA_KV_HEADS)
             for h0 in range(g * SWA_GROUP, (g + 1) * SWA_GROUP, hpu) for n in range(n_sub)]

    def scores(u, buf):
        g, h0, n = u
        q = q_ref[0, h0:h0 + hpu, n * blk:(n + 1) * blk, :]
        k_prev = kp_ref[0, g] if n == 0 else k_ref[0, g, (n - 1) * blk:n * blk, :]
        kk = jnp.concatenate([k_prev, k_ref[0, g, n * blk:(n + 1) * blk, :]], axis=0)
        s_scr[buf] = _dot_nt(kk, q.reshape(cols, SWA_HEAD_DIM))

    def softmax_pv(u, buf):
        g, h0, n = u
        sink = sink_ref[:, h0 * blk:(h0 + hpu) * blk] * LOG2_E
        s = s_scr[buf] + bias_scr[...]
        if n == 0:
            s = s + no_prev
        m = jnp.maximum(jnp.max(s, axis=0, keepdims=True), sink)
        e = jnp.exp2(s - m)
        denom = jnp.sum(e, axis=0, keepdims=True) + jnp.exp2(sink - m)
        vt_prev = vtp_ref[0, g] if n == 0 else vt_ref[0, g, :, (n - 1) * blk:n * blk]
        vvt = jnp.concatenate([vt_prev, vt_ref[0, g, :, n * blk:(n + 1) * blk]], axis=1)
        o = (_dot(vvt, e.astype(BF16)) / denom).T
        for hh in range(hpu):
            h = h0 + hh
            o_ref[0, n * blk:(n + 1) * blk, h * SWA_HEAD_DIM:(h + 1) * SWA_HEAD_DIM] = (
                o[hh * blk:(hh + 1) * blk, :].astype(BF16))

    scores(units[0], 0)
    for idx, u in enumerate(units):
        if idx + 1 < len(units):
            scores(units[idx + 1], (idx + 1) % 2)
        softmax_pv(u, idx % 2)


def _swa(q_a, k_a, vt_a, sink_lanes, batch, seq, tq, heads_per_unit):
    assert SWA_GROUP % heads_per_unit == 0
    n_sub = tq // SWA_WINDOW
    cur = lambda b, i: (b, 0, i, 0)
    prev = lambda b, i: (b, 0, jnp.maximum(i * n_sub - 1, 0), 0)
    return pl.pallas_call(
        functools.partial(_swa_kernel, n_sub=n_sub),
        grid=(batch, seq // tq),
        in_specs=[
            pl.BlockSpec((1, SWA_HEADS, tq, SWA_HEAD_DIM), cur),
            pl.BlockSpec((1, SWA_KV_HEADS, tq, SWA_HEAD_DIM), cur),
            pl.BlockSpec((1, SWA_KV_HEADS, SWA_WINDOW, SWA_HEAD_DIM), prev),
            pl.BlockSpec((1, SWA_KV_HEADS, SWA_HEAD_DIM, tq), lambda b, i: (b, 0, 0, i)),
            pl.BlockSpec((1, SWA_KV_HEADS, SWA_HEAD_DIM, SWA_WINDOW),
                         lambda b, i: (b, 0, 0, jnp.maximum(i * n_sub - 1, 0))),
            pl.BlockSpec(sink_lanes.shape, lambda b, i: (0, 0)),
        ],
        out_specs=pl.BlockSpec((1, tq, D_MODEL), lambda b, i: (b, i, 0)),
        out_shape=jax.ShapeDtypeStruct((batch, seq, D_MODEL), BF16),
        scratch_shapes=[pltpu.VMEM((2, 2 * SWA_WINDOW, heads_per_unit * SWA_WINDOW), F32),
                        pltpu.VMEM((2 * SWA_WINDOW, heads_per_unit * SWA_WINDOW), F32)],
        compiler_params=pltpu.CompilerParams(
            dimension_semantics=("arbitrary", "arbitrary"), vmem_limit_bytes=VMEM_LIMIT),
        name="swa",
    )(q_a, k_a, k_a, vt_a, vt_a, sink_lanes)


def _mla_kernel(qt_ref, k_ref, vt_ref, o_ref, m_scr, l_scr, acc_scr, s_scr, *, tq, tk):
    ratio = tq // tk
    assert tq == ratio * tk and ratio % 2 == 0
    chunks = tk // MLA_KV_CHUNK
    q_chunks = tq // MLA_KV_CHUNK
    n_q = qt_ref.shape[2] // q_chunks

    def q_block(i, carry):
        m_scr[...] = jnp.full(m_scr.shape, NEG_INF, F32)
        l_scr[...] = jnp.zeros(l_scr.shape, F32)
        acc_scr[...] = jnp.zeros(acc_scr.shape, F32)

        def scores(j, buf, q0=0, blk=i):
            k = k_ref[0, 0, pl.ds(pl.multiple_of(j * tk, tk), tk), :]
            for c in range(q0 // MLA_KV_CHUNK, q_chunks):
                s_scr[buf, :, c * MLA_KV_CHUNK:(c + 1) * MLA_KV_CHUNK] = _dot(
                    k, qt_ref[0, 0, blk * q_chunks + c])

        def softmax_pv(j, buf, q0=0, key_offset=None):
            s = s_scr[buf, :, q0:]
            if key_offset is not None:
                kpos = lax.broadcasted_iota(jnp.int32, s.shape, 0) + key_offset
                qpos = lax.broadcasted_iota(jnp.int32, s.shape, 1) + q0
                s = jnp.where(kpos <= qpos, s, NEG_INF)
            m_old = m_scr[:, q0:]
            m_new = jnp.maximum(m_old, jnp.max(s, axis=0, keepdims=True))
            alpha = jnp.exp2(m_old - m_new)
            p = jnp.exp2(s - m_new)
            l_scr[:, q0:] = alpha * l_scr[:, q0:] + jnp.sum(p, axis=0, keepdims=True)
            pb = p.astype(BF16)
            pv = _dot(vt_ref[0, 0, j * chunks], pb[:MLA_KV_CHUNK])
            for c in range(1, chunks):
                pv += _dot(vt_ref[0, 0, j * chunks + c],
                           pb[c * MLA_KV_CHUNK:(c + 1) * MLA_KV_CHUNK])
            acc_scr[:, q0:] = alpha * acc_scr[:, q0:] + pv
            m_scr[:, q0:] = m_new

        @pl.when(i == 0)
        def _():
            scores(0, 0)

        def pair(jj, c):
            t = 2 * jj
            scores(t + 1, 1)
            softmax_pv(t, 0)
            scores(t + 2, 0)
            softmax_pv(t + 1, 1)
            return c

        lax.fori_loop(0, (ratio // 2) * i, pair, 0)
        for d in range(ratio):
            if d + 1 < ratio:
                scores(ratio * i + d + 1, (d + 1) % 2, q0=(d + 1) * tk)
            else:
                scores(0, 0, blk=jnp.minimum(i + 1, n_q - 1))
            softmax_pv(ratio * i + d, d % 2, q0=d * tk, key_offset=d * tk)

        o = acc_scr[...] / l_scr[...]
        o_ref[0, pl.ds(pl.multiple_of(i * tq, tq), tq), :] = o.T.astype(o_ref.dtype)
        return carry

    lax.fori_loop(0, n_q, q_block, 0)


def _mla(q_m, k_m, vt_m, batch, seq, tq, tk):
    return pl.pallas_call(
        functools.partial(_mla_kernel, tq=tq, tk=tk),
        grid=(batch, MLA_HEADS),
        in_specs=[
            pl.BlockSpec((1, 1, seq // MLA_KV_CHUNK, MLA_QK_DIM, MLA_KV_CHUNK),
                         lambda b, h: (b, h, 0, 0, 0)),
            pl.BlockSpec((1, 1, seq, MLA_QK_DIM), lambda b, h: (b, h, 0, 0)),
            pl.BlockSpec((1, 1, seq // MLA_KV_CHUNK, MLA_V_DIM, MLA_KV_CHUNK),
                         lambda b, h: (b, h, 0, 0, 0)),
        ],
        out_specs=pl.BlockSpec((1, seq, MLA_V_DIM), lambda b, h: (b, 0, h)),
        out_shape=jax.ShapeDtypeStruct((batch, seq, D_MODEL), BF16),
        scratch_shapes=[
            pltpu.VMEM((1, tq), F32),
            pltpu.VMEM((1, tq), F32),
            pltpu.VMEM((MLA_V_DIM, tq), F32),
            pltpu.VMEM((2, tk, tq), F32),
        ],
        compiler_params=pltpu.CompilerParams(
            dimension_semantics=("arbitrary", "arbitrary"), vmem_limit_bytes=VMEM_LIMIT),
        name="mla",
    )(q_m, k_m, vt_m)


def _mix_and_route(x_ref, oa_ref, ob_ref, ga_ref, gb_ref, wo_ref, gf_ref, wr_ref, br_ref):
    merged = (ga_ref[...].astype(F32) * oa_ref[...].astype(F32)
              + gb_ref[...].astype(F32) * ob_ref[...].astype(F32)).astype(BF16)
    h = x_ref[...] + _dot(merged, wo_ref[...])
    xn = _rms(h, gf_ref[...])

    tm = xn.shape[0]
    x_hi = xn.astype(BF16)
    x_lo = (xn - x_hi.astype(F32)).astype(BF16)
    prod = _dot(jnp.concatenate([x_hi, x_lo], axis=0), wr_ref[...])
    lg = (prod[:tm, :LANES] + prod[:tm, LANES:] + prod[tm:, :LANES] + prod[tm:, LANES:]
          + br_ref[...])
    lg_t = lg.T
    big = jnp.int32(LANES)
    g_row = lax.broadcasted_iota(jnp.int32, (8, tm), 0)
    gl = jnp.where(g_row < N_GROUPS, lg_t[N_EXPERTS:N_EXPERTS + 8], NEG_INF)
    g_max = jnp.max(gl, axis=0, keepdims=True)
    g_w = 1.0 / jnp.sum(jnp.exp(gl - g_max), axis=0, keepdims=True)
    gidx = jnp.min(jnp.where(gl == g_max, g_row, big), axis=0, keepdims=True)

    row = lax.broadcasted_iota(jnp.int32, (N_EXPERTS, tm), 0)
    sel = (row >= gidx * EXPERTS_PER_GROUP) & (row < (gidx + 1) * EXPERTS_PER_GROUP)
    el = jnp.where(sel, lg_t[:N_EXPERTS], NEG_INF)
    e_exp = jnp.exp(el - jnp.max(el, axis=0, keepdims=True))
    e_prob = e_exp / jnp.sum(e_exp, axis=0, keepdims=True)
    p1 = jnp.max(e_prob, axis=0, keepdims=True)
    i1 = jnp.min(jnp.where(sel & (e_prob == p1), row, big), axis=0, keepdims=True)
    rest = sel & (row != i1)
    p2 = jnp.max(jnp.where(rest, e_prob, -1.0), axis=0, keepdims=True)
    i2 = jnp.min(jnp.where(rest & (e_prob == p2), row, big), axis=0, keepdims=True)
    top_sum = p1 + p2
    comb = jnp.where(row == i1, p1 / top_sum, jnp.where(row == i2, p2 / top_sum, 0.0)) * g_w
    gid_rows = jnp.where(g_row == 0, gidx.astype(F32), 0.0)
    comb_t = jnp.concatenate(
        [comb, gid_rows, jnp.zeros((LANES - N_EXPERTS - 8, tm), F32)], axis=0)
    return h, x_hi, comb_t


def _block_tail_kernel(x_ref, oa_ref, ob_ref, ga_ref, gb_ref, wout_ref, gf_ref, wr_ref, br_ref,
                       wi_ref, wo_ref, p_ref, gp_ref, wg_ref, wp_ref, gfin_ref, o_ref, *, chunk):
    tm = x_ref.shape[0]
    h, xn, comb_t = _mix_and_route(
        x_ref, oa_ref, ob_ref, ga_ref, gb_ref, wout_ref, gf_ref, wr_ref, br_ref)
    o_ref[...] = h
    comb = comb_t.T
    comb_hi = comb.astype(BF16)
    comb_lo = (comb - comb_hi.astype(F32)).astype(BF16)
    comb_hl = jnp.concatenate([comb_hi, comb_lo], axis=1)
    gid_col = comb[:, N_EXPERTS:N_EXPERTS + 1]
    gid_row = comb_t[N_EXPERTS:N_EXPERTS + 1, :]
    member_col = gid_col == lax.broadcasted_iota(jnp.int32, (tm, LANES), 1).astype(F32)
    member_row = gid_row == lax.broadcasted_iota(jnp.int32, (8, tm), 0).astype(F32)
    r_i = lax.broadcasted_iota(jnp.int32, (tm, tm), 0)
    c_i = lax.broadcasted_iota(jnp.int32, (tm, tm), 1)
    ranks_row = _dot(member_row.astype(BF16), (r_i < c_i).astype(BF16))
    ranks_col = _dot((r_i > c_i).astype(BF16), member_col.astype(BF16))
    pad = 2 * LANES
    never = -4.0 * tm
    slot_row = lax.broadcasted_iota(jnp.int32, (chunk, tm), 0).astype(F32)
    slot_col = lax.broadcasted_iota(jnp.int32, (tm, pad), 1).astype(F32)
    slot_col = jnp.where(slot_col < float(chunk), slot_col, never)

    groups = range(N_GROUPS)
    in_rows = [member_row[g:g + 1] for g in groups]
    rank_rows = [jnp.where(in_rows[g], ranks_row[g:g + 1], -2.0 * never) for g in groups]
    rank_cols = [jnp.where(member_col[:, g:g + 1], ranks_col[:, g:g + 1], -2.0 * never)
                 for g in groups]

    def one_hots(g, base):
        gather = jnp.where(rank_rows[g] - base == slot_row, 1.0, 0.0).astype(BF16)
        scatter = jnp.where(rank_cols[g] - base == slot_col, 1.0, 0.0).astype(BF16)
        return gather, scatter

    def compact(gather):
        xc = _dot(gather, xn).astype(BF16)
        wc = _dot(gather, comb_hl)
        return xc, wc[:, :LANES] + wc[:, LANES:]

    def experts_in(g, xc):
        return [_dot(xc, wi_ref[g * EXPERTS_PER_GROUP + j]) for j in range(EXPERTS_PER_GROUP)]

    def activate(g, hids, wc):
        acts = []
        for j, hid in enumerate(hids):
            e = g * EXPERTS_PER_GROUP + j
            gate = hid[:, :D_EXPERT]
            act = (gate * _sigmoid(gate)) * hid[:, D_EXPERT:] * wc[:, e:e + 1]
            acts.append(act.astype(BF16))
        return acts

    def experts_out(g, acts):
        yc = _dot(acts[0], wo_ref[g * EXPERTS_PER_GROUP])
        for j in range(1, EXPERTS_PER_GROUP):
            yc = yc + _dot(acts[j], wo_ref[g * EXPERTS_PER_GROUP + j])
        return jnp.concatenate(
            [yc.astype(BF16), jnp.zeros((pad - chunk, D_MODEL), BF16)], axis=0)

    hots = [one_hots(g, 0.0) for g in groups]
    compacted = [compact(hots[g][0]) for g in groups]
    hidden = [experts_in(g, compacted[g][0]) for g in groups]
    acts = [activate(g, hidden[g], compacted[g][1]) for g in groups]
    outs = [experts_out(g, acts[g]) for g in groups]
    o_ref[...] += _dot(jnp.concatenate([hots[g][1] for g in groups], axis=1),
                       jnp.concatenate(outs, axis=0))

    for g in groups:
        count = jnp.sum(in_rows[g].astype(jnp.int32))
        n_chunks = (count + (chunk - 1)) // chunk

        def chunk_body(c, carry, g=g):
            gather, scatter = one_hots(g, (c * chunk).astype(F32))
            xc, wc = compact(gather)
            yc = experts_out(g, activate(g, experts_in(g, xc), wc))
            o_ref[...] += _dot(scatter, yc)
            return carry

        lax.fori_loop(1, n_chunks, chunk_body, 0)

    h = o_ref[...]
    gate = _sigmoid(_dot(_rms(h, gp_ref[...]).astype(BF16), wg_ref[...]))
    h = h + gate * _dot(p_ref[...].astype(BF16), wp_ref[...])
    o_ref[...] = _rms(h, gfin_ref[...])


def _block_tail(x2, o_a, o_b, g_a, g_b, w_o, g_ffn, w_r, b_r, w_ei, w_eo, p2, g_ple, w_pg, w_pp,
                g_final, tm, chunk):
    tokens = x2.shape[0]
    tok = lambda t: (t, 0)
    const = lambda t: (0, 0)
    return pl.pallas_call(
        functools.partial(_block_tail_kernel, chunk=chunk),
        grid=(tokens // tm,),
        in_specs=[
            pl.BlockSpec((tm, D_MODEL), tok),
            pl.BlockSpec((tm, D_MODEL), tok),
            pl.BlockSpec((tm, D_MODEL), tok),
            pl.BlockSpec((tm, D_MODEL), tok),
            pl.BlockSpec((tm, D_MODEL), tok),
            pl.BlockSpec((D_MODEL, D_MODEL), const, pipeline_mode=pl.Buffered(1)),
            pl.BlockSpec((1, D_MODEL), const),
            pl.BlockSpec((D_MODEL, 2 * LANES), const, pipeline_mode=pl.Buffered(1)),
            pl.BlockSpec((1, LANES), const),
            pl.BlockSpec(w_ei.shape, lambda t: (0, 0, 0), pipeline_mode=pl.Buffered(1)),
            pl.BlockSpec(w_eo.shape, lambda t: (0, 0, 0), pipeline_mode=pl.Buffered(1)),
            pl.BlockSpec((tm, PLE_DIM), tok),
            pl.BlockSpec((1, D_MODEL), const),
            pl.BlockSpec((D_MODEL, D_MODEL), const, pipeline_mode=pl.Buffered(1)),
            pl.BlockSpec((PLE_DIM, D_MODEL), const, pipeline_mode=pl.Buffered(1)),
            pl.BlockSpec((1, D_MODEL), const),
        ],
        out_specs=pl.BlockSpec((tm, D_MODEL), tok),
        out_shape=jax.ShapeDtypeStruct((tokens, D_MODEL), F32),
        compiler_params=pltpu.CompilerParams(
            dimension_semantics=("arbitrary",), vmem_limit_bytes=VMEM_LIMIT),
        name="block_tail",
    )(x2, o_a, o_b, g_a, g_b, w_o, g_ffn, w_r, b_r, w_ei, w_eo, p2, g_ple, w_pg, w_pp, g_final)


def _rope_tables(seq):
    half = SWA_HEAD_DIM // 2
    pos = jnp.arange(seq, dtype=F32)
    inv = ROPE_THETA ** (-jnp.arange(0, SWA_HEAD_DIM, 2, dtype=F32) / SWA_HEAD_DIM)
    ang = pos[:, None] * inv[None, :]
    cos, sin = jnp.cos(ang), jnp.sin(ang)
    reps = LANES // SWA_HEAD_DIM
    cos_t = jnp.broadcast_to(cos[:, None, :], (seq, 2 * reps, half)).reshape(seq, LANES)
    sin_t = jnp.broadcast_to(
        jnp.stack([-sin, sin], axis=1)[:, None], (seq, reps, 2, half)).reshape(seq, LANES)
    return cos_t, sin_t, cos.T, sin.T


def _layer(h2d, p2d, g_mix, w_in, sinks, g_q, w_uq, g_kv, w_ukv, w_out, g_ffn, w_rg, b_rg,
           w_re, b_re, w_ei, w_eo, g_ple, w_pg, w_pp, g_final, rope, batch, seq):
    cos_t, sin_t, cos_h, sin_h = rope
    assert SWA_HEAD_DIM == MLA_ROPE_DIM, "one rotary table serves both mixers"
    w_main = w_in[:, :_C_GATES].astype(BF16)
    w_gates = w_in[:, _C_GATES:].astype(BF16)
    assert w_gates.shape[1] == 2 * D_MODEL
    w_uqt = jnp.transpose(
        w_uq.reshape(MLA_Q_RANK, MLA_HEADS, MLA_QK_DIM), (1, 2, 0)).astype(BF16)
    ukv = w_ukv.reshape(MLA_KV_RANK, MLA_HEADS, MLA_NOPE_DIM + MLA_V_DIM)
    w_uk_c = ukv[:, :, :MLA_NOPE_DIM].reshape(MLA_KV_RANK, -1).astype(BF16)
    w_uvt = jnp.transpose(ukv[:, :, MLA_NOPE_DIM:], (1, 2, 0)).astype(BF16)

    q_a, k_a, v_a, g_a, g_b, q_m, k_m, v_m = _inproj(
        h2d, g_mix[None], w_main, w_gates, cos_t, sin_t, cos_h, sin_h, g_q[None], w_uqt,
        g_kv[None], w_uk_c, w_uvt, batch, seq, tm=MLA_KV_CHUNK)

    sink_lanes = jnp.repeat(sinks.astype(F32), SWA_WINDOW)[None]
    o_a = _swa(q_a, k_a, v_a, sink_lanes, batch, seq, tq=SWA_TQ,
               heads_per_unit=SWA_HEADS_PER_UNIT)
    o_b = _mla(q_m, k_m, v_m, batch, seq, tq=MLA_TQ, tk=MLA_TK)

    w_r = jnp.concatenate(
        [w_re, w_rg, jnp.zeros((D_MODEL, LANES - N_EXPERTS - N_GROUPS), F32)], axis=1)
    w_r_hi = w_r.astype(BF16)
    w_r_lo = (w_r - w_r_hi.astype(F32)).astype(BF16)
    w_r2 = jnp.concatenate([w_r_hi, w_r_lo], axis=1)
    b_r = jnp.concatenate(
        [b_re, b_rg, jnp.zeros((LANES - N_EXPERTS - N_GROUPS,), F32)])[None]
    tokens = batch * seq
    return _block_tail(
        h2d, o_a.reshape(tokens, D_MODEL), o_b.reshape(tokens, D_MODEL), g_a, g_b,
        w_out.astype(BF16), g_ffn[None], w_r2, b_r, w_ei.astype(BF16), w_eo.astype(BF16), p2d,
        g_ple[None], w_pg.astype(BF16), w_pp.astype(BF16), g_final[None], tm=TAIL_TM,
        chunk=MOE_CHUNK)


def kernel(x, p, g_mix, w_in, swa_sinks, mla_g_q, mla_w_uq, mla_g_kv, mla_w_ukv, w_out, g_ffn,
           w_router_group, b_router_group, w_router_expert, b_router_expert, w_expert_in,
           w_expert_out, g_ple, w_ple_gate, w_ple_proj, g_final):
    batch, seq, d = x.shape
    depth = p.shape[0]
    assert d == D_MODEL and depth == 1, "final RMSNorm is fused into the single layer"
    rope = _rope_tables(seq)
    out = _layer(
        x.reshape(batch * seq, d), p[0].reshape(batch * seq, PLE_DIM), g_mix[0], w_in[0],
        swa_sinks[0], mla_g_q[0], mla_w_uq[0], mla_g_kv[0], mla_w_ukv[0], w_out[0], g_ffn[0],
        w_router_group[0], b_router_group[0], w_router_expert[0], b_router_expert[0],
        w_expert_in[0], w_expert_out[0], g_ple[0], w_ple_gate[0], w_ple_proj[0], g_final,
        rope, batch, seq)
    return out.reshape(batch, seq, d)
```

```python
import functools
import math

import jax
import jax.numpy as jnp
from jax import lax
from jax.experimental import pallas as pl
from jax.experimental.pallas import tpu as pltpu

D_MODEL = 1024
PLE_DIM = 256
ROPE_THETA = 10000.0
EPS = 1e-6
NEG_INF = -1e30

SWA_HEAD_DIM = 64
SWA_HEADS = D_MODEL // SWA_HEAD_DIM
SWA_KV_HEADS = SWA_HEADS // 8
SWA_GROUP = SWA_HEADS // SWA_KV_HEADS
SWA_WINDOW = 128

MLA_NOPE_DIM = 128
MLA_ROPE_DIM = 64
MLA_V_DIM = 128
MLA_HEADS = D_MODEL // MLA_V_DIM
MLA_Q_RANK = 256
MLA_KV_RANK = 128
MLA_QK_DIM = MLA_NOPE_DIM + MLA_ROPE_DIM

N_GROUPS = 4
EXPERTS_PER_GROUP = 4
N_EXPERTS = N_GROUPS * EXPERTS_PER_GROUP
D_EXPERT = 256

LANES = 128
LOG2_E = 1.4426950408889634
VMEM_LIMIT = 56 * 1024 * 1024

MLA_KV_CHUNK = 512
SWA_TQ = 512
SWA_HEADS_PER_UNIT = 4
MLA_TQ = 1024
MLA_TK = 512
TAIL_TM = 512
MOE_CHUNK = 160

_C_QA = 0
_C_KV = _C_QA + SWA_HEADS * SWA_HEAD_DIM
_C_CQ = _C_KV + 2 * SWA_KV_HEADS * SWA_HEAD_DIM
_C_CKVR = _C_CQ + MLA_Q_RANK
_C_GATES = _C_CKVR + MLA_KV_RANK + MLA_ROPE_DIM

BF16 = jnp.bfloat16
F32 = jnp.float32


def _dot(a, b):
    return jnp.dot(a, b, preferred_element_type=F32)


def _dot_nt(a, b):
    return lax.dot_general(a, b, (((1,), (1,)), ((), ())), preferred_element_type=F32)


def _rms(x, g):
    r = lax.rsqrt(jnp.mean(x * x, axis=-1, keepdims=True) + EPS)
    return x * r * g


def _sigmoid(x):
    return 1.0 / (1.0 + jnp.exp(-x))


def _rope_lanes(x, cos, sin_signed):
    lane = lax.broadcasted_iota(jnp.int32, x.shape, 1)
    upper = (lane & (SWA_HEAD_DIM // 2)) != 0
    partner = jnp.where(upper, pltpu.roll(x, SWA_HEAD_DIM // 2, 1),
                        pltpu.roll(x, LANES - SWA_HEAD_DIM // 2, 1))
    return x * cos + partner * sin_signed


def _inproj_kernel(x_ref, g_ref, w_ref, wg_ref, cos_ref, sin_ref, cost_ref, sint_ref, gq_ref,
                   wuqt_ref, gkv_ref, wuk_ref, wuvt_ref, qa_ref, ka_ref, va_ref, ga_ref, gb_ref,
                   qm_ref, km_ref, vt_ref):
    xn = _rms(x_ref[...], g_ref[...]).astype(BF16)
    cos = cos_ref[...]
    sin = sin_ref[...]
    tm = xn.shape[0]

    cq = _rms(_dot(xn, w_ref[:, _C_CQ:_C_CKVR]), gq_ref[...])
    ckvr = _dot(xn, w_ref[:, _C_CKVR:_C_GATES])
    ckv = _rms(ckvr[:, :MLA_KV_RANK], gkv_ref[...])
    kr = jnp.concatenate(
        [ckvr[:, MLA_KV_RANK:], jnp.zeros((tm, LANES - MLA_ROPE_DIM), F32)], axis=1)
    kr = _rope_lanes(kr, cos, sin)[:, :MLA_ROPE_DIM].astype(BF16)
    qm_scale = LOG2_E / math.sqrt(MLA_QK_DIM)
    cq_t = cq.T.astype(BF16)
    ckv_t = ckv.T.astype(BF16)
    kn = _dot(ckv.astype(BF16), wuk_ref[...]).astype(BF16)
    vt_all = _dot(wuvt_ref[...].reshape(MLA_HEADS * MLA_V_DIM, MLA_KV_RANK), ckv_t)
    qt_all = _dot(wuqt_ref[...].reshape(MLA_HEADS * MLA_QK_DIM, MLA_Q_RANK), cq_t)

    qa = _dot(xn, w_ref[:, _C_QA:_C_KV])
    kv = _dot(xn, w_ref[:, _C_KV:_C_CQ])
    gate_a = _dot(xn, wg_ref[:, :D_MODEL])
    gate_b = _dot(xn, wg_ref[:, D_MODEL:])

    cos_h, sin_h = cost_ref[...], sint_ref[...]
    half = MLA_ROPE_DIM // 2
    for h in range(MLA_HEADS):
        km_ref[0, h, :, :MLA_NOPE_DIM] = kn[:, h * MLA_NOPE_DIM:(h + 1) * MLA_NOPE_DIM]
        km_ref[0, h, :, MLA_NOPE_DIM:] = kr
        vt_ref[0, h, 0] = vt_all[h * MLA_V_DIM:(h + 1) * MLA_V_DIM].astype(BF16)
        q_t = qt_all[h * MLA_QK_DIM:(h + 1) * MLA_QK_DIM]
        x1 = q_t[MLA_NOPE_DIM:MLA_NOPE_DIM + half]
        x2 = q_t[MLA_NOPE_DIM + half:]
        q_t = jnp.concatenate(
            [q_t[:MLA_NOPE_DIM], x1 * cos_h - x2 * sin_h, x2 * cos_h + x1 * sin_h], axis=0)
        qm_ref[0, h, 0] = (q_t * qm_scale).astype(BF16)

    qa_scale = LOG2_E / math.sqrt(SWA_HEAD_DIM)
    for j in range(SWA_HEADS // 2):
        q = (_rope_lanes(qa[:, j * LANES:(j + 1) * LANES], cos, sin) * qa_scale).astype(BF16)
        qa_ref[0, 2 * j] = q[:, :SWA_HEAD_DIM]
        qa_ref[0, 2 * j + 1] = q[:, SWA_HEAD_DIM:]
    k = _rope_lanes(kv[:, :LANES], cos, sin).astype(BF16)
    ka_ref[0, 0] = k[:, :SWA_HEAD_DIM]
    ka_ref[0, 1] = k[:, SWA_HEAD_DIM:]
    vt = kv[:, LANES:].T.astype(BF16)
    va_ref[0, 0] = vt[:SWA_HEAD_DIM]
    va_ref[0, 1] = vt[SWA_HEAD_DIM:]

    ga_ref[...] = _sigmoid(gate_a).astype(BF16)
    gb_ref[...] = _sigmoid(gate_b).astype(BF16)


def _inproj(x2, g_mix, w_main, w_gates, cos_t, sin_t, cos_h, sin_h, g_q, w_uqt, g_kv, w_uk, w_uvt,
            batch, seq, tm):
    tokens = batch * seq
    nt = seq // tm
    const = lambda t: (0, 0)
    tok = lambda t: (t, 0)
    head = lambda t: (t // nt, 0, t % nt, 0)
    out_shape = (
        jax.ShapeDtypeStruct((batch, SWA_HEADS, seq, SWA_HEAD_DIM), BF16),
        jax.ShapeDtypeStruct((batch, SWA_KV_HEADS, seq, SWA_HEAD_DIM), BF16),
        jax.ShapeDtypeStruct((batch, SWA_KV_HEADS, SWA_HEAD_DIM, seq), BF16),
        jax.ShapeDtypeStruct((tokens, D_MODEL), BF16),
        jax.ShapeDtypeStruct((tokens, D_MODEL), BF16),
        jax.ShapeDtypeStruct((batch, MLA_HEADS, nt, MLA_QK_DIM, tm), BF16),
        jax.ShapeDtypeStruct((batch, MLA_HEADS, seq, MLA_QK_DIM), BF16),
        jax.ShapeDtypeStruct((batch, MLA_HEADS, nt, MLA_V_DIM, tm), BF16),
    )
    return pl.pallas_call(
        _inproj_kernel,
        grid=(tokens // tm,),
        in_specs=[
            pl.BlockSpec((tm, D_MODEL), tok),
            pl.BlockSpec((1, D_MODEL), const),
            pl.BlockSpec(w_main.shape, const, pipeline_mode=pl.Buffered(1)),
            pl.BlockSpec(w_gates.shape, const, pipeline_mode=pl.Buffered(1)),
            pl.BlockSpec((tm, LANES), lambda t: (t % nt, 0)),
            pl.BlockSpec((tm, LANES), lambda t: (t % nt, 0)),
            pl.BlockSpec((MLA_ROPE_DIM // 2, tm), lambda t: (0, t % nt)),
            pl.BlockSpec((MLA_ROPE_DIM // 2, tm), lambda t: (0, t % nt)),
            pl.BlockSpec((1, MLA_Q_RANK), const),
            pl.BlockSpec(w_uqt.shape, lambda t: (0, 0, 0), pipeline_mode=pl.Buffered(1)),
            pl.BlockSpec((1, MLA_KV_RANK), const),
            pl.BlockSpec(w_uk.shape, const, pipeline_mode=pl.Buffered(1)),
            pl.BlockSpec(w_uvt.shape, lambda t: (0, 0, 0), pipeline_mode=pl.Buffered(1)),
        ],
        out_specs=(
            pl.BlockSpec((1, SWA_HEADS, tm, SWA_HEAD_DIM), head),
            pl.BlockSpec((1, SWA_KV_HEADS, tm, SWA_HEAD_DIM), head),
            pl.BlockSpec((1, SWA_KV_HEADS, SWA_HEAD_DIM, tm), lambda t: (t // nt, 0, 0, t % nt)),
            pl.BlockSpec((tm, D_MODEL), tok),
            pl.BlockSpec((tm, D_MODEL), tok),
            pl.BlockSpec((1, MLA_HEADS, 1, MLA_QK_DIM, tm), lambda t: (t // nt, 0, t % nt, 0, 0)),
            pl.BlockSpec((1, MLA_HEADS, tm, MLA_QK_DIM), head),
            pl.BlockSpec((1, MLA_HEADS, 1, MLA_V_DIM, tm), lambda t: (t // nt, 0, t % nt, 0, 0)),
        ),
        out_shape=out_shape,
        compiler_params=pltpu.CompilerParams(
            dimension_semantics=("arbitrary",), vmem_limit_bytes=VMEM_LIMIT),
        name="inproj",
    )(x2, g_mix, w_main, w_gates, cos_t, sin_t, cos_h, sin_h, g_q, w_uqt, g_kv, w_uk, w_uvt)


def _swa_kernel(q_ref, k_ref, kp_ref, vt_ref, vtp_ref, sink_ref, o_ref, s_scr, bias_scr, *,
                n_sub):
    blk = SWA_WINDOW
    cols = s_scr.shape[2]
    hpu = cols // blk

    @pl.when((pl.program_id(0) == 0) & (pl.program_id(1) == 0))
    def _():
        kj = lax.broadcasted_iota(jnp.int32, (2 * blk, cols), 0)
        qi = lax.broadcasted_iota(jnp.int32, (2 * blk, cols), 1) & (blk - 1)
        dist = blk + qi - kj
        bias_scr[...] = jnp.where((dist >= 0) & (dist < SWA_WINDOW), 0.0, NEG_INF)

    key_row = lax.broadcasted_iota(jnp.int32, (2 * blk, 1), 0)
    no_prev = jnp.where((key_row < blk) & (pl.program_id(1) == 0), NEG_INF, 0.0)
    units = [(g, h0, n) for g in range(SWA_KV_HEADS)
             for h0 in range(g * SWA_GROUP, (g + 1) * SWA_GROUP, hpu) for n in range(n_sub)]

    def scores(u, buf):
        g, h0, n = u
        q = q_ref[0, h0:h0 + hpu, n * blk:(n + 1) * blk, :]
        k_prev = kp_ref[0, g] if n == 0 else k_ref[0, g, (n - 1) * blk:n * blk, :]
        kk = jnp.concatenate([k_prev, k_ref[0, g, n * blk:(n + 1) * blk, :]], axis=0)
        s_scr[buf] = _dot_nt(kk, q.reshape(cols, SWA_HEAD_DIM))

    def softmax_pv(u, buf):
        g, h0, n = u
        sink = sink_ref[:, h0 * blk:(h0 + hpu) * blk] * LOG2_E
        s = s_scr[buf] + bias_scr[...]
        if n == 0:
            s = s + no_prev
        m = jnp.maximum(jnp.max(s, axis=0, keepdims=True), sink)
        e = jnp.exp2(s - m)
        denom = jnp.sum(e, axis=0, keepdims=True) + jnp.exp2(sink - m)
        vt_prev = vtp_ref[0, g] if n == 0 else vt_ref[0, g, :, (n - 1) * blk:n * blk]
        vvt = jnp.concatenate([vt_prev, vt_ref[0, g, :, n * blk:(n + 1) * blk]], axis=1)
        o = (_dot(vvt, e.astype(BF16)) / denom).T
        for hh in range(hpu):
            h = h0 + hh
            o_ref[0, n * blk:(n + 1) * blk, h * SWA_HEAD_DIM:(h + 1) * SWA_HEAD_DIM] = (
                o[hh * blk:(hh + 1) * blk, :].astype(BF16))

    scores(units[0], 0)
    for idx, u in enumerate(units):
        if idx + 1 < len(units):
            scores(units[idx + 1], (idx + 1) % 2)
        softmax_pv(u, idx % 2)


def _swa(q_a, k_a, vt_a, sink_lanes, batch, seq, tq, heads_per_unit):
    assert SWA_GROUP % heads_per_unit == 0
    n_sub = tq // SWA_WINDOW
    cur = lambda b, i: (b, 0, i, 0)
    prev = lambda b, i: (b, 0, jnp.maximum(i * n_sub - 1, 0), 0)
    return pl.pallas_call(
        functools.partial(_swa_kernel, n_sub=n_sub),
        grid=(batch, seq // tq),
        in_specs=[
            pl.BlockSpec((1, SWA_HEADS, tq, SWA_HEAD_DIM), cur),
            pl.BlockSpec((1, SWA_KV_HEADS, tq, SWA_HEAD_DIM), cur),
            pl.BlockSpec((1, SWA_KV_HEADS, SWA_WINDOW, SWA_HEAD_DIM), prev),
            pl.BlockSpec((1, SWA_KV_HEADS, SWA_HEAD_DIM, tq), lambda b, i: (b, 0, 0, i)),
            pl.BlockSpec((1, SWA_KV_HEADS, SWA_HEAD_DIM, SWA_WINDOW),
                         lambda b, i: (b, 0, 0, jnp.maximum(i * n_sub - 1, 0))),
            pl.BlockSpec(sink_lanes.shape, lambda b, i: (0, 0)),
        ],
        out_specs=pl.BlockSpec((1, tq, D_MODEL), lambda b, i: (b, i, 0)),
        out_shape=jax.ShapeDtypeStruct((batch, seq, D_MODEL), BF16),
        scratch_shapes=[pltpu.VMEM((2, 2 * SWA_WINDOW, heads_per_unit * SWA_WINDOW), F32),
                        pltpu.VMEM((2 * SWA_WINDOW, heads_per_unit * SWA_WINDOW), F32)],
        compiler_params=pltpu.CompilerParams(
            dimension_semantics=("arbitrary", "arbitrary"), vmem_limit_bytes=VMEM_LIMIT),
        name="swa",
    )(q_a, k_a, k_a, vt_a, vt_a, sink_lanes)


def _mla_kernel(qt_ref, k_ref, vt_ref, o_ref, m_scr, l_scr, acc_scr, s_scr, *, tq, tk):
    ratio = tq // tk
    assert tq == ratio * tk and ratio % 2 == 0
    chunks = tk // MLA_KV_CHUNK
    q_chunks = tq // MLA_KV_CHUNK
    n_q = qt_ref.shape[2] // q_chunks

    def q_block(i, carry):
        m_scr[...] = jnp.full(m_scr.shape, NEG_INF, F32)
        l_scr[...] = jnp.zeros(l_scr.shape, F32)
        acc_scr[...] = jnp.zeros(acc_scr.shape, F32)

        def scores(j, buf, q0=0, blk=i):
            k = k_ref[0, 0, pl.ds(pl.multiple_of(j * tk, tk), tk), :]
            for c in range(q0 // MLA_KV_CHUNK, q_chunks):
                s_scr[buf, :, c * MLA_KV_CHUNK:(c + 1) * MLA_KV_CHUNK] = _dot(
                    k, qt_ref[0, 0, blk * q_chunks + c])

        def softmax_pv(j, buf, q0=0, key_offset=None):
            s = s_scr[buf, :, q0:]
            if key_offset is not None:
                kpos = lax.broadcasted_iota(jnp.int32, s.shape, 0) + key_offset
                qpos = lax.broadcasted_iota(jnp.int32, s.shape, 1) + q0
                s = jnp.where(kpos <= qpos, s, NEG_INF)
            m_old = m_scr[:, q0:]
            m_new = jnp.maximum(m_old, jnp.max(s, axis=0, keepdims=True))
            alpha = jnp.exp2(m_old - m_new)
            p = jnp.exp2(s - m_new)
            l_scr[:, q0:] = alpha * l_scr[:, q0:] + jnp.sum(p, axis=0, keepdims=True)
            pb = p.astype(BF16)
            pv = _dot(vt_ref[0, 0, j * chunks], pb[:MLA_KV_CHUNK])
            for c in range(1, chunks):
                pv += _dot(vt_ref[0, 0, j * chunks + c],
                           pb[c * MLA_KV_CHUNK:(c + 1) * MLA_KV_CHUNK])
            acc_scr[:, q0:] = alpha * acc_scr[:, q0:] + pv
            m_scr[:, q0:] = m_new

        @pl.when(i == 0)
        def _():
            scores(0, 0)

        def pair(jj, c):
            t = 2 * jj
            scores(t + 1, 1)
            softmax_pv(t, 0)
            scores(t + 2, 0)
            softmax_pv(t + 1, 1)
            return c

        lax.fori_loop(0, (ratio // 2) * i, pair, 0)
        for d in range(ratio):
            if d + 1 < ratio:
                scores(ratio * i + d + 1, (d + 1) % 2, q0=(d + 1) * tk)
            else:
                scores(0, 0, blk=jnp.minimum(i + 1, n_q - 1))
            softmax_pv(ratio * i + d, d % 2, q0=d * tk, key_offset=d * tk)

        o = acc_scr[...] / l_scr[...]
        o_ref[0, pl.ds(pl.multiple_of(i * tq, tq), tq), :] = o.T.astype(o_ref.dtype)
        return carry

    lax.fori_loop(0, n_q, q_block, 0)


def _mla(q_m, k_m, vt_m, batch, seq, tq, tk):
    return pl.pallas_call(
        functools.partial(_mla_kernel, tq=tq, tk=tk),
        grid=(batch, MLA_HEADS),
        in_specs=[
            pl.BlockSpec((1, 1, seq // MLA_KV_CHUNK, MLA_QK_DIM, MLA_KV_CHUNK),
                         lambda b, h: (b, h, 0, 0, 0)),
            pl.BlockSpec((1, 1, seq, MLA_QK_DIM), lambda b, h: (b, h, 0, 0)),
            pl.BlockSpec((1, 1, seq // MLA_KV_CHUNK, MLA_V_DIM, MLA_KV_CHUNK),
                         lambda b, h: (b, h, 0, 0, 0)),
        ],
        out_specs=pl.BlockSpec((1, seq, MLA_V_DIM), lambda b, h: (b, 0, h)),
        out_shape=jax.ShapeDtypeStruct((batch, seq, D_MODEL), BF16),
        scratch_shapes=[
            pltpu.VMEM((1, tq), F32),
            pltpu.VMEM((1, tq), F32),
            pltpu.VMEM((MLA_V_DIM, tq), F32),
            pltpu.VMEM((2, tk, tq), F32),
        ],
        compiler_params=pltpu.CompilerParams(
            dimension_semantics=("arbitrary", "arbitrary"), vmem_limit_bytes=VMEM_LIMIT),
        name="mla",
    )(q_m, k_m, vt_m)


def _mix_and_route(x_ref, oa_ref, ob_ref, ga_ref, gb_ref, wo_ref, gf_ref, wr_ref, br_ref):
    merged = (ga_ref[...].astype(F32) * oa_ref[...].astype(F32)
              + gb_ref[...].astype(F32) * ob_ref[...].astype(F32)).astype(BF16)
    h = x_ref[...] + _dot(merged, wo_ref[...])
    xn = _rms(h, gf_ref[...])

    tm = xn.shape[0]
    x_hi = xn.astype(BF16)
    x_lo = (xn - x_hi.astype(F32)).astype(BF16)
    prod_hi = _dot(x_hi, wr_ref[...])
    prod_lo = _dot(x_lo, wr_ref[...])
    lg = (prod_hi[:, :LANES] + prod_hi[:, LANES:] + prod_lo[:, :LANES] + prod_lo[:, LANES:]
          + br_ref[...])
    lg_t = lg.T
    big = jnp.int32(LANES)
    g_row = lax.broadcasted_iota(jnp.int32, (8, tm), 0)
    gl = jnp.where(g_row < N_GROUPS, lg_t[N_EXPERTS:N_EXPERTS + 8], NEG_INF)
    g_max = jnp.max(gl, axis=0, keepdims=True)
    g_w = 1.0 / jnp.sum(jnp.exp(gl - g_max), axis=0, keepdims=True)
    gidx = jnp.min(jnp.where(gl == g_max, g_row, big), axis=0, keepdims=True)

    row = lax.broadcasted_iota(jnp.int32, (N_EXPERTS, tm), 0)
    sel = (row >= gidx * EXPERTS_PER_GROUP) & (row < (gidx + 1) * EXPERTS_PER_GROUP)
    el = jnp.where(sel, lg_t[:N_EXPERTS], NEG_INF)
    e_exp = jnp.exp(el - jnp.max(el, axis=0, keepdims=True))
    e_prob = e_exp / jnp.sum(e_exp, axis=0, keepdims=True)
    p1 = jnp.max(e_prob, axis=0, keepdims=True)
    i1 = jnp.min(jnp.where(sel & (e_prob == p1), row, big), axis=0, keepdims=True)
    rest = sel & (row != i1)
    p2 = jnp.max(jnp.where(rest, e_prob, -1.0), axis=0, keepdims=True)
    i2 = jnp.min(jnp.where(rest & (e_prob == p2), row, big), axis=0, keepdims=True)
    top_sum = p1 + p2
    comb = jnp.where(row == i1, p1 / top_sum, jnp.where(row == i2, p2 / top_sum, 0.0)) * g_w
    gid_rows = jnp.where(g_row == 0, gidx.astype(F32), 0.0)
    comb_t = jnp.concatenate(
        [comb, gid_rows, jnp.zeros((LANES - N_EXPERTS - 8, tm), F32)], axis=0)
    return h, x_hi, comb_t


def _block_tail_kernel(x_ref, oa_ref, ob_ref, ga_ref, gb_ref, wout_ref, gf_ref, wr_ref, br_ref,
                       wi_ref, wo_ref, p_ref, gp_ref, wg_ref, wp_ref, gfin_ref, o_ref, *, chunk):
    tm = x_ref.shape[0]
    h, xn, comb_t = _mix_and_route(
        x_ref, oa_ref, ob_ref, ga_ref, gb_ref, wout_ref, gf_ref, wr_ref, br_ref)
    o_ref[...] = h
    comb = comb_t.T
    comb_hi = comb.astype(BF16)
    comb_lo = (comb - comb_hi.astype(F32)).astype(BF16)
    comb_hl = jnp.concatenate([comb_hi, comb_lo], axis=1)
    gid_col = comb[:, N_EXPERTS:N_EXPERTS + 1]
    gid_row = comb_t[N_EXPERTS:N_EXPERTS + 1, :]
    member_col = gid_col == lax.broadcasted_iota(jnp.int32, (tm, LANES), 1).astype(F32)
    member_row = gid_row == lax.broadcasted_iota(jnp.int32, (8, tm), 0).astype(F32)
    r_i = lax.broadcasted_iota(jnp.int32, (tm, tm), 0)
    c_i = lax.broadcasted_iota(jnp.int32, (tm, tm), 1)
    ranks_row = _dot(member_row.astype(BF16), (r_i < c_i).astype(BF16))
    ranks_col = _dot((r_i > c_i).astype(BF16), member_col.astype(BF16))
    pad = 2 * LANES
    never = -4.0 * tm
    slot_row = lax.broadcasted_iota(jnp.int32, (chunk, tm), 0).astype(F32)
    slot_col = lax.broadcasted_iota(jnp.int32, (tm, pad), 1).astype(F32)
    slot_col = jnp.where(slot_col < float(chunk), slot_col, never)

    groups = range(N_GROUPS)
    in_rows = [member_row[g:g + 1] for g in groups]
    rank_rows = [jnp.where(in_rows[g], ranks_row[g:g + 1], -2.0 * never) for g in groups]
    rank_cols = [jnp.where(member_col[:, g:g + 1], ranks_col[:, g:g + 1], -2.0 * never)
                 for g in groups]

    def one_hots(g, base):
        gather = jnp.where(rank_rows[g] - base == slot_row, 1.0, 0.0).astype(BF16)
        scatter = jnp.where(rank_cols[g] - base == slot_col, 1.0, 0.0).astype(BF16)
        return gather, scatter

    def compact(gather):
        xc = _dot(gather, xn).astype(BF16)
        wc = _dot(gather, comb_hl)
        return xc, wc[:, :LANES] + wc[:, LANES:]

    def experts_in(g, xc):
        return [_dot(xc, wi_ref[g * EXPERTS_PER_GROUP + j]) for j in range(EXPERTS_PER_GROUP)]

    def activate(g, hids, wc):
        acts = []
        for j, hid in enumerate(hids):
            e = g * EXPERTS_PER_GROUP + j
            gate = hid[:, :D_EXPERT]
            act = (gate * _sigmoid(gate)) * hid[:, D_EXPERT:] * wc[:, e:e + 1]
            acts.append(act.astype(BF16))
        return acts

    def experts_out(g, acts):
        yc = _dot(acts[0], wo_ref[g * EXPERTS_PER_GROUP])
        for j in range(1, EXPERTS_PER_GROUP):
            yc = yc + _dot(acts[j], wo_ref[g * EXPERTS_PER_GROUP + j])
        return jnp.concatenate(
            [yc.astype(BF16), jnp.zeros((pad - chunk, D_MODEL), BF16)], axis=0)

    hots = [one_hots(g, 0.0) for g in groups]
    compacted = [compact(hots[g][0]) for g in groups]
    hidden = [experts_in(g, compacted[g][0]) for g in groups]
    acts = [activate(g, hidden[g], compacted[g][1]) for g in groups]
    outs = [experts_out(g, acts[g]) for g in groups]
    o_ref[...] += _dot(jnp.concatenate([hots[g][1] for g in groups], axis=1),
                       jnp.concatenate(outs, axis=0))

    for g in groups:
        count = jnp.sum(in_rows[g].astype(jnp.int32))
        n_chunks = (count + (chunk - 1)) // chunk

        def chunk_body(c, carry, g=g):
            gather, scatter = one_hots(g, (c * chunk).astype(F32))
            xc, wc = compact(gather)
            yc = experts_out(g, activate(g, experts_in(g, xc), wc))
            o_ref[...] += _dot(scatter, yc)
            return carry

        lax.fori_loop(1, n_chunks, chunk_body, 0)

    h = o_ref[...]
    gate = _sigmoid(_dot(_rms(h, gp_ref[...]).astype(BF16), wg_ref[...]))
    h = h + gate * _dot(p_ref[...].astype(BF16), wp_ref[...])
    o_ref[...] = _rms(h, gfin_ref[...])


def _block_tail(x2, o_a, o_b, g_a, g_b, w_o, g_ffn, w_r, b_r, w_ei, w_eo, p2, g_ple, w_pg, w_pp,
                g_final, tm, chunk):
    tokens = x2.shape[0]
    tok = lambda t: (t, 0)
    const = lambda t: (0, 0)
    return pl.pallas_call(
        functools.partial(_block_tail_kernel, chunk=chunk),
        grid=(tokens // tm,),
        in_specs=[
            pl.BlockSpec((tm, D_MODEL), tok),
            pl.BlockSpec((tm, D_MODEL), tok),
            pl.BlockSpec((tm, D_MODEL), tok),
            pl.BlockSpec((tm, D_MODEL), tok),
            pl.BlockSpec((tm, D_MODEL), tok),
            pl.BlockSpec((D_MODEL, D_MODEL), const, pipeline_mode=pl.Buffered(1)),
            pl.BlockSpec((1, D_MODEL), const),
            pl.BlockSpec((D_MODEL, 2 * LANES), const, pipeline_mode=pl.Buffered(1)),
            pl.BlockSpec((1, LANES), const),
            pl.BlockSpec(w_ei.shape, lambda t: (0, 0, 0), pipeline_mode=pl.Buffered(1)),
            pl.BlockSpec(w_eo.shape, lambda t: (0, 0, 0), pipeline_mode=pl.Buffered(1)),
            pl.BlockSpec((tm, PLE_DIM), tok),
            pl.BlockSpec((1, D_MODEL), const),
            pl.BlockSpec((D_MODEL, D_MODEL), const, pipeline_mode=pl.Buffered(1)),
            pl.BlockSpec((PLE_DIM, D_MODEL), const, pipeline_mode=pl.Buffered(1)),
            pl.BlockSpec((1, D_MODEL), const),
        ],
        out_specs=pl.BlockSpec((tm, D_MODEL), tok),
        out_shape=jax.ShapeDtypeStruct((tokens, D_MODEL), F32),
        compiler_params=pltpu.CompilerParams(
            dimension_semantics=("arbitrary",), vmem_limit_bytes=VMEM_LIMIT),
        name="block_tail",
    )(x2, o_a, o_b, g_a, g_b, w_o, g_ffn, w_r, b_r, w_ei, w_eo, p2, g_ple, w_pg, w_pp, g_final)


def _rope_tables(seq):
    half = SWA_HEAD_DIM // 2
    pos = jnp.arange(seq, dtype=F32)
    inv = ROPE_THETA ** (-jnp.arange(0, SWA_HEAD_DIM, 2, dtype=F32) / SWA_HEAD_DIM)
    ang = pos[:, None] * inv[None, :]
    cos, sin = jnp.cos(ang), jnp.sin(ang)
    reps = LANES // SWA_HEAD_DIM
    cos_t = jnp.broadcast_to(cos[:, None, :], (seq, 2 * reps, half)).reshape(seq, LANES)
    sin_t = jnp.broadcast_to(
        jnp.stack([-sin, sin], axis=1)[:, None], (seq, reps, 2, half)).reshape(seq, LANES)
    return cos_t, sin_t, cos.T, sin.T


def _layer(h2d, p2d, g_mix, w_in, sinks, g_q, w_uq, g_kv, w_ukv, w_out, g_ffn, w_rg, b_rg,
           w_re, b_re, w_ei, w_eo, g_ple, w_pg, w_pp, g_final, rope, batch, seq):
    cos_t, sin_t, cos_h, sin_h = rope
    assert SWA_HEAD_DIM == MLA_ROPE_DIM, "one rotary table serves both mixers"
    w_main = w_in[:, :_C_GATES].astype(BF16)
    w_gates = w_in[:, _C_GATES:].astype(BF16)
    assert w_gates.shape[1] == 2 * D_MODEL
    w_uqt = jnp.transpose(
        w_uq.reshape(MLA_Q_RANK, MLA_HEADS, MLA_QK_DIM), (1, 2, 0)).astype(BF16)
    ukv = w_ukv.reshape(MLA_KV_RANK, MLA_HEADS, MLA_NOPE_DIM + MLA_V_DIM)
    w_uk_c = ukv[:, :, :MLA_NOPE_DIM].reshape(MLA_KV_RANK, -1).astype(BF16)
    w_uvt = jnp.transpose(ukv[:, :, MLA_NOPE_DIM:], (1, 2, 0)).astype(BF16)

    q_a, k_a, v_a, g_a, g_b, q_m, k_m, v_m = _inproj(
        h2d, g_mix[None], w_main, w_gates, cos_t, sin_t, cos_h, sin_h, g_q[None], w_uqt,
        g_kv[None], w_uk_c, w_uvt, batch, seq, tm=MLA_KV_CHUNK)

    sink_lanes = jnp.repeat(sinks.astype(F32), SWA_WINDOW)[None]
    o_a = _swa(q_a, k_a, v_a, sink_lanes, batch, seq, tq=SWA_TQ,
               heads_per_unit=SWA_HEADS_PER_UNIT)
    o_b = _mla(q_m, k_m, v_m, batch, seq, tq=MLA_TQ, tk=MLA_TK)

    w_r = jnp.concatenate(
        [w_re, w_rg, jnp.zeros((D_MODEL, LANES - N_EXPERTS - N_GROUPS), F32)], axis=1)
    w_r_hi = w_r.astype(BF16)
    w_r_lo = (w_r - w_r_hi.astype(F32)).astype(BF16)
    w_r2 = jnp.concatenate([w_r_hi, w_r_lo], axis=1)
    b_r = jnp.concatenate(
        [b_re, b_rg, jnp.zeros((LANES - N_EXPERTS - N_GROUPS,), F32)])[None]
    tokens = batch * seq
    return _block_tail(
        h2d, o_a.reshape(tokens, D_MODEL), o_b.reshape(tokens, D_MODEL), g_a, g_b,
        w_out.astype(BF16), g_ffn[None], w_r2, b_r, w_ei.astype(BF16), w_eo.astype(BF16), p2d,
        g_ple[None], w_pg.astype(BF16), w_pp.astype(BF16), g_final[None], tm=TAIL_TM,
        chunk=MOE_CHUNK)


def kernel(x, p, g_mix, w_in, swa_sinks, mla_g_q, mla_w_uq, mla_g_kv, mla_w_ukv, w_out, g_ffn,
           w_router_group, b_router_group, w_router_expert, b_router_expert, w_expert_in,
           w_expert_out, g_ple, w_ple_gate, w_ple_proj, g_final):
    batch, seq, d = x.shape
    depth = p.shape[0]
    assert d == D_MODEL and depth == 1, "final RMSNorm is fused into the single layer"
    rope = _rope_tables(seq)
    out = _layer(
        x.reshape(batch * seq, d), p[0].reshape(batch * seq, PLE_DIM), g_mix[0], w_in[0],
        swa_sinks[0], mla_g_q[0], mla_w_uq[0], mla_g_kv[0], mla_w_ukv[0], w_out[0], g_ffn[0],
        w_router_group[0], b_router_group[0], w_router_expert[0], b_router_expert[0],
        w_expert_in[0], w_expert_out[0], g_ple[0], w_ple_gate[0], w_ple_proj[0], g_final,
        rope, batch, seq)
    return out.reshape(batch, seq, d)
```

```python
import functools
import math

import jax
import jax.numpy as jnp
from jax import lax
from jax.experimental import pallas as pl
from jax.experimental.pallas import tpu as pltpu

D_MODEL = 1024
PLE_DIM = 256
ROPE_THETA = 10000.0
EPS = 1e-6
NEG_INF = -1e30

SWA_HEAD_DIM = 64
SWA_HEADS = D_MODEL // SWA_HEAD_DIM
SWA_KV_HEADS = SWA_HEADS // 8
SWA_GROUP = SWA_HEADS // SWA_KV_HEADS
SWA_WINDOW = 128

MLA_NOPE_DIM = 128
MLA_ROPE_DIM = 64
MLA_V_DIM = 128
MLA_V_AUG = MLA_V_DIM + 16
MLA_HEADS = D_MODEL // MLA_V_DIM
MLA_Q_RANK = 256
MLA_KV_RANK = 128
MLA_QK_DIM = MLA_NOPE_DIM + MLA_ROPE_DIM

N_GROUPS = 4
EXPERTS_PER_GROUP = 4
N_EXPERTS = N_GROUPS * EXPERTS_PER_GROUP
D_EXPERT = 256

LANES = 128
LOG2_E = 1.4426950408889634
VMEM_LIMIT = 56 * 1024 * 1024

MLA_KV_CHUNK = 512
SWA_TQ = 512
SWA_HEADS_PER_UNIT = 4
MLA_TQ = 1024
MLA_TK = 512
TAIL_TM = 512
MOE_CHUNK = 160

_C_QA = 0
_C_KV = _C_QA + SWA_HEADS * SWA_HEAD_DIM
_C_CQ = _C_KV + 2 * SWA_KV_HEADS * SWA_HEAD_DIM
_C_CKVR = _C_CQ + MLA_Q_RANK
_C_GATES = _C_CKVR + MLA_KV_RANK + MLA_ROPE_DIM

BF16 = jnp.bfloat16
F32 = jnp.float32


def _dot(a, b):
    return jnp.dot(a, b, preferred_element_type=F32)


def _dot_nt(a, b):
    return lax.dot_general(a, b, (((1,), (1,)), ((), ())), preferred_element_type=F32)


def _rms(x, g):
    r = lax.rsqrt(jnp.mean(x * x, axis=-1, keepdims=True) + EPS)
    return x * r * g


def _sigmoid(x):
    return 1.0 / (1.0 + jnp.exp(-x))


def _rope_lanes(x, cos, sin_signed):
    lane = lax.broadcasted_iota(jnp.int32, x.shape, 1)
    upper = (lane & (SWA_HEAD_DIM // 2)) != 0
    partner = jnp.where(upper, pltpu.roll(x, SWA_HEAD_DIM // 2, 1),
                        pltpu.roll(x, LANES - SWA_HEAD_DIM // 2, 1))
    return x * cos + partner * sin_signed


def _inproj_kernel(x_ref, g_ref, w_ref, wg_ref, cos_ref, sin_ref, cost_ref, sint_ref, gq_ref,
                   wuqt_ref, gkv_ref, wuk_ref, wuvt_ref, qa_ref, ka_ref, va_ref, ga_ref, gb_ref,
                   qm_ref, km_ref, vt_ref):
    xn = _rms(x_ref[...], g_ref[...]).astype(BF16)
    cos = cos_ref[...]
    sin = sin_ref[...]
    tm = xn.shape[0]

    cq = _rms(_dot(xn, w_ref[:, _C_CQ:_C_CKVR]), gq_ref[...])
    ckvr = _dot(xn, w_ref[:, _C_CKVR:_C_GATES])
    ckv = _rms(ckvr[:, :MLA_KV_RANK], gkv_ref[...])
    kr = jnp.concatenate(
        [ckvr[:, MLA_KV_RANK:], jnp.zeros((tm, LANES - MLA_ROPE_DIM), F32)], axis=1)
    kr = _rope_lanes(kr, cos, sin)[:, :MLA_ROPE_DIM].astype(BF16)
    qm_scale = LOG2_E / math.sqrt(MLA_QK_DIM)
    cq_t = cq.T.astype(BF16)
    ckv_t = ckv.T.astype(BF16)
    kn = _dot(ckv.astype(BF16), wuk_ref[...]).astype(BF16)
    vt_all = _dot(wuvt_ref[...].reshape(MLA_HEADS * MLA_V_DIM, MLA_KV_RANK), ckv_t)
    qt_all = _dot(wuqt_ref[...].reshape(MLA_HEADS * MLA_QK_DIM, MLA_Q_RANK), cq_t)

    qa = _dot(xn, w_ref[:, _C_QA:_C_KV])
    kv = _dot(xn, w_ref[:, _C_KV:_C_CQ])
    gate_a = _dot(xn, wg_ref[:, :D_MODEL])
    gate_b = _dot(xn, wg_ref[:, D_MODEL:])

    cos_h, sin_h = cost_ref[...], sint_ref[...]
    half = MLA_ROPE_DIM // 2
    for h in range(MLA_HEADS):
        km_ref[0, h, :, :MLA_NOPE_DIM] = kn[:, h * MLA_NOPE_DIM:(h + 1) * MLA_NOPE_DIM]
        km_ref[0, h, :, MLA_NOPE_DIM:] = kr
        vt_ref[0, h, 0, :MLA_V_DIM] = vt_all[h * MLA_V_DIM:(h + 1) * MLA_V_DIM].astype(BF16)
        vt_ref[0, h, 0, MLA_V_DIM:] = jnp.ones((MLA_V_AUG - MLA_V_DIM, tm), BF16)
        q_t = qt_all[h * MLA_QK_DIM:(h + 1) * MLA_QK_DIM]
        x1 = q_t[MLA_NOPE_DIM:MLA_NOPE_DIM + half]
        x2 = q_t[MLA_NOPE_DIM + half:]
        q_t = jnp.concatenate(
            [q_t[:MLA_NOPE_DIM], x1 * cos_h - x2 * sin_h, x2 * cos_h + x1 * sin_h], axis=0)
        qm_ref[0, h, 0] = (q_t * qm_scale).astype(BF16)

    qa_scale = LOG2_E / math.sqrt(SWA_HEAD_DIM)
    for j in range(SWA_HEADS // 2):
        q = (_rope_lanes(qa[:, j * LANES:(j + 1) * LANES], cos, sin) * qa_scale).astype(BF16)
        qa_ref[0, 2 * j] = q[:, :SWA_HEAD_DIM]
        qa_ref[0, 2 * j + 1] = q[:, SWA_HEAD_DIM:]
    k = _rope_lanes(kv[:, :LANES], cos, sin).astype(BF16)
    ka_ref[0, 0] = k[:, :SWA_HEAD_DIM]
    ka_ref[0, 1] = k[:, SWA_HEAD_DIM:]
    vt = kv[:, LANES:].T.astype(BF16)
    va_ref[0, 0] = vt[:SWA_HEAD_DIM]
    va_ref[0, 1] = vt[SWA_HEAD_DIM:]

    ga_ref[...] = _sigmoid(gate_a).astype(BF16)
    gb_ref[...] = _sigmoid(gate_b).astype(BF16)


def _inproj(x2, g_mix, w_main, w_gates, cos_t, sin_t, cos_h, sin_h, g_q, w_uqt, g_kv, w_uk, w_uvt,
            batch, seq, tm):
    tokens = batch * seq
    nt = seq // tm
    const = lambda t: (0, 0)
    tok = lambda t: (t, 0)
    head = lambda t: (t // nt, 0, t % nt, 0)
    out_shape = (
        jax.ShapeDtypeStruct((batch, SWA_HEADS, seq, SWA_HEAD_DIM), BF16),
        jax.ShapeDtypeStruct((batch, SWA_KV_HEADS, seq, SWA_HEAD_DIM), BF16),
        jax.ShapeDtypeStruct((batch, SWA_KV_HEADS, SWA_HEAD_DIM, seq), BF16),
        jax.ShapeDtypeStruct((tokens, D_MODEL), BF16),
        jax.ShapeDtypeStruct((tokens, D_MODEL), BF16),
        jax.ShapeDtypeStruct((batch, MLA_HEADS, nt, MLA_QK_DIM, tm), BF16),
        jax.ShapeDtypeStruct((batch, MLA_HEADS, seq, MLA_QK_DIM), BF16),
        jax.ShapeDtypeStruct((batch, MLA_HEADS, nt, MLA_V_AUG, tm), BF16),
    )
    return pl.pallas_call(
        _inproj_kernel,
        grid=(tokens // tm,),
        in_specs=[
            pl.BlockSpec((tm, D_MODEL), tok),
            pl.BlockSpec((1, D_MODEL), const),
            pl.BlockSpec(w_main.shape, const, pipeline_mode=pl.Buffered(1)),
            pl.BlockSpec(w_gates.shape, const, pipeline_mode=pl.Buffered(1)),
            pl.BlockSpec((tm, LANES), lambda t: (t % nt, 0)),
            pl.BlockSpec((tm, LANES), lambda t: (t % nt, 0)),
            pl.BlockSpec((MLA_ROPE_DIM // 2, tm), lambda t: (0, t % nt)),
            pl.BlockSpec((MLA_ROPE_DIM // 2, tm), lambda t: (0, t % nt)),
            pl.BlockSpec((1, MLA_Q_RANK), const),
            pl.BlockSpec(w_uqt.shape, lambda t: (0, 0, 0), pipeline_mode=pl.Buffered(1)),
            pl.BlockSpec((1, MLA_KV_RANK), const),
            pl.BlockSpec(w_uk.shape, const, pipeline_mode=pl.Buffered(1)),
            pl.BlockSpec(w_uvt.shape, lambda t: (0, 0, 0), pipeline_mode=pl.Buffered(1)),
        ],
        out_specs=(
            pl.BlockSpec((1, SWA_HEADS, tm, SWA_HEAD_DIM), head),
            pl.BlockSpec((1, SWA_KV_HEADS, tm, SWA_HEAD_DIM), head),
            pl.BlockSpec((1, SWA_KV_HEADS, SWA_HEAD_DIM, tm), lambda t: (t // nt, 0, 0, t % nt)),
            pl.BlockSpec((tm, D_MODEL), tok),
            pl.BlockSpec((tm, D_MODEL), tok),
            pl.BlockSpec((1, MLA_HEADS, 1, MLA_QK_DIM, tm), lambda t: (t // nt, 0, t % nt, 0, 0)),
            pl.BlockSpec((1, MLA_HEADS, tm, MLA_QK_DIM), head),
            pl.BlockSpec((1, MLA_HEADS, 1, MLA_V_AUG, tm), lambda t: (t // nt, 0, t % nt, 0, 0)),
        ),
        out_shape=out_shape,
        compiler_params=pltpu.CompilerParams(
            dimension_semantics=("arbitrary",), vmem_limit_bytes=VMEM_LIMIT),
        name="inproj",
    )(x2, g_mix, w_main, w_gates, cos_t, sin_t, cos_h, sin_h, g_q, w_uqt, g_kv, w_uk, w_uvt)


def _swa_kernel(q_ref, k_ref, kp_ref, vt_ref, vtp_ref, sink_ref, o_ref, s_scr, bias_scr, *,
                n_sub):
    blk = SWA_WINDOW
    cols = s_scr.shape[2]
    hpu = cols // blk

    @pl.when((pl.program_id(0) == 0) & (pl.program_id(1) == 0))
    def _():
        kj = lax.broadcasted_iota(jnp.int32, (2 * blk, cols), 0)
        qi = lax.broadcasted_iota(jnp.int32, (2 * blk, cols), 1) & (blk - 1)
        dist = blk + qi - kj
        bias_scr[...] = jnp.where((dist >= 0) & (dist < SWA_WINDOW), 0.0, NEG_INF)

    key_row = lax.broadcasted_iota(jnp.int32, (2 * blk, 1), 0)
    no_prev = jnp.where((key_row < blk) & (pl.program_id(1) == 0), NEG_INF, 0.0)
    units = [(g, h0, n) for g in range(SWA_KV_HEADS)
             for h0 in range(g * SWA_GROUP, (g + 1) * SWA_GROUP, hpu) for n in range(n_sub)]

    def scores(u, buf):
        g, h0, n = u
        q = q_ref[0, h0:h0 + hpu, n * blk:(n + 1) * blk, :]
        k_prev = kp_ref[0, g] if n == 0 else k_ref[0, g, (n - 1) * blk:n * blk, :]
        kk = jnp.concatenate([k_prev, k_ref[0, g, n * blk:(n + 1) * blk, :]], axis=0)
        s_scr[buf] = _dot_nt(kk, q.reshape(cols, SWA_HEAD_DIM))

    def softmax_pv(u, buf):
        g, h0, n = u
        sink = sink_ref[:, h0 * blk:(h0 + hpu) * blk] * LOG2_E
        s = s_scr[buf] + bias_scr[...]
        if n == 0:
            s = s + no_prev
        m = jnp.maximum(jnp.max(s, axis=0, keepdims=True), sink)
        e = jnp.exp2(s - m)
        denom = jnp.sum(e, axis=0, keepdims=True) + jnp.exp2(sink - m)
        vt_prev = vtp_ref[0, g] if n == 0 else vt_ref[0, g, :, (n - 1) * blk:n * blk]
        vvt = jnp.concatenate([vt_prev, vt_ref[0, g, :, n * blk:(n + 1) * blk]], axis=1)
        o = (_dot(vvt, e.astype(BF16)) / denom).T
        for hh in range(hpu):
            h = h0 + hh
            o_ref[0, n * blk:(n + 1) * blk, h * SWA_HEAD_DIM:(h + 1) * SWA_HEAD_DIM] = (
                o[hh * blk:(hh + 1) * blk, :].astype(BF16))

    scores(units[0], 0)
    for idx, u in enumerate(units):
        if idx + 1 < len(units):
            scores(units[idx + 1], (idx + 1) % 2)
        softmax_pv(u, idx % 2)


def _swa(q_a, k_a, vt_a, sink_lanes, batch, seq, tq, heads_per_unit):
    assert SWA_GROUP % heads_per_unit == 0
    n_sub = tq // SWA_WINDOW
    cur = lambda b, i: (b, 0, i, 0)
    prev = lambda b, i: (b, 0, jnp.maximum(i * n_sub - 1, 0), 0)
    return pl.pallas_call(
        functools.partial(_swa_kernel, n_sub=n_sub),
        grid=(batch, seq // tq),
        in_specs=[
            pl.BlockSpec((1, SWA_HEADS, tq, SWA_HEAD_DIM), cur),
            pl.BlockSpec((1, SWA_KV_HEADS, tq, SWA_HEAD_DIM), cur),
            pl.BlockSpec((1, SWA_KV_HEADS, SWA_WINDOW, SWA_HEAD_DIM), prev),
            pl.BlockSpec((1, SWA_KV_HEADS, SWA_HEAD_DIM, tq), lambda b, i: (b, 0, 0, i)),
            pl.BlockSpec((1, SWA_KV_HEADS, SWA_HEAD_DIM, SWA_WINDOW),
                         lambda b, i: (b, 0, 0, jnp.maximum(i * n_sub - 1, 0))),
            pl.BlockSpec(sink_lanes.shape, lambda b, i: (0, 0)),
        ],
        out_specs=pl.BlockSpec((1, tq, D_MODEL), lambda b, i: (b, i, 0)),
        out_shape=jax.ShapeDtypeStruct((batch, seq, D_MODEL), BF16),
        scratch_shapes=[pltpu.VMEM((2, 2 * SWA_WINDOW, heads_per_unit * SWA_WINDOW), F32),
                        pltpu.VMEM((2 * SWA_WINDOW, heads_per_unit * SWA_WINDOW), F32)],
        compiler_params=pltpu.CompilerParams(
            dimension_semantics=("arbitrary", "arbitrary"), vmem_limit_bytes=VMEM_LIMIT),
        name="swa",
    )(q_a, k_a, k_a, vt_a, vt_a, sink_lanes)


def _mla_kernel(qt_ref, k_ref, vt_ref, o_ref, m_scr, acc_scr, s_scr, *, tq, tk):
    ratio = tq // tk
    assert tq == ratio * tk and ratio % 2 == 0
    chunks = tk // MLA_KV_CHUNK
    q_chunks = tq // MLA_KV_CHUNK
    n_q = qt_ref.shape[2] // q_chunks

    def q_block(i, carry):
        m_scr[...] = jnp.full(m_scr.shape, NEG_INF, F32)
        acc_scr[...] = jnp.zeros(acc_scr.shape, F32)

        def scores(j, buf, q0=0, blk=i):
            k = k_ref[0, 0, pl.ds(pl.multiple_of(j * tk, tk), tk), :]
            for c in range(q0 // MLA_KV_CHUNK, q_chunks):
                s_scr[buf, :, c * MLA_KV_CHUNK:(c + 1) * MLA_KV_CHUNK] = _dot(
                    k, qt_ref[0, 0, blk * q_chunks + c])

        def softmax_pv(j, buf, q0=0, key_offset=None):
            s = s_scr[buf, :, q0:]
            if key_offset is not None:
                kpos = lax.broadcasted_iota(jnp.int32, s.shape, 0) + key_offset
                qpos = lax.broadcasted_iota(jnp.int32, s.shape, 1) + q0
                s = jnp.where(kpos <= qpos, s, NEG_INF)
            m_old = m_scr[:, q0:]
            m_new = jnp.maximum(m_old, jnp.max(s, axis=0, keepdims=True))
            alpha = jnp.exp2(m_old - m_new)
            pb = jnp.exp2(s - m_new).astype(BF16)
            pv = _dot(vt_ref[0, 0, j * chunks], pb[:MLA_KV_CHUNK])
            for c in range(1, chunks):
                pv += _dot(vt_ref[0, 0, j * chunks + c],
                           pb[c * MLA_KV_CHUNK:(c + 1) * MLA_KV_CHUNK])
            acc_scr[:, q0:] = alpha * acc_scr[:, q0:] + pv
            m_scr[:, q0:] = m_new

        @pl.when(i == 0)
        def _():
            scores(0, 0)

        def pair(jj, c):
            t = 2 * jj
            scores(t + 1, 1)
            softmax_pv(t, 0)
            scores(t + 2, 0)
            softmax_pv(t + 1, 1)
            return c

        lax.fori_loop(0, (ratio // 2) * i, pair, 0)
        for d in range(ratio):
            if d + 1 < ratio:
                scores(ratio * i + d + 1, (d + 1) % 2, q0=(d + 1) * tk)
            else:
                scores(0, 0, blk=jnp.minimum(i + 1, n_q - 1))
            softmax_pv(ratio * i + d, d % 2, q0=d * tk, key_offset=d * tk)

        o = acc_scr[:MLA_V_DIM] / acc_scr[MLA_V_DIM:MLA_V_DIM + 1]
        o_ref[0, pl.ds(pl.multiple_of(i * tq, tq), tq), :] = o.T.astype(o_ref.dtype)
        return carry

    lax.fori_loop(0, n_q, q_block, 0)


def _mla(q_m, k_m, vt_m, batch, seq, tq, tk):
    return pl.pallas_call(
        functools.partial(_mla_kernel, tq=tq, tk=tk),
        grid=(batch, MLA_HEADS),
        in_specs=[
            pl.BlockSpec((1, 1, seq // MLA_KV_CHUNK, MLA_QK_DIM, MLA_KV_CHUNK),
                         lambda b, h: (b, h, 0, 0, 0)),
            pl.BlockSpec((1, 1, seq, MLA_QK_DIM), lambda b, h: (b, h, 0, 0)),
            pl.BlockSpec((1, 1, seq // MLA_KV_CHUNK, MLA_V_AUG, MLA_KV_CHUNK),
                         lambda b, h: (b, h, 0, 0, 0)),
        ],
        out_specs=pl.BlockSpec((1, seq, MLA_V_DIM), lambda b, h: (b, 0, h)),
        out_shape=jax.ShapeDtypeStruct((batch, seq, D_MODEL), BF16),
        scratch_shapes=[
            pltpu.VMEM((1, tq), F32),
            pltpu.VMEM((MLA_V_AUG, tq), F32),
            pltpu.VMEM((2, tk, tq), F32),
        ],
        compiler_params=pltpu.CompilerParams(
            dimension_semantics=("arbitrary", "arbitrary"), vmem_limit_bytes=VMEM_LIMIT),
        name="mla",
    )(q_m, k_m, vt_m)


def _mix_and_route(x_ref, oa_ref, ob_ref, ga_ref, gb_ref, wo_ref, gf_ref, wr_ref, br_ref):
    merged = (ga_ref[...].astype(F32) * oa_ref[...].astype(F32)
              + gb_ref[...].astype(F32) * ob_ref[...].astype(F32)).astype(BF16)
    h = x_ref[...] + _dot(merged, wo_ref[...])
    xn = _rms(h, gf_ref[...])

    tm = xn.shape[0]
    x_hi = xn.astype(BF16)
    x_lo = (xn - x_hi.astype(F32)).astype(BF16)
    prod_hi = _dot(x_hi, wr_ref[...])
    prod_lo = _dot(x_lo, wr_ref[...])
    lg = (prod_hi[:, :LANES] + prod_hi[:, LANES:] + prod_lo[:, :LANES] + prod_lo[:, LANES:]
          + br_ref[...])
    lg_t = lg.T
    big = jnp.int32(LANES)
    g_row = lax.broadcasted_iota(jnp.int32, (8, tm), 0)
    gl = jnp.where(g_row < N_GROUPS, lg_t[N_EXPERTS:N_EXPERTS + 8], NEG_INF)
    g_max = jnp.max(gl, axis=0, keepdims=True)
    g_w = 1.0 / jnp.sum(jnp.exp(gl - g_max), axis=0, keepdims=True)
    gidx = jnp.min(jnp.where(gl == g_max, g_row, big), axis=0, keepdims=True)

    row = lax.broadcasted_iota(jnp.int32, (N_EXPERTS, tm), 0)
    sel = (row >= gidx * EXPERTS_PER_GROUP) & (row < (gidx + 1) * EXPERTS_PER_GROUP)
    el = jnp.where(sel, lg_t[:N_EXPERTS], NEG_INF)
    e_exp = jnp.exp(el - jnp.max(el, axis=0, keepdims=True))
    e_prob = e_exp / jnp.sum(e_exp, axis=0, keepdims=True)
    p1 = jnp.max(e_prob, axis=0, keepdims=True)
    i1 = jnp.min(jnp.where(sel & (e_prob == p1), row, big), axis=0, keepdims=True)
    rest = sel & (row != i1)
    p2 = jnp.max(jnp.where(rest, e_prob, -1.0), axis=0, keepdims=True)
    i2 = jnp.min(jnp.where(rest & (e_prob == p2), row, big), axis=0, keepdims=True)
    top_sum = p1 + p2
    comb = jnp.where(row == i1, p1 / top_sum, jnp.where(row == i2, p2 / top_sum, 0.0)) * g_w
    gid_rows = jnp.where(g_row == 0, gidx.astype(F32), 0.0)
    comb_t = jnp.concatenate(
        [comb, gid_rows, jnp.zeros((LANES - N_EXPERTS - 8, tm), F32)], axis=0)
    return h, x_hi, comb_t


def _block_tail_kernel(x_ref, oa_ref, ob_ref, ga_ref, gb_ref, wout_ref, gf_ref, wr_ref, br_ref,
                       wi_ref, wo_ref, p_ref, gp_ref, wg_ref, wp_ref, gfin_ref, o_ref, *, chunk):
    tm = x_ref.shape[0]
    h, xn, comb_t = _mix_and_route(
        x_ref, oa_ref, ob_ref, ga_ref, gb_ref, wout_ref, gf_ref, wr_ref, br_ref)
    o_ref[...] = h
    comb = comb_t.T
    comb_hi = comb.astype(BF16)
    comb_lo = (comb - comb_hi.astype(F32)).astype(BF16)
    comb_hl = jnp.concatenate([comb_hi, comb_lo], axis=1)
    gid_col = comb[:, N_EXPERTS:N_EXPERTS + 1]
    gid_row = comb_t[N_EXPERTS:N_EXPERTS + 1, :]
    member_col = gid_col == lax.broadcasted_iota(jnp.int32, (tm, LANES), 1).astype(F32)
    member_row = gid_row == lax.broadcasted_iota(jnp.int32, (8, tm), 0).astype(F32)
    r_i = lax.broadcasted_iota(jnp.int32, (tm, tm), 0)
    c_i = lax.broadcasted_iota(jnp.int32, (tm, tm), 1)
    ranks_row = _dot(member_row.astype(BF16), (r_i < c_i).astype(BF16))
    ranks_col = _dot((r_i > c_i).astype(BF16), member_col.astype(BF16))
    pad = 2 * LANES
    never = -4.0 * tm
    slot_row = lax.broadcasted_iota(jnp.int32, (chunk, tm), 0).astype(F32)
    slot_col = lax.broadcasted_iota(jnp.int32, (tm, pad), 1).astype(F32)
    slot_col = jnp.where(slot_col < float(chunk), slot_col, never)

    groups = range(N_GROUPS)
    in_rows = [member_row[g:g + 1] for g in groups]
    rank_rows = [jnp.where(in_rows[g], ranks_row[g:g + 1], -2.0 * never) for g in groups]
    rank_cols = [jnp.where(member_col[:, g:g + 1], ranks_col[:, g:g + 1], -2.0 * never)
                 for g in groups]

    def one_hots(g, base):
        gather = jnp.where(rank_rows[g] - base == slot_row, 1.0, 0.0).astype(BF16)
        scatter = jnp.where(rank_cols[g] - base == slot_col, 1.0, 0.0).astype(BF16)
        return gather, scatter

    def compact(gather):
        xc = _dot(gather, xn).astype(BF16)
        wc = _dot(gather, comb_hl)
        return xc, wc[:, :LANES] + wc[:, LANES:]

    def experts_in(g, xc):
        return [_dot(xc, wi_ref[g * EXPERTS_PER_GROUP + j]) for j in range(EXPERTS_PER_GROUP)]

    def activate(g, hids, wc):
        acts = []
        for j, hid in enumerate(hids):
            e = g * EXPERTS_PER_GROUP + j
            gate = hid[:, :D_EXPERT]
            act = (gate * _sigmoid(gate)) * hid[:, D_EXPERT:] * wc[:, e:e + 1]
            acts.append(act.astype(BF16))
        return acts

    def experts_out(g, acts):
        yc = _dot(acts[0], wo_ref[g * EXPERTS_PER_GROUP])
        for j in range(1, EXPERTS_PER_GROUP):
            yc = yc + _dot(acts[j], wo_ref[g * EXPERTS_PER_GROUP + j])
        return jnp.concatenate(
            [yc.astype(BF16), jnp.zeros((pad - chunk, D_MODEL), BF16)], axis=0)

    hots = [one_hots(g, 0.0) for g in groups]
    compacted = [compact(hots[g][0]) for g in groups]
    hidden = [experts_in(g, compacted[g][0]) for g in groups]
    acts = [activate(g, hidden[g], compacted[g][1]) for g in groups]
    outs = [experts_out(g, acts[g]) for g in groups]
    o_ref[...] += _dot(jnp.concatenate([hots[g][1] for g in groups], axis=1),
                       jnp.concatenate(outs, axis=0))

    for g in groups:
        count = jnp.sum(in_rows[g].astype(jnp.int32))
        n_chunks = (count + (chunk - 1)) // chunk

        def chunk_body(c, carry, g=g):
            gather, scatter = one_hots(g, (c * chunk).astype(F32))
            xc, wc = compact(gather)
            yc = experts_out(g, activate(g, experts_in(g, xc), wc))
            o_ref[...] += _dot(scatter, yc)
            return carry

        lax.fori_loop(1, n_chunks, chunk_body, 0)

    h = o_ref[...]
    gate = _sigmoid(_dot(_rms(h, gp_ref[...]).astype(BF16), wg_ref[...]))
    h = h + gate * _dot(p_ref[...].astype(BF16), wp_ref[...])
    o_ref[...] = _rms(h, gfin_ref[...])


def _block_tail(x2, o_a, o_b, g_a, g_b, w_o, g_ffn, w_r, b_r, w_ei, w_eo, p2, g_ple, w_pg, w_pp,
                g_final, tm, chunk):
    tokens = x2.shape[0]
    tok = lambda t: (t, 0)
    const = lambda t: (0, 0)
    return pl.pallas_call(
        functools.partial(_block_tail_kernel, chunk=chunk),
        grid=(tokens // tm,),
        in_specs=[
            pl.BlockSpec((tm, D_MODEL), tok),
            pl.BlockSpec((tm, D_MODEL), tok),
            pl.BlockSpec((tm, D_MODEL), tok),
            pl.BlockSpec((tm, D_MODEL), tok),
            pl.BlockSpec((tm, D_MODEL), tok),
            pl.BlockSpec((D_MODEL, D_MODEL), const, pipeline_mode=pl.Buffered(1)),
            pl.BlockSpec((1, D_MODEL), const),
            pl.BlockSpec((D_MODEL, 2 * LANES), const, pipeline_mode=pl.Buffered(1)),
            pl.BlockSpec((1, LANES), const),
            pl.BlockSpec(w_ei.shape, lambda t: (0, 0, 0), pipeline_mode=pl.Buffered(1)),
            pl.BlockSpec(w_eo.shape, lambda t: (0, 0, 0), pipeline_mode=pl.Buffered(1)),
            pl.BlockSpec((tm, PLE_DIM), tok),
            pl.BlockSpec((1, D_MODEL), const),
            pl.BlockSpec((D_MODEL, D_MODEL), const, pipeline_mode=pl.Buffered(1)),
            pl.BlockSpec((PLE_DIM, D_MODEL), const, pipeline_mode=pl.Buffered(1)),
            pl.BlockSpec((1, D_MODEL), const),
        ],
        out_specs=pl.BlockSpec((tm, D_MODEL), tok),
        out_shape=jax.ShapeDtypeStruct((tokens, D_MODEL), F32),
        compiler_params=pltpu.CompilerParams(
            dimension_semantics=("arbitrary",), vmem_limit_bytes=VMEM_LIMIT),
        name="block_tail",
    )(x2, o_a, o_b, g_a, g_b, w_o, g_ffn, w_r, b_r, w_ei, w_eo, p2, g_ple, w_pg, w_pp, g_final)


def _rope_tables(seq):
    half = SWA_HEAD_DIM // 2
    pos = jnp.arange(seq, dtype=F32)
    inv = ROPE_THETA ** (-jnp.arange(0, SWA_HEAD_DIM, 2, dtype=F32) / SWA_HEAD_DIM)
    ang = pos[:, None] * inv[None, :]
    cos, sin = jnp.cos(ang), jnp.sin(ang)
    reps = LANES // SWA_HEAD_DIM
    cos_t = jnp.broadcast_to(cos[:, None, :], (seq, 2 * reps, half)).reshape(seq, LANES)
    sin_t = jnp.broadcast_to(
        jnp.stack([-sin, sin], axis=1)[:, None], (seq, reps, 2, half)).reshape(seq, LANES)
    return cos_t, sin_t, cos.T, sin.T


def _layer(h2d, p2d, g_mix, w_in, sinks, g_q, w_uq, g_kv, w_ukv, w_out, g_ffn, w_rg, b_rg,
           w_re, b_re, w_ei, w_eo, g_ple, w_pg, w_pp, g_final, rope, batch, seq):
    cos_t, sin_t, cos_h, sin_h = rope
    assert SWA_HEAD_DIM == MLA_ROPE_DIM, "one rotary table serves both mixers"
    w_main = w_in[:, :_C_GATES].astype(BF16)
    w_gates = w_in[:, _C_GATES:].astype(BF16)
    assert w_gates.shape[1] == 2 * D_MODEL
    w_uqt = jnp.transpose(
        w_uq.reshape(MLA_Q_RANK, MLA_HEADS, MLA_QK_DIM), (1, 2, 0)).astype(BF16)
    ukv = w_ukv.reshape(MLA_KV_RANK, MLA_HEADS, MLA_NOPE_DIM + MLA_V_DIM)
    w_uk_c = ukv[:, :, :MLA_NOPE_DIM].reshape(MLA_KV_RANK, -1).astype(BF16)
    w_uvt = jnp.transpose(ukv[:, :, MLA_NOPE_DIM:], (1, 2, 0)).astype(BF16)

    q_a, k_a, v_a, g_a, g_b, q_m, k_m, v_m = _inproj(
        h2d, g_mix[None], w_main, w_gates, cos_t, sin_t, cos_h, sin_h, g_q[None], w_uqt,
        g_kv[None], w_uk_c, w_uvt, batch, seq, tm=MLA_KV_CHUNK)

    sink_lanes = jnp.repeat(sinks.astype(F32), SWA_WINDOW)[None]
    o_a = _swa(q_a, k_a, v_a, sink_lanes, batch, seq, tq=SWA_TQ,
               heads_per_unit=SWA_HEADS_PER_UNIT)
    o_b = _mla(q_m, k_m, v_m, batch, seq, tq=MLA_TQ, tk=MLA_TK)

    w_r = jnp.concatenate(
        [w_re, w_rg, jnp.zeros((D_MODEL, LANES - N_EXPERTS - N_GROUPS), F32)], axis=1)
    w_r_hi = w_r.astype(BF16)
    w_r_lo = (w_r - w_r_hi.astype(F32)).astype(BF16)
    w_r2 = jnp.concatenate([w_r_hi, w_r_lo], axis=1)
    b_r = jnp.concatenate(
        [b_re, b_rg, jnp.zeros((LANES - N_EXPERTS - N_GROUPS,), F32)])[None]
    tokens = batch * seq
    return _block_tail(
        h2d, o_a.reshape(tokens, D_MODEL), o_b.reshape(tokens, D_MODEL), g_a, g_b,
        w_out.astype(BF16), g_ffn[None], w_r2, b_r, w_ei.astype(BF16), w_eo.astype(BF16), p2d,
        g_ple[None], w_pg.astype(BF16), w_pp.astype(BF16), g_final[None], tm=TAIL_TM,
        chunk=MOE_CHUNK)


def kernel(x, p, g_mix, w_in, swa_sinks, mla_g_q, mla_w_uq, mla_g_kv, mla_w_ukv, w_out, g_ffn,
           w_router_group, b_router_group, w_router_expert, b_router_expert, w_expert_in,
           w_expert_out, g_ple, w_ple_gate, w_ple_proj, g_final):
    batch, seq, d = x.shape
    depth = p.shape[0]
    assert d == D_MODEL and depth == 1, "final RMSNorm is fused into the single layer"
    rope = _rope_tables(seq)
    out = _layer(
        x.reshape(batch * seq, d), p[0].reshape(batch * seq, PLE_DIM), g_mix[0], w_in[0],
        swa_sinks[0], mla_g_q[0], mla_w_uq[0], mla_g_kv[0], mla_w_ukv[0], w_out[0], g_ffn[0],
        w_router_group[0], b_router_group[0], w_router_expert[0], b_router_expert[0],
        w_expert_in[0], w_expert_out[0], g_ple[0], w_ple_gate[0], w_ple_proj[0], g_final,
        rope, batch, seq)
    return out.reshape(batch, seq, d)
```

```python
import functools
import math

import jax
import jax.numpy as jnp
from jax import lax
from jax.experimental import pallas as pl
from jax.experimental.pallas import tpu as pltpu

D_MODEL = 1024
PLE_DIM = 256
ROPE_THETA = 10000.0
EPS = 1e-6
NEG_INF = -1e30

SWA_HEAD_DIM = 64
SWA_HEADS = D_MODEL // SWA_HEAD_DIM
SWA_KV_HEADS = SWA_HEADS // 8
SWA_GROUP = SWA_HEADS // SWA_KV_HEADS
SWA_WINDOW = 128

MLA_NOPE_DIM = 128
MLA_ROPE_DIM = 64
MLA_V_DIM = 128
MLA_V_AUG = MLA_V_DIM + 16
MLA_HEADS = D_MODEL // MLA_V_DIM
MLA_Q_RANK = 256
MLA_KV_RANK = 128
MLA_QK_DIM = MLA_NOPE_DIM + MLA_ROPE_DIM

N_GROUPS = 4
EXPERTS_PER_GROUP = 4
N_EXPERTS = N_GROUPS * EXPERTS_PER_GROUP
D_EXPERT = 256

LANES = 128
LOG2_E = 1.4426950408889634
VMEM_LIMIT = 56 * 1024 * 1024

MLA_KV_CHUNK = 512
SWA_TQ = 1024
SWA_HEADS_PER_UNIT = 4
MLA_TQ = 1024
MLA_TK = 512
TAIL_TM = 512
MOE_CHUNK = 160

_C_QA = 0
_C_KV = _C_QA + SWA_HEADS * SWA_HEAD_DIM
_C_CQ = _C_KV + 2 * SWA_KV_HEADS * SWA_HEAD_DIM
_C_CKVR = _C_CQ + MLA_Q_RANK
_C_GATES = _C_CKVR + MLA_KV_RANK + MLA_ROPE_DIM

BF16 = jnp.bfloat16
F32 = jnp.float32


def _dot(a, b):
    return jnp.dot(a, b, preferred_element_type=F32)


def _dot_nt(a, b):
    return lax.dot_general(a, b, (((1,), (1,)), ((), ())), preferred_element_type=F32)


def _rms(x, g):
    r = lax.rsqrt(jnp.mean(x * x, axis=-1, keepdims=True) + EPS)
    return x * r * g


def _sigmoid(x):
    return 1.0 / (1.0 + jnp.exp(-x))


def _rope_lanes(x, cos, sin_signed):
    lane = lax.broadcasted_iota(jnp.int32, x.shape, 1)
    upper = (lane & (SWA_HEAD_DIM // 2)) != 0
    partner = jnp.where(upper, pltpu.roll(x, SWA_HEAD_DIM // 2, 1),
                        pltpu.roll(x, LANES - SWA_HEAD_DIM // 2, 1))
    return x * cos + partner * sin_signed


def _inproj_kernel(x_ref, g_ref, w_ref, cos_ref, sin_ref, cost_ref, sint_ref, gq_ref,
                   wuqt_ref, gkv_ref, wuk_ref, wuvt_ref, qa_ref, ka_ref, va_ref, ga_ref, gb_ref,
                   qm_ref, km_ref, vt_ref, wg_ref):
    @pl.when(pl.program_id(0) == 0)
    def _():
        wg_ref[...] = w_ref[:, _C_GATES:]

    xn = _rms(x_ref[...], g_ref[...]).astype(BF16)
    cos = cos_ref[...]
    sin = sin_ref[...]
    tm = xn.shape[0]

    cq = _rms(_dot(xn, w_ref[:, _C_CQ:_C_CKVR]), gq_ref[...])
    ckvr = _dot(xn, w_ref[:, _C_CKVR:_C_GATES])
    ckv = _rms(ckvr[:, :MLA_KV_RANK], gkv_ref[...])
    kr = jnp.concatenate(
        [ckvr[:, MLA_KV_RANK:], jnp.zeros((tm, LANES - MLA_ROPE_DIM), F32)], axis=1)
    kr = _rope_lanes(kr, cos, sin)[:, :MLA_ROPE_DIM].astype(BF16)
    qm_scale = LOG2_E / math.sqrt(MLA_QK_DIM)
    cq_t = cq.T.astype(BF16)
    ckv_t = ckv.T.astype(BF16)
    kn = _dot(ckv.astype(BF16), wuk_ref[...]).astype(BF16)
    vt_all = _dot(wuvt_ref[...].reshape(MLA_HEADS * MLA_V_DIM, MLA_KV_RANK), ckv_t)
    qt_all = _dot(wuqt_ref[...].reshape(MLA_HEADS * MLA_QK_DIM, MLA_Q_RANK), cq_t)

    qa = _dot(xn, w_ref[:, _C_QA:_C_KV])
    kv = _dot(xn, w_ref[:, _C_KV:_C_CQ])
    gate_a = _dot(xn, wg_ref[:, :D_MODEL])
    gate_b = _dot(xn, wg_ref[:, D_MODEL:])

    cos_h, sin_h = cost_ref[...], sint_ref[...]
    half = MLA_ROPE_DIM // 2
    for h in range(MLA_HEADS):
        km_ref[0, h, :, :MLA_NOPE_DIM] = kn[:, h * MLA_NOPE_DIM:(h + 1) * MLA_NOPE_DIM]
        km_ref[0, h, :, MLA_NOPE_DIM:] = kr
        vt_ref[0, h, 0, :MLA_V_DIM] = vt_all[h * MLA_V_DIM:(h + 1) * MLA_V_DIM].astype(BF16)
        vt_ref[0, h, 0, MLA_V_DIM:] = jnp.ones((MLA_V_AUG - MLA_V_DIM, tm), BF16)
        q_t = qt_all[h * MLA_QK_DIM:(h + 1) * MLA_QK_DIM]
        x1 = q_t[MLA_NOPE_DIM:MLA_NOPE_DIM + half]
        x2 = q_t[MLA_NOPE_DIM + half:]
        q_t = jnp.concatenate(
            [q_t[:MLA_NOPE_DIM], x1 * cos_h - x2 * sin_h, x2 * cos_h + x1 * sin_h], axis=0)
        qm_ref[0, h, 0] = (q_t * qm_scale).astype(BF16)

    qa_scale = LOG2_E / math.sqrt(SWA_HEAD_DIM)
    for j in range(SWA_HEADS // 2):
        q = (_rope_lanes(qa[:, j * LANES:(j + 1) * LANES], cos, sin) * qa_scale).astype(BF16)
        qa_ref[0, 2 * j] = q[:, :SWA_HEAD_DIM]
        qa_ref[0, 2 * j + 1] = q[:, SWA_HEAD_DIM:]
    k = _rope_lanes(kv[:, :LANES], cos, sin).astype(BF16)
    ka_ref[0, 0] = k[:, :SWA_HEAD_DIM]
    ka_ref[0, 1] = k[:, SWA_HEAD_DIM:]
    vt = kv[:, LANES:].T.astype(BF16)
    va_ref[0, 0] = vt[:SWA_HEAD_DIM]
    va_ref[0, 1] = vt[SWA_HEAD_DIM:]

    ga_ref[...] = _sigmoid(gate_a).astype(BF16)
    gb_ref[...] = _sigmoid(gate_b).astype(BF16)


def _inproj(x2, g_mix, w_in, cos_t, sin_t, cos_h, sin_h, g_q, w_uqt, g_kv, w_uk, w_uvt,
            batch, seq, tm):
    tokens = batch * seq
    nt = seq // tm
    const = lambda t: (0, 0)
    tok = lambda t: (t, 0)
    head = lambda t: (t // nt, 0, t % nt, 0)
    out_shape = (
        jax.ShapeDtypeStruct((batch, SWA_HEADS, seq, SWA_HEAD_DIM), BF16),
        jax.ShapeDtypeStruct((batch, SWA_KV_HEADS, seq, SWA_HEAD_DIM), BF16),
        jax.ShapeDtypeStruct((batch, SWA_KV_HEADS, SWA_HEAD_DIM, seq), BF16),
        jax.ShapeDtypeStruct((tokens, D_MODEL), BF16),
        jax.ShapeDtypeStruct((tokens, D_MODEL), BF16),
        jax.ShapeDtypeStruct((batch, MLA_HEADS, nt, MLA_QK_DIM, tm), BF16),
        jax.ShapeDtypeStruct((batch, MLA_HEADS, seq, MLA_QK_DIM), BF16),
        jax.ShapeDtypeStruct((batch, MLA_HEADS, nt, MLA_V_AUG, tm), BF16),
    )
    return pl.pallas_call(
        _inproj_kernel,
        grid=(tokens // tm,),
        in_specs=[
            pl.BlockSpec((tm, D_MODEL), tok),
            pl.BlockSpec((1, D_MODEL), const),
            pl.BlockSpec(w_in.shape, const, pipeline_mode=pl.Buffered(1)),
            pl.BlockSpec((tm, LANES), lambda t: (t % nt, 0)),
            pl.BlockSpec((tm, LANES), lambda t: (t % nt, 0)),
            pl.BlockSpec((MLA_ROPE_DIM // 2, tm), lambda t: (0, t % nt)),
            pl.BlockSpec((MLA_ROPE_DIM // 2, tm), lambda t: (0, t % nt)),
            pl.BlockSpec((1, MLA_Q_RANK), const),
            pl.BlockSpec(w_uqt.shape, lambda t: (0, 0, 0), pipeline_mode=pl.Buffered(1)),
            pl.BlockSpec((1, MLA_KV_RANK), const),
            pl.BlockSpec(w_uk.shape, const, pipeline_mode=pl.Buffered(1)),
            pl.BlockSpec(w_uvt.shape, lambda t: (0, 0, 0), pipeline_mode=pl.Buffered(1)),
        ],
        out_specs=(
            pl.BlockSpec((1, SWA_HEADS, tm, SWA_HEAD_DIM), head),
            pl.BlockSpec((1, SWA_KV_HEADS, tm, SWA_HEAD_DIM), head),
            pl.BlockSpec((1, SWA_KV_HEADS, SWA_HEAD_DIM, tm), lambda t: (t // nt, 0, 0, t % nt)),
            pl.BlockSpec((tm, D_MODEL), tok),
            pl.BlockSpec((tm, D_MODEL), tok),
            pl.BlockSpec((1, MLA_HEADS, 1, MLA_QK_DIM, tm), lambda t: (t // nt, 0, t % nt, 0, 0)),
            pl.BlockSpec((1, MLA_HEADS, tm, MLA_QK_DIM), head),
            pl.BlockSpec((1, MLA_HEADS, 1, MLA_V_AUG, tm), lambda t: (t // nt, 0, t % nt, 0, 0)),
        ),
        out_shape=out_shape,
        scratch_shapes=[pltpu.VMEM((D_MODEL, 2 * D_MODEL), BF16)],
        compiler_params=pltpu.CompilerParams(
            dimension_semantics=("arbitrary",), vmem_limit_bytes=VMEM_LIMIT),
        name="inproj",
    )(x2, g_mix, w_in, cos_t, sin_t, cos_h, sin_h, g_q, w_uqt, g_kv, w_uk, w_uvt)


def _swa_kernel(q_ref, k_ref, kp_ref, vt_ref, vtp_ref, sink_ref, o_ref, s_scr, bias_scr, *,
                n_sub):
    blk = SWA_WINDOW
    cols = s_scr.shape[2]
    hpu = cols // blk

    @pl.when((pl.program_id(0) == 0) & (pl.program_id(1) == 0))
    def _():
        kj = lax.broadcasted_iota(jnp.int32, (2 * blk, cols), 0)
        qi = lax.broadcasted_iota(jnp.int32, (2 * blk, cols), 1) & (blk - 1)
        dist = blk + qi - kj
        bias_scr[...] = jnp.where((dist >= 0) & (dist < SWA_WINDOW), 0.0, NEG_INF)

    key_row = lax.broadcasted_iota(jnp.int32, (2 * blk, 1), 0)
    no_prev = jnp.where((key_row < blk) & (pl.program_id(1) == 0), NEG_INF, 0.0)
    units = [(g, h0, n) for g in range(SWA_KV_HEADS)
             for h0 in range(g * SWA_GROUP, (g + 1) * SWA_GROUP, hpu) for n in range(n_sub)]

    def scores(u, buf):
        g, h0, n = u
        q = q_ref[0, h0:h0 + hpu, n * blk:(n + 1) * blk, :]
        k_prev = kp_ref[0, g] if n == 0 else k_ref[0, g, (n - 1) * blk:n * blk, :]
        kk = jnp.concatenate([k_prev, k_ref[0, g, n * blk:(n + 1) * blk, :]], axis=0)
        s_scr[buf] = _dot_nt(kk, q.reshape(cols, SWA_HEAD_DIM))

    def softmax_pv(u, buf):
        g, h0, n = u
        sink = sink_ref[:, h0 * blk:(h0 + hpu) * blk] * LOG2_E
        s = s_scr[buf] + bias_scr[...]
        if n == 0:
            s = s + no_prev
        m = jnp.maximum(jnp.max(s, axis=0, keepdims=True), sink)
        e = jnp.exp2(s - m).astype(BF16)
        vt_prev = vtp_ref[0, g] if n == 0 else vt_ref[0, g, :, (n - 1) * blk:n * blk]
        vvt = jnp.concatenate([vt_prev, vt_ref[0, g, :, n * blk:(n + 1) * blk]], axis=1)
        vvt = jnp.concatenate([vvt, jnp.ones((16, 2 * blk), BF16)], axis=0)
        pv = _dot(vvt, e)
        denom = pv[SWA_HEAD_DIM:SWA_HEAD_DIM + 1] + jnp.exp2(sink - m)
        o = (pv[:SWA_HEAD_DIM] / denom).T
        for hh in range(hpu):
            h = h0 + hh
            o_ref[0, n * blk:(n + 1) * blk, h * SWA_HEAD_DIM:(h + 1) * SWA_HEAD_DIM] = (
                o[hh * blk:(hh + 1) * blk, :].astype(BF16))

    scores(units[0], 0)
    for idx, u in enumerate(units):
        if idx + 1 < len(units):
            scores(units[idx + 1], (idx + 1) % 2)
        softmax_pv(u, idx % 2)


def _swa(q_a, k_a, vt_a, sink_lanes, batch, seq, tq, heads_per_unit):
    assert SWA_GROUP % heads_per_unit == 0
    n_sub = tq // SWA_WINDOW
    cur = lambda b, i: (b, 0, i, 0)
    prev = lambda b, i: (b, 0, jnp.maximum(i * n_sub - 1, 0), 0)
    return pl.pallas_call(
        functools.partial(_swa_kernel, n_sub=n_sub),
        grid=(batch, seq // tq),
        in_specs=[
            pl.BlockSpec((1, SWA_HEADS, tq, SWA_HEAD_DIM), cur),
            pl.BlockSpec((1, SWA_KV_HEADS, tq, SWA_HEAD_DIM), cur),
            pl.BlockSpec((1, SWA_KV_HEADS, SWA_WINDOW, SWA_HEAD_DIM), prev),
            pl.BlockSpec((1, SWA_KV_HEADS, SWA_HEAD_DIM, tq), lambda b, i: (b, 0, 0, i)),
            pl.BlockSpec((1, SWA_KV_HEADS, SWA_HEAD_DIM, SWA_WINDOW),
                         lambda b, i: (b, 0, 0, jnp.maximum(i * n_sub - 1, 0))),
            pl.BlockSpec(sink_lanes.shape, lambda b, i: (0, 0)),
        ],
        out_specs=pl.BlockSpec((1, tq, D_MODEL), lambda b, i: (b, i, 0)),
        out_shape=jax.ShapeDtypeStruct((batch, seq, D_MODEL), BF16),
        scratch_shapes=[pltpu.VMEM((2, 2 * SWA_WINDOW, heads_per_unit * SWA_WINDOW), F32),
                        pltpu.VMEM((2 * SWA_WINDOW, heads_per_unit * SWA_WINDOW), F32)],
        compiler_params=pltpu.CompilerParams(
            dimension_semantics=("arbitrary", "arbitrary"), vmem_limit_bytes=VMEM_LIMIT),
        name="swa",
    )(q_a, k_a, k_a, vt_a, vt_a, sink_lanes)


def _mla_kernel(qt_ref, k_ref, vt_ref, o_ref, m_scr, acc_scr, s_scr, *, tq, tk):
    ratio = tq // tk
    assert tq == ratio * tk and ratio % 2 == 0
    chunks = tk // MLA_KV_CHUNK
    q_chunks = tq // MLA_KV_CHUNK
    n_q = qt_ref.shape[2] // q_chunks

    def q_block(i, carry):
        m_scr[...] = jnp.full(m_scr.shape, NEG_INF, F32)
        acc_scr[...] = jnp.zeros(acc_scr.shape, F32)

        def scores(j, buf, q0=0, blk=i):
            k = k_ref[0, 0, pl.ds(pl.multiple_of(j * tk, tk), tk), :]
            for c in range(q0 // MLA_KV_CHUNK, q_chunks):
                s_scr[buf, :, c * MLA_KV_CHUNK:(c + 1) * MLA_KV_CHUNK] = _dot(
                    k, qt_ref[0, 0, blk * q_chunks + c])

        def softmax_pv(j, buf, q0=0, key_offset=None):
            s = s_scr[buf, :, q0:]
            if key_offset is not None:
                kpos = lax.broadcasted_iota(jnp.int32, s.shape, 0) + key_offset
                qpos = lax.broadcasted_iota(jnp.int32, s.shape, 1) + q0
                s = jnp.where(kpos <= qpos, s, NEG_INF)
            m_old = m_scr[:, q0:]
            m_new = jnp.maximum(m_old, jnp.max(s, axis=0, keepdims=True))
            alpha = jnp.exp2(m_old - m_new)
            pb = jnp.exp2(s - m_new).astype(BF16)
            pv = _dot(vt_ref[0, 0, j * chunks], pb[:MLA_KV_CHUNK])
            for c in range(1, chunks):
                pv += _dot(vt_ref[0, 0, j * chunks + c],
                           pb[c * MLA_KV_CHUNK:(c + 1) * MLA_KV_CHUNK])
            acc_scr[:, q0:] = alpha * acc_scr[:, q0:] + pv
            m_scr[:, q0:] = m_new

        @pl.when(i == 0)
        def _():
            scores(0, 0)

        def pair(jj, c):
            t = 2 * jj
            scores(t + 1, 1)
            softmax_pv(t, 0)
            scores(t + 2, 0)
            softmax_pv(t + 1, 1)
            return c

        lax.fori_loop(0, (ratio // 2) * i, pair, 0)
        for d in range(ratio):
            if d + 1 < ratio:
                scores(ratio * i + d + 1, (d + 1) % 2, q0=(d + 1) * tk)
            else:
                scores(0, 0, blk=jnp.minimum(i + 1, n_q - 1))
            softmax_pv(ratio * i + d, d % 2, q0=d * tk, key_offset=d * tk)

        o = acc_scr[:MLA_V_DIM] / acc_scr[MLA_V_DIM:MLA_V_DIM + 1]
        o_ref[0, pl.ds(pl.multiple_of(i * tq, tq), tq), :] = o.T.astype(o_ref.dtype)
        return carry

    lax.fori_loop(0, n_q, q_block, 0)


def _mla(q_m, k_m, vt_m, batch, seq, tq, tk):
    return pl.pallas_call(
        functools.partial(_mla_kernel, tq=tq, tk=tk),
        grid=(batch, MLA_HEADS),
        in_specs=[
            pl.BlockSpec((1, 1, seq // MLA_KV_CHUNK, MLA_QK_DIM, MLA_KV_CHUNK),
                         lambda b, h: (b, h, 0, 0, 0)),
            pl.BlockSpec((1, 1, seq, MLA_QK_DIM), lambda b, h: (b, h, 0, 0)),
            pl.BlockSpec((1, 1, seq // MLA_KV_CHUNK, MLA_V_AUG, MLA_KV_CHUNK),
                         lambda b, h: (b, h, 0, 0, 0)),
        ],
        out_specs=pl.BlockSpec((1, seq, MLA_V_DIM), lambda b, h: (b, 0, h)),
        out_shape=jax.ShapeDtypeStruct((batch, seq, D_MODEL), BF16),
        scratch_shapes=[
            pltpu.VMEM((1, tq), F32),
            pltpu.VMEM((MLA_V_AUG, tq), F32),
            pltpu.VMEM((2, tk, tq), F32),
        ],
        compiler_params=pltpu.CompilerParams(
            dimension_semantics=("arbitrary", "arbitrary"), vmem_limit_bytes=VMEM_LIMIT),
        name="mla",
    )(q_m, k_m, vt_m)


def _mix_and_route(x_ref, oa_ref, ob_ref, ga_ref, gb_ref, wo_ref, gf_ref, wr_ref, br_ref):
    merged = (ga_ref[...].astype(F32) * oa_ref[...].astype(F32)
              + gb_ref[...].astype(F32) * ob_ref[...].astype(F32)).astype(BF16)
    h = x_ref[...] + _dot(merged, wo_ref[...])
    xn = _rms(h, gf_ref[...])

    tm = xn.shape[0]
    x_hi = xn.astype(BF16)
    x_lo = (xn - x_hi.astype(F32)).astype(BF16)
    prod_hi = _dot(x_hi, wr_ref[...])
    prod_lo = _dot(x_lo, wr_ref[...])
    lg = (prod_hi[:, :LANES] + prod_hi[:, LANES:] + prod_lo[:, :LANES] + prod_lo[:, LANES:]
          + br_ref[...])
    lg_t = lg.T
    big = jnp.int32(LANES)
    g_row = lax.broadcasted_iota(jnp.int32, (8, tm), 0)
    gl = jnp.where(g_row < N_GROUPS, lg_t[N_EXPERTS:N_EXPERTS + 8], NEG_INF)
    g_max = jnp.max(gl, axis=0, keepdims=True)
    g_w = 1.0 / jnp.sum(jnp.exp(gl - g_max), axis=0, keepdims=True)
    gidx = jnp.min(jnp.where(gl == g_max, g_row, big), axis=0, keepdims=True)

    row = lax.broadcasted_iota(jnp.int32, (N_EXPERTS, tm), 0)
    sel = (row >= gidx * EXPERTS_PER_GROUP) & (row < (gidx + 1) * EXPERTS_PER_GROUP)
    el = jnp.where(sel, lg_t[:N_EXPERTS], NEG_INF)
    e_exp = jnp.exp(el - jnp.max(el, axis=0, keepdims=True))
    e_prob = e_exp / jnp.sum(e_exp, axis=0, keepdims=True)
    p1 = jnp.max(e_prob, axis=0, keepdims=True)
    i1 = jnp.min(jnp.where(sel & (e_prob == p1), row, big), axis=0, keepdims=True)
    rest = sel & (row != i1)
    p2 = jnp.max(jnp.where(rest, e_prob, -1.0), axis=0, keepdims=True)
    i2 = jnp.min(jnp.where(rest & (e_prob == p2), row, big), axis=0, keepdims=True)
    top_sum = p1 + p2
    comb = jnp.where(row == i1, p1 / top_sum, jnp.where(row == i2, p2 / top_sum, 0.0)) * g_w
    gid_rows = jnp.where(g_row == 0, gidx.astype(F32), 0.0)
    comb_t = jnp.concatenate(
        [comb, gid_rows, jnp.zeros((LANES - N_EXPERTS - 8, tm), F32)], axis=0)
    return h, x_hi, comb_t


def _block_tail_kernel(x_ref, oa_ref, ob_ref, ga_ref, gb_ref, wout_ref, gf_ref, wr_ref, br_ref,
                       wi_ref, wo_ref, p_ref, gp_ref, wg_ref, wp_ref, gfin_ref, o_ref, *, chunk):
    tm = x_ref.shape[0]
    h, xn, comb_t = _mix_and_route(
        x_ref, oa_ref, ob_ref, ga_ref, gb_ref, wout_ref, gf_ref, wr_ref, br_ref)
    o_ref[...] = h
    comb = comb_t.T
    comb_hi = comb.astype(BF16)
    comb_lo = (comb - comb_hi.astype(F32)).astype(BF16)
    comb_hl = jnp.concatenate([comb_hi, comb_lo], axis=1)
    gid_col = comb[:, N_EXPERTS:N_EXPERTS + 1]
    gid_row = comb_t[N_EXPERTS:N_EXPERTS + 1, :]
    member_col = gid_col == lax.broadcasted_iota(jnp.int32, (tm, LANES), 1).astype(F32)
    member_row = gid_row == lax.broadcasted_iota(jnp.int32, (8, tm), 0).astype(F32)
    r_i = lax.broadcasted_iota(jnp.int32, (tm, tm), 0)
    c_i = lax.broadcasted_iota(jnp.int32, (tm, tm), 1)
    ranks_row = _dot(member_row.astype(BF16), (r_i < c_i).astype(BF16))
    ranks_col = _dot((r_i > c_i).astype(BF16), member_col.astype(BF16))
    pad = 2 * LANES
    never = -4.0 * tm
    slot_row = lax.broadcasted_iota(jnp.int32, (chunk, tm), 0).astype(F32)
    slot_col = lax.broadcasted_iota(jnp.int32, (tm, pad), 1).astype(F32)
    slot_col = jnp.where(slot_col < float(chunk), slot_col, never)

    groups = range(N_GROUPS)
    in_rows = [member_row[g:g + 1] for g in groups]
    rank_rows = [jnp.where(in_rows[g], ranks_row[g:g + 1], -2.0 * never) for g in groups]
    rank_cols = [jnp.where(member_col[:, g:g + 1], ranks_col[:, g:g + 1], -2.0 * never)
                 for g in groups]

    def one_hots(g, base):
        gather = jnp.where(rank_rows[g] - base == slot_row, 1.0, 0.0).astype(BF16)
        scatter = jnp.where(rank_cols[g] - base == slot_col, 1.0, 0.0).astype(BF16)
        return gather, scatter

    def compact(gather):
        xc = _dot(gather, xn).astype(BF16)
        wc = _dot(gather, comb_hl)
        return xc, wc[:, :LANES] + wc[:, LANES:]

    def experts_in(g, xc):
        return [_dot(xc, wi_ref[g * EXPERTS_PER_GROUP + j]) for j in range(EXPERTS_PER_GROUP)]

    def activate(g, hids, wc):
        acts = []
        for j, hid in enumerate(hids):
            e = g * EXPERTS_PER_GROUP + j
            gate = hid[:, :D_EXPERT]
            act = (gate * _sigmoid(gate)) * hid[:, D_EXPERT:] * wc[:, e:e + 1]
            acts.append(act.astype(BF16))
        return acts

    def experts_out(g, acts):
        yc = _dot(acts[0], wo_ref[g * EXPERTS_PER_GROUP])
        for j in range(1, EXPERTS_PER_GROUP):
            yc = yc + _dot(acts[j], wo_ref[g * EXPERTS_PER_GROUP + j])
        return jnp.concatenate(
            [yc.astype(BF16), jnp.zeros((pad - chunk, D_MODEL), BF16)], axis=0)

    hots = [one_hots(g, 0.0) for g in groups]
    compacted = [compact(hots[g][0]) for g in groups]
    hidden = [experts_in(g, compacted[g][0]) for g in groups]
    acts = [activate(g, hidden[g], compacted[g][1]) for g in groups]
    outs = [experts_out(g, acts[g]) for g in groups]
    o_ref[...] += _dot(jnp.concatenate([hots[g][1] for g in groups], axis=1),
                       jnp.concatenate(outs, axis=0))

    for g in groups:
        count = jnp.sum(in_rows[g].astype(jnp.int32))
        n_chunks = (count + (chunk - 1)) // chunk

        def chunk_body(c, carry, g=g):
            gather, scatter = one_hots(g, (c * chunk).astype(F32))
            xc, wc = compact(gather)
            yc = experts_out(g, activate(g, experts_in(g, xc), wc))
            o_ref[...] += _dot(scatter, yc)
            return carry

        lax.fori_loop(1, n_chunks, chunk_body, 0)

    h = o_ref[...]
    gate = _sigmoid(_dot(_rms(h, gp_ref[...]).astype(BF16), wg_ref[...]))
    h = h + gate * _dot(p_ref[...].astype(BF16), wp_ref[...])
    o_ref[...] = _rms(h, gfin_ref[...])


def _block_tail(x2, o_a, o_b, g_a, g_b, w_o, g_ffn, w_r, b_r, w_ei, w_eo, p2, g_ple, w_pg, w_pp,
                g_final, tm, chunk):
    tokens = x2.shape[0]
    tok = lambda t: (t, 0)
    const = lambda t: (0, 0)
    return pl.pallas_call(
        functools.partial(_block_tail_kernel, chunk=chunk),
        grid=(tokens // tm,),
        in_specs=[
            pl.BlockSpec((tm, D_MODEL), tok),
            pl.BlockSpec((tm, D_MODEL), tok),
            pl.BlockSpec((tm, D_MODEL), tok),
            pl.BlockSpec((tm, D_MODEL), tok),
            pl.BlockSpec((tm, D_MODEL), tok),
            pl.BlockSpec((D_MODEL, D_MODEL), const, pipeline_mode=pl.Buffered(1)),
            pl.BlockSpec((1, D_MODEL), const),
            pl.BlockSpec((D_MODEL, 2 * LANES), const, pipeline_mode=pl.Buffered(1)),
            pl.BlockSpec((1, LANES), const),
            pl.BlockSpec(w_ei.shape, lambda t: (0, 0, 0), pipeline_mode=pl.Buffered(1)),
            pl.BlockSpec(w_eo.shape, lambda t: (0, 0, 0), pipeline_mode=pl.Buffered(1)),
            pl.BlockSpec((tm, PLE_DIM), tok),
            pl.BlockSpec((1, D_MODEL), const),
            pl.BlockSpec((D_MODEL, D_MODEL), const, pipeline_mode=pl.Buffered(1)),
            pl.BlockSpec((PLE_DIM, D_MODEL), const, pipeline_mode=pl.Buffered(1)),
            pl.BlockSpec((1, D_MODEL), const),
        ],
        out_specs=pl.BlockSpec((tm, D_MODEL), tok),
        out_shape=jax.ShapeDtypeStruct((tokens, D_MODEL), F32),
        compiler_params=pltpu.CompilerParams(
            dimension_semantics=("arbitrary",), vmem_limit_bytes=VMEM_LIMIT),
        name="block_tail",
    )(x2, o_a, o_b, g_a, g_b, w_o, g_ffn, w_r, b_r, w_ei, w_eo, p2, g_ple, w_pg, w_pp, g_final)


def _rope_tables(seq):
    half = SWA_HEAD_DIM // 2
    pos = jnp.arange(seq, dtype=F32)
    inv = ROPE_THETA ** (-jnp.arange(0, SWA_HEAD_DIM, 2, dtype=F32) / SWA_HEAD_DIM)
    ang = pos[:, None] * inv[None, :]
    cos, sin = jnp.cos(ang), jnp.sin(ang)
    reps = LANES // SWA_HEAD_DIM
    cos_t = jnp.broadcast_to(cos[:, None, :], (seq, 2 * reps, half)).reshape(seq, LANES)
    sin_t = jnp.broadcast_to(
        jnp.stack([-sin, sin], axis=1)[:, None], (seq, reps, 2, half)).reshape(seq, LANES)
    return cos_t, sin_t, cos.T, sin.T


def _layer(h2d, p2d, g_mix, w_in, sinks, g_q, w_uq, g_kv, w_ukv, w_out, g_ffn, w_rg, b_rg,
           w_re, b_re, w_ei, w_eo, g_ple, w_pg, w_pp, g_final, rope, batch, seq):
    cos_t, sin_t, cos_h, sin_h = rope
    assert SWA_HEAD_DIM == MLA_ROPE_DIM, "one rotary table serves both mixers"
    assert w_in.shape[1] == _C_GATES + 2 * D_MODEL
    w_uqt = jnp.transpose(
        w_uq.reshape(MLA_Q_RANK, MLA_HEADS, MLA_QK_DIM), (1, 2, 0)).astype(BF16)
    ukv = w_ukv.reshape(MLA_KV_RANK, MLA_HEADS, MLA_NOPE_DIM + MLA_V_DIM)
    w_uk_c = ukv[:, :, :MLA_NOPE_DIM].reshape(MLA_KV_RANK, -1).astype(BF16)
    w_uvt = jnp.transpose(ukv[:, :, MLA_NOPE_DIM:], (1, 2, 0)).astype(BF16)

    q_a, k_a, v_a, g_a, g_b, q_m, k_m, v_m = _inproj(
        h2d, g_mix[None], w_in.astype(BF16), cos_t, sin_t, cos_h, sin_h, g_q[None], w_uqt,
        g_kv[None], w_uk_c, w_uvt, batch, seq, tm=MLA_KV_CHUNK)

    sink_lanes = jnp.repeat(sinks.astype(F32), SWA_WINDOW)[None]
    o_a = _swa(q_a, k_a, v_a, sink_lanes, batch, seq, tq=SWA_TQ,
               heads_per_unit=SWA_HEADS_PER_UNIT)
    o_b = _mla(q_m, k_m, v_m, batch, seq, tq=MLA_TQ, tk=MLA_TK)

    w_r = jnp.concatenate(
        [w_re, w_rg, jnp.zeros((D_MODEL, LANES - N_EXPERTS - N_GROUPS), F32)], axis=1)
    w_r_hi = w_r.astype(BF16)
    w_r_lo = (w_r - w_r_hi.astype(F32)).astype(BF16)
    w_r2 = jnp.concatenate([w_r_hi, w_r_lo], axis=1)
    b_r = jnp.concatenate(
        [b_re, b_rg, jnp.zeros((LANES - N_EXPERTS - N_GROUPS,), F32)])[None]
    tokens = batch * seq
    return _block_tail(
        h2d, o_a.reshape(tokens, D_MODEL), o_b.reshape(tokens, D_MODEL), g_a, g_b,
        w_out.astype(BF16), g_ffn[None], w_r2, b_r, w_ei.astype(BF16), w_eo.astype(BF16), p2d,
        g_ple[None], w_pg.astype(BF16), w_pp.astype(BF16), g_final[None], tm=TAIL_TM,
        chunk=MOE_CHUNK)


def kernel(x, p, g_mix, w_in, swa_sinks, mla_g_q, mla_w_uq, mla_g_kv, mla_w_ukv, w_out, g_ffn,
           w_router_group, b_router_group, w_router_expert, b_router_expert, w_expert_in,
           w_expert_out, g_ple, w_ple_gate, w_ple_proj, g_final):
    batch, seq, d = x.shape
    depth = p.shape[0]
    assert d == D_MODEL and depth == 1, "final RMSNorm is fused into the single layer"
    rope = _rope_tables(seq)
    out = _layer(
        x.reshape(batch * seq, d), p[0].reshape(batch * seq, PLE_DIM), g_mix[0], w_in[0],
        swa_sinks[0], mla_g_q[0], mla_w_uq[0], mla_g_kv[0], mla_w_ukv[0], w_out[0], g_ffn[0],
        w_router_group[0], b_router_group[0], w_router_expert[0], b_router_expert[0],
        w_expert_in[0], w_expert_out[0], g_ple[0], w_ple_gate[0], w_ple_proj[0], g_final,
        rope, batch, seq)
    return out.reshape(batch, seq, d)
```

```python
import functools
import math

import jax
import jax.numpy as jnp
from jax import lax
from jax.experimental import pallas as pl
from jax.experimental.pallas import tpu as pltpu

D_MODEL = 1024
PLE_DIM = 256
ROPE_THETA = 10000.0
EPS = 1e-6
NEG_INF = -1e30

SWA_HEAD_DIM = 64
SWA_HEADS = D_MODEL // SWA_HEAD_DIM
SWA_KV_HEADS = SWA_HEADS // 8
SWA_GROUP = SWA_HEADS // SWA_KV_HEADS
SWA_WINDOW = 128

MLA_NOPE_DIM = 128
MLA_ROPE_DIM = 64
MLA_V_DIM = 128
MLA_V_AUG = MLA_V_DIM + 16
MLA_HEADS = D_MODEL // MLA_V_DIM
MLA_Q_RANK = 256
MLA_KV_RANK = 128
MLA_QK_DIM = MLA_NOPE_DIM + MLA_ROPE_DIM

N_GROUPS = 4
EXPERTS_PER_GROUP = 4
N_EXPERTS = N_GROUPS * EXPERTS_PER_GROUP
D_EXPERT = 256

LANES = 128
LOG2_E = 1.4426950408889634
VMEM_LIMIT = 56 * 1024 * 1024

MLA_KV_CHUNK = 512
SWA_TQ = 1024
SWA_HEADS_PER_UNIT = 4
MLA_TQ = 1024
MLA_TK = 512
TAIL_TM = 512
MOE_CHUNK = 160

_C_QA = 0
_C_KV = _C_QA + SWA_HEADS * SWA_HEAD_DIM
_C_CQ = _C_KV + 2 * SWA_KV_HEADS * SWA_HEAD_DIM
_C_CKVR = _C_CQ + MLA_Q_RANK
_C_GATES = _C_CKVR + MLA_KV_RANK + MLA_ROPE_DIM

BF16 = jnp.bfloat16
F32 = jnp.float32


def _dot(a, b):
    return jnp.dot(a, b, preferred_element_type=F32)


def _dot_nt(a, b):
    return lax.dot_general(a, b, (((1,), (1,)), ((), ())), preferred_element_type=F32)


def _rms(x, g):
    r = lax.rsqrt(jnp.mean(x * x, axis=-1, keepdims=True) + EPS)
    return x * r * g


def _sigmoid(x):
    return 1.0 / (1.0 + jnp.exp(-x))


def _rope_lanes(x, cos, sin_signed):
    lane = lax.broadcasted_iota(jnp.int32, x.shape, 1)
    upper = (lane & (SWA_HEAD_DIM // 2)) != 0
    partner = jnp.where(upper, pltpu.roll(x, SWA_HEAD_DIM // 2, 1),
                        pltpu.roll(x, LANES - SWA_HEAD_DIM // 2, 1))
    return x * cos + partner * sin_signed


def _inproj_kernel(x_ref, g_ref, w_ref, cos_ref, sin_ref, cost_ref, sint_ref, gq_ref,
                   wuqt_ref, gkv_ref, wuk_ref, wuvt_ref, qa_ref, ka_ref, va_ref, ga_ref, gb_ref,
                   qm_ref, km_ref, vt_ref, wg_ref):
    @pl.when(pl.program_id(0) == 0)
    def _():
        wg_ref[...] = w_ref[:, _C_GATES:]

    xn = _rms(x_ref[...], g_ref[...]).astype(BF16)
    cos = cos_ref[...]
    sin = sin_ref[...]
    tm = xn.shape[0]

    cq = _rms(_dot(xn, w_ref[:, _C_CQ:_C_CKVR]), gq_ref[...])
    ckvr = _dot(xn, w_ref[:, _C_CKVR:_C_GATES])
    ckv = _rms(ckvr[:, :MLA_KV_RANK], gkv_ref[...])
    kr = jnp.concatenate(
        [ckvr[:, MLA_KV_RANK:], jnp.zeros((tm, LANES - MLA_ROPE_DIM), F32)], axis=1)
    kr = _rope_lanes(kr, cos, sin)[:, :MLA_ROPE_DIM].astype(BF16)
    qm_scale = LOG2_E / math.sqrt(MLA_QK_DIM)
    cq_t = cq.T.astype(BF16)
    ckv_t = ckv.T.astype(BF16)
    kn = _dot(ckv.astype(BF16), wuk_ref[...]).astype(BF16)
    vt_all = _dot(wuvt_ref[...].reshape(MLA_HEADS * MLA_V_DIM, MLA_KV_RANK), ckv_t)
    qt_all = _dot(wuqt_ref[...].reshape(MLA_HEADS * MLA_QK_DIM, MLA_Q_RANK), cq_t)

    qa = _dot(xn, w_ref[:, _C_QA:_C_KV])
    kv = _dot(xn, w_ref[:, _C_KV:_C_CQ])
    gate_a = _dot(xn, wg_ref[:, :D_MODEL])
    gate_b = _dot(xn, wg_ref[:, D_MODEL:])

    cos_h, sin_h = cost_ref[...], sint_ref[...]
    half = MLA_ROPE_DIM // 2
    for h in range(MLA_HEADS):
        km_ref[0, h, :, :MLA_NOPE_DIM] = kn[:, h * MLA_NOPE_DIM:(h + 1) * MLA_NOPE_DIM]
        km_ref[0, h, :, MLA_NOPE_DIM:] = kr
        vt_ref[0, h, 0, :MLA_V_DIM] = vt_all[h * MLA_V_DIM:(h + 1) * MLA_V_DIM].astype(BF16)
        vt_ref[0, h, 0, MLA_V_DIM:] = jnp.ones((MLA_V_AUG - MLA_V_DIM, tm), BF16)
        q_t = qt_all[h * MLA_QK_DIM:(h + 1) * MLA_QK_DIM]
        x1 = q_t[MLA_NOPE_DIM:MLA_NOPE_DIM + half]
        x2 = q_t[MLA_NOPE_DIM + half:]
        q_t = jnp.concatenate(
            [q_t[:MLA_NOPE_DIM], x1 * cos_h - x2 * sin_h, x2 * cos_h + x1 * sin_h], axis=0)
        qm_ref[0, h, 0] = (q_t * qm_scale).astype(BF16)

    qa_scale = LOG2_E / math.sqrt(SWA_HEAD_DIM)
    for j in range(SWA_HEADS // 2):
        q = (_rope_lanes(qa[:, j * LANES:(j + 1) * LANES], cos, sin) * qa_scale).astype(BF16)
        qa_ref[0, 2 * j] = q[:, :SWA_HEAD_DIM]
        qa_ref[0, 2 * j + 1] = q[:, SWA_HEAD_DIM:]
    k = _rope_lanes(kv[:, :LANES], cos, sin).astype(BF16)
    ka_ref[0, 0] = k[:, :SWA_HEAD_DIM]
    ka_ref[0, 1] = k[:, SWA_HEAD_DIM:]
    vt = kv[:, LANES:].T.astype(BF16)
    va_ref[0, 0] = vt[:SWA_HEAD_DIM]
    va_ref[0, 1] = vt[SWA_HEAD_DIM:]

    ga_ref[...] = _sigmoid(gate_a).astype(BF16)
    gb_ref[...] = _sigmoid(gate_b).astype(BF16)


def _inproj(x2, g_mix, w_in, cos_t, sin_t, cos_h, sin_h, g_q, w_uqt, g_kv, w_uk, w_uvt,
            batch, seq, tm):
    tokens = batch * seq
    nt = seq // tm
    const = lambda t: (0, 0)
    tok = lambda t: (t, 0)
    head = lambda t: (t // nt, 0, t % nt, 0)
    out_shape = (
        jax.ShapeDtypeStruct((batch, SWA_HEADS, seq, SWA_HEAD_DIM), BF16),
        jax.ShapeDtypeStruct((batch, SWA_KV_HEADS, seq, SWA_HEAD_DIM), BF16),
        jax.ShapeDtypeStruct((batch, SWA_KV_HEADS, SWA_HEAD_DIM, seq), BF16),
        jax.ShapeDtypeStruct((tokens, D_MODEL), BF16),
        jax.ShapeDtypeStruct((tokens, D_MODEL), BF16),
        jax.ShapeDtypeStruct((batch, MLA_HEADS, nt, MLA_QK_DIM, tm), BF16),
        jax.ShapeDtypeStruct((batch, MLA_HEADS, seq, MLA_QK_DIM), BF16),
        jax.ShapeDtypeStruct((batch, MLA_HEADS, nt, MLA_V_AUG, tm), BF16),
    )
    return pl.pallas_call(
        _inproj_kernel,
        grid=(tokens // tm,),
        in_specs=[
            pl.BlockSpec((tm, D_MODEL), tok),
            pl.BlockSpec((1, D_MODEL), const),
            pl.BlockSpec(w_in.shape, const, pipeline_mode=pl.Buffered(1)),
            pl.BlockSpec((tm, LANES), lambda t: (t % nt, 0)),
            pl.BlockSpec((tm, LANES), lambda t: (t % nt, 0)),
            pl.BlockSpec((MLA_ROPE_DIM // 2, tm), lambda t: (0, t % nt)),
            pl.BlockSpec((MLA_ROPE_DIM // 2, tm), lambda t: (0, t % nt)),
            pl.BlockSpec((1, MLA_Q_RANK), const),
            pl.BlockSpec(w_uqt.shape, lambda t: (0, 0, 0), pipeline_mode=pl.Buffered(1)),
            pl.BlockSpec((1, MLA_KV_RANK), const),
            pl.BlockSpec(w_uk.shape, const, pipeline_mode=pl.Buffered(1)),
            pl.BlockSpec(w_uvt.shape, lambda t: (0, 0, 0), pipeline_mode=pl.Buffered(1)),
        ],
        out_specs=(
            pl.BlockSpec((1, SWA_HEADS, tm, SWA_HEAD_DIM), head),
            pl.BlockSpec((1, SWA_KV_HEADS, tm, SWA_HEAD_DIM), head),
            pl.BlockSpec((1, SWA_KV_HEADS, SWA_HEAD_DIM, tm), lambda t: (t // nt, 0, 0, t % nt)),
            pl.BlockSpec((tm, D_MODEL), tok),
            pl.BlockSpec((tm, D_MODEL), tok),
            pl.BlockSpec((1, MLA_HEADS, 1, MLA_QK_DIM, tm), lambda t: (t // nt, 0, t % nt, 0, 0)),
            pl.BlockSpec((1, MLA_HEADS, tm, MLA_QK_DIM), head),
            pl.BlockSpec((1, MLA_HEADS, 1, MLA_V_AUG, tm), lambda t: (t // nt, 0, t % nt, 0, 0)),
        ),
        out_shape=out_shape,
        scratch_shapes=[pltpu.VMEM((D_MODEL, 2 * D_MODEL), BF16)],
        compiler_params=pltpu.CompilerParams(
            dimension_semantics=("arbitrary",), vmem_limit_bytes=VMEM_LIMIT),
        name="inproj",
    )(x2, g_mix, w_in, cos_t, sin_t, cos_h, sin_h, g_q, w_uqt, g_kv, w_uk, w_uvt)


def _swa_kernel(q_ref, k_ref, kp_ref, vt_ref, vtp_ref, sink_ref, o_ref, s_scr, bias_scr, *,
                n_sub):
    blk = SWA_WINDOW
    cols = s_scr.shape[2]
    hpu = cols // blk

    @pl.when((pl.program_id(0) == 0) & (pl.program_id(1) == 0))
    def _():
        kj = lax.broadcasted_iota(jnp.int32, (2 * blk, cols), 0)
        qi = lax.broadcasted_iota(jnp.int32, (2 * blk, cols), 1) & (blk - 1)
        dist = blk + qi - kj
        bias_scr[...] = jnp.where((dist >= 0) & (dist < SWA_WINDOW), 0.0, NEG_INF)

    key_row = lax.broadcasted_iota(jnp.int32, (2 * blk, 1), 0)
    no_prev = jnp.where((key_row < blk) & (pl.program_id(1) == 0), NEG_INF, 0.0)
    units = [(g, h0, n) for g in range(SWA_KV_HEADS)
             for h0 in range(g * SWA_GROUP, (g + 1) * SWA_GROUP, hpu) for n in range(n_sub)]

    def scores(u, buf):
        g, h0, n = u
        q = q_ref[0, h0:h0 + hpu, n * blk:(n + 1) * blk, :]
        k_prev = kp_ref[0, g] if n == 0 else k_ref[0, g, (n - 1) * blk:n * blk, :]
        kk = jnp.concatenate([k_prev, k_ref[0, g, n * blk:(n + 1) * blk, :]], axis=0)
        s_scr[buf] = _dot_nt(kk, q.reshape(cols, SWA_HEAD_DIM))

    def softmax_pv(u, buf):
        g, h0, n = u
        sink = sink_ref[:, h0 * blk:(h0 + hpu) * blk] * LOG2_E
        s = s_scr[buf] + bias_scr[...]
        if n == 0:
            s = s + no_prev
        m = jnp.maximum(jnp.max(s, axis=0, keepdims=True), sink)
        e = jnp.exp2(s - m).astype(BF16)
        vt_prev = vtp_ref[0, g] if n == 0 else vt_ref[0, g, :, (n - 1) * blk:n * blk]
        vvt = jnp.concatenate([vt_prev, vt_ref[0, g, :, n * blk:(n + 1) * blk]], axis=1)
        vvt = jnp.concatenate([vvt, jnp.ones((16, 2 * blk), BF16)], axis=0)
        pv = _dot(vvt, e)
        denom = pv[SWA_HEAD_DIM:SWA_HEAD_DIM + 1] + jnp.exp2(sink - m)
        o = (pv[:SWA_HEAD_DIM] / denom).T
        for hh in range(hpu):
            h = h0 + hh
            o_ref[0, n * blk:(n + 1) * blk, h * SWA_HEAD_DIM:(h + 1) * SWA_HEAD_DIM] = (
                o[hh * blk:(hh + 1) * blk, :].astype(BF16))

    scores(units[0], 0)
    for idx, u in enumerate(units):
        if idx + 1 < len(units):
            scores(units[idx + 1], (idx + 1) % 2)
        softmax_pv(u, idx % 2)


def _swa(q_a, k_a, vt_a, sink_lanes, batch, seq, tq, heads_per_unit):
    assert SWA_GROUP % heads_per_unit == 0
    n_sub = tq // SWA_WINDOW
    cur = lambda b, i: (b, 0, i, 0)
    prev = lambda b, i: (b, 0, jnp.maximum(i * n_sub - 1, 0), 0)
    return pl.pallas_call(
        functools.partial(_swa_kernel, n_sub=n_sub),
        grid=(batch, seq // tq),
        in_specs=[
            pl.BlockSpec((1, SWA_HEADS, tq, SWA_HEAD_DIM), cur),
            pl.BlockSpec((1, SWA_KV_HEADS, tq, SWA_HEAD_DIM), cur),
            pl.BlockSpec((1, SWA_KV_HEADS, SWA_WINDOW, SWA_HEAD_DIM), prev),
            pl.BlockSpec((1, SWA_KV_HEADS, SWA_HEAD_DIM, tq), lambda b, i: (b, 0, 0, i)),
            pl.BlockSpec((1, SWA_KV_HEADS, SWA_HEAD_DIM, SWA_WINDOW),
                         lambda b, i: (b, 0, 0, jnp.maximum(i * n_sub - 1, 0))),
            pl.BlockSpec(sink_lanes.shape, lambda b, i: (0, 0)),
        ],
        out_specs=pl.BlockSpec((1, tq, D_MODEL), lambda b, i: (b, i, 0)),
        out_shape=jax.ShapeDtypeStruct((batch, seq, D_MODEL), BF16),
        scratch_shapes=[pltpu.VMEM((2, 2 * SWA_WINDOW, heads_per_unit * SWA_WINDOW), F32),
                        pltpu.VMEM((2 * SWA_WINDOW, heads_per_unit * SWA_WINDOW), F32)],
        compiler_params=pltpu.CompilerParams(
            dimension_semantics=("arbitrary", "arbitrary"), vmem_limit_bytes=VMEM_LIMIT),
        name="swa",
    )(q_a, k_a, k_a, vt_a, vt_a, sink_lanes)


def _mla_kernel(qt_ref, k_ref, vt_ref, o_ref, m_scr, acc_scr, s_scr, *, tq, tk):
    ratio = tq // tk
    assert tq == ratio * tk and ratio % 2 == 0
    chunks = tk // MLA_KV_CHUNK
    q_chunks = tq // MLA_KV_CHUNK
    n_q = qt_ref.shape[2] // q_chunks

    def q_block(i, carry):
        m_scr[...] = jnp.full(m_scr.shape, NEG_INF, F32)
        acc_scr[...] = jnp.zeros(acc_scr.shape, F32)

        def scores(j, buf, q0=0, blk=i):
            k = k_ref[0, 0, pl.ds(pl.multiple_of(j * tk, tk), tk), :]
            for c in range(q0 // MLA_KV_CHUNK, q_chunks):
                s_scr[buf, :, c * MLA_KV_CHUNK:(c + 1) * MLA_KV_CHUNK] = _dot(
                    k, qt_ref[0, 0, blk * q_chunks + c])

        def softmax_pv(j, buf, q0=0, key_offset=None):
            s = s_scr[buf, :, q0:]
            if key_offset is not None:
                kpos = lax.broadcasted_iota(jnp.int32, s.shape, 0) + key_offset
                qpos = lax.broadcasted_iota(jnp.int32, s.shape, 1) + q0
                s = jnp.where(kpos <= qpos, s, NEG_INF)
            m_old = m_scr[:, q0:]
            m_new = jnp.maximum(m_old, jnp.max(s, axis=0, keepdims=True))
            alpha = jnp.exp2(m_old - m_new)
            pb = jnp.exp2(s - m_new).astype(BF16)
            pv = _dot(vt_ref[0, 0, j * chunks], pb[:MLA_KV_CHUNK])
            for c in range(1, chunks):
                pv += _dot(vt_ref[0, 0, j * chunks + c],
                           pb[c * MLA_KV_CHUNK:(c + 1) * MLA_KV_CHUNK])
            acc_scr[:, q0:] = alpha * acc_scr[:, q0:] + pv
            m_scr[:, q0:] = m_new

        @pl.when(i == 0)
        def _():
            scores(0, 0)

        def pair(jj, c):
            t = 2 * jj
            scores(t + 1, 1)
            softmax_pv(t, 0)
            scores(t + 2, 0)
            softmax_pv(t + 1, 1)
            return c

        lax.fori_loop(0, (ratio // 2) * i, pair, 0)
        for d in range(ratio):
            if d + 1 < ratio:
                scores(ratio * i + d + 1, (d + 1) % 2, q0=(d + 1) * tk)
            else:
                scores(0, 0, blk=jnp.minimum(i + 1, n_q - 1))
            softmax_pv(ratio * i + d, d % 2, q0=d * tk, key_offset=d * tk)

        o = acc_scr[:MLA_V_DIM] / acc_scr[MLA_V_DIM:MLA_V_DIM + 1]
        o_ref[0, pl.ds(pl.multiple_of(i * tq, tq), tq), :] = o.T.astype(o_ref.dtype)
        return carry

    lax.fori_loop(0, n_q, q_block, 0)


def _mla(q_m, k_m, vt_m, batch, seq, tq, tk):
    return pl.pallas_call(
        functools.partial(_mla_kernel, tq=tq, tk=tk),
        grid=(batch, MLA_HEADS),
        in_specs=[
            pl.BlockSpec((1, 1, seq // MLA_KV_CHUNK, MLA_QK_DIM, MLA_KV_CHUNK),
                         lambda b, h: (b, h, 0, 0, 0)),
            pl.BlockSpec((1, 1, seq, MLA_QK_DIM), lambda b, h: (b, h, 0, 0)),
            pl.BlockSpec((1, 1, seq // MLA_KV_CHUNK, MLA_V_AUG, MLA_KV_CHUNK),
                         lambda b, h: (b, h, 0, 0, 0)),
        ],
        out_specs=pl.BlockSpec((1, seq, MLA_V_DIM), lambda b, h: (b, 0, h)),
        out_shape=jax.ShapeDtypeStruct((batch, seq, D_MODEL), BF16),
        scratch_shapes=[
            pltpu.VMEM((1, tq), F32),
            pltpu.VMEM((MLA_V_AUG, tq), F32),
            pltpu.VMEM((2, tk, tq), F32),
        ],
        compiler_params=pltpu.CompilerParams(
            dimension_semantics=("arbitrary", "arbitrary"), vmem_limit_bytes=VMEM_LIMIT),
        name="mla",
    )(q_m, k_m, vt_m)


def _mix_and_route(x_ref, oa_ref, ob_ref, ga_ref, gb_ref, wo_ref, gf_ref, wr_ref, br_ref):
    merged = (ga_ref[...].astype(F32) * oa_ref[...].astype(F32)
              + gb_ref[...].astype(F32) * ob_ref[...].astype(F32)).astype(BF16)
    h = x_ref[...] + _dot(merged, wo_ref[...])
    xn = _rms(h, gf_ref[...])

    tm = xn.shape[0]
    x_hi = xn.astype(BF16)
    x_lo = (xn - x_hi.astype(F32)).astype(BF16)
    prod_hi = _dot(x_hi, wr_ref[...])
    prod_lo = _dot(x_lo, wr_ref[...])
    lg = (prod_hi[:, :LANES] + prod_hi[:, LANES:] + prod_lo[:, :LANES] + prod_lo[:, LANES:]
          + br_ref[...])
    lg_t = lg.T
    big = jnp.int32(LANES)
    g_row = lax.broadcasted_iota(jnp.int32, (8, tm), 0)
    gl = jnp.where(g_row < N_GROUPS, lg_t[N_EXPERTS:N_EXPERTS + 8], NEG_INF)
    g_max = jnp.max(gl, axis=0, keepdims=True)
    g_w = 1.0 / jnp.sum(jnp.exp(gl - g_max), axis=0, keepdims=True)
    gidx = jnp.min(jnp.where(gl == g_max, g_row, big), axis=0, keepdims=True)

    row = lax.broadcasted_iota(jnp.int32, (N_EXPERTS, tm), 0)
    sel = (row >= gidx * EXPERTS_PER_GROUP) & (row < (gidx + 1) * EXPERTS_PER_GROUP)
    el = jnp.where(sel, lg_t[:N_EXPERTS], NEG_INF)
    e_exp = jnp.exp(el - jnp.max(el, axis=0, keepdims=True))
    e_prob = e_exp / jnp.sum(e_exp, axis=0, keepdims=True)
    p1 = jnp.max(e_prob, axis=0, keepdims=True)
    i1 = jnp.min(jnp.where(sel & (e_prob == p1), row, big), axis=0, keepdims=True)
    rest = sel & (row != i1)
    p2 = jnp.max(jnp.where(rest, e_prob, -1.0), axis=0, keepdims=True)
    i2 = jnp.min(jnp.where(rest & (e_prob == p2), row, big), axis=0, keepdims=True)
    top_sum = p1 + p2
    comb = jnp.where(row == i1, p1 / top_sum, jnp.where(row == i2, p2 / top_sum, 0.0)) * g_w
    gid_rows = jnp.where(g_row == 0, gidx.astype(F32), 0.0)
    comb_t = jnp.concatenate(
        [comb, gid_rows, jnp.zeros((LANES - N_EXPERTS - 8, tm), F32)], axis=0)
    return h, x_hi, comb_t


def _block_tail_kernel(x_ref, oa_ref, ob_ref, ga_ref, gb_ref, wout_ref, gf_ref, wr_ref, br_ref,
                       wi_ref, wo_ref, p_ref, gp_ref, wg_ref, wp_ref, gfin_ref, o_ref, *, chunk):
    tm = x_ref.shape[0]
    h, xn, comb_t = _mix_and_route(
        x_ref, oa_ref, ob_ref, ga_ref, gb_ref, wout_ref, gf_ref, wr_ref, br_ref)
    o_ref[...] = h
    comb = comb_t.T
    comb_hi = comb.astype(BF16)
    comb_lo = (comb - comb_hi.astype(F32)).astype(BF16)
    comb_hl = jnp.concatenate([comb_hi, comb_lo], axis=1)
    gid_col = comb[:, N_EXPERTS:N_EXPERTS + 1]
    gid_row = comb_t[N_EXPERTS:N_EXPERTS + 1, :]
    member_col = gid_col == lax.broadcasted_iota(jnp.int32, (tm, LANES), 1).astype(F32)
    member_row = gid_row == lax.broadcasted_iota(jnp.int32, (8, tm), 0).astype(F32)
    r_i = lax.broadcasted_iota(jnp.int32, (tm, tm), 0)
    c_i = lax.broadcasted_iota(jnp.int32, (tm, tm), 1)
    ranks_row = _dot(member_row.astype(BF16), (r_i < c_i).astype(BF16))
    ranks_col = _dot((r_i > c_i).astype(BF16), member_col.astype(BF16))
    pad = 2 * LANES
    never = -4.0 * tm
    slot_row = lax.broadcasted_iota(jnp.int32, (chunk, tm), 0).astype(F32)
    slot_col = lax.broadcasted_iota(jnp.int32, (tm, pad), 1).astype(F32)
    slot_col = jnp.where(slot_col < float(chunk), slot_col, never)

    groups = range(N_GROUPS)
    in_rows = [member_row[g:g + 1] for g in groups]
    rank_rows = [jnp.where(in_rows[g], ranks_row[g:g + 1], -2.0 * never) for g in groups]
    rank_cols = [jnp.where(member_col[:, g:g + 1], ranks_col[:, g:g + 1], -2.0 * never)
                 for g in groups]

    def one_hots(g, base):
        gather = jnp.where(rank_rows[g] - base == slot_row, 1.0, 0.0).astype(BF16)
        scatter = jnp.where(rank_cols[g] - base == slot_col, 1.0, 0.0).astype(BF16)
        return gather, scatter

    def compact(gather):
        xc = _dot(gather, xn).astype(BF16)
        wc = _dot(gather, comb_hl)
        return xc, wc[:, :LANES] + wc[:, LANES:]

    def experts_in(g, xc):
        return [_dot(xc, wi_ref[g * EXPERTS_PER_GROUP + j]) for j in range(EXPERTS_PER_GROUP)]

    def activate(g, hids, wc):
        acts = []
        for j, hid in enumerate(hids):
            e = g * EXPERTS_PER_GROUP + j
            gate = hid[:, :D_EXPERT]
            act = (gate * _sigmoid(gate)) * hid[:, D_EXPERT:] * wc[:, e:e + 1]
            acts.append(act.astype(BF16))
        return acts

    def experts_out(g, acts):
        yc = _dot(acts[0], wo_ref[g * EXPERTS_PER_GROUP])
        for j in range(1, EXPERTS_PER_GROUP):
            yc = yc + _dot(acts[j], wo_ref[g * EXPERTS_PER_GROUP + j])
        return jnp.concatenate(
            [yc.astype(BF16), jnp.zeros((pad - chunk, D_MODEL), BF16)], axis=0)

    hots = [one_hots(g, 0.0) for g in groups]
    compacted = [compact(hots[g][0]) for g in groups]
    hidden = [experts_in(g, compacted[g][0]) for g in groups]
    acts = [activate(g, hidden[g], compacted[g][1]) for g in groups]
    outs = [experts_out(g, acts[g]) for g in groups]
    o_ref[...] += _dot(jnp.concatenate([hots[g][1] for g in groups], axis=1),
                       jnp.concatenate(outs, axis=0))

    for g in groups:
        count = jnp.sum(in_rows[g].astype(jnp.int32))
        n_chunks = (count + (chunk - 1)) // chunk

        def chunk_body(c, carry, g=g):
            gather, scatter = one_hots(g, (c * chunk).astype(F32))
            xc, wc = compact(gather)
            yc = experts_out(g, activate(g, experts_in(g, xc), wc))
            o_ref[...] += _dot(scatter, yc)
            return carry

        lax.fori_loop(1, n_chunks, chunk_body, 0)

    h = o_ref[...]
    gate = _sigmoid(_dot(_rms(h, gp_ref[...]).astype(BF16), wg_ref[...]))
    h = h + gate * _dot(p_ref[...].astype(BF16), wp_ref[...])
    o_ref[...] = _rms(h, gfin_ref[...])


def _block_tail(x2, o_a, o_b, g_a, g_b, w_o, g_ffn, w_r, b_r, w_ei, w_eo, p2, g_ple, w_pg, w_pp,
                g_final, tm, chunk):
    tokens = x2.shape[0]
    tok = lambda t: (t, 0)
    const = lambda t: (0, 0)
    return pl.pallas_call(
        functools.partial(_block_tail_kernel, chunk=chunk),
        grid=(tokens // tm,),
        in_specs=[
            pl.BlockSpec((tm, D_MODEL), tok),
            pl.BlockSpec((tm, D_MODEL), tok),
            pl.BlockSpec((tm, D_MODEL), tok),
            pl.BlockSpec((tm, D_MODEL), tok),
            pl.BlockSpec((tm, D_MODEL), tok),
            pl.BlockSpec((D_MODEL, D_MODEL), const, pipeline_mode=pl.Buffered(1)),
            pl.BlockSpec((1, D_MODEL), const),
            pl.BlockSpec((D_MODEL, 2 * LANES), const, pipeline_mode=pl.Buffered(1)),
            pl.BlockSpec((1, LANES), const),
            pl.BlockSpec(w_ei.shape, lambda t: (0, 0, 0), pipeline_mode=pl.Buffered(1)),
            pl.BlockSpec(w_eo.shape, lambda t: (0, 0, 0), pipeline_mode=pl.Buffered(1)),
            pl.BlockSpec((tm, PLE_DIM), tok),
            pl.BlockSpec((1, D_MODEL), const),
            pl.BlockSpec((D_MODEL, D_MODEL), const, pipeline_mode=pl.Buffered(1)),
            pl.BlockSpec((PLE_DIM, D_MODEL), const, pipeline_mode=pl.Buffered(1)),
            pl.BlockSpec((1, D_MODEL), const),
        ],
        out_specs=pl.BlockSpec((tm, D_MODEL), tok),
        out_shape=jax.ShapeDtypeStruct((tokens, D_MODEL), F32),
        compiler_params=pltpu.CompilerParams(
            dimension_semantics=("arbitrary",), vmem_limit_bytes=VMEM_LIMIT),
        name="block_tail",
    )(x2, o_a, o_b, g_a, g_b, w_o, g_ffn, w_r, b_r, w_ei, w_eo, p2, g_ple, w_pg, w_pp, g_final)


def _rope_tables(seq):
    half = SWA_HEAD_DIM // 2
    pos = jnp.arange(seq, dtype=F32)
    inv = ROPE_THETA ** (-jnp.arange(0, SWA_HEAD_DIM, 2, dtype=F32) / SWA_HEAD_DIM)
    ang = pos[:, None] * inv[None, :]
    cos, sin = jnp.cos(ang), jnp.sin(ang)
    reps = LANES // SWA_HEAD_DIM
    cos_t = jnp.broadcast_to(cos[:, None, :], (seq, 2 * reps, half)).reshape(seq, LANES)
    sin_t = jnp.broadcast_to(
        jnp.stack([-sin, sin], axis=1)[:, None], (seq, reps, 2, half)).reshape(seq, LANES)
    ang_t = inv[:, None] * pos[None, :]
    return cos_t, sin_t, jnp.cos(ang_t), jnp.sin(ang_t)


def _layer(h2d, p2d, g_mix, w_in, sinks, g_q, w_uq, g_kv, w_ukv, w_out, g_ffn, w_rg, b_rg,
           w_re, b_re, w_ei, w_eo, g_ple, w_pg, w_pp, g_final, rope, batch, seq):
    cos_t, sin_t, cos_h, sin_h = rope
    assert SWA_HEAD_DIM == MLA_ROPE_DIM, "one rotary table serves both mixers"
    assert w_in.shape[1] == _C_GATES + 2 * D_MODEL
    w_uqt = jnp.transpose(
        w_uq.reshape(MLA_Q_RANK, MLA_HEADS, MLA_QK_DIM), (1, 2, 0)).astype(BF16)
    ukv = w_ukv.reshape(MLA_KV_RANK, MLA_HEADS, MLA_NOPE_DIM + MLA_V_DIM)
    w_uk_c = ukv[:, :, :MLA_NOPE_DIM].reshape(MLA_KV_RANK, -1).astype(BF16)
    w_uvt = jnp.transpose(ukv[:, :, MLA_NOPE_DIM:], (1, 2, 0)).astype(BF16)

    q_a, k_a, v_a, g_a, g_b, q_m, k_m, v_m = _inproj(
        h2d, g_mix[None], w_in.astype(BF16), cos_t, sin_t, cos_h, sin_h, g_q[None], w_uqt,
        g_kv[None], w_uk_c, w_uvt, batch, seq, tm=MLA_KV_CHUNK)

    sink_lanes = jnp.repeat(sinks.astype(F32), SWA_WINDOW)[None]
    o_a = _swa(q_a, k_a, v_a, sink_lanes, batch, seq, tq=SWA_TQ,
               heads_per_unit=SWA_HEADS_PER_UNIT)
    o_b = _mla(q_m, k_m, v_m, batch, seq, tq=MLA_TQ, tk=MLA_TK)

    w_r = jnp.concatenate(
        [w_re, w_rg, jnp.zeros((D_MODEL, LANES - N_EXPERTS - N_GROUPS), F32)], axis=1)
    w_r_hi = w_r.astype(BF16)
    w_r_lo = (w_r - w_r_hi.astype(F32)).astype(BF16)
    w_r2 = jnp.concatenate([w_r_hi, w_r_lo], axis=1)
    b_r = jnp.concatenate(
        [b_re, b_rg, jnp.zeros((LANES - N_EXPERTS - N_GROUPS,), F32)])[None]
    tokens = batch * seq
    return _block_tail(
        h2d, o_a.reshape(tokens, D_MODEL), o_b.reshape(tokens, D_MODEL), g_a, g_b,
        w_out.astype(BF16), g_ffn[None], w_r2, b_r, w_ei.astype(BF16), w_eo.astype(BF16), p2d,
        g_ple[None], w_pg.astype(BF16), w_pp.astype(BF16), g_final[None], tm=TAIL_TM,
        chunk=MOE_CHUNK)


def kernel(x, p, g_mix, w_in, swa_sinks, mla_g_q, mla_w_uq, mla_g_kv, mla_w_ukv, w_out, g_ffn,
           w_router_group, b_router_group, w_router_expert, b_router_expert, w_expert_in,
           w_expert_out, g_ple, w_ple_gate, w_ple_proj, g_final):
    batch, seq, d = x.shape
    depth = p.shape[0]
    assert d == D_MODEL and depth == 1, "final RMSNorm is fused into the single layer"
    rope = _rope_tables(seq)
    out = _layer(
        x.reshape(batch * seq, d), p[0].reshape(batch * seq, PLE_DIM), g_mix[0], w_in[0],
        swa_sinks[0], mla_g_q[0], mla_w_uq[0], mla_g_kv[0], mla_w_ukv[0], w_out[0], g_ffn[0],
        w_router_group[0], b_router_group[0], w_router_expert[0], b_router_expert[0],
        w_expert_in[0], w_expert_out[0], g_ple[0], w_ple_gate[0], w_ple_proj[0], g_final,
        rope, batch, seq)
    return out.reshape(batch, seq, d)
```

```python
import functools
import math

import jax
import jax.numpy as jnp
from jax import lax
from jax.experimental import pallas as pl
from jax.experimental.pallas import tpu as pltpu

D_MODEL = 1024
PLE_DIM = 256
ROPE_THETA = 10000.0
EPS = 1e-6
NEG_INF = -1e30

SWA_HEAD_DIM = 64
SWA_HEADS = D_MODEL // SWA_HEAD_DIM
SWA_KV_HEADS = SWA_HEADS // 8
SWA_GROUP = SWA_HEADS // SWA_KV_HEADS
SWA_WINDOW = 128

MLA_NOPE_DIM = 128
MLA_ROPE_DIM = 64
MLA_V_DIM = 128
MLA_V_AUG = MLA_V_DIM + 16
MLA_HEADS = D_MODEL // MLA_V_DIM
MLA_Q_RANK = 256
MLA_KV_RANK = 128
MLA_QK_DIM = MLA_NOPE_DIM + MLA_ROPE_DIM

N_GROUPS = 4
EXPERTS_PER_GROUP = 4
N_EXPERTS = N_GROUPS * EXPERTS_PER_GROUP
D_EXPERT = 256

LANES = 128
LOG2_E = 1.4426950408889634
VMEM_LIMIT = 56 * 1024 * 1024

MLA_KV_CHUNK = 512
SWA_TQ = 1024
SWA_HEADS_PER_UNIT = 4
MLA_TQ = 1024
MLA_TK = 512
TAIL_TM = 512
MOE_CHUNK = 160

_C_QA = 0
_C_KV = _C_QA + SWA_HEADS * SWA_HEAD_DIM
_C_CQ = _C_KV + 2 * SWA_KV_HEADS * SWA_HEAD_DIM
_C_CKVR = _C_CQ + MLA_Q_RANK
_C_GATES = _C_CKVR + MLA_KV_RANK + MLA_ROPE_DIM

BF16 = jnp.bfloat16
F32 = jnp.float32


def _dot(a, b):
    return jnp.dot(a, b, preferred_element_type=F32)


def _dot_nt(a, b):
    return lax.dot_general(a, b, (((1,), (1,)), ((), ())), preferred_element_type=F32)


def _rms(x, g):
    r = lax.rsqrt(jnp.mean(x * x, axis=-1, keepdims=True) + EPS)
    return x * r * g


def _sigmoid(x):
    return 0.5 * jnp.tanh(0.5 * x) + 0.5


def _rope_lanes(x, cos, sin_signed):
    lane = lax.broadcasted_iota(jnp.int32, x.shape, 1)
    upper = (lane & (SWA_HEAD_DIM // 2)) != 0
    partner = jnp.where(upper, pltpu.roll(x, SWA_HEAD_DIM // 2, 1),
                        pltpu.roll(x, LANES - SWA_HEAD_DIM // 2, 1))
    return x * cos + partner * sin_signed


def _inproj_kernel(x_ref, g_ref, w_ref, cos_ref, sin_ref, cost_ref, sint_ref, gq_ref,
                   wuqt_ref, gkv_ref, wuk_ref, wuvt_ref, qa_ref, ka_ref, va_ref, ga_ref, gb_ref,
                   qm_ref, km_ref, vt_ref, wg_ref):
    @pl.when(pl.program_id(0) == 0)
    def _():
        wg_ref[...] = w_ref[:, _C_GATES:]

    xn = _rms(x_ref[...], g_ref[...]).astype(BF16)
    cos = cos_ref[...]
    sin = sin_ref[...]
    tm = xn.shape[0]

    cq = _rms(_dot(xn, w_ref[:, _C_CQ:_C_CKVR]), gq_ref[...])
    ckvr = _dot(xn, w_ref[:, _C_CKVR:_C_GATES])
    ckv = _rms(ckvr[:, :MLA_KV_RANK], gkv_ref[...])
    kr = jnp.concatenate(
        [ckvr[:, MLA_KV_RANK:], jnp.zeros((tm, LANES - MLA_ROPE_DIM), F32)], axis=1)
    kr = _rope_lanes(kr, cos, sin)[:, :MLA_ROPE_DIM].astype(BF16)
    qm_scale = LOG2_E / math.sqrt(MLA_QK_DIM)
    cq_t = cq.T.astype(BF16)
    ckv_t = ckv.T.astype(BF16)
    kn = _dot(ckv.astype(BF16), wuk_ref[...]).astype(BF16)
    vt_all = _dot(wuvt_ref[...].reshape(MLA_HEADS * MLA_V_DIM, MLA_KV_RANK), ckv_t)
    qt_all = _dot(wuqt_ref[...].reshape(MLA_HEADS * MLA_QK_DIM, MLA_Q_RANK), cq_t)

    qa = _dot(xn, w_ref[:, _C_QA:_C_KV])
    kv = _dot(xn, w_ref[:, _C_KV:_C_CQ])
    gate_a = _dot(xn, wg_ref[:, :D_MODEL])
    gate_b = _dot(xn, wg_ref[:, D_MODEL:])

    cos_h, sin_h = cost_ref[...], sint_ref[...]
    half = MLA_ROPE_DIM // 2
    for h in range(MLA_HEADS):
        km_ref[0, h, :, :MLA_NOPE_DIM] = kn[:, h * MLA_NOPE_DIM:(h + 1) * MLA_NOPE_DIM]
        km_ref[0, h, :, MLA_NOPE_DIM:] = kr
        vt_ref[0, h, 0, :MLA_V_DIM] = vt_all[h * MLA_V_DIM:(h + 1) * MLA_V_DIM].astype(BF16)
        vt_ref[0, h, 0, MLA_V_DIM:] = jnp.ones((MLA_V_AUG - MLA_V_DIM, tm), BF16)
        q_t = qt_all[h * MLA_QK_DIM:(h + 1) * MLA_QK_DIM]
        x1 = q_t[MLA_NOPE_DIM:MLA_NOPE_DIM + half]
        x2 = q_t[MLA_NOPE_DIM + half:]
        q_t = jnp.concatenate(
            [q_t[:MLA_NOPE_DIM], x1 * cos_h - x2 * sin_h, x2 * cos_h + x1 * sin_h], axis=0)
        qm_ref[0, h, 0] = (q_t * qm_scale).astype(BF16)

    qa_scale = LOG2_E / math.sqrt(SWA_HEAD_DIM)
    for j in range(SWA_HEADS // 2):
        q = (_rope_lanes(qa[:, j * LANES:(j + 1) * LANES], cos, sin) * qa_scale).astype(BF16)
        qa_ref[0, 2 * j] = q[:, :SWA_HEAD_DIM]
        qa_ref[0, 2 * j + 1] = q[:, SWA_HEAD_DIM:]
    k = _rope_lanes(kv[:, :LANES], cos, sin).astype(BF16)
    ka_ref[0, 0] = k[:, :SWA_HEAD_DIM]
    ka_ref[0, 1] = k[:, SWA_HEAD_DIM:]
    vt = kv[:, LANES:].T.astype(BF16)
    va_ref[0, 0] = vt[:SWA_HEAD_DIM]
    va_ref[0, 1] = vt[SWA_HEAD_DIM:]

    ga_ref[...] = _sigmoid(gate_a).astype(BF16)
    gb_ref[...] = _sigmoid(gate_b).astype(BF16)


def _inproj(x2, g_mix, w_in, cos_t, sin_t, cos_h, sin_h, g_q, w_uqt, g_kv, w_uk, w_uvt,
            batch, seq, tm):
    tokens = batch * seq
    nt = seq // tm
    const = lambda t: (0, 0)
    tok = lambda t: (t, 0)
    head = lambda t: (t // nt, 0, t % nt, 0)
    out_shape = (
        jax.ShapeDtypeStruct((batch, SWA_HEADS, seq, SWA_HEAD_DIM), BF16),
        jax.ShapeDtypeStruct((batch, SWA_KV_HEADS, seq, SWA_HEAD_DIM), BF16),
        jax.ShapeDtypeStruct((batch, SWA_KV_HEADS, SWA_HEAD_DIM, seq), BF16),
        jax.ShapeDtypeStruct((tokens, D_MODEL), BF16),
        jax.ShapeDtypeStruct((tokens, D_MODEL), BF16),
        jax.ShapeDtypeStruct((batch, MLA_HEADS, nt, MLA_QK_DIM, tm), BF16),
        jax.ShapeDtypeStruct((batch, MLA_HEADS, seq, MLA_QK_DIM), BF16),
        jax.ShapeDtypeStruct((batch, MLA_HEADS, nt, MLA_V_AUG, tm), BF16),
    )
    return pl.pallas_call(
        _inproj_kernel,
        grid=(tokens // tm,),
        in_specs=[
            pl.BlockSpec((tm, D_MODEL), tok),
            pl.BlockSpec((1, D_MODEL), const),
            pl.BlockSpec(w_in.shape, const, pipeline_mode=pl.Buffered(1)),
            pl.BlockSpec((tm, LANES), lambda t: (t % nt, 0)),
            pl.BlockSpec((tm, LANES), lambda t: (t % nt, 0)),
            pl.BlockSpec((MLA_ROPE_DIM // 2, tm), lambda t: (0, t % nt)),
            pl.BlockSpec((MLA_ROPE_DIM // 2, tm), lambda t: (0, t % nt)),
            pl.BlockSpec((1, MLA_Q_RANK), const),
            pl.BlockSpec(w_uqt.shape, lambda t: (0, 0, 0), pipeline_mode=pl.Buffered(1)),
            pl.BlockSpec((1, MLA_KV_RANK), const),
            pl.BlockSpec(w_uk.shape, const, pipeline_mode=pl.Buffered(1)),
            pl.BlockSpec(w_uvt.shape, lambda t: (0, 0, 0), pipeline_mode=pl.Buffered(1)),
        ],
        out_specs=(
            pl.BlockSpec((1, SWA_HEADS, tm, SWA_HEAD_DIM), head),
            pl.BlockSpec((1, SWA_KV_HEADS, tm, SWA_HEAD_DIM), head),
            pl.BlockSpec((1, SWA_KV_HEADS, SWA_HEAD_DIM, tm), lambda t: (t // nt, 0, 0, t % nt)),
            pl.BlockSpec((tm, D_MODEL), tok),
            pl.BlockSpec((tm, D_MODEL), tok),
            pl.BlockSpec((1, MLA_HEADS, 1, MLA_QK_DIM, tm), lambda t: (t // nt, 0, t % nt, 0, 0)),
            pl.BlockSpec((1, MLA_HEADS, tm, MLA_QK_DIM), head),
            pl.BlockSpec((1, MLA_HEADS, 1, MLA_V_AUG, tm), lambda t: (t // nt, 0, t % nt, 0, 0)),
        ),
        out_shape=out_shape,
        scratch_shapes=[pltpu.VMEM((D_MODEL, 2 * D_MODEL), BF16)],
        compiler_params=pltpu.CompilerParams(
            dimension_semantics=("arbitrary",), vmem_limit_bytes=VMEM_LIMIT),
        name="inproj",
    )(x2, g_mix, w_in, cos_t, sin_t, cos_h, sin_h, g_q, w_uqt, g_kv, w_uk, w_uvt)


def _swa_kernel(q_ref, k_ref, kp_ref, vt_ref, vtp_ref, sink_ref, o_ref, s_scr, bias_scr, *,
                n_sub):
    blk = SWA_WINDOW
    cols = s_scr.shape[2]
    hpu = cols // blk

    @pl.when((pl.program_id(0) == 0) & (pl.program_id(1) == 0))
    def _():
        kj = lax.broadcasted_iota(jnp.int32, (2 * blk, cols), 0)
        qi = lax.broadcasted_iota(jnp.int32, (2 * blk, cols), 1) & (blk - 1)
        dist = blk + qi - kj
        bias_scr[...] = jnp.where((dist >= 0) & (dist < SWA_WINDOW), 0.0, NEG_INF)

    key_row = lax.broadcasted_iota(jnp.int32, (2 * blk, 1), 0)
    no_prev = jnp.where((key_row < blk) & (pl.program_id(1) == 0), NEG_INF, 0.0)
    units = [(g, h0, n) for g in range(SWA_KV_HEADS)
             for h0 in range(g * SWA_GROUP, (g + 1) * SWA_GROUP, hpu) for n in range(n_sub)]

    def scores(u, buf):
        g, h0, n = u
        q = q_ref[0, h0:h0 + hpu, n * blk:(n + 1) * blk, :]
        k_prev = kp_ref[0, g] if n == 0 else k_ref[0, g, (n - 1) * blk:n * blk, :]
        kk = jnp.concatenate([k_prev, k_ref[0, g, n * blk:(n + 1) * blk, :]], axis=0)
        s_scr[buf] = _dot_nt(kk, q.reshape(cols, SWA_HEAD_DIM))

    def softmax_pv(u, buf):
        g, h0, n = u
        sink = sink_ref[:, h0 * blk:(h0 + hpu) * blk] * LOG2_E
        s = s_scr[buf] + bias_scr[...]
        if n == 0:
            s = s + no_prev
        m = jnp.maximum(jnp.max(s, axis=0, keepdims=True), sink)
        e = jnp.exp2(s - m).astype(BF16)
        vt_prev = vtp_ref[0, g] if n == 0 else vt_ref[0, g, :, (n - 1) * blk:n * blk]
        vvt = jnp.concatenate([vt_prev, vt_ref[0, g, :, n * blk:(n + 1) * blk]], axis=1)
        vvt = jnp.concatenate([vvt, jnp.ones((16, 2 * blk), BF16)], axis=0)
        pv = _dot(vvt, e)
        denom = pv[SWA_HEAD_DIM:SWA_HEAD_DIM + 1] + jnp.exp2(sink - m)
        o = (pv[:SWA_HEAD_DIM] / denom).T
        for hh in range(hpu):
            h = h0 + hh
            o_ref[0, n * blk:(n + 1) * blk, h * SWA_HEAD_DIM:(h + 1) * SWA_HEAD_DIM] = (
                o[hh * blk:(hh + 1) * blk, :].astype(BF16))

    scores(units[0], 0)
    for idx, u in enumerate(units):
        if idx + 1 < len(units):
            scores(units[idx + 1], (idx + 1) % 2)
        softmax_pv(u, idx % 2)


def _swa(q_a, k_a, vt_a, sink_lanes, batch, seq, tq, heads_per_unit):
    assert SWA_GROUP % heads_per_unit == 0
    n_sub = tq // SWA_WINDOW
    cur = lambda b, i: (b, 0, i, 0)
    prev = lambda b, i: (b, 0, jnp.maximum(i * n_sub - 1, 0), 0)
    return pl.pallas_call(
        functools.partial(_swa_kernel, n_sub=n_sub),
        grid=(batch, seq // tq),
        in_specs=[
            pl.BlockSpec((1, SWA_HEADS, tq, SWA_HEAD_DIM), cur),
            pl.BlockSpec((1, SWA_KV_HEADS, tq, SWA_HEAD_DIM), cur),
            pl.BlockSpec((1, SWA_KV_HEADS, SWA_WINDOW, SWA_HEAD_DIM), prev),
            pl.BlockSpec((1, SWA_KV_HEADS, SWA_HEAD_DIM, tq), lambda b, i: (b, 0, 0, i)),
            pl.BlockSpec((1, SWA_KV_HEADS, SWA_HEAD_DIM, SWA_WINDOW),
                         lambda b, i: (b, 0, 0, jnp.maximum(i * n_sub - 1, 0))),
            pl.BlockSpec(sink_lanes.shape, lambda b, i: (0, 0)),
        ],
        out_specs=pl.BlockSpec((1, tq, D_MODEL), lambda b, i: (b, i, 0)),
        out_shape=jax.ShapeDtypeStruct((batch, seq, D_MODEL), BF16),
        scratch_shapes=[pltpu.VMEM((2, 2 * SWA_WINDOW, heads_per_unit * SWA_WINDOW), F32),
                        pltpu.VMEM((2 * SWA_WINDOW, heads_per_unit * SWA_WINDOW), F32)],
        compiler_params=pltpu.CompilerParams(
            dimension_semantics=("arbitrary", "arbitrary"), vmem_limit_bytes=VMEM_LIMIT),
        name="swa",
    )(q_a, k_a, k_a, vt_a, vt_a, sink_lanes)


def _mla_kernel(qt_ref, k_ref, vt_ref, o_ref, m_scr, acc_scr, s_scr, *, tq, tk):
    ratio = tq // tk
    assert tq == ratio * tk and ratio % 2 == 0
    chunks = tk // MLA_KV_CHUNK
    q_chunks = tq // MLA_KV_CHUNK
    n_q = qt_ref.shape[2] // q_chunks

    def q_block(i, carry):
        m_scr[...] = jnp.full(m_scr.shape, NEG_INF, F32)
        acc_scr[...] = jnp.zeros(acc_scr.shape, F32)

        def scores(j, buf, q0=0, blk=i):
            k = k_ref[0, 0, pl.ds(pl.multiple_of(j * tk, tk), tk), :]
            for c in range(q0 // MLA_KV_CHUNK, q_chunks):
                s_scr[buf, :, c * MLA_KV_CHUNK:(c + 1) * MLA_KV_CHUNK] = _dot(
                    k, qt_ref[0, 0, blk * q_chunks + c])

        def softmax_pv(j, buf, q0=0, key_offset=None):
            s = s_scr[buf, :, q0:]
            if key_offset is not None:
                kpos = lax.broadcasted_iota(jnp.int32, s.shape, 0) + key_offset
                qpos = lax.broadcasted_iota(jnp.int32, s.shape, 1) + q0
                s = jnp.where(kpos <= qpos, s, NEG_INF)
            m_old = m_scr[:, q0:]
            m_new = jnp.maximum(m_old, jnp.max(s, axis=0, keepdims=True))
            alpha = jnp.exp2(m_old - m_new)
            pb = jnp.exp2(s - m_new).astype(BF16)
            pv = _dot(vt_ref[0, 0, j * chunks], pb[:MLA_KV_CHUNK])
            for c in range(1, chunks):
                pv += _dot(vt_ref[0, 0, j * chunks + c],
                           pb[c * MLA_KV_CHUNK:(c + 1) * MLA_KV_CHUNK])
            acc_scr[:, q0:] = alpha * acc_scr[:, q0:] + pv
            m_scr[:, q0:] = m_new

        @pl.when(i == 0)
        def _():
            scores(0, 0)

        def pair(jj, c):
            t = 2 * jj
            scores(t + 1, 1)
            softmax_pv(t, 0)
            scores(t + 2, 0)
            softmax_pv(t + 1, 1)
            return c

        lax.fori_loop(0, (ratio // 2) * i, pair, 0)
        for d in range(ratio):
            if d + 1 < ratio:
                scores(ratio * i + d + 1, (d + 1) % 2, q0=(d + 1) * tk)
            else:
                scores(0, 0, blk=jnp.minimum(i + 1, n_q - 1))
            softmax_pv(ratio * i + d, d % 2, q0=d * tk, key_offset=d * tk)

        o = acc_scr[:MLA_V_DIM] / acc_scr[MLA_V_DIM:MLA_V_DIM + 1]
        o_ref[0, pl.ds(pl.multiple_of(i * tq, tq), tq), :] = o.T.astype(o_ref.dtype)
        return carry

    lax.fori_loop(0, n_q, q_block, 0)


def _mla(q_m, k_m, vt_m, batch, seq, tq, tk):
    return pl.pallas_call(
        functools.partial(_mla_kernel, tq=tq, tk=tk),
        grid=(batch, MLA_HEADS),
        in_specs=[
            pl.BlockSpec((1, 1, seq // MLA_KV_CHUNK, MLA_QK_DIM, MLA_KV_CHUNK),
                         lambda b, h: (b, h, 0, 0, 0)),
            pl.BlockSpec((1, 1, seq, MLA_QK_DIM), lambda b, h: (b, h, 0, 0)),
            pl.BlockSpec((1, 1, seq // MLA_KV_CHUNK, MLA_V_AUG, MLA_KV_CHUNK),
                         lambda b, h: (b, h, 0, 0, 0)),
        ],
        out_specs=pl.BlockSpec((1, seq, MLA_V_DIM), lambda b, h: (b, 0, h)),
        out_shape=jax.ShapeDtypeStruct((batch, seq, D_MODEL), BF16),
        scratch_shapes=[
            pltpu.VMEM((1, tq), F32),
            pltpu.VMEM((MLA_V_AUG, tq), F32),
            pltpu.VMEM((2, tk, tq), F32),
        ],
        compiler_params=pltpu.CompilerParams(
            dimension_semantics=("arbitrary", "arbitrary"), vmem_limit_bytes=VMEM_LIMIT),
        name="mla",
    )(q_m, k_m, vt_m)


def _mix_and_route(x_ref, oa_ref, ob_ref, ga_ref, gb_ref, wo_ref, gf_ref, wr_ref, br_ref):
    merged = (ga_ref[...].astype(F32) * oa_ref[...].astype(F32)
              + gb_ref[...].astype(F32) * ob_ref[...].astype(F32)).astype(BF16)
    h = x_ref[...] + _dot(merged, wo_ref[...])
    xn = _rms(h, gf_ref[...])

    tm = xn.shape[0]
    x_hi = xn.astype(BF16)
    x_lo = (xn - x_hi.astype(F32)).astype(BF16)
    prod_hi = _dot(x_hi, wr_ref[...])
    prod_lo = _dot(x_lo, wr_ref[...])
    lg = (prod_hi[:, :LANES] + prod_hi[:, LANES:] + prod_lo[:, :LANES] + prod_lo[:, LANES:]
          + br_ref[...])
    lg_t = lg.T
    big = jnp.int32(LANES)
    g_row = lax.broadcasted_iota(jnp.int32, (8, tm), 0)
    gl = jnp.where(g_row < N_GROUPS, lg_t[N_EXPERTS:N_EXPERTS + 8], NEG_INF)
    g_max = jnp.max(gl, axis=0, keepdims=True)
    g_w = 1.0 / jnp.sum(jnp.exp(gl - g_max), axis=0, keepdims=True)
    gidx = jnp.min(jnp.where(gl == g_max, g_row, big), axis=0, keepdims=True)

    row = lax.broadcasted_iota(jnp.int32, (N_EXPERTS, tm), 0)
    sel = (row >= gidx * EXPERTS_PER_GROUP) & (row < (gidx + 1) * EXPERTS_PER_GROUP)
    el = jnp.where(sel, lg_t[:N_EXPERTS], NEG_INF)
    e_exp = jnp.exp(el - jnp.max(el, axis=0, keepdims=True))
    e_prob = e_exp / jnp.sum(e_exp, axis=0, keepdims=True)
    p1 = jnp.max(e_prob, axis=0, keepdims=True)
    i1 = jnp.min(jnp.where(sel & (e_prob == p1), row, big), axis=0, keepdims=True)
    rest = sel & (row != i1)
    p2 = jnp.max(jnp.where(rest, e_prob, -1.0), axis=0, keepdims=True)
    i2 = jnp.min(jnp.where(rest & (e_prob == p2), row, big), axis=0, keepdims=True)
    top_sum = p1 + p2
    comb = jnp.where(row == i1, p1 / top_sum, jnp.where(row == i2, p2 / top_sum, 0.0)) * g_w
    gid_rows = jnp.where(g_row == 0, gidx.astype(F32), 0.0)
    comb_t = jnp.concatenate(
        [comb, gid_rows, jnp.zeros((LANES - N_EXPERTS - 8, tm), F32)], axis=0)
    return h, x_hi, comb_t


def _block_tail_kernel(x_ref, oa_ref, ob_ref, ga_ref, gb_ref, wout_ref, gf_ref, wr_ref, br_ref,
                       wi_ref, wo_ref, p_ref, gp_ref, wg_ref, wp_ref, gfin_ref, o_ref, *, chunk):
    tm = x_ref.shape[0]
    h, xn, comb_t = _mix_and_route(
        x_ref, oa_ref, ob_ref, ga_ref, gb_ref, wout_ref, gf_ref, wr_ref, br_ref)
    o_ref[...] = h
    comb = comb_t.T
    comb_hi = comb.astype(BF16)
    comb_lo = (comb - comb_hi.astype(F32)).astype(BF16)
    comb_hl = jnp.concatenate([comb_hi, comb_lo], axis=1)
    gid_col = comb[:, N_EXPERTS:N_EXPERTS + 1]
    gid_row = comb_t[N_EXPERTS:N_EXPERTS + 1, :]
    member_col = gid_col == lax.broadcasted_iota(jnp.int32, (tm, LANES), 1).astype(F32)
    member_row = gid_row == lax.broadcasted_iota(jnp.int32, (8, tm), 0).astype(F32)
    r_i = lax.broadcasted_iota(jnp.int32, (tm, tm), 0)
    c_i = lax.broadcasted_iota(jnp.int32, (tm, tm), 1)
    ranks_row = _dot(member_row.astype(BF16), (r_i < c_i).astype(BF16))
    ranks_col = _dot((r_i > c_i).astype(BF16), member_col.astype(BF16))
    pad = 2 * LANES
    never = -4.0 * tm
    slot_row = lax.broadcasted_iota(jnp.int32, (chunk, tm), 0).astype(F32)
    slot_col = lax.broadcasted_iota(jnp.int32, (tm, pad), 1).astype(F32)
    slot_col = jnp.where(slot_col < float(chunk), slot_col, never)

    groups = range(N_GROUPS)
    in_rows = [member_row[g:g + 1] for g in groups]
    rank_rows = [jnp.where(in_rows[g], ranks_row[g:g + 1], -2.0 * never) for g in groups]
    rank_cols = [jnp.where(member_col[:, g:g + 1], ranks_col[:, g:g + 1], -2.0 * never)
                 for g in groups]

    def one_hots(g, base):
        gather = jnp.where(rank_rows[g] - base == slot_row, 1.0, 0.0).astype(BF16)
        scatter = jnp.where(rank_cols[g] - base == slot_col, 1.0, 0.0).astype(BF16)
        return gather, scatter

    def compact(gather):
        xc = _dot(gather, xn).astype(BF16)
        wc = _dot(gather, comb_hl)
        return xc, wc[:, :LANES] + wc[:, LANES:]

    def experts_in(g, xc):
        return [_dot(xc, wi_ref[g * EXPERTS_PER_GROUP + j]) for j in range(EXPERTS_PER_GROUP)]

    def activate(g, hids, wc):
        acts = []
        for j, hid in enumerate(hids):
            e = g * EXPERTS_PER_GROUP + j
            gate = hid[:, :D_EXPERT]
            act = (gate * _sigmoid(gate)) * hid[:, D_EXPERT:] * wc[:, e:e + 1]
            acts.append(act.astype(BF16))
        return acts

    def experts_out(g, acts):
        yc = _dot(acts[0], wo_ref[g * EXPERTS_PER_GROUP])
        for j in range(1, EXPERTS_PER_GROUP):
            yc = yc + _dot(acts[j], wo_ref[g * EXPERTS_PER_GROUP + j])
        return jnp.concatenate(
            [yc.astype(BF16), jnp.zeros((pad - chunk, D_MODEL), BF16)], axis=0)

    hots = [one_hots(g, 0.0) for g in groups]
    compacted = [compact(hots[g][0]) for g in groups]
    hidden = [experts_in(g, compacted[g][0]) for g in groups]
    acts = [activate(g, hidden[g], compacted[g][1]) for g in groups]
    outs = [experts_out(g, acts[g]) for g in groups]
    o_ref[...] += _dot(jnp.concatenate([hots[g][1] for g in groups], axis=1),
                       jnp.concatenate(outs, axis=0))

    for g in groups:
        count = jnp.sum(in_rows[g].astype(jnp.int32))
        n_chunks = (count + (chunk - 1)) // chunk

        def chunk_body(c, carry, g=g):
            gather, scatter = one_hots(g, (c * chunk).astype(F32))
            xc, wc = compact(gather)
            yc = experts_out(g, activate(g, experts_in(g, xc), wc))
            o_ref[...] += _dot(scatter, yc)
            return carry

        lax.fori_loop(1, n_chunks, chunk_body, 0)

    h = o_ref[...]
    gate = _sigmoid(_dot(_rms(h, gp_ref[...]).astype(BF16), wg_ref[...]))
    h = h + gate * _dot(p_ref[...].astype(BF16), wp_ref[...])
    o_ref[...] = _rms(h, gfin_ref[...])


def _block_tail(x2, o_a, o_b, g_a, g_b, w_o, g_ffn, w_r, b_r, w_ei, w_eo, p2, g_ple, w_pg, w_pp,
                g_final, tm, chunk):
    tokens = x2.shape[0]
    tok = lambda t: (t, 0)
    const = lambda t: (0, 0)
    return pl.pallas_call(
        functools.partial(_block_tail_kernel, chunk=chunk),
        grid=(tokens // tm,),
        in_specs=[
            pl.BlockSpec((tm, D_MODEL), tok),
            pl.BlockSpec((tm, D_MODEL), tok),
            pl.BlockSpec((tm, D_MODEL), tok),
            pl.BlockSpec((tm, D_MODEL), tok),
            pl.BlockSpec((tm, D_MODEL), tok),
            pl.BlockSpec((D_MODEL, D_MODEL), const, pipeline_mode=pl.Buffered(1)),
            pl.BlockSpec((1, D_MODEL), const),
            pl.BlockSpec((D_MODEL, 2 * LANES), const, pipeline_mode=pl.Buffered(1)),
            pl.BlockSpec((1, LANES), const),
            pl.BlockSpec(w_ei.shape, lambda t: (0, 0, 0), pipeline_mode=pl.Buffered(1)),
            pl.BlockSpec(w_eo.shape, lambda t: (0, 0, 0), pipeline_mode=pl.Buffered(1)),
            pl.BlockSpec((tm, PLE_DIM), tok),
            pl.BlockSpec((1, D_MODEL), const),
            pl.BlockSpec((D_MODEL, D_MODEL), const, pipeline_mode=pl.Buffered(1)),
            pl.BlockSpec((PLE_DIM, D_MODEL), const, pipeline_mode=pl.Buffered(1)),
            pl.BlockSpec((1, D_MODEL), const),
        ],
        out_specs=pl.BlockSpec((tm, D_MODEL), tok),
        out_shape=jax.ShapeDtypeStruct((tokens, D_MODEL), F32),
        compiler_params=pltpu.CompilerParams(
            dimension_semantics=("arbitrary",), vmem_limit_bytes=VMEM_LIMIT),
        name="block_tail",
    )(x2, o_a, o_b, g_a, g_b, w_o, g_ffn, w_r, b_r, w_ei, w_eo, p2, g_ple, w_pg, w_pp, g_final)


def _rope_tables(seq):
    half = SWA_HEAD_DIM // 2
    pos = jnp.arange(seq, dtype=F32)
    inv = ROPE_THETA ** (-jnp.arange(0, SWA_HEAD_DIM, 2, dtype=F32) / SWA_HEAD_DIM)
    ang = pos[:, None] * inv[None, :]
    cos, sin = jnp.cos(ang), jnp.sin(ang)
    reps = LANES // SWA_HEAD_DIM
    cos_t = jnp.broadcast_to(cos[:, None, :], (seq, 2 * reps, half)).reshape(seq, LANES)
    sin_t = jnp.broadcast_to(
        jnp.stack([-sin, sin], axis=1)[:, None], (seq, reps, 2, half)).reshape(seq, LANES)
    ang_t = inv[:, None] * pos[None, :]
    return cos_t, sin_t, jnp.cos(ang_t), jnp.sin(ang_t)


def _layer(h2d, p2d, g_mix, w_in, sinks, g_q, w_uq, g_kv, w_ukv, w_out, g_ffn, w_rg, b_rg,
           w_re, b_re, w_ei, w_eo, g_ple, w_pg, w_pp, g_final, rope, batch, seq):
    cos_t, sin_t, cos_h, sin_h = rope
    assert SWA_HEAD_DIM == MLA_ROPE_DIM, "one rotary table serves both mixers"
    assert w_in.shape[1] == _C_GATES + 2 * D_MODEL
    w_uqt = jnp.transpose(
        w_uq.reshape(MLA_Q_RANK, MLA_HEADS, MLA_QK_DIM), (1, 2, 0)).astype(BF16)
    ukv = w_ukv.reshape(MLA_KV_RANK, MLA_HEADS, MLA_NOPE_DIM + MLA_V_DIM)
    w_uk_c = ukv[:, :, :MLA_NOPE_DIM].reshape(MLA_KV_RANK, -1).astype(BF16)
    w_uvt = jnp.transpose(ukv[:, :, MLA_NOPE_DIM:], (1, 2, 0)).astype(BF16)

    q_a, k_a, v_a, g_a, g_b, q_m, k_m, v_m = _inproj(
        h2d, g_mix[None], w_in.astype(BF16), cos_t, sin_t, cos_h, sin_h, g_q[None], w_uqt,
        g_kv[None], w_uk_c, w_uvt, batch, seq, tm=MLA_KV_CHUNK)

    sink_lanes = jnp.repeat(sinks.astype(F32), SWA_WINDOW)[None]
    o_a = _swa(q_a, k_a, v_a, sink_lanes, batch, seq, tq=SWA_TQ,
               heads_per_unit=SWA_HEADS_PER_UNIT)
    o_b = _mla(q_m, k_m, v_m, batch, seq, tq=MLA_TQ, tk=MLA_TK)

    w_r = jnp.concatenate(
        [w_re, w_rg, jnp.zeros((D_MODEL, LANES - N_EXPERTS - N_GROUPS), F32)], axis=1)
    w_r_hi = w_r.astype(BF16)
    w_r_lo = (w_r - w_r_hi.astype(F32)).astype(BF16)
    w_r2 = jnp.concatenate([w_r_hi, w_r_lo], axis=1)
    b_r = jnp.concatenate(
        [b_re, b_rg, jnp.zeros((LANES - N_EXPERTS - N_GROUPS,), F32)])[None]
    tokens = batch * seq
    return _block_tail(
        h2d, o_a.reshape(tokens, D_MODEL), o_b.reshape(tokens, D_MODEL), g_a, g_b,
        w_out.astype(BF16), g_ffn[None], w_r2, b_r, w_ei.astype(BF16), w_eo.astype(BF16), p2d,
        g_ple[None], w_pg.astype(BF16), w_pp.astype(BF16), g_final[None], tm=TAIL_TM,
        chunk=MOE_CHUNK)


def kernel(x, p, g_mix, w_in, swa_sinks, mla_g_q, mla_w_uq, mla_g_kv, mla_w_ukv, w_out, g_ffn,
           w_router_group, b_router_group, w_router_expert, b_router_expert, w_expert_in,
           w_expert_out, g_ple, w_ple_gate, w_ple_proj, g_final):
    batch, seq, d = x.shape
    depth = p.shape[0]
    assert d == D_MODEL and depth == 1, "final RMSNorm is fused into the single layer"
    rope = _rope_tables(seq)
    out = _layer(
        x.reshape(batch * seq, d), p[0].reshape(batch * seq, PLE_DIM), g_mix[0], w_in[0],
        swa_sinks[0], mla_g_q[0], mla_w_uq[0], mla_g_kv[0], mla_w_ukv[0], w_out[0], g_ffn[0],
        w_router_group[0], b_router_group[0], w_router_expert[0], b_router_expert[0],
        w_expert_in[0], w_expert_out[0], g_ple[0], w_ple_gate[0], w_ple_proj[0], g_final,
        rope, batch, seq)
    return out.reshape(batch, seq, d)
```
